```python
import jax, jax.numpy as jnp
from jax import lax
import numpy as np

D_MODEL = 1024
BATCH = 8
SEQ = 2048
DEPTH = 1
DEC_BATCH = 128
DEC_SEQ = 1
PAST_LEN = 16384
PAGE_SIZE = 128

CONV_WIDTH = 3
CONV_DIM = D_MODEL
GLA_HEADS = 4
GLA_DK = (D_MODEL // 2) // GLA_HEADS
GLA_DV = D_MODEL // GLA_HEADS
GATE_RANK = 16
GATE_NORMALIZER = 16.0
GLA_CHUNK = 64
D_FF = -(-8 * D_MODEL // (3 * 256)) * 256
PLE_DIM = 256
EPS = 1e-6
SPLITS = (CONV_DIM, CONV_DIM, CONV_DIM,
          GLA_HEADS * GLA_DK, GLA_HEADS * GLA_DK,
          GLA_HEADS * GLA_DV, GLA_HEADS * GLA_DV,
          GATE_RANK, D_MODEL, D_MODEL)
N_IN_COLS = sum(SPLITS)

kernel_name = "hybrid_shortconv_gla_gated_merge_step"


def rms_norm(x, w):
    x32 = x.astype(jnp.float32)
    y = x32 * lax.rsqrt(jnp.mean(x32 * x32, axis=-1, keepdims=True) + EPS)
    return (y * w.astype(jnp.float32)).astype(x.dtype)


def gla_recurrence(q, k, v, logw, s0):
    bsz, t = q.shape[0], q.shape[1]
    c = min(GLA_CHUNK, t)
    n = -(-t // c)
    pad = n * c - t

    def prep(a):
        a = jnp.pad(a, ((0, 0), (0, pad), (0, 0), (0, 0)))
        return a.reshape(bsz, n, c, a.shape[2], a.shape[3]).transpose(1, 0, 3, 2, 4)

    qc, kc, vc, wc = prep(q), prep(k), prep(v), prep(logw)
    mask = jnp.tril(jnp.ones((c, c), dtype=bool))

    def step(s, inp):
        qi, ki, vi, wi = inp
        b = jnp.cumsum(wi, axis=2)
        inter = jnp.einsum('bhtk,bhkv->bhtv', qi * jnp.exp(b), s)
        diff = b[:, :, :, None, :] - b[:, :, None, :, :]
        decay = jnp.where(mask[:, :, None], jnp.exp(jnp.minimum(diff, 0.0)), 0.0)
        scores = jnp.einsum('bhtk,bhsk,bhtsk->bhts', qi, ki, decay)
        intra = jnp.einsum('bhts,bhsv->bhtv', scores, vi)
        b_last = b[:, :, -1, :]
        s_new = jnp.exp(b_last)[..., None] * s + jnp.einsum(
            'bhsk,bhsv->bhkv', ki * jnp.exp(b_last[:, :, None, :] - b), vi)
        return s_new, inter + intra

    s_fin, o = lax.scan(step, s0, (qc, kc, vc, wc))
    o = o.transpose(1, 0, 3, 2, 4).reshape(bsz, n * c, GLA_HEADS, GLA_DV)[:, :t]
    return o, s_fin


def trunk_layer(x, p, conv_buf, s0, w_norm_mix_pre, w_in, w_conv, w_a_out, w_gk, b_gk, w_gla_norm,
                w_b_out, w_o, w_norm_mix_post, w_norm_ffn_pre, w_ffn_gate, w_ffn_up, w_ffn_down,
                w_norm_ffn_post, w_ple_proj, w_ple_gate, w_norm_ple_post):
    bsz, t, _ = x.shape
    hn = rms_norm(x, w_norm_mix_pre)
    z = jnp.einsum('btd,dn->btn', hn, w_in)
    idx = [int(i) for i in np.cumsum(SPLITS)[:-1]]
    b_a, c_a, x_a, q, k, v, g, gk_lr, gate_a, gate_b = jnp.split(z, idx, axis=-1)

    u = c_a * x_a
    up = jnp.concatenate([conv_buf.astype(u.dtype), u], axis=1)
    y_conv = (w_conv[0] * up[:, 0:t] + w_conv[1] * up[:, 1:t + 1] + w_conv[2] * up[:, 2:t + 2])
    new_buf = up[:, t:t + CONV_WIDTH - 1]
    y_a = jnp.einsum('btc,cd->btd', b_a * y_conv, w_a_out)

    qh = q.reshape(bsz, t, GLA_HEADS, GLA_DK).astype(jnp.float32) * (GLA_DK ** -0.5)
    kh = k.reshape(bsz, t, GLA_HEADS, GLA_DK).astype(jnp.float32)
    vh = v.reshape(bsz, t, GLA_HEADS, GLA_DV).astype(jnp.float32)
    gk = (jnp.einsum('btr,rk->btk', gk_lr, w_gk) + b_gk).astype(jnp.float32)
    logw = (jax.nn.log_sigmoid(gk) / GATE_NORMALIZER).reshape(bsz, t, GLA_HEADS, GLA_DK)
    o, s_new = gla_recurrence(qh, kh, vh, logw, s0.astype(jnp.float32))
    o = o * lax.rsqrt(jnp.mean(o * o, axis=-1, keepdims=True) + EPS) * w_gla_norm.astype(jnp.float32)
    o = (o * jax.nn.silu(g.reshape(bsz, t, GLA_HEADS, GLA_DV).astype(jnp.float32))).astype(x.dtype)
    y_b = jnp.einsum('btf,fd->btd', o.reshape(bsz, t, GLA_HEADS * GLA_DV), w_b_out)

    merged = jax.nn.sigmoid(gate_a) * y_a + jax.nn.sigmoid(gate_b) * y_b
    mix = jnp.einsum('btd,de->bte', merged, w_o)
    h = x + rms_norm(mix, w_norm_mix_post)

    f = rms_norm(h, w_norm_ffn_pre)
    f = jax.nn.silu(jnp.einsum('btd,df->btf', f, w_ffn_gate)) * jnp.einsum('btd,df->btf', f, w_ffn_up)
    f = jnp.einsum('btf,fd->btd', f, w_ffn_down)
    h = h + rms_norm(f, w_norm_ffn_post)

    e = jnp.einsum('btp,pd->btd', p, w_ple_proj) * jax.nn.sigmoid(jnp.einsum('btd,de->bte', h, w_ple_gate))
    h = h + rms_norm(e, w_norm_ple_post)
    return h, new_buf, s_new


def setup_inputs(seed: int = 0) -> dict:
    key = jax.random.key(seed)
    ks = jax.random.split(key, 32)
    f32 = jnp.float32

    def nrm(k, shape, scale):
        return jax.random.normal(k, shape, f32) * scale

    def gain(k, shape):
        return 1.0 + 0.05 * jax.random.normal(k, shape, f32)

    return {
        "x_prompt": nrm(ks[0], (BATCH, SEQ, D_MODEL), 1.0),
        "x_sample": nrm(ks[1], (DEC_BATCH, DEC_SEQ, D_MODEL), 1.0),
        "state_conv": nrm(ks[2], (DEPTH, DEC_BATCH, CONV_WIDTH - 1, CONV_DIM), 1.0),
        "state_gla": nrm(ks[3], (DEPTH, DEC_BATCH, GLA_HEADS, GLA_DK, GLA_DV), 0.5),
        "p_prompt": nrm(ks[4], (DEPTH, BATCH, SEQ, PLE_DIM), 1.0),
        "p_sample": nrm(ks[5], (DEPTH, DEC_BATCH, DEC_SEQ, PLE_DIM), 1.0),
        "w_norm_mix_pre": gain(ks[6], (DEPTH, D_MODEL)),
        "w_in": nrm(ks[7], (DEPTH, D_MODEL, N_IN_COLS), D_MODEL ** -0.5),
        "w_conv": nrm(ks[8], (DEPTH, CONV_WIDTH, CONV_DIM), CONV_WIDTH ** -0.5),
        "w_a_out": nrm(ks[9], (DEPTH, CONV_DIM, D_MODEL), CONV_DIM ** -0.5),
        "w_gk": nrm(ks[10], (DEPTH, GATE_RANK, GLA_HEADS * GLA_DK), GATE_RANK ** -0.5),
        "b_gk": nrm(ks[11], (DEPTH, GLA_HEADS * GLA_DK), 0.1),
        "w_gla_norm": gain(ks[12], (DEPTH, GLA_DV)),
        "w_b_out": nrm(ks[13], (DEPTH, GLA_HEADS * GLA_DV, D_MODEL), (GLA_HEADS * GLA_DV) ** -0.5),
        "w_o": nrm(ks[14], (DEPTH, D_MODEL, D_MODEL), D_MODEL ** -0.5),
        "w_norm_mix_post": gain(ks[15], (DEPTH, D_MODEL)),
        "w_norm_ffn_pre": gain(ks[16], (DEPTH, D_MODEL)),
        "w_ffn_gate": nrm(ks[17], (DEPTH, D_MODEL, D_FF), D_MODEL ** -0.5),
        "w_ffn_up": nrm(ks[18], (DEPTH, D_MODEL, D_FF), D_MODEL ** -0.5),
        "w_ffn_down": nrm(ks[19], (DEPTH, D_FF, D_MODEL), D_FF ** -0.5),
        "w_norm_ffn_post": gain(ks[20], (DEPTH, D_MODEL)),
        "w_ple_proj": nrm(ks[21], (DEPTH, PLE_DIM, D_MODEL), PLE_DIM ** -0.5),
        "w_ple_gate": nrm(ks[22], (DEPTH, D_MODEL, D_MODEL), D_MODEL ** -0.5),
        "w_norm_ple_post": gain(ks[23], (DEPTH, D_MODEL)),
    }


def reference(x_prompt, x_sample, state_conv, state_gla, p_prompt, p_sample,
              w_norm_mix_pre, w_in, w_conv, w_a_out, w_gk, b_gk, w_gla_norm, w_b_out, w_o,
              w_norm_mix_post, w_norm_ffn_pre, w_ffn_gate, w_ffn_up, w_ffn_down, w_norm_ffn_post,
              w_ple_proj, w_ple_gate, w_norm_ple_post):
    hp, hs = x_prompt, x_sample
    conv_p, gla_p, conv_s, gla_s = [], [], [], []
    for i in range(DEPTH):
        weights = (w_norm_mix_pre[i], w_in[i], w_conv[i], w_a_out[i], w_gk[i], b_gk[i], w_gla_norm[i],
                   w_b_out[i], w_o[i], w_norm_mix_post[i], w_norm_ffn_pre[i], w_ffn_gate[i], w_ffn_up[i],
                   w_ffn_down[i], w_norm_ffn_post[i], w_ple_proj[i], w_ple_gate[i], w_norm_ple_post[i])
        buf0 = jnp.zeros((BATCH, CONV_WIDTH - 1, CONV_DIM), x_prompt.dtype)
        s0 = jnp.zeros((BATCH, GLA_HEADS, GLA_DK, GLA_DV), jnp.float32)
        hp, cbp, sp = trunk_layer(hp, p_prompt[i], buf0, s0, *weights)
        hs, cbs, ss = trunk_layer(hs, p_sample[i], state_conv[i], state_gla[i], *weights)
        conv_p.append(cbp); gla_p.append(sp); conv_s.append(cbs); gla_s.append(ss)
    new_conv_prompt = jnp.stack(conv_p)
    new_gla_prompt = jnp.stack(gla_p)
    new_conv_sample = jnp.stack(conv_s)
    new_gla_sample = jnp.stack(gla_s)
    return (hp, hs, new_conv_prompt, new_gla_prompt, new_conv_sample, new_gla_sample)
```

```python
import functools

import numpy as np
import jax
import jax.numpy as jnp
from jax import lax
from jax.experimental import pallas as pl
from jax.experimental.pallas import tpu as pltpu

D_MODEL = 1024
CONV_DIM = D_MODEL
GLA_HEADS = 4
GLA_DK = 128
GLA_DV = 256
QK_DIM = GLA_HEADS * GLA_DK
V_DIM = GLA_HEADS * GLA_DV
GATE_RANK = 16
GATE_NORMALIZER = 16.0
D_FF = 2816
PLE_DIM = 256
EPS = 1e-6

LANES = 128
SUBLANES = 8
VMEM_LIMIT_BYTES = 56 * 1024 * 1024

OFF_B, OFF_C, OFF_X = 0, 1024, 2048
OFF_Q, OFF_K, OFF_V, OFF_G = 3072, 3584, 4096, 5120
OFF_GA, OFF_GB = 6144, 7168
N_MAIN = 8192
GKLR_START = 6144

SEQ_TILE = 256
N_LEVELS = 8
FFN_TILE = 512
FFN_CHUNK = 1408
SAMPLE_BLOCK = 8

F32 = jnp.float32
BF16 = jnp.bfloat16


def _rms(x, w):
    return x * lax.rsqrt(jnp.mean(x * x, axis=-1, keepdims=True) + EPS) * w


def _sigmoid(x):
    return 1.0 / (1.0 + jnp.exp(-x))


def _log_sigmoid(x):
    return jnp.minimum(x, 0.0) - jnp.log(1.0 + jnp.exp(-jnp.abs(x)))


def _dot(a, b):
    return jnp.dot(a, b, preferred_element_type=F32)


def _dot_nt(a, b):
    return lax.dot_general(a, b, (((1,), (1,)), ((), ())), preferred_element_type=F32)


def _column_broadcast(row):
    return jnp.broadcast_to(row, (LANES, LANES)).T


def _projections(hn, w_in_ref, w_gklr_ref, w_gk_ref, b_gk_ref):
    def proj(off, width):
        return _dot(hn, w_in_ref[:, off:off + width])

    gk_lr = _dot(hn, w_gklr_ref[...])
    gk = _dot(gk_lr.astype(BF16), w_gk_ref[...]) + b_gk_ref[...]
    logw = _log_sigmoid(gk) * (1.0 / GATE_NORMALIZER)
    return proj, logw


def _mix_out(x, o_heads, g, gate_a, gate_b, y_a, w_gn, w_b_ref, w_o_ref, wn_post):
    normed = []
    for h in range(GLA_HEADS):
        o = o_heads[h]
        gh = g[:, h * GLA_DV:(h + 1) * GLA_DV]
        o = o * lax.rsqrt(jnp.mean(o * o, axis=-1, keepdims=True) + EPS) * w_gn
        normed.append((o * (gh * _sigmoid(gh))).astype(BF16))
    y_b = _dot(jnp.concatenate(normed, axis=1), w_b_ref[...])
    merged = _sigmoid(gate_a) * y_a + _sigmoid(gate_b) * y_b
    mix = _dot(merged.astype(BF16), w_o_ref[...])
    return x + _rms(mix, wn_post)


def _prompt_mixer_kernel(x_ref, lev_ref, tri_ref, wn_pre_ref, w_in_ref, w_gklr_ref, w_gk_ref,
                         b_gk_ref, w_conv_ref, w_a_ref, w_gn_ref, w_b_ref, w_o_ref, wn_post_ref,
                         h_ref, conv_ref, state_ref, cum_ref):
    t = SEQ_TILE

    @pl.when(pl.program_id(1) == 0)
    def _():
        conv_ref[...] = jnp.zeros_like(conv_ref)
        state_ref[...] = jnp.zeros_like(state_ref)

    x = x_ref[0]
    hn = _rms(x, wn_pre_ref[...]).astype(BF16)
    proj, logw = _projections(hn, w_in_ref, w_gklr_ref, w_gk_ref, b_gk_ref)

    u = proj(OFF_C, CONV_DIM) * proj(OFF_X, CONV_DIM)
    prev2 = conv_ref[0, 0, 0:1, :]
    prev1 = conv_ref[0, 0, 1:2, :]
    row = lax.broadcasted_iota(jnp.int32, (t, CONV_DIM), 0)
    u1 = jnp.where(row == 0, prev1, pltpu.roll(u, 1, 0))
    u2 = jnp.where(row == 0, prev2, jnp.where(row == 1, prev1, pltpu.roll(u, 2, 0)))
    wc = w_conv_ref[...]
    y_conv = wc[0:1] * u2 + wc[1:2] * u1 + wc[2:3] * u
    conv_ref[0, 0] = u[t - 2:t]
    y_a = _dot((proj(OFF_B, CONV_DIM) * y_conv).astype(BF16), w_a_ref[...])

    hi = logw.astype(BF16)
    lo = (logw - hi.astype(F32)).astype(BF16)
    cum = _dot(tri_ref[...], hi) + _dot(tri_ref[...], lo)
    cum_ref[...] = cum
    cum_last = cum[t - 1:t]

    q = proj(OFF_Q, QK_DIM) * (GLA_DK ** -0.5)
    k = proj(OFF_K, QK_DIM)
    v = proj(OFF_V, V_DIM).astype(BF16)

    lev = lev_ref[...]
    rowq = lax.broadcasted_iota(jnp.int32, (t, QK_DIM), 0)

    def level_log_factor(i):
        m = 1 << i
        if i == 0:
            return jnp.where((rowq & 1) == 1, logw, 0.0)
        if i == 1:
            up = pltpu.roll(logw, t - 1, 0)
            dn = pltpu.roll(logw, 1, 0)
            r4 = rowq & 3
            return jnp.where(r4 == 0, up, jnp.where(r4 == 1, 0.0, jnp.where(r4 == 2, logw, logw + dn)))
        pieces = []
        for blk in range(t // (2 * m)):
            r = blk * 2 * m + m - 1
            pieces.append(jnp.broadcast_to(cum_ref[r:r + 1, :], (2 * m, QK_DIM)))
        ref_rows = pieces[0] if len(pieces) == 1 else jnp.concatenate(pieces, axis=0)
        return -jnp.abs(cum - ref_rows)

    def head(a, h):
        return a[:, h * GLA_DK:(h + 1) * GLA_DK]

    qb = q.astype(BF16)
    kb = k.astype(BF16)
    scores = [jnp.where(lev == -1, _dot_nt(head(qb, h), head(kb, h)), 0.0) for h in range(GLA_HEADS)]
    for i in range(N_LEVELS):
        e = jnp.exp(level_log_factor(i))
        qe = (q * e).astype(BF16)
        ke = (k * e).astype(BF16)
        for h in range(GLA_HEADS):
            scores[h] = jnp.where(lev == i, _dot_nt(head(qe, h), head(ke, h)), scores[h])

    q_in = (q * jnp.exp(cum)).astype(BF16)
    k_out = k * jnp.exp(cum_last - cum)
    a_last = jnp.exp(cum_last)

    o_heads = []
    for h in range(GLA_HEADS):
        vh = v[:, h * GLA_DV:(h + 1) * GLA_DV]
        s_old = state_ref[0, 0, h]
        o = _dot(scores[h].astype(BF16), vh) + _dot(head(q_in, h), s_old.astype(BF16))
        o_heads.append(o)
        a_col = _column_broadcast(head(a_last, h))
        a_col = jnp.concatenate([a_col, a_col], axis=1)
        state_ref[0, 0, h] = a_col * s_old + _dot(head(k_out, h).T.astype(BF16), vh)

    h_ref[0] = _mix_out(x, o_heads, proj(OFF_G, V_DIM), proj(OFF_GA, D_MODEL), proj(OFF_GB, D_MODEL),
                        y_a, w_gn_ref[...], w_b_ref, w_o_ref, wn_post_ref[...])


def _sample_mixer_kernel(x_ref, cbuf_ref, st_ref, wn_pre_ref, w_in_ref, w_gklr_ref, w_gk_ref,
                         b_gk_ref, w_conv_ref, w_a_ref, w_gn_ref, w_b_ref, w_o_ref, wn_post_ref,
                         h_ref, conv_ref, st_out_ref,
                         q_scr, k_scr, a_scr, v_scr, o_scr, ya_scr, g_scr, ga_scr, gb_scr):
    step = pl.program_id(0)

    @pl.when(step == 0)
    def _():
        x = x_ref[...]
        hn = _rms(x, wn_pre_ref[...]).astype(BF16)
        proj, logw = _projections(hn, w_in_ref, w_gklr_ref, w_gk_ref, b_gk_ref)
        u = proj(OFF_C, CONV_DIM) * proj(OFF_X, CONV_DIM)
        buf0 = cbuf_ref[:, 0:CONV_DIM]
        buf1 = cbuf_ref[:, CONV_DIM:2 * CONV_DIM]
        wc = w_conv_ref[...]
        y_conv = wc[0:1] * buf0 + wc[1:2] * buf1 + wc[2:3] * u
        conv_ref[:, 0:CONV_DIM] = buf1
        conv_ref[:, CONV_DIM:2 * CONV_DIM] = u
        ya_scr[...] = _dot((proj(OFF_B, CONV_DIM) * y_conv).astype(BF16), w_a_ref[...])
        blocked = lambda a: a.reshape(a.shape[0] // SAMPLE_BLOCK, SAMPLE_BLOCK, a.shape[1])
        q_scr[...] = blocked(proj(OFF_Q, QK_DIM) * (GLA_DK ** -0.5))
        k_scr[...] = blocked(proj(OFF_K, QK_DIM))
        a_scr[...] = blocked(jnp.exp(logw))
        v_scr[...] = blocked(proj(OFF_V, V_DIM))
        g_scr[...] = proj(OFF_G, V_DIM)
        ga_scr[...] = proj(OFF_GA, D_MODEL)
        gb_scr[...] = proj(OFF_GB, D_MODEL)

    for n in range(SAMPLE_BLOCK):
        q_row = q_scr[step, n:n + 1, :]
        k_row = k_scr[step, n:n + 1, :]
        a_row = a_scr[step, n:n + 1, :]
        v_row = v_scr[step, n:n + 1, :]
        for h in range(GLA_HEADS):
            sl = slice(h * GLA_DK, (h + 1) * GLA_DK)
            a_col = _column_broadcast(a_row[:, sl])
            k_col = _column_broadcast(k_row[:, sl])
            q_col = _column_broadcast(q_row[:, sl])
            for half in range(GLA_DV // LANES):
                c0 = half * LANES
                s_old = st_ref[n, h, :, c0:c0 + LANES]
                vv = v_row[:, h * GLA_DV + c0:h * GLA_DV + c0 + LANES]
                s_new = a_col * s_old + k_col * vv
                st_out_ref[n, h, :, c0:c0 + LANES] = s_new
                o_scr[step, n:n + 1, h * GLA_DV + c0:h * GLA_DV + c0 + LANES] = jnp.sum(
                    q_col * s_new, axis=0, keepdims=True)

    @pl.when(step == pl.num_programs(0) - 1)
    def _():
        o = o_scr[...].reshape(x_ref.shape[0], V_DIM)
        o_heads = [o[:, h * GLA_DV:(h + 1) * GLA_DV] for h in range(GLA_HEADS)]
        h_ref[...] = _mix_out(x_ref[...], o_heads, g_scr[...], ga_scr[...], gb_scr[...], ya_scr[...],
                              w_gn_ref[...], w_b_ref, w_o_ref, wn_post_ref[...])


def _ffn_kernel(h_ref, p_ref, wn_pre_ref, w_gate_ref, w_up_ref, w_down_ref, wn_post_ref,
                w_pp_ref, w_pg_ref, wn_ple_ref, out_ref):
    h = h_ref[...]
    f = _rms(h, wn_pre_ref[...]).astype(BF16)
    acc = None
    for c in range(D_FF // FFN_CHUNK):
        sl = slice(c * FFN_CHUNK, (c + 1) * FFN_CHUNK)
        gate = _dot(f, w_gate_ref[:, sl])
        up = _dot(f, w_up_ref[:, sl])
        act = (gate * _sigmoid(gate) * up).astype(BF16)
        part = _dot(act, w_down_ref[sl, :])
        acc = part if acc is None else acc + part
    h = h + _rms(acc, wn_post_ref[...])
    e = _dot(p_ref[...].astype(BF16), w_pp_ref[...]) * _sigmoid(_dot(h.astype(BF16), w_pg_ref[...]))
    out_ref[...] = h + _rms(e, wn_ple_ref[...])


def _resident(shape):
    return pl.BlockSpec(shape, lambda *_: (0,) * len(shape), pipeline_mode=pl.Buffered(1))


def _level_map(t):
    idx = np.arange(t)
    xor = idx[:, None] ^ idx[None, :]
    lev = np.floor(np.log2(np.maximum(xor, 1))).astype(np.int32)
    lev = np.where(idx[:, None] > idx[None, :], lev, -2)
    lev = np.where(idx[:, None] == idx[None, :], -1, lev)
    return jnp.asarray(lev, dtype=jnp.int32)


def _mixer_weight_specs():
    return [
        _resident((1, D_MODEL)),
        _resident((D_MODEL, N_MAIN)),
        _resident((D_MODEL, LANES)),
        _resident((LANES, QK_DIM)),
        _resident((1, QK_DIM)),
        _resident((3, CONV_DIM)),
        _resident((CONV_DIM, D_MODEL)),
        _resident((1, GLA_DV)),
        _resident((V_DIM, D_MODEL)),
        _resident((D_MODEL, D_MODEL)),
        _resident((1, D_MODEL)),
    ]


def _prompt_mixer(x, mixer_weights):
    b, s, _ = x.shape
    t = SEQ_TILE
    tri = jnp.asarray(np.tril(np.ones((t, t), np.float32)), dtype=BF16)
    return pl.pallas_call(
        _prompt_mixer_kernel,
        grid=(b, s // t),
        in_specs=[pl.BlockSpec((1, t, D_MODEL), lambda i, j: (i, j, 0)),
                  _resident((t, t)), _resident((t, t))] + _mixer_weight_specs(),
        out_specs=[pl.BlockSpec((1, t, D_MODEL), lambda i, j: (i, j, 0)),
                   pl.BlockSpec((1, 1, 2, CONV_DIM), lambda i, j: (0, i, 0, 0)),
                   pl.BlockSpec((1, 1, GLA_HEADS, GLA_DK, GLA_DV), lambda i, j: (0, i, 0, 0, 0))],
        out_shape=[jax.ShapeDtypeStruct((b, s, D_MODEL), F32),
                   jax.ShapeDtypeStruct((1, b, 2, CONV_DIM), F32),
                   jax.ShapeDtypeStruct((1, b, GLA_HEADS, GLA_DK, GLA_DV), F32)],
        scratch_shapes=[pltpu.VMEM((t, QK_DIM), F32)],
        compiler_params=pltpu.CompilerParams(
            dimension_semantics=("arbitrary", "arbitrary"), vmem_limit_bytes=VMEM_LIMIT_BYTES),
        name="prompt_mixer",
    )(x, _level_map(t), tri, *mixer_weights)


def _sample_mixer(x, conv_buf, state, mixer_weights):
    n = x.shape[0]
    nb = SAMPLE_BLOCK
    state_spec = pl.BlockSpec((nb, GLA_HEADS, GLA_DK, GLA_DV), lambda i: (i, 0, 0, 0))
    whole = lambda shape: pl.BlockSpec(shape, lambda i: (0,) * len(shape))
    return pl.pallas_call(
        _sample_mixer_kernel,
        grid=(n // nb,),
        in_specs=[_resident((n, D_MODEL)), _resident((n, 2 * CONV_DIM)), state_spec] + _mixer_weight_specs(),
        out_specs=[whole((n, D_MODEL)), whole((n, 2 * CONV_DIM)), state_spec],
        out_shape=[jax.ShapeDtypeStruct((n, D_MODEL), F32),
                   jax.ShapeDtypeStruct((n, 2 * CONV_DIM), F32),
                   jax.ShapeDtypeStruct((n, GLA_HEADS, GLA_DK, GLA_DV), F32)],
        scratch_shapes=[pltpu.VMEM((n // nb, nb, QK_DIM), F32), pltpu.VMEM((n // nb, nb, QK_DIM), F32),
                        pltpu.VMEM((n // nb, nb, QK_DIM), F32), pltpu.VMEM((n // nb, nb, V_DIM), F32),
                        pltpu.VMEM((n // nb, nb, V_DIM), F32), pltpu.VMEM((n, D_MODEL), F32),
                        pltpu.VMEM((n, V_DIM), F32), pltpu.VMEM((n, D_MODEL), F32),
                        pltpu.VMEM((n, D_MODEL), F32)],
        compiler_params=pltpu.CompilerParams(
            dimension_semantics=("arbitrary",), vmem_limit_bytes=VMEM_LIMIT_BYTES),
        name="sample_mixer",
    )(x, conv_buf, state, *mixer_weights)


def _ffn(h, p, ffn_weights, tile):
    rows = h.shape[0]
    return pl.pallas_call(
        _ffn_kernel,
        grid=(rows // tile,),
        in_specs=[pl.BlockSpec((tile, D_MODEL), lambda i: (i, 0)),
                  pl.BlockSpec((tile, PLE_DIM), lambda i: (i, 0)),
                  _resident((1, D_MODEL)), _resident((D_MODEL, D_FF)), _resident((D_MODEL, D_FF)),
                  _resident((D_FF, D_MODEL)), _resident((1, D_MODEL)), _resident((PLE_DIM, D_MODEL)),
                  _resident((D_MODEL, D_MODEL)), _resident((1, D_MODEL))],
        out_specs=pl.BlockSpec((tile, D_MODEL), lambda i: (i, 0)),
        out_shape=jax.ShapeDtypeStruct((rows, D_MODEL), F32),
        compiler_params=pltpu.CompilerParams(
            dimension_semantics=("arbitrary",), vmem_limit_bytes=VMEM_LIMIT_BYTES),
        name="ffn_ple",
    )(h, p, *ffn_weights)


def kernel(x_prompt, x_sample, state_conv, state_gla, p_prompt, p_sample, w_norm_mix_pre, w_in, w_conv, w_a_out, w_gk, b_gk, w_gla_norm, w_b_out, w_o, w_norm_mix_post, w_norm_ffn_pre, w_ffn_gate, w_ffn_up, w_ffn_down, w_norm_ffn_post, w_ple_proj, w_ple_gate, w_norm_ple_post):
    depth = w_in.shape[0]
    batch, seq, _ = x_prompt.shape
    n_dec = x_sample.shape[0]
    assert x_sample.shape[1] == 1, "the sample group carries one new token per sequence"
    assert seq % SEQ_TILE == 0 and (batch * seq) % FFN_TILE == 0 and n_dec % SAMPLE_BLOCK == 0

    hp = x_prompt
    hs = x_sample.reshape(n_dec, D_MODEL)
    conv_p, gla_p, conv_s, gla_s = [], [], [], []
    for i in range(depth):
        row = lambda w: w[i].reshape(1, -1)
        w_in_i = w_in[i]
        w_main = jnp.concatenate(
            [w_in_i[:, :GKLR_START], w_in_i[:, GKLR_START + GATE_RANK:]], axis=1).astype(BF16)
        w_gklr = jnp.pad(w_in_i[:, GKLR_START:GKLR_START + GATE_RANK],
                         ((0, 0), (0, LANES - GATE_RANK))).astype(BF16)
        w_gk_pad = jnp.pad(w_gk[i], ((0, LANES - GATE_RANK), (0, 0))).astype(BF16)
        mixer_weights = (row(w_norm_mix_pre), w_main, w_gklr, w_gk_pad, row(b_gk), w_conv[i],
                         w_a_out[i].astype(BF16), row(w_gla_norm), w_b_out[i].astype(BF16),
                         w_o[i].astype(BF16), row(w_norm_mix_post))
        ffn_weights = (row(w_norm_ffn_pre), w_ffn_gate[i].astype(BF16), w_ffn_up[i].astype(BF16),
                       w_ffn_down[i].astype(BF16), row(w_norm_ffn_post), w_ple_proj[i].astype(BF16),
                       w_ple_gate[i].astype(BF16), row(w_norm_ple_post))

        hp_mid, cbp, sp = _prompt_mixer(hp, mixer_weights)
        hp = _ffn(hp_mid.reshape(batch * seq, D_MODEL), p_prompt[i].reshape(batch * seq, PLE_DIM),
                  ffn_weights, FFN_TILE).reshape(batch, seq, D_MODEL)

        hs_mid, cbs, ss = _sample_mixer(hs, state_conv[i].reshape(n_dec, 2 * CONV_DIM), state_gla[i],
                                        mixer_weights)
        hs = _ffn(hs_mid, p_sample[i].reshape(n_dec, PLE_DIM), ffn_weights, n_dec)

        conv_p.append(cbp[0]); gla_p.append(sp[0])
        conv_s.append(cbs.reshape(n_dec, 2, CONV_DIM)); gla_s.append(ss)
    return (hp, hs.reshape(n_dec, 1, D_MODEL), jnp.stack(conv_p), jnp.stack(gla_p),
            jnp.stack(conv_s), jnp.stack(gla_s))
```

```python
import functools

import numpy as np
import jax
import jax.numpy as jnp
from jax import lax
from jax.experimental import pallas as pl
from jax.experimental.pallas import tpu as pltpu

D_MODEL = 1024
CONV_DIM = D_MODEL
GLA_HEADS = 4
GLA_DK = 128
GLA_DV = 256
QK_DIM = GLA_HEADS * GLA_DK
V_DIM = GLA_HEADS * GLA_DV
GATE_RANK = 16
GATE_NORMALIZER = 16.0
D_FF = 2816
PLE_DIM = 256
EPS = 1e-6
LOG2_E = 1.4426950408889634

LANES = 128
SUBLANES = 8
VMEM_LIMIT_BYTES = 56 * 1024 * 1024

OFF_B, OFF_C, OFF_X = 0, 1024, 2048
OFF_Q, OFF_K, OFF_V, OFF_G = 3072, 3584, 4096, 5120
OFF_GA, OFF_GB = 6144, 7168
N_MAIN = 8192
GKLR_START = 6144

SEQ_TILE = 256
N_LEVELS = 8
FFN_TILE = 512
FFN_CHUNK = 1408
SAMPLE_BLOCK = 8

F32 = jnp.float32
BF16 = jnp.bfloat16


def _rms(x, w):
    return x * lax.rsqrt(jnp.mean(x * x, axis=-1, keepdims=True) + EPS) * w


def _sigmoid(x):
    return 1.0 / (1.0 + jnp.exp(-x))


def _log_sigmoid(x):
    return jnp.minimum(x, 0.0) - jnp.log(1.0 + jnp.exp(-jnp.abs(x)))


def _dot(a, b):
    return jnp.dot(a, b, preferred_element_type=F32)


def _dot_nt(a, b):
    return lax.dot_general(a, b, (((1,), (1,)), ((), ())), preferred_element_type=F32)


def _column_broadcast(row):
    return jnp.broadcast_to(row, (LANES, LANES)).T


def _projections(hn, w_in_ref, w_gklr_ref, w_gk_ref, b_gk_ref, log_base_scale):
    def proj(off, width):
        return _dot(hn, w_in_ref[:, off:off + width])

    gk_lr = _dot(hn, w_gklr_ref[...])
    gk = _dot(gk_lr.astype(BF16), w_gk_ref[...]) + b_gk_ref[...]
    logw = _log_sigmoid(gk) * (log_base_scale / GATE_NORMALIZER)
    return proj, logw


def _mix_out(x, o_heads, g, gate_a, gate_b, y_a, w_gn, w_b_ref, w_o_ref, wn_post):
    normed = []
    for h in range(GLA_HEADS):
        o = o_heads[h]
        gh = g[:, h * GLA_DV:(h + 1) * GLA_DV]
        o = o * lax.rsqrt(jnp.mean(o * o, axis=-1, keepdims=True) + EPS) * w_gn
        normed.append((o * (gh * _sigmoid(gh))).astype(BF16))
    y_b = _dot(jnp.concatenate(normed, axis=1), w_b_ref[...])
    merged = _sigmoid(gate_a) * y_a + _sigmoid(gate_b) * y_b
    mix = _dot(merged.astype(BF16), w_o_ref[...])
    return x + _rms(mix, wn_post)


def _prompt_mixer_kernel(x_ref, lev_ref, tri_ref, wn_pre_ref, w_in_ref, w_gklr_ref, w_gk_ref,
                         b_gk_ref, w_conv_ref, w_a_ref, w_gn_ref, w_b_ref, w_o_ref, wn_post_ref,
                         h_ref, conv_ref, state_ref, cum_ref):
    t = SEQ_TILE

    @pl.when(pl.program_id(1) == 0)
    def _():
        conv_ref[...] = jnp.zeros_like(conv_ref)
        state_ref[...] = jnp.zeros_like(state_ref)

    x = x_ref[0]
    hn = _rms(x, wn_pre_ref[...]).astype(BF16)
    proj, logw2 = _projections(hn, w_in_ref, w_gklr_ref, w_gk_ref, b_gk_ref, LOG2_E)

    u = proj(OFF_C, CONV_DIM) * proj(OFF_X, CONV_DIM)
    prev2 = conv_ref[0, 0, 0:1, :]
    prev1 = conv_ref[0, 0, 1:2, :]
    row = lax.broadcasted_iota(jnp.int32, (t, CONV_DIM), 0)
    u1 = jnp.where(row == 0, prev1, pltpu.roll(u, 1, 0))
    u2 = jnp.where(row == 0, prev2, jnp.where(row == 1, prev1, pltpu.roll(u, 2, 0)))
    wc = w_conv_ref[...]
    y_conv = wc[0:1] * u2 + wc[1:2] * u1 + wc[2:3] * u
    conv_ref[0, 0] = u[t - 2:t]
    y_a = _dot((proj(OFF_B, CONV_DIM) * y_conv).astype(BF16), w_a_ref[...])

    hi = logw2.astype(BF16)
    lo = (logw2 - hi.astype(F32)).astype(BF16)
    cum = _dot(tri_ref[...], hi) + _dot(tri_ref[...], lo)
    cum_ref[...] = cum
    cum_last = cum[t - 1:t]

    q = proj(OFF_Q, QK_DIM) * (GLA_DK ** -0.5)
    k = proj(OFF_K, QK_DIM)
    v = proj(OFF_V, V_DIM).astype(BF16)

    half = t // 2
    lev = lev_ref[...]
    rowq = lax.broadcasted_iota(jnp.int32, (t, QK_DIM), 0)

    def level_log2_factor(i):
        m = 1 << i
        if i == 0:
            return jnp.where((rowq & 1) == 1, logw2, 0.0)
        if i == 1:
            up = pltpu.roll(logw2, t - 1, 0)
            dn = pltpu.roll(logw2, 1, 0)
            r4 = rowq & 3
            return jnp.where(r4 == 0, up, jnp.where(r4 == 1, 0.0, jnp.where(r4 == 2, logw2, logw2 + dn)))
        pieces = []
        for blk in range(t // (2 * m)):
            r = blk * 2 * m + m - 1
            pieces.append(jnp.broadcast_to(cum_ref[r:r + 1, :], (2 * m, QK_DIM)))
        return -jnp.abs(cum - jnp.concatenate(pieces, axis=0))

    def head(a, h):
        return a[:, h * GLA_DK:(h + 1) * GLA_DK]

    q_lv = [q.astype(BF16)]
    k_lv = [k.astype(BF16)]
    for i in range(N_LEVELS - 1):
        e = jnp.exp2(level_log2_factor(i))
        q_lv.append((q * e).astype(BF16))
        k_lv.append((k * e).astype(BF16))
    cum_mid = cum_ref[half - 1:half, :]
    q_top = (q[half:] * jnp.exp2(cum[half:] - cum_mid)).astype(BF16)
    k_top = (k[:half] * jnp.exp2(cum_mid - cum[:half])).astype(BF16)

    q_in = (q * jnp.exp2(cum)).astype(BF16)
    k_out = k * jnp.exp2(cum_last - cum)
    a_last = jnp.exp2(cum_last)

    o_heads = []
    for h in range(GLA_HEADS):
        vh = v[:, h * GLA_DV:(h + 1) * GLA_DV]
        diag = [0.0, 0.0]
        for i in range(N_LEVELS):
            s_i = _dot_nt(head(q_lv[i], h), head(k_lv[i], h))
            diag = [jnp.where(lev == i - 1, s_i[r0:r0 + half, r0:r0 + half], diag[j])
                    for j, r0 in enumerate((0, half))]
        diag = [d.astype(BF16) for d in diag]
        p_low = jnp.concatenate([_dot_nt(head(q_top, h), head(k_top, h)).astype(BF16), diag[1]], axis=1)
        s_old = state_ref[0, 0, h]
        o = jnp.concatenate([_dot(diag[0], vh[:half]), _dot(p_low, vh)], axis=0)
        o_heads.append(o + _dot(head(q_in, h), s_old.astype(BF16)))
        a_col = _column_broadcast(head(a_last, h))
        a_col = jnp.concatenate([a_col, a_col], axis=1)
        state_ref[0, 0, h] = a_col * s_old + _dot(head(k_out, h).T.astype(BF16), vh)

    h_ref[0] = _mix_out(x, o_heads, proj(OFF_G, V_DIM), proj(OFF_GA, D_MODEL), proj(OFF_GB, D_MODEL),
                        y_a, w_gn_ref[...], w_b_ref, w_o_ref, wn_post_ref[...])


def _sample_mixer_kernel(x_ref, cbuf_ref, st_ref, wn_pre_ref, w_in_ref, w_gklr_ref, w_gk_ref,
                         b_gk_ref, w_conv_ref, w_a_ref, w_gn_ref, w_b_ref, w_o_ref, wn_post_ref,
                         h_ref, conv_ref, st_out_ref,
                         q_scr, k_scr, a_scr, v_scr, o_scr, ya_scr, g_scr, ga_scr, gb_scr):
    step = pl.program_id(0)

    @pl.when(step == 0)
    def _():
        x = x_ref[...]
        hn = _rms(x, wn_pre_ref[...]).astype(BF16)
        proj, logw = _projections(hn, w_in_ref, w_gklr_ref, w_gk_ref, b_gk_ref, 1.0)
        u = proj(OFF_C, CONV_DIM) * proj(OFF_X, CONV_DIM)
        buf0 = cbuf_ref[:, 0:CONV_DIM]
        buf1 = cbuf_ref[:, CONV_DIM:2 * CONV_DIM]
        wc = w_conv_ref[...]
        y_conv = wc[0:1] * buf0 + wc[1:2] * buf1 + wc[2:3] * u
        conv_ref[:, 0:CONV_DIM] = buf1
        conv_ref[:, CONV_DIM:2 * CONV_DIM] = u
        ya_scr[...] = _dot((proj(OFF_B, CONV_DIM) * y_conv).astype(BF16), w_a_ref[...])
        blocked = lambda a: a.reshape(a.shape[0] // SAMPLE_BLOCK, SAMPLE_BLOCK, a.shape[1])
        q_scr[...] = blocked(proj(OFF_Q, QK_DIM) * (GLA_DK ** -0.5))
        k_scr[...] = blocked(proj(OFF_K, QK_DIM))
        a_scr[...] = blocked(jnp.exp(logw))
        v_scr[...] = blocked(proj(OFF_V, V_DIM))
        g_scr[...] = proj(OFF_G, V_DIM)
        ga_scr[...] = proj(OFF_GA, D_MODEL)
        gb_scr[...] = proj(OFF_GB, D_MODEL)

    for n in range(SAMPLE_BLOCK):
        q_row = q_scr[step, n:n + 1, :]
        k_row = k_scr[step, n:n + 1, :]
        a_row = a_scr[step, n:n + 1, :]
        v_row = v_scr[step, n:n + 1, :]
        for h in range(GLA_HEADS):
            sl = slice(h * GLA_DK, (h + 1) * GLA_DK)
            a_col = _column_broadcast(a_row[:, sl])
            k_col = _column_broadcast(k_row[:, sl])
            q_col = _column_broadcast(q_row[:, sl])
            for half in range(GLA_DV // LANES):
                c0 = half * LANES
                s_old = st_ref[n, h, :, c0:c0 + LANES]
                vv = v_row[:, h * GLA_DV + c0:h * GLA_DV + c0 + LANES]
                s_new = a_col * s_old + k_col * vv
                st_out_ref[n, h, :, c0:c0 + LANES] = s_new
                o_scr[step, n:n + 1, h * GLA_DV + c0:h * GLA_DV + c0 + LANES] = jnp.sum(
                    q_col * s_new, axis=0, keepdims=True)

    @pl.when(step == pl.num_programs(0) - 1)
    def _():
        o = o_scr[...].reshape(x_ref.shape[0], V_DIM)
        o_heads = [o[:, h * GLA_DV:(h + 1) * GLA_DV] for h in range(GLA_HEADS)]
        h_ref[...] = _mix_out(x_ref[...], o_heads, g_scr[...], ga_scr[...], gb_scr[...], ya_scr[...],
                              w_gn_ref[...], w_b_ref, w_o_ref, wn_post_ref[...])


def _ffn_kernel(h_ref, p_ref, wn_pre_ref, w_gate_ref, w_up_ref, w_down_ref, wn_post_ref,
                w_pp_ref, w_pg_ref, wn_ple_ref, out_ref):
    h = h_ref[...]
    f = _rms(h, wn_pre_ref[...]).astype(BF16)
    acc = None
    for c in range(D_FF // FFN_CHUNK):
        sl = slice(c * FFN_CHUNK, (c + 1) * FFN_CHUNK)
        gate = _dot(f, w_gate_ref[:, sl])
        up = _dot(f, w_up_ref[:, sl])
        act = (gate * _sigmoid(gate) * up).astype(BF16)
        part = _dot(act, w_down_ref[sl, :])
        acc = part if acc is None else acc + part
    h = h + _rms(acc, wn_post_ref[...])
    e = _dot(p_ref[...].astype(BF16), w_pp_ref[...]) * _sigmoid(_dot(h.astype(BF16), w_pg_ref[...]))
    out_ref[...] = h + _rms(e, wn_ple_ref[...])


def _resident(shape):
    return pl.BlockSpec(shape, lambda *_: (0,) * len(shape), pipeline_mode=pl.Buffered(1))


def _level_map(t):
    idx = np.arange(t)
    xor = idx[:, None] ^ idx[None, :]
    lev = np.floor(np.log2(np.maximum(xor, 1))).astype(np.int32)
    lev = np.where(idx[:, None] > idx[None, :], lev, -2)
    lev = np.where(idx[:, None] == idx[None, :], -1, lev)
    return jnp.asarray(lev, dtype=jnp.int32)


def _mixer_weight_specs():
    return [
        _resident((1, D_MODEL)),
        _resident((D_MODEL, N_MAIN)),
        _resident((D_MODEL, LANES)),
        _resident((LANES, QK_DIM)),
        _resident((1, QK_DIM)),
        _resident((3, CONV_DIM)),
        _resident((CONV_DIM, D_MODEL)),
        _resident((1, GLA_DV)),
        _resident((V_DIM, D_MODEL)),
        _resident((D_MODEL, D_MODEL)),
        _resident((1, D_MODEL)),
    ]


def _prompt_mixer(x, mixer_weights):
    b, s, _ = x.shape
    t = SEQ_TILE
    tri = jnp.asarray(np.tril(np.ones((t, t), np.float32)), dtype=BF16)
    return pl.pallas_call(
        _prompt_mixer_kernel,
        grid=(b, s // t),
        in_specs=[pl.BlockSpec((1, t, D_MODEL), lambda i, j: (i, j, 0)),
                  _resident((t // 2, t // 2)), _resident((t, t))] + _mixer_weight_specs(),
        out_specs=[pl.BlockSpec((1, t, D_MODEL), lambda i, j: (i, j, 0)),
                   pl.BlockSpec((1, 1, 2, CONV_DIM), lambda i, j: (0, i, 0, 0)),
                   pl.BlockSpec((1, 1, GLA_HEADS, GLA_DK, GLA_DV), lambda i, j: (0, i, 0, 0, 0))],
        out_shape=[jax.ShapeDtypeStruct((b, s, D_MODEL), F32),
                   jax.ShapeDtypeStruct((1, b, 2, CONV_DIM), F32),
                   jax.ShapeDtypeStruct((1, b, GLA_HEADS, GLA_DK, GLA_DV), F32)],
        scratch_shapes=[pltpu.VMEM((t, QK_DIM), F32)],
        compiler_params=pltpu.CompilerParams(
            dimension_semantics=("arbitrary", "arbitrary"), vmem_limit_bytes=VMEM_LIMIT_BYTES),
        name="prompt_mixer",
    )(x, _level_map(t // 2), tri, *mixer_weights)


def _sample_mixer(x, conv_buf, state, mixer_weights):
    n = x.shape[0]
    nb = SAMPLE_BLOCK
    state_spec = pl.BlockSpec((nb, GLA_HEADS, GLA_DK, GLA_DV), lambda i: (i, 0, 0, 0))
    whole = lambda shape: pl.BlockSpec(shape, lambda i: (0,) * len(shape))
    return pl.pallas_call(
        _sample_mixer_kernel,
        grid=(n // nb,),
        in_specs=[_resident((n, D_MODEL)), _resident((n, 2 * CONV_DIM)), state_spec] + _mixer_weight_specs(),
        out_specs=[whole((n, D_MODEL)), whole((n, 2 * CONV_DIM)), state_spec],
        out_shape=[jax.ShapeDtypeStruct((n, D_MODEL), F32),
                   jax.ShapeDtypeStruct((n, 2 * CONV_DIM), F32),
                   jax.ShapeDtypeStruct((n, GLA_HEADS, GLA_DK, GLA_DV), F32)],
        scratch_shapes=[pltpu.VMEM((n // nb, nb, QK_DIM), F32), pltpu.VMEM((n // nb, nb, QK_DIM), F32),
                        pltpu.VMEM((n // nb, nb, QK_DIM), F32), pltpu.VMEM((n // nb, nb, V_DIM), F32),
                        pltpu.VMEM((n // nb, nb, V_DIM), F32), pltpu.VMEM((n, D_MODEL), F32),
                        pltpu.VMEM((n, V_DIM), F32), pltpu.VMEM((n, D_MODEL), F32),
                        pltpu.VMEM((n, D_MODEL), F32)],
        compiler_params=pltpu.CompilerParams(
            dimension_semantics=("arbitrary",), vmem_limit_bytes=VMEM_LIMIT_BYTES),
        name="sample_mixer",
    )(x, conv_buf, state, *mixer_weights)


def _ffn(h, p, ffn_weights, tile):
    rows = h.shape[0]
    return pl.pallas_call(
        _ffn_kernel,
        grid=(rows // tile,),
        in_specs=[pl.BlockSpec((tile, D_MODEL), lambda i: (i, 0)),
                  pl.BlockSpec((tile, PLE_DIM), lambda i: (i, 0)),
                  _resident((1, D_MODEL)), _resident((D_MODEL, D_FF)), _resident((D_MODEL, D_FF)),
                  _resident((D_FF, D_MODEL)), _resident((1, D_MODEL)), _resident((PLE_DIM, D_MODEL)),
                  _resident((D_MODEL, D_MODEL)), _resident((1, D_MODEL))],
        out_specs=pl.BlockSpec((tile, D_MODEL), lambda i: (i, 0)),
        out_shape=jax.ShapeDtypeStruct((rows, D_MODEL), F32),
        compiler_params=pltpu.CompilerParams(
            dimension_semantics=("arbitrary",), vmem_limit_bytes=VMEM_LIMIT_BYTES),
        name="ffn_ple",
    )(h, p, *ffn_weights)


def kernel(x_prompt, x_sample, state_conv, state_gla, p_prompt, p_sample, w_norm_mix_pre, w_in, w_conv, w_a_out, w_gk, b_gk, w_gla_norm, w_b_out, w_o, w_norm_mix_post, w_norm_ffn_pre, w_ffn_gate, w_ffn_up, w_ffn_down, w_norm_ffn_post, w_ple_proj, w_ple_gate, w_norm_ple_post):
    depth = w_in.shape[0]
    batch, seq, _ = x_prompt.shape
    n_dec = x_sample.shape[0]
    assert x_sample.shape[1] == 1, "the sample group carries one new token per sequence"
    assert seq % SEQ_TILE == 0 and (batch * seq) % FFN_TILE == 0 and n_dec % SAMPLE_BLOCK == 0

    hp = x_prompt
    hs = x_sample.reshape(n_dec, D_MODEL)
    conv_p, gla_p, conv_s, gla_s = [], [], [], []
    for i in range(depth):
        row = lambda w: w[i].reshape(1, -1)
        w_in_i = w_in[i]
        w_main = jnp.concatenate(
            [w_in_i[:, :GKLR_START].astype(BF16), w_in_i[:, GKLR_START + GATE_RANK:].astype(BF16)], axis=1)
        w_gklr = jnp.pad(w_in_i[:, GKLR_START:GKLR_START + GATE_RANK].astype(BF16),
                         ((0, 0), (0, LANES - GATE_RANK)))
        w_gk_pad = jnp.pad(w_gk[i], ((0, LANES - GATE_RANK), (0, 0))).astype(BF16)
        mixer_weights = (row(w_norm_mix_pre), w_main, w_gklr, w_gk_pad, row(b_gk), w_conv[i],
                         w_a_out[i].astype(BF16), row(w_gla_norm), w_b_out[i].astype(BF16),
                         w_o[i].astype(BF16), row(w_norm_mix_post))
        ffn_weights = (row(w_norm_ffn_pre), w_ffn_gate[i].astype(BF16), w_ffn_up[i].astype(BF16),
                       w_ffn_down[i].astype(BF16), row(w_norm_ffn_post), w_ple_proj[i].astype(BF16),
                       w_ple_gate[i].astype(BF16), row(w_norm_ple_post))

        hp_mid, cbp, sp = _prompt_mixer(hp, mixer_weights)
        hp = _ffn(hp_mid.reshape(batch * seq, D_MODEL), p_prompt[i].reshape(batch * seq, PLE_DIM),
                  ffn_weights, FFN_TILE).reshape(batch, seq, D_MODEL)

        hs_mid, cbs, ss = _sample_mixer(hs, state_conv[i].reshape(n_dec, 2 * CONV_DIM), state_gla[i],
                                        mixer_weights)
        hs = _ffn(hs_mid, p_sample[i].reshape(n_dec, PLE_DIM), ffn_weights, n_dec)

        conv_p.append(cbp[0]); gla_p.append(sp[0])
        conv_s.append(cbs.reshape(n_dec, 2, CONV_DIM)); gla_s.append(ss)
    return (hp, hs.reshape(n_dec, 1, D_MODEL), jnp.stack(conv_p), jnp.stack(gla_p),
            jnp.stack(conv_s), jnp.stack(gla_s))
```

```python
import functools

import numpy as np
import jax
import jax.numpy as jnp
from jax import lax
from jax.experimental import pallas as pl
from jax.experimental.pallas import tpu as pltpu

D_MODEL = 1024
CONV_DIM = D_MODEL
GLA_HEADS = 4
GLA_DK = 128
GLA_DV = 256
QK_DIM = GLA_HEADS * GLA_DK
V_DIM = GLA_HEADS * GLA_DV
GATE_RANK = 16
GATE_NORMALIZER = 16.0
D_FF = 2816
PLE_DIM = 256
EPS = 1e-6
LOG2_E = 1.4426950408889634

LANES = 128
SUBLANES = 8
VMEM_LIMIT_BYTES = 56 * 1024 * 1024

OFF_B, OFF_C, OFF_X = 0, 1024, 2048
OFF_Q, OFF_K, OFF_V, OFF_G = 3072, 3584, 4096, 5120
OFF_GA, OFF_GB = 6144, 7168
N_HEAD = 6144
N_TAIL = 2048
GKLR_START = 6144

GLA_CHUNK = 256
N_LEVELS = 8
SEQ_TILE = 512
FFN_TILE = 512
FFN_CHUNK = 1408
SAMPLE_BLOCK = 8

F32 = jnp.float32
BF16 = jnp.bfloat16


def _rms(x, w):
    return x * lax.rsqrt(jnp.mean(x * x, axis=-1, keepdims=True) + EPS) * w


def _sigmoid(x):
    return 1.0 / (1.0 + jnp.exp(-x))


def _log_sigmoid(x):
    return jnp.minimum(x, 0.0) - jnp.log(1.0 + jnp.exp(-jnp.abs(x)))


def _dot(a, b):
    return jnp.dot(a, b, preferred_element_type=F32)


def _dot_nt(a, b):
    return lax.dot_general(a, b, (((1,), (1,)), ((), ())), preferred_element_type=F32)


def _column_broadcast(row):
    return jnp.broadcast_to(row, (LANES, LANES)).T


def _projections(hn, w_head_ref, w_tail_ref, w_gklr_ref, w_gk_ref, b_gk_ref, log_base_scale):
    def proj(off, width):
        if off < N_HEAD:
            return _dot(hn, w_head_ref[:, off:off + width])
        return _dot(hn, w_tail_ref[:, off - N_HEAD:off - N_HEAD + width])

    gk_lr = _dot(hn, w_gklr_ref[...])
    gk = _dot(gk_lr.astype(BF16), w_gk_ref[...]) + b_gk_ref[...]
    logw = _log_sigmoid(gk) * (log_base_scale / GATE_NORMALIZER)
    return proj, logw


def _mix_out(x, o_heads, g, gate_a, gate_b, y_a, w_gn, w_b_ref, w_o_ref, wn_post):
    normed = []
    for h in range(GLA_HEADS):
        o = o_heads[h]
        gh = g[:, h * GLA_DV:(h + 1) * GLA_DV]
        o = o * lax.rsqrt(jnp.mean(o * o, axis=-1, keepdims=True) + EPS) * w_gn
        normed.append((o * (gh * _sigmoid(gh))).astype(BF16))
    y_b = _dot(jnp.concatenate(normed, axis=1), w_b_ref[...])
    merged = _sigmoid(gate_a) * y_a + _sigmoid(gate_b) * y_b
    mix = _dot(merged.astype(BF16), w_o_ref[...])
    return x + _rms(mix, wn_post)


def _prompt_mixer_kernel(x_ref, lev_ref, tri_ref, wn_pre_ref, w_head_ref, w_tail_ref, w_gklr_ref,
                         w_gk_ref, b_gk_ref, w_conv_ref, w_a_ref, w_gn_ref, w_b_ref, w_o_ref,
                         wn_post_ref, h_ref, conv_ref, state_ref, cum_ref):
    @pl.when(pl.program_id(1) == 0)
    def _():
        conv_ref[...] = jnp.zeros_like(conv_ref)
        state_ref[...] = jnp.zeros_like(state_ref)

    for c in range(SEQ_TILE // GLA_CHUNK):
        rows = slice(c * GLA_CHUNK, (c + 1) * GLA_CHUNK)
        h_ref[0, rows] = _prompt_mixer_chunk(
            x_ref[0, rows], cum_ref.at[c], lev_ref, tri_ref, wn_pre_ref, w_head_ref, w_tail_ref,
            w_gklr_ref, w_gk_ref, b_gk_ref, w_conv_ref, w_a_ref, w_gn_ref, w_b_ref, w_o_ref,
            wn_post_ref, conv_ref, state_ref)


def _prompt_mixer_chunk(x, cum_ref, lev_ref, tri_ref, wn_pre_ref, w_head_ref, w_tail_ref, w_gklr_ref,
                        w_gk_ref, b_gk_ref, w_conv_ref, w_a_ref, w_gn_ref, w_b_ref, w_o_ref,
                        wn_post_ref, conv_ref, state_ref):
    t = GLA_CHUNK
    hn = _rms(x, wn_pre_ref[...]).astype(BF16)
    proj, logw2 = _projections(hn, w_head_ref, w_tail_ref, w_gklr_ref, w_gk_ref, b_gk_ref, LOG2_E)

    u = proj(OFF_C, CONV_DIM) * proj(OFF_X, CONV_DIM)
    prev2 = conv_ref[0, 0, 0:1, :]
    prev1 = conv_ref[0, 0, 1:2, :]
    row = lax.broadcasted_iota(jnp.int32, (t, CONV_DIM), 0)
    u1 = jnp.where(row == 0, prev1, pltpu.roll(u, 1, 0))
    u2 = jnp.where(row == 0, prev2, jnp.where(row == 1, prev1, pltpu.roll(u, 2, 0)))
    wc = w_conv_ref[...]
    y_conv = wc[0:1] * u2 + wc[1:2] * u1 + wc[2:3] * u
    conv_ref[0, 0] = u[t - 2:t]
    y_a = _dot((proj(OFF_B, CONV_DIM) * y_conv).astype(BF16), w_a_ref[...])

    hi = logw2.astype(BF16)
    lo = (logw2 - hi.astype(F32)).astype(BF16)
    cum = _dot(tri_ref[...], hi) + _dot(tri_ref[...], lo)
    cum_ref[...] = cum
    cum_last = cum[t - 1:t]

    q = proj(OFF_Q, QK_DIM) * (GLA_DK ** -0.5)
    k = proj(OFF_K, QK_DIM)
    v = proj(OFF_V, V_DIM).astype(BF16)

    half = t // 2
    lev = lev_ref[...]
    rowq = lax.broadcasted_iota(jnp.int32, (t, QK_DIM), 0)

    def level_log2_factor(i):
        m = 1 << i
        if i == 0:
            return jnp.where((rowq & 1) == 1, logw2, 0.0)
        if i == 1:
            up = pltpu.roll(logw2, t - 1, 0)
            dn = pltpu.roll(logw2, 1, 0)
            r4 = rowq & 3
            return jnp.where(r4 == 0, up, jnp.where(r4 == 1, 0.0, jnp.where(r4 == 2, logw2, logw2 + dn)))
        pieces = []
        for blk in range(t // (2 * m)):
            r = blk * 2 * m + m - 1
            pieces.append(jnp.broadcast_to(cum_ref[r:r + 1, :], (2 * m, QK_DIM)))
        return -jnp.abs(cum - jnp.concatenate(pieces, axis=0))

    def head(a, h):
        return a[:, h * GLA_DK:(h + 1) * GLA_DK]

    q_lv = [q.astype(BF16)]
    k_lv = [k.astype(BF16)]
    for i in range(N_LEVELS - 1):
        e = jnp.exp2(level_log2_factor(i))
        q_lv.append((q * e).astype(BF16))
        k_lv.append((k * e).astype(BF16))
    cum_mid = cum_ref[half - 1:half, :]
    q_top = (q[half:] * jnp.exp2(cum[half:] - cum_mid)).astype(BF16)
    k_top = (k[:half] * jnp.exp2(cum_mid - cum[:half])).astype(BF16)

    q_in = (q * jnp.exp2(cum)).astype(BF16)
    k_out = k * jnp.exp2(cum_last - cum)
    a_last = jnp.exp2(cum_last)

    o_heads = []
    for h in range(GLA_HEADS):
        vh = v[:, h * GLA_DV:(h + 1) * GLA_DV]
        diag = [0.0, 0.0]
        for i in range(N_LEVELS):
            s_i = _dot_nt(head(q_lv[i], h), head(k_lv[i], h))
            diag = [jnp.where(lev == i - 1, s_i[r0:r0 + half, r0:r0 + half], diag[j])
                    for j, r0 in enumerate((0, half))]
        diag = [d.astype(BF16) for d in diag]
        p_low = jnp.concatenate([_dot_nt(head(q_top, h), head(k_top, h)).astype(BF16), diag[1]], axis=1)
        s_old = state_ref[0, 0, h]
        o = jnp.concatenate([_dot(diag[0], vh[:half]), _dot(p_low, vh)], axis=0)
        o_heads.append(o + _dot(head(q_in, h), s_old.astype(BF16)))
        a_col = _column_broadcast(head(a_last, h))
        a_col = jnp.concatenate([a_col, a_col], axis=1)
        state_ref[0, 0, h] = a_col * s_old + _dot(head(k_out, h).T.astype(BF16), vh)

    return _mix_out(x, o_heads, proj(OFF_G, V_DIM), proj(OFF_GA, D_MODEL), proj(OFF_GB, D_MODEL),
                    y_a, w_gn_ref[...], w_b_ref, w_o_ref, wn_post_ref[...])


def _sample_mixer_kernel(x_ref, cbuf_ref, st_ref, wn_pre_ref, w_head_ref, w_tail_ref, w_gklr_ref,
                         w_gk_ref, b_gk_ref, w_conv_ref, w_a_ref, w_gn_ref, w_b_ref, w_o_ref, wn_post_ref,
                         h_ref, conv_ref, st_out_ref,
                         q_scr, k_scr, a_scr, v_scr, o_scr, ya_scr, g_scr, ga_scr, gb_scr):
    step = pl.program_id(0)

    @pl.when(step == 0)
    def _():
        x = x_ref[...]
        hn = _rms(x, wn_pre_ref[...]).astype(BF16)
        proj, logw = _projections(hn, w_head_ref, w_tail_ref, w_gklr_ref, w_gk_ref, b_gk_ref, 1.0)
        u = proj(OFF_C, CONV_DIM) * proj(OFF_X, CONV_DIM)
        buf0 = cbuf_ref[:, 0:CONV_DIM]
        buf1 = cbuf_ref[:, CONV_DIM:2 * CONV_DIM]
        wc = w_conv_ref[...]
        y_conv = wc[0:1] * buf0 + wc[1:2] * buf1 + wc[2:3] * u
        conv_ref[:, 0:CONV_DIM] = buf1
        conv_ref[:, CONV_DIM:2 * CONV_DIM] = u
        ya_scr[...] = _dot((proj(OFF_B, CONV_DIM) * y_conv).astype(BF16), w_a_ref[...])
        blocked = lambda a: a.reshape(a.shape[0] // SAMPLE_BLOCK, SAMPLE_BLOCK, a.shape[1])
        q_scr[...] = blocked(proj(OFF_Q, QK_DIM) * (GLA_DK ** -0.5))
        k_scr[...] = blocked(proj(OFF_K, QK_DIM))
        a_scr[...] = blocked(jnp.exp(logw))
        v_scr[...] = blocked(proj(OFF_V, V_DIM))
        g_scr[...] = proj(OFF_G, V_DIM)
        ga_scr[...] = proj(OFF_GA, D_MODEL)
        gb_scr[...] = proj(OFF_GB, D_MODEL)

    for n in range(SAMPLE_BLOCK):
        q_row = q_scr[step, n:n + 1, :]
        k_row = k_scr[step, n:n + 1, :]
        a_row = a_scr[step, n:n + 1, :]
        v_row = v_scr[step, n:n + 1, :]
        for h in range(GLA_HEADS):
            sl = slice(h * GLA_DK, (h + 1) * GLA_DK)
            a_col = _column_broadcast(a_row[:, sl])
            k_col = _column_broadcast(k_row[:, sl])
            q_col = _column_broadcast(q_row[:, sl])
            for half in range(GLA_DV // LANES):
                c0 = half * LANES
                s_old = st_ref[n, h, :, c0:c0 + LANES]
                vv = v_row[:, h * GLA_DV + c0:h * GLA_DV + c0 + LANES]
                s_new = a_col * s_old + k_col * vv
                st_out_ref[n, h, :, c0:c0 + LANES] = s_new
                o_scr[step, n:n + 1, h * GLA_DV + c0:h * GLA_DV + c0 + LANES] = jnp.sum(
                    q_col * s_new, axis=0, keepdims=True)

    @pl.when(step == pl.num_programs(0) - 1)
    def _():
        o = o_scr[...].reshape(x_ref.shape[0], V_DIM)
        o_heads = [o[:, h * GLA_DV:(h + 1) * GLA_DV] for h in range(GLA_HEADS)]
        h_ref[...] = _mix_out(x_ref[...], o_heads, g_scr[...], ga_scr[...], gb_scr[...], ya_scr[...],
                              w_gn_ref[...], w_b_ref, w_o_ref, wn_post_ref[...])


def _ffn_kernel(h_ref, p_ref, wn_pre_ref, w_gate_ref, w_up_ref, w_down_ref, wn_post_ref,
                w_pp_ref, w_pg_ref, wn_ple_ref, out_ref):
    h = h_ref[...]
    f = _rms(h, wn_pre_ref[...]).astype(BF16)
    acc = None
    for c in range(D_FF // FFN_CHUNK):
        sl = slice(c * FFN_CHUNK, (c + 1) * FFN_CHUNK)
        gate = _dot(f, w_gate_ref[:, sl])
        up = _dot(f, w_up_ref[:, sl])
        act = (gate * _sigmoid(gate) * up).astype(BF16)
        part = _dot(act, w_down_ref[sl, :])
        acc = part if acc is None else acc + part
    h = h + _rms(acc, wn_post_ref[...])
    e = _dot(p_ref[...].astype(BF16), w_pp_ref[...]) * _sigmoid(_dot(h.astype(BF16), w_pg_ref[...]))
    out_ref[...] = h + _rms(e, wn_ple_ref[...])


def _resident(shape):
    return pl.BlockSpec(shape, lambda *_: (0,) * len(shape), pipeline_mode=pl.Buffered(1))


def _level_map(t):
    idx = np.arange(t)
    xor = idx[:, None] ^ idx[None, :]
    lev = np.floor(np.log2(np.maximum(xor, 1))).astype(np.int32)
    lev = np.where(idx[:, None] > idx[None, :], lev, -2)
    lev = np.where(idx[:, None] == idx[None, :], -1, lev)
    return jnp.asarray(lev, dtype=jnp.int32)


def _mixer_weight_specs():
    return [
        _resident((1, D_MODEL)),
        _resident((D_MODEL, N_HEAD)),
        _resident((D_MODEL, N_TAIL)),
        _resident((D_MODEL, LANES)),
        _resident((LANES, QK_DIM)),
        _resident((1, QK_DIM)),
        _resident((3, CONV_DIM)),
        _resident((CONV_DIM, D_MODEL)),
        _resident((1, GLA_DV)),
        _resident((V_DIM, D_MODEL)),
        _resident((D_MODEL, D_MODEL)),
        _resident((1, D_MODEL)),
    ]


def _prompt_mixer(x, mixer_weights):
    b, s, _ = x.shape
    t = SEQ_TILE
    c = GLA_CHUNK
    tri = jnp.asarray(np.tril(np.ones((c, c), np.float32)), dtype=BF16)
    return pl.pallas_call(
        _prompt_mixer_kernel,
        grid=(b, s // t),
        in_specs=[pl.BlockSpec((1, t, D_MODEL), lambda i, j: (i, j, 0)),
                  _resident((c // 2, c // 2)), _resident((c, c))] + _mixer_weight_specs(),
        out_specs=[pl.BlockSpec((1, t, D_MODEL), lambda i, j: (i, j, 0)),
                   pl.BlockSpec((1, 1, 2, CONV_DIM), lambda i, j: (0, i, 0, 0)),
                   pl.BlockSpec((1, 1, GLA_HEADS, GLA_DK, GLA_DV), lambda i, j: (0, i, 0, 0, 0))],
        out_shape=[jax.ShapeDtypeStruct((b, s, D_MODEL), F32),
                   jax.ShapeDtypeStruct((1, b, 2, CONV_DIM), F32),
                   jax.ShapeDtypeStruct((1, b, GLA_HEADS, GLA_DK, GLA_DV), F32)],
        scratch_shapes=[pltpu.VMEM((t // c, c, QK_DIM), F32)],
        compiler_params=pltpu.CompilerParams(
            dimension_semantics=("arbitrary", "arbitrary"), vmem_limit_bytes=VMEM_LIMIT_BYTES),
        name="prompt_mixer",
    )(x, _level_map(c // 2), tri, *mixer_weights)


def _sample_mixer(x, conv_buf, state, mixer_weights):
    n = x.shape[0]
    nb = SAMPLE_BLOCK
    state_spec = pl.BlockSpec((nb, GLA_HEADS, GLA_DK, GLA_DV), lambda i: (i, 0, 0, 0))
    whole = lambda shape: pl.BlockSpec(shape, lambda i: (0,) * len(shape))
    return pl.pallas_call(
        _sample_mixer_kernel,
        grid=(n // nb,),
        in_specs=[_resident((n, D_MODEL)), _resident((n, 2 * CONV_DIM)), state_spec] + _mixer_weight_specs(),
        out_specs=[whole((n, D_MODEL)), whole((n, 2 * CONV_DIM)), state_spec],
        out_shape=[jax.ShapeDtypeStruct((n, D_MODEL), F32),
                   jax.ShapeDtypeStruct((n, 2 * CONV_DIM), F32),
                   jax.ShapeDtypeStruct((n, GLA_HEADS, GLA_DK, GLA_DV), F32)],
        scratch_shapes=[pltpu.VMEM((n // nb, nb, QK_DIM), F32), pltpu.VMEM((n // nb, nb, QK_DIM), F32),
                        pltpu.VMEM((n // nb, nb, QK_DIM), F32), pltpu.VMEM((n // nb, nb, V_DIM), F32),
                        pltpu.VMEM((n // nb, nb, V_DIM), F32), pltpu.VMEM((n, D_MODEL), F32),
                        pltpu.VMEM((n, V_DIM), F32), pltpu.VMEM((n, D_MODEL), F32),
                        pltpu.VMEM((n, D_MODEL), F32)],
        compiler_params=pltpu.CompilerParams(
            dimension_semantics=("arbitrary",), vmem_limit_bytes=VMEM_LIMIT_BYTES),
        name="sample_mixer",
    )(x, conv_buf, state, *mixer_weights)


def _ffn(h, p, ffn_weights, tile):
    rows = h.shape[0]
    return pl.pallas_call(
        _ffn_kernel,
        grid=(rows // tile,),
        in_specs=[pl.BlockSpec((tile, D_MODEL), lambda i: (i, 0)),
                  pl.BlockSpec((tile, PLE_DIM), lambda i: (i, 0)),
                  _resident((1, D_MODEL)), _resident((D_MODEL, D_FF)), _resident((D_MODEL, D_FF)),
                  _resident((D_FF, D_MODEL)), _resident((1, D_MODEL)), _resident((PLE_DIM, D_MODEL)),
                  _resident((D_MODEL, D_MODEL)), _resident((1, D_MODEL))],
        out_specs=pl.BlockSpec((tile, D_MODEL), lambda i: (i, 0)),
        out_shape=jax.ShapeDtypeStruct((rows, D_MODEL), F32),
        compiler_params=pltpu.CompilerParams(
            dimension_semantics=("arbitrary",), vmem_limit_bytes=VMEM_LIMIT_BYTES),
        name="ffn_ple",
    )(h, p, *ffn_weights)


def kernel(x_prompt, x_sample, state_conv, state_gla, p_prompt, p_sample, w_norm_mix_pre, w_in, w_conv, w_a_out, w_gk, b_gk, w_gla_norm, w_b_out, w_o, w_norm_mix_post, w_norm_ffn_pre, w_ffn_gate, w_ffn_up, w_ffn_down, w_norm_ffn_post, w_ple_proj, w_ple_gate, w_norm_ple_post):
    depth = w_in.shape[0]
    batch, seq, _ = x_prompt.shape
    n_dec = x_sample.shape[0]
    assert x_sample.shape[1] == 1, "the sample group carries one new token per sequence"
    assert seq % SEQ_TILE == 0 and (batch * seq) % FFN_TILE == 0 and n_dec % SAMPLE_BLOCK == 0

    hp = x_prompt
    hs = x_sample.reshape(n_dec, D_MODEL)
    conv_p, gla_p, conv_s, gla_s = [], [], [], []
    for i in range(depth):
        row = lambda w: w[i].reshape(1, -1)
        w_in_i = w_in[i]
        w_head = w_in_i[:, :GKLR_START].astype(BF16)
        w_tail = w_in_i[:, GKLR_START + GATE_RANK:].astype(BF16)
        w_gklr = jnp.pad(w_in_i[:, GKLR_START:GKLR_START + GATE_RANK].astype(BF16),
                         ((0, 0), (0, LANES - GATE_RANK)))
        w_gk_pad = jnp.pad(w_gk[i], ((0, LANES - GATE_RANK), (0, 0))).astype(BF16)
        mixer_weights = (row(w_norm_mix_pre), w_head, w_tail, w_gklr, w_gk_pad, row(b_gk), w_conv[i],
                         w_a_out[i].astype(BF16), row(w_gla_norm), w_b_out[i].astype(BF16),
                         w_o[i].astype(BF16), row(w_norm_mix_post))
        ffn_weights = (row(w_norm_ffn_pre), w_ffn_gate[i].astype(BF16), w_ffn_up[i].astype(BF16),
                       w_ffn_down[i].astype(BF16), row(w_norm_ffn_post), w_ple_proj[i].astype(BF16),
                       w_ple_gate[i].astype(BF16), row(w_norm_ple_post))

        hp_mid, cbp, sp = _prompt_mixer(hp, mixer_weights)
        hp = _ffn(hp_mid.reshape(batch * seq, D_MODEL), p_prompt[i].reshape(batch * seq, PLE_DIM),
                  ffn_weights, FFN_TILE).reshape(batch, seq, D_MODEL)

        hs_mid, cbs, ss = _sample_mixer(hs, state_conv[i].reshape(n_dec, 2 * CONV_DIM), state_gla[i],
                                        mixer_weights)
        hs = _ffn(hs_mid, p_sample[i].reshape(n_dec, PLE_DIM), ffn_weights, n_dec)

        conv_p.append(cbp[0]); gla_p.append(sp[0])
        conv_s.append(cbs.reshape(n_dec, 2, CONV_DIM)); gla_s.append(ss)
    return (hp, hs.reshape(n_dec, 1, D_MODEL), jnp.stack(conv_p), jnp.stack(gla_p),
            jnp.stack(conv_s), jnp.stack(gla_s))
```

```python
import functools

import numpy as np
import jax
import jax.numpy as jnp
from jax import lax
from jax.experimental import pallas as pl
from jax.experimental.pallas import tpu as pltpu

D_MODEL = 1024
CONV_DIM = D_MODEL
GLA_HEADS = 4
GLA_DK = 128
GLA_DV = 256
QK_DIM = GLA_HEADS * GLA_DK
V_DIM = GLA_HEADS * GLA_DV
GATE_RANK = 16
GATE_NORMALIZER = 16.0
D_FF = 2816
PLE_DIM = 256
EPS = 1e-6
LOG2_E = 1.4426950408889634

LANES = 128
SUBLANES = 8
VMEM_LIMIT_BYTES = 56 * 1024 * 1024

OFF_B, OFF_C, OFF_X = 0, 1024, 2048
OFF_Q, OFF_K, OFF_V, OFF_G = 3072, 3584, 4096, 5120
OFF_GA, OFF_GB = 6144, 7168
N_HEAD = 6144
N_TAIL = 2048
GKLR_START = 6144

GLA_CHUNK = 256
N_LEVELS = 8
SEQ_TILE = 512
FFN_TILE = 512
FFN_CHUNK = 1408
SAMPLE_BLOCK = 8

F32 = jnp.float32
BF16 = jnp.bfloat16


def _rms(x, w):
    return x * lax.rsqrt(jnp.mean(x * x, axis=-1, keepdims=True) + EPS) * w


def _sigmoid(x):
    return 1.0 / (1.0 + jnp.exp2(x * -LOG2_E))


def _log2_sigmoid(x, scale):
    return (jnp.minimum(x, 0.0) - jnp.log(1.0 + jnp.exp2(jnp.abs(x) * -LOG2_E))) * (scale * LOG2_E)


def _dot(a, b):
    return jnp.dot(a, b, preferred_element_type=F32)


def _dot_nt(a, b):
    return lax.dot_general(a, b, (((1,), (1,)), ((), ())), preferred_element_type=F32)


def _column_broadcast(row):
    return jnp.broadcast_to(row, (LANES, LANES)).T


def _projections(hn, w_head_ref, w_tail_ref, w_gklr_ref, w_gk_ref, b_gk_ref):
    def proj(off, width):
        if off < N_HEAD:
            return _dot(hn, w_head_ref[:, off:off + width])
        return _dot(hn, w_tail_ref[:, off - N_HEAD:off - N_HEAD + width])

    gk_lr = _dot(hn, w_gklr_ref[...])
    gk = _dot(gk_lr.astype(BF16), w_gk_ref[...]) + b_gk_ref[...]
    return proj, _log2_sigmoid(gk, 1.0 / GATE_NORMALIZER)


def _mix_out(x, o_heads, g, gate_a, gate_b, y_a, w_gn, w_b_ref, w_o_ref, wn_post):
    normed = []
    for h in range(GLA_HEADS):
        o = o_heads[h]
        gh = g[:, h * GLA_DV:(h + 1) * GLA_DV]
        o = o * lax.rsqrt(jnp.mean(o * o, axis=-1, keepdims=True) + EPS) * w_gn
        normed.append((o * (gh * _sigmoid(gh))).astype(BF16))
    y_b = _dot(jnp.concatenate(normed, axis=1), w_b_ref[...])
    merged = _sigmoid(gate_a) * y_a + _sigmoid(gate_b) * y_b
    mix = _dot(merged.astype(BF16), w_o_ref[...])
    return x + _rms(mix, wn_post)


def _prompt_mixer_kernel(x_ref, lev_ref, tri_ref, wn_pre_ref, w_head_ref, w_tail_ref, w_gklr_ref,
                         w_gk_ref, b_gk_ref, w_conv_ref, w_a_ref, w_gn_ref, w_b_ref, w_o_ref,
                         wn_post_ref, h_ref, conv_ref, state_ref, dec_ref):
    @pl.when(pl.program_id(1) == 0)
    def _():
        conv_ref[...] = jnp.zeros_like(conv_ref)
        state_ref[...] = jnp.zeros_like(state_ref)

    for c in range(SEQ_TILE // GLA_CHUNK):
        rows = slice(c * GLA_CHUNK, (c + 1) * GLA_CHUNK)
        h_ref[0, rows] = _prompt_mixer_chunk(
            x_ref[0, rows], dec_ref.at[c], lev_ref, tri_ref, wn_pre_ref, w_head_ref, w_tail_ref,
            w_gklr_ref, w_gk_ref, b_gk_ref, w_conv_ref, w_a_ref, w_gn_ref, w_b_ref, w_o_ref,
            wn_post_ref, conv_ref, state_ref)


def _prompt_mixer_chunk(x, dec_ref, lev_ref, tri_ref, wn_pre_ref, w_head_ref, w_tail_ref, w_gklr_ref,
                        w_gk_ref, b_gk_ref, w_conv_ref, w_a_ref, w_gn_ref, w_b_ref, w_o_ref,
                        wn_post_ref, conv_ref, state_ref):
    t = GLA_CHUNK
    hn = _rms(x, wn_pre_ref[...]).astype(BF16)
    proj, logw2 = _projections(hn, w_head_ref, w_tail_ref, w_gklr_ref, w_gk_ref, b_gk_ref)

    u = proj(OFF_C, CONV_DIM) * proj(OFF_X, CONV_DIM)
    prev2 = conv_ref[0, 0, 0:1, :]
    prev1 = conv_ref[0, 0, 1:2, :]
    row = lax.broadcasted_iota(jnp.int32, (t, CONV_DIM), 0)
    u1 = jnp.where(row == 0, prev1, pltpu.roll(u, 1, 0))
    u2 = jnp.where(row == 0, prev2, jnp.where(row == 1, prev1, pltpu.roll(u, 2, 0)))
    wc = w_conv_ref[...]
    y_conv = wc[0:1] * u2 + wc[1:2] * u1 + wc[2:3] * u
    conv_ref[0, 0] = u[t - 2:t]
    y_a = _dot((proj(OFF_B, CONV_DIM) * y_conv).astype(BF16), w_a_ref[...])

    dec_ref[0] = logw2
    logw2 = dec_ref[0]
    hi = logw2.astype(BF16)
    lo = (logw2 - hi.astype(F32)).astype(BF16)
    cum = _dot(tri_ref[...], hi) + _dot(tri_ref[...], lo)
    dec_ref[1] = cum
    cum_last = cum[t - 1:t]

    q = proj(OFF_Q, QK_DIM) * (GLA_DK ** -0.5)
    k = proj(OFF_K, QK_DIM)
    v = proj(OFF_V, V_DIM).astype(BF16)

    half = t // 2
    lev = lev_ref[...]
    rowq = lax.broadcasted_iota(jnp.int32, (t, QK_DIM), 0)

    def level_log2_factor(i):
        m = 1 << i
        if i == 0:
            return jnp.where((rowq & 1) == 1, logw2, 0.0)
        if i == 1:
            up = pltpu.roll(logw2, t - 1, 0)
            dn = pltpu.roll(logw2, 1, 0)
            r4 = rowq & 3
            return jnp.where(r4 == 0, up, jnp.where(r4 == 1, 0.0, jnp.where(r4 == 2, logw2, logw2 + dn)))
        pieces = []
        for blk in range(t // (2 * m)):
            r = blk * 2 * m + m - 1
            pieces.append(jnp.broadcast_to(dec_ref[1, r:r + 1, :], (2 * m, QK_DIM)))
        return -jnp.abs(cum - jnp.concatenate(pieces, axis=0))

    def head(a, h):
        return a[:, h * GLA_DK:(h + 1) * GLA_DK]

    q_lv = [q.astype(BF16)]
    kt_lv = [k.T.astype(BF16)]
    for i in range(N_LEVELS - 1):
        e = jnp.exp2(level_log2_factor(i))
        q_lv.append((q * e).astype(BF16))
        kt_lv.append((k * e).T.astype(BF16))
    cum_mid = dec_ref[1, half - 1:half, :]
    q_top = (q[half:] * jnp.exp2(cum[half:] - cum_mid)).astype(BF16)
    kt_top = (k[:half] * jnp.exp2(cum_mid - cum[:half])).T.astype(BF16)

    def head_t(a, h):
        return a[h * GLA_DK:(h + 1) * GLA_DK, :]

    q_in = (q * jnp.exp2(cum)).astype(BF16)
    k_out = k * jnp.exp2(cum_last - cum)
    a_last = jnp.exp2(cum_last)

    o_heads = []
    for h in range(GLA_HEADS):
        vh = v[:, h * GLA_DV:(h + 1) * GLA_DV]
        diag = [0.0, 0.0]
        for i in range(N_LEVELS):
            s_i = _dot(head(q_lv[i], h), head_t(kt_lv[i], h))
            diag = [jnp.where(lev == i - 1, s_i[r0:r0 + half, r0:r0 + half], diag[j])
                    for j, r0 in enumerate((0, half))]
        diag = [d.astype(BF16) for d in diag]
        p_low = jnp.concatenate([_dot(head(q_top, h), head_t(kt_top, h)).astype(BF16), diag[1]], axis=1)
        s_old = state_ref[0, 0, h]
        o = jnp.concatenate([_dot(diag[0], vh[:half]), _dot(p_low, vh)], axis=0)
        o_heads.append(o + _dot(head(q_in, h), s_old.astype(BF16)))
        a_col = _column_broadcast(head(a_last, h))
        a_col = jnp.concatenate([a_col, a_col], axis=1)
        state_ref[0, 0, h] = a_col * s_old + _dot(head(k_out, h).T.astype(BF16), vh)

    return _mix_out(x, o_heads, proj(OFF_G, V_DIM), proj(OFF_GA, D_MODEL), proj(OFF_GB, D_MODEL),
                    y_a, w_gn_ref[...], w_b_ref, w_o_ref, wn_post_ref[...])


def _sample_mixer_kernel(x_ref, cbuf_ref, st_ref, wn_pre_ref, w_head_ref, w_tail_ref, w_gklr_ref,
                         w_gk_ref, b_gk_ref, w_conv_ref, w_a_ref, w_gn_ref, w_b_ref, w_o_ref, wn_post_ref,
                         h_ref, conv_ref, st_out_ref,
                         q_scr, k_scr, a_scr, v_scr, o_scr, ya_scr, g_scr, ga_scr, gb_scr):
    step = pl.program_id(0)

    @pl.when(step == 0)
    def _():
        x = x_ref[...]
        hn = _rms(x, wn_pre_ref[...]).astype(BF16)
        proj, logw2 = _projections(hn, w_head_ref, w_tail_ref, w_gklr_ref, w_gk_ref, b_gk_ref)
        u = proj(OFF_C, CONV_DIM) * proj(OFF_X, CONV_DIM)
        buf0 = cbuf_ref[:, 0:CONV_DIM]
        buf1 = cbuf_ref[:, CONV_DIM:2 * CONV_DIM]
        wc = w_conv_ref[...]
        y_conv = wc[0:1] * buf0 + wc[1:2] * buf1 + wc[2:3] * u
        conv_ref[:, 0:CONV_DIM] = buf1
        conv_ref[:, CONV_DIM:2 * CONV_DIM] = u
        ya_scr[...] = _dot((proj(OFF_B, CONV_DIM) * y_conv).astype(BF16), w_a_ref[...])
        blocked = lambda a: a.reshape(a.shape[0] // SAMPLE_BLOCK, SAMPLE_BLOCK, a.shape[1])
        q_scr[...] = blocked(proj(OFF_Q, QK_DIM) * (GLA_DK ** -0.5))
        k_scr[...] = blocked(proj(OFF_K, QK_DIM))
        a_scr[...] = blocked(jnp.exp2(logw2))
        v_scr[...] = blocked(proj(OFF_V, V_DIM))
        g_scr[...] = proj(OFF_G, V_DIM)
        ga_scr[...] = proj(OFF_GA, D_MODEL)
        gb_scr[...] = proj(OFF_GB, D_MODEL)

    for n in range(SAMPLE_BLOCK):
        q_row = q_scr[step, n:n + 1, :]
        k_row = k_scr[step, n:n + 1, :]
        a_row = a_scr[step, n:n + 1, :]
        v_row = v_scr[step, n:n + 1, :]
        for h in range(GLA_HEADS):
            sl = slice(h * GLA_DK, (h + 1) * GLA_DK)
            a_col = _column_broadcast(a_row[:, sl])
            k_col = _column_broadcast(k_row[:, sl])
            q_col = _column_broadcast(q_row[:, sl])
            for half in range(GLA_DV // LANES):
                c0 = half * LANES
                s_old = st_ref[n, h, :, c0:c0 + LANES]
                vv = v_row[:, h * GLA_DV + c0:h * GLA_DV + c0 + LANES]
                s_new = a_col * s_old + k_col * vv
                st_out_ref[n, h, :, c0:c0 + LANES] = s_new
                o_scr[step, n:n + 1, h * GLA_DV + c0:h * GLA_DV + c0 + LANES] = jnp.sum(
                    q_col * s_new, axis=0, keepdims=True)

    @pl.when(step == pl.num_programs(0) - 1)
    def _():
        o = o_scr[...].reshape(x_ref.shape[0], V_DIM)
        o_heads = [o[:, h * GLA_DV:(h + 1) * GLA_DV] for h in range(GLA_HEADS)]
        h_ref[...] = _mix_out(x_ref[...], o_heads, g_scr[...], ga_scr[...], gb_scr[...], ya_scr[...],
                              w_gn_ref[...], w_b_ref, w_o_ref, wn_post_ref[...])


def _ffn_kernel(h_ref, p_ref, wn_pre_ref, w_gate_ref, w_up_ref, w_down_ref, wn_post_ref,
                w_pp_ref, w_pg_ref, wn_ple_ref, out_ref):
    h = h_ref[...]
    f = _rms(h, wn_pre_ref[...]).astype(BF16)
    acc = None
    for c in range(D_FF // FFN_CHUNK):
        sl = slice(c * FFN_CHUNK, (c + 1) * FFN_CHUNK)
        gate = _dot(f, w_gate_ref[:, sl])
        up = _dot(f, w_up_ref[:, sl])
        act = (gate * _sigmoid(gate) * up).astype(BF16)
        part = _dot(act, w_down_ref[sl, :])
        acc = part if acc is None else acc + part
    h = h + _rms(acc, wn_post_ref[...])
    e = _dot(p_ref[...].astype(BF16), w_pp_ref[...]) * _sigmoid(_dot(h.astype(BF16), w_pg_ref[...]))
    out_ref[...] = h + _rms(e, wn_ple_ref[...])


def _resident(shape):
    return pl.BlockSpec(shape, lambda *_: (0,) * len(shape), pipeline_mode=pl.Buffered(1))


def _level_map(t):
    idx = np.arange(t)
    xor = idx[:, None] ^ idx[None, :]
    lev = np.floor(np.log2(np.maximum(xor, 1))).astype(np.int32)
    lev = np.where(idx[:, None] > idx[None, :], lev, -2)
    lev = np.where(idx[:, None] == idx[None, :], -1, lev)
    return jnp.asarray(lev, dtype=jnp.int32)


def _mixer_weight_specs():
    return [
        _resident((1, D_MODEL)),
        _resident((D_MODEL, N_HEAD)),
        _resident((D_MODEL, N_TAIL)),
        _resident((D_MODEL, LANES)),
        _resident((LANES, QK_DIM)),
        _resident((1, QK_DIM)),
        _resident((3, CONV_DIM)),
        _resident((CONV_DIM, D_MODEL)),
        _resident((1, GLA_DV)),
        _resident((V_DIM, D_MODEL)),
        _resident((D_MODEL, D_MODEL)),
        _resident((1, D_MODEL)),
    ]


def _prompt_mixer(x, mixer_weights):
    b, s, _ = x.shape
    t = SEQ_TILE
    c = GLA_CHUNK
    tri = jnp.asarray(np.tril(np.ones((c, c), np.float32)), dtype=BF16)
    return pl.pallas_call(
        _prompt_mixer_kernel,
        grid=(b, s // t),
        in_specs=[pl.BlockSpec((1, t, D_MODEL), lambda i, j: (i, j, 0)),
                  _resident((c // 2, c // 2)), _resident((c, c))] + _mixer_weight_specs(),
        out_specs=[pl.BlockSpec((1, t, D_MODEL), lambda i, j: (i, j, 0)),
                   pl.BlockSpec((1, 1, 2, CONV_DIM), lambda i, j: (0, i, 0, 0)),
                   pl.BlockSpec((1, 1, GLA_HEADS, GLA_DK, GLA_DV), lambda i, j: (0, i, 0, 0, 0))],
        out_shape=[jax.ShapeDtypeStruct((b, s, D_MODEL), F32),
                   jax.ShapeDtypeStruct((1, b, 2, CONV_DIM), F32),
                   jax.ShapeDtypeStruct((1, b, GLA_HEADS, GLA_DK, GLA_DV), F32)],
        scratch_shapes=[pltpu.VMEM((t // c, 2, c, QK_DIM), F32)],
        compiler_params=pltpu.CompilerParams(
            dimension_semantics=("arbitrary", "arbitrary"), vmem_limit_bytes=VMEM_LIMIT_BYTES),
        name="prompt_mixer",
    )(x, _level_map(c // 2), tri, *mixer_weights)


def _sample_mixer(x, conv_buf, state, mixer_weights):
    n = x.shape[0]
    nb = SAMPLE_BLOCK
    state_spec = pl.BlockSpec((nb, GLA_HEADS, GLA_DK, GLA_DV), lambda i: (i, 0, 0, 0))
    whole = lambda shape: pl.BlockSpec(shape, lambda i: (0,) * len(shape))
    return pl.pallas_call(
        _sample_mixer_kernel,
        grid=(n // nb,),
        in_specs=[_resident((n, D_MODEL)), _resident((n, 2 * CONV_DIM)), state_spec] + _mixer_weight_specs(),
        out_specs=[whole((n, D_MODEL)), whole((n, 2 * CONV_DIM)), state_spec],
        out_shape=[jax.ShapeDtypeStruct((n, D_MODEL), F32),
                   jax.ShapeDtypeStruct((n, 2 * CONV_DIM), F32),
                   jax.ShapeDtypeStruct((n, GLA_HEADS, GLA_DK, GLA_DV), F32)],
        scratch_shapes=[pltpu.VMEM((n // nb, nb, QK_DIM), F32), pltpu.VMEM((n // nb, nb, QK_DIM), F32),
                        pltpu.VMEM((n // nb, nb, QK_DIM), F32), pltpu.VMEM((n // nb, nb, V_DIM), F32),
                        pltpu.VMEM((n // nb, nb, V_DIM), F32), pltpu.VMEM((n, D_MODEL), F32),
                        pltpu.VMEM((n, V_DIM), F32), pltpu.VMEM((n, D_MODEL), F32),
                        pltpu.VMEM((n, D_MODEL), F32)],
        compiler_params=pltpu.CompilerParams(
            dimension_semantics=("arbitrary",), vmem_limit_bytes=VMEM_LIMIT_BYTES),
        name="sample_mixer",
    )(x, conv_buf, state, *mixer_weights)


def _ffn(h, p, ffn_weights, tile):
    rows = h.shape[0]
    return pl.pallas_call(
        _ffn_kernel,
        grid=(rows // tile,),
        in_specs=[pl.BlockSpec((tile, D_MODEL), lambda i: (i, 0)),
                  pl.BlockSpec((tile, PLE_DIM), lambda i: (i, 0)),
                  _resident((1, D_MODEL)), _resident((D_MODEL, D_FF)), _resident((D_MODEL, D_FF)),
                  _resident((D_FF, D_MODEL)), _resident((1, D_MODEL)), _resident((PLE_DIM, D_MODEL)),
                  _resident((D_MODEL, D_MODEL)), _resident((1, D_MODEL))],
        out_specs=pl.BlockSpec((tile, D_MODEL), lambda i: (i, 0)),
        out_shape=jax.ShapeDtypeStruct((rows, D_MODEL), F32),
        compiler_params=pltpu.CompilerParams(
            dimension_semantics=("arbitrary",), vmem_limit_bytes=VMEM_LIMIT_BYTES),
        name="ffn_ple",
    )(h, p, *ffn_weights)


def kernel(x_prompt, x_sample, state_conv, state_gla, p_prompt, p_sample, w_norm_mix_pre, w_in, w_conv, w_a_out, w_gk, b_gk, w_gla_norm, w_b_out, w_o, w_norm_mix_post, w_norm_ffn_pre, w_ffn_gate, w_ffn_up, w_ffn_down, w_norm_ffn_post, w_ple_proj, w_ple_gate, w_norm_ple_post):
    depth = w_in.shape[0]
    batch, seq, _ = x_prompt.shape
    n_dec = x_sample.shape[0]
    assert x_sample.shape[1] == 1, "the sample group carries one new token per sequence"
    assert seq % SEQ_TILE == 0 and (batch * seq) % FFN_TILE == 0 and n_dec % SAMPLE_BLOCK == 0

    hp = x_prompt
    hs = x_sample.reshape(n_dec, D_MODEL)
    conv_p, gla_p, conv_s, gla_s = [], [], [], []
    for i in range(depth):
        row = lambda w: w[i].reshape(1, -1)
        w_in_i = w_in[i]
        w_head = w_in_i[:, :GKLR_START].astype(BF16)
        w_tail = w_in_i[:, GKLR_START + GATE_RANK:].astype(BF16)
        w_gklr = jnp.pad(w_in_i[:, GKLR_START:GKLR_START + GATE_RANK].astype(BF16),
                         ((0, 0), (0, LANES - GATE_RANK)))
        w_gk_pad = jnp.pad(w_gk[i], ((0, LANES - GATE_RANK), (0, 0))).astype(BF16)
        mixer_weights = (row(w_norm_mix_pre), w_head, w_tail, w_gklr, w_gk_pad, row(b_gk), w_conv[i],
                         w_a_out[i].astype(BF16), row(w_gla_norm), w_b_out[i].astype(BF16),
                         w_o[i].astype(BF16), row(w_norm_mix_post))
        ffn_weights = (row(w_norm_ffn_pre), w_ffn_gate[i].astype(BF16), w_ffn_up[i].astype(BF16),
                       w_ffn_down[i].astype(BF16), row(w_norm_ffn_post), w_ple_proj[i].astype(BF16),
                       w_ple_gate[i].astype(BF16), row(w_norm_ple_post))

        hp_mid, cbp, sp = _prompt_mixer(hp, mixer_weights)
        hp = _ffn(hp_mid.reshape(batch * seq, D_MODEL), p_prompt[i].reshape(batch * seq, PLE_DIM),
                  ffn_weights, FFN_TILE).reshape(batch, seq, D_MODEL)

        hs_mid, cbs, ss = _sample_mixer(hs, state_conv[i].reshape(n_dec, 2 * CONV_DIM), state_gla[i],
                                        mixer_weights)
        hs = _ffn(hs_mid, p_sample[i].reshape(n_dec, PLE_DIM), ffn_weights, n_dec)

        conv_p.append(cbp[0]); gla_p.append(sp[0])
        conv_s.append(cbs.reshape(n_dec, 2, CONV_DIM)); gla_s.append(ss)
    return (hp, hs.reshape(n_dec, 1, D_MODEL), jnp.stack(conv_p), jnp.stack(gla_p),
            jnp.stack(conv_s), jnp.stack(gla_s))
```

```python
import functools

import numpy as np
import jax
import jax.numpy as jnp
from jax import lax
from jax.experimental import pallas as pl
from jax.experimental.pallas import tpu as pltpu

D_MODEL = 1024
CONV_DIM = D_MODEL
GLA_HEADS = 4
GLA_DK = 128
GLA_DV = 256
QK_DIM = GLA_HEADS * GLA_DK
V_DIM = GLA_HEADS * GLA_DV
GATE_RANK = 16
GATE_NORMALIZER = 16.0
D_FF = 2816
PLE_DIM = 256
EPS = 1e-6
LOG2_E = 1.4426950408889634

LANES = 128
SUBLANES = 8
VMEM_LIMIT_BYTES = 56 * 1024 * 1024

OFF_B, OFF_C, OFF_X = 0, 1024, 2048
OFF_Q, OFF_K, OFF_V, OFF_G = 3072, 3584, 4096, 5120
OFF_GA, OFF_GB = 6144, 7168
N_HEAD = 6144
N_TAIL = 2048
GKLR_START = 6144

GLA_CHUNK = 256
N_LEVELS = 8
SEQ_TILE = 512
FFN_TILE = 512
FFN_CHUNK = 1408
SAMPLE_BLOCK = 8
CAST_STEPS = 8

F32 = jnp.float32
BF16 = jnp.bfloat16


def _rms(x, w):
    return x * lax.rsqrt(jnp.mean(x * x, axis=-1, keepdims=True) + EPS) * w


def _sigmoid(x):
    return 1.0 / (1.0 + jnp.exp2(x * -LOG2_E))


def _log2_sigmoid(x, scale):
    return (jnp.minimum(x, 0.0) - jnp.log(1.0 + jnp.exp2(jnp.abs(x) * -LOG2_E))) * (scale * LOG2_E)


def _dot(a, b):
    return jnp.dot(a, b, preferred_element_type=F32)


def _dot_nt(a, b):
    return lax.dot_general(a, b, (((1,), (1,)), ((), ())), preferred_element_type=F32)


def _column_broadcast(row):
    return jnp.broadcast_to(row, (LANES, LANES)).T


def _projections(hn, w_head_ref, w_tail_ref, w_gklr_ref, w_gk_ref, b_gk_ref):
    def proj(off, width):
        if off < N_HEAD:
            return _dot(hn, w_head_ref[:, off:off + width])
        return _dot(hn, w_tail_ref[:, off - N_HEAD:off - N_HEAD + width])

    gk_lr = _dot(hn, w_gklr_ref[...])
    gk = _dot(gk_lr.astype(BF16), w_gk_ref[...]) + b_gk_ref[...]
    return proj, _log2_sigmoid(gk, 1.0 / GATE_NORMALIZER)


def _mix_out(x, o_heads, g, gate_a, gate_b, y_a, w_gn, w_b_ref, w_o_ref, wn_post):
    normed = []
    for h in range(GLA_HEADS):
        o = o_heads[h]
        gh = g[:, h * GLA_DV:(h + 1) * GLA_DV]
        o = o * lax.rsqrt(jnp.mean(o * o, axis=-1, keepdims=True) + EPS) * w_gn
        normed.append((o * (gh * _sigmoid(gh))).astype(BF16))
    y_b = _dot(jnp.concatenate(normed, axis=1), w_b_ref[...])
    merged = _sigmoid(gate_a) * y_a + _sigmoid(gate_b) * y_b
    mix = _dot(merged.astype(BF16), w_o_ref[...])
    return x + _rms(mix, wn_post)


def _prompt_mixer_kernel(x_ref, lev_ref, tri_ref, wn_pre_ref, w_head_ref, w_tail_ref, w_gklr_ref,
                         w_gk_ref, b_gk_ref, w_conv_ref, w_a_ref, w_gn_ref, w_b_ref, w_o_ref,
                         wn_post_ref, h_ref, conv_ref, state_ref, dec_ref):
    @pl.when(pl.program_id(1) == 0)
    def _():
        conv_ref[...] = jnp.zeros_like(conv_ref)
        state_ref[...] = jnp.zeros_like(state_ref)

    for c in range(SEQ_TILE // GLA_CHUNK):
        rows = slice(c * GLA_CHUNK, (c + 1) * GLA_CHUNK)
        h_ref[0, rows] = _prompt_mixer_chunk(
            x_ref[0, rows], dec_ref.at[c], lev_ref, tri_ref, wn_pre_ref, w_head_ref, w_tail_ref,
            w_gklr_ref, w_gk_ref, b_gk_ref, w_conv_ref, w_a_ref, w_gn_ref, w_b_ref, w_o_ref,
            wn_post_ref, conv_ref, state_ref)


def _prompt_mixer_chunk(x, dec_ref, lev_ref, tri_ref, wn_pre_ref, w_head_ref, w_tail_ref, w_gklr_ref,
                        w_gk_ref, b_gk_ref, w_conv_ref, w_a_ref, w_gn_ref, w_b_ref, w_o_ref,
                        wn_post_ref, conv_ref, state_ref):
    t = GLA_CHUNK
    hn = _rms(x, wn_pre_ref[...]).astype(BF16)
    proj, logw2 = _projections(hn, w_head_ref, w_tail_ref, w_gklr_ref, w_gk_ref, b_gk_ref)

    u = proj(OFF_C, CONV_DIM) * proj(OFF_X, CONV_DIM)
    prev2 = conv_ref[0, 0, 0:1, :]
    prev1 = conv_ref[0, 0, 1:2, :]
    row = lax.broadcasted_iota(jnp.int32, (t, CONV_DIM), 0)
    u1 = jnp.where(row == 0, prev1, pltpu.roll(u, 1, 0))
    u2 = jnp.where(row == 0, prev2, jnp.where(row == 1, prev1, pltpu.roll(u, 2, 0)))
    wc = w_conv_ref[...]
    y_conv = wc[0:1] * u2 + wc[1:2] * u1 + wc[2:3] * u
    conv_ref[0, 0] = u[t - 2:t]
    y_a = _dot((proj(OFF_B, CONV_DIM) * y_conv).astype(BF16), w_a_ref[...])

    dec_ref[0] = logw2
    logw2 = dec_ref[0]
    hi = logw2.astype(BF16)
    lo = (logw2 - hi.astype(F32)).astype(BF16)
    cum = _dot(tri_ref[...], hi) + _dot(tri_ref[...], lo)
    dec_ref[1] = cum
    cum_last = cum[t - 1:t]

    q = proj(OFF_Q, QK_DIM) * (GLA_DK ** -0.5)
    k = proj(OFF_K, QK_DIM)
    v = proj(OFF_V, V_DIM).astype(BF16)

    half = t // 2
    lev = lev_ref[...]
    rowq = lax.broadcasted_iota(jnp.int32, (t, QK_DIM), 0)

    def level_log2_factor(i):
        m = 1 << i
        if i == 0:
            return jnp.where((rowq & 1) == 1, logw2, 0.0)
        if i == 1:
            up = pltpu.roll(logw2, t - 1, 0)
            dn = pltpu.roll(logw2, 1, 0)
            r4 = rowq & 3
            return jnp.where(r4 == 0, up, jnp.where(r4 == 1, 0.0, jnp.where(r4 == 2, logw2, logw2 + dn)))
        pieces = []
        for blk in range(t // (2 * m)):
            r = blk * 2 * m + m - 1
            pieces.append(jnp.broadcast_to(dec_ref[1, r:r + 1, :], (2 * m, QK_DIM)))
        return -jnp.abs(cum - jnp.concatenate(pieces, axis=0))

    def head(a, h):
        return a[:, h * GLA_DK:(h + 1) * GLA_DK]

    q_lv = [q.astype(BF16)]
    k_lv = [k.astype(BF16)]
    for i in range(N_LEVELS - 1):
        e = jnp.exp2(level_log2_factor(i))
        q_lv.append((q * e).astype(BF16))
        k_lv.append((k * e).astype(BF16))
    cum_mid = dec_ref[1, half - 1:half, :]
    q_top = (q[half:] * jnp.exp2(cum[half:] - cum_mid)).astype(BF16)
    k_top = (k[:half] * jnp.exp2(cum_mid - cum[:half])).astype(BF16)

    q_in = (q * jnp.exp2(cum)).astype(BF16)
    k_out = k * jnp.exp2(cum_last - cum)
    a_last = jnp.exp2(cum_last)

    o_heads = []
    for h in range(GLA_HEADS):
        vh = v[:, h * GLA_DV:(h + 1) * GLA_DV]
        diag = [0.0, 0.0]
        for i in range(N_LEVELS):
            s_i = _dot_nt(head(q_lv[i], h), head(k_lv[i], h))
            diag = [jnp.where(lev == i - 1, s_i[r0:r0 + half, r0:r0 + half], diag[j])
                    for j, r0 in enumerate((0, half))]
        diag = [d.astype(BF16) for d in diag]
        p_low = jnp.concatenate([_dot_nt(head(q_top, h), head(k_top, h)).astype(BF16), diag[1]], axis=1)
        s_old = state_ref[0, 0, h]
        o = jnp.concatenate([_dot(diag[0], vh[:half]), _dot(p_low, vh)], axis=0)
        o_heads.append(o + _dot(head(q_in, h), s_old.astype(BF16)))
        a_col = _column_broadcast(head(a_last, h))
        a_col = jnp.concatenate([a_col, a_col], axis=1)
        state_ref[0, 0, h] = a_col * s_old + _dot(head(k_out, h).T.astype(BF16), vh)

    return _mix_out(x, o_heads, proj(OFF_G, V_DIM), proj(OFF_GA, D_MODEL), proj(OFF_GB, D_MODEL),
                    y_a, w_gn_ref[...], w_b_ref, w_o_ref, wn_post_ref[...])


def _sample_mixer_kernel(x_ref, cbuf_ref, st_ref, wn_pre_ref, w_head_ref, w_tail_ref, w_gklr_ref,
                         w_gk_ref, b_gk_ref, w_conv_ref, w_a_ref, w_gn_ref, w_b_ref, w_o_ref, wn_post_ref,
                         h_ref, conv_ref, st_out_ref,
                         q_scr, k_scr, a_scr, v_scr, o_scr, ya_scr, g_scr, ga_scr, gb_scr):
    step = pl.program_id(0)

    @pl.when(step == 0)
    def _():
        x = x_ref[...]
        hn = _rms(x, wn_pre_ref[...]).astype(BF16)
        proj, logw2 = _projections(hn, w_head_ref, w_tail_ref, w_gklr_ref, w_gk_ref, b_gk_ref)
        u = proj(OFF_C, CONV_DIM) * proj(OFF_X, CONV_DIM)
        buf0 = cbuf_ref[:, 0:CONV_DIM]
        buf1 = cbuf_ref[:, CONV_DIM:2 * CONV_DIM]
        wc = w_conv_ref[...]
        y_conv = wc[0:1] * buf0 + wc[1:2] * buf1 + wc[2:3] * u
        conv_ref[:, 0:CONV_DIM] = buf1
        conv_ref[:, CONV_DIM:2 * CONV_DIM] = u
        ya_scr[...] = _dot((proj(OFF_B, CONV_DIM) * y_conv).astype(BF16), w_a_ref[...])
        blocked = lambda a: a.reshape(a.shape[0] // SAMPLE_BLOCK, SAMPLE_BLOCK, a.shape[1])
        q_scr[...] = blocked(proj(OFF_Q, QK_DIM) * (GLA_DK ** -0.5))
        k_scr[...] = blocked(proj(OFF_K, QK_DIM))
        a_scr[...] = blocked(jnp.exp2(logw2))
        v_scr[...] = blocked(proj(OFF_V, V_DIM))
        g_scr[...] = proj(OFF_G, V_DIM)
        ga_scr[...] = proj(OFF_GA, D_MODEL)
        gb_scr[...] = proj(OFF_GB, D_MODEL)

    for n in range(SAMPLE_BLOCK):
        q_row = q_scr[step, n:n + 1, :]
        k_row = k_scr[step, n:n + 1, :]
        a_row = a_scr[step, n:n + 1, :]
        v_row = v_scr[step, n:n + 1, :]
        for h in range(GLA_HEADS):
            sl = slice(h * GLA_DK, (h + 1) * GLA_DK)
            a_col = _column_broadcast(a_row[:, sl])
            k_col = _column_broadcast(k_row[:, sl])
            q_col = _column_broadcast(q_row[:, sl])
            for half in range(GLA_DV // LANES):
                c0 = half * LANES
                s_old = st_ref[n, h, :, c0:c0 + LANES]
                vv = v_row[:, h * GLA_DV + c0:h * GLA_DV + c0 + LANES]
                s_new = a_col * s_old + k_col * vv
                st_out_ref[n, h, :, c0:c0 + LANES] = s_new
                o_scr[step, n:n + 1, h * GLA_DV + c0:h * GLA_DV + c0 + LANES] = jnp.sum(
                    q_col * s_new, axis=0, keepdims=True)

    @pl.when(step == pl.num_programs(0) - 1)
    def _():
        o = o_scr[...].reshape(x_ref.shape[0], V_DIM)
        o_heads = [o[:, h * GLA_DV:(h + 1) * GLA_DV] for h in range(GLA_HEADS)]
        h_ref[...] = _mix_out(x_ref[...], o_heads, g_scr[...], ga_scr[...], gb_scr[...], ya_scr[...],
                              w_gn_ref[...], w_b_ref, w_o_ref, wn_post_ref[...])


def _ffn_kernel(h_ref, p_ref, wn_pre_ref, w_gate_ref, w_up_ref, w_down_ref, wn_post_ref,
                w_pp_ref, w_pg_ref, wn_ple_ref, out_ref):
    h = h_ref[...]
    f = _rms(h, wn_pre_ref[...]).astype(BF16)
    acc = None
    for c in range(D_FF // FFN_CHUNK):
        sl = slice(c * FFN_CHUNK, (c + 1) * FFN_CHUNK)
        gate = _dot(f, w_gate_ref[:, sl])
        up = _dot(f, w_up_ref[:, sl])
        act = (gate * _sigmoid(gate) * up).astype(BF16)
        part = _dot(act, w_down_ref[sl, :])
        acc = part if acc is None else acc + part
    h = h + _rms(acc, wn_post_ref[...])
    e = _dot(p_ref[...].astype(BF16), w_pp_ref[...]) * _sigmoid(_dot(h.astype(BF16), w_pg_ref[...]))
    out_ref[...] = h + _rms(e, wn_ple_ref[...])


def _cast_weights_kernel(w_in_ref, *refs):
    n = (len(refs) - 3) // 2
    head_ref, tail_ref, gklr_ref = refs[n:n + 3]
    head_ref[...] = w_in_ref[:, :N_HEAD].astype(BF16)
    tail_ref[...] = w_in_ref[:, GKLR_START + GATE_RANK:].astype(BF16)
    lane = lax.broadcasted_iota(jnp.int32, gklr_ref.shape, 1)
    gklr_ref[...] = jnp.where(lane < GATE_RANK, w_in_ref[:, GKLR_START:GKLR_START + LANES], 0.0).astype(BF16)
    for src, dst in zip(refs[:n], refs[n + 3:]):
        dst[...] = src[...].astype(BF16)


def _cast_weights(w_in, others):
    def spec(shape):
        return pl.BlockSpec((shape[0] // CAST_STEPS, shape[1]), lambda i: (i, 0))

    k = w_in.shape[0]
    out_shapes = [(k, N_HEAD), (k, N_TAIL), (k, LANES)] + [w.shape for w in others]
    return pl.pallas_call(
        _cast_weights_kernel,
        grid=(CAST_STEPS,),
        in_specs=[spec(w_in.shape)] + [spec(w.shape) for w in others],
        out_specs=[spec(s) for s in out_shapes],
        out_shape=[jax.ShapeDtypeStruct(s, BF16) for s in out_shapes],
        compiler_params=pltpu.CompilerParams(
            dimension_semantics=("arbitrary",), vmem_limit_bytes=VMEM_LIMIT_BYTES),
        name="cast_weights",
    )(w_in, *others)


def _resident(shape):
    return pl.BlockSpec(shape, lambda *_: (0,) * len(shape), pipeline_mode=pl.Buffered(1))


def _level_map(t):
    idx = np.arange(t)
    xor = idx[:, None] ^ idx[None, :]
    lev = np.floor(np.log2(np.maximum(xor, 1))).astype(np.int32)
    lev = np.where(idx[:, None] > idx[None, :], lev, -2)
    lev = np.where(idx[:, None] == idx[None, :], -1, lev)
    return jnp.asarray(lev, dtype=jnp.int32)


def _mixer_weight_specs():
    return [
        _resident((1, D_MODEL)),
        _resident((D_MODEL, N_HEAD)),
        _resident((D_MODEL, N_TAIL)),
        _resident((D_MODEL, LANES)),
        _resident((LANES, QK_DIM)),
        _resident((1, QK_DIM)),
        _resident((3, CONV_DIM)),
        _resident((CONV_DIM, D_MODEL)),
        _resident((1, GLA_DV)),
        _resident((V_DIM, D_MODEL)),
        _resident((D_MODEL, D_MODEL)),
        _resident((1, D_MODEL)),
    ]


def _prompt_mixer(x, mixer_weights):
    b, s, _ = x.shape
    t = SEQ_TILE
    c = GLA_CHUNK
    tri = jnp.asarray(np.tril(np.ones((c, c), np.float32)), dtype=BF16)
    return pl.pallas_call(
        _prompt_mixer_kernel,
        grid=(b, s // t),
        in_specs=[pl.BlockSpec((1, t, D_MODEL), lambda i, j: (i, j, 0)),
                  _resident((c // 2, c // 2)), _resident((c, c))] + _mixer_weight_specs(),
        out_specs=[pl.BlockSpec((1, t, D_MODEL), lambda i, j: (i, j, 0)),
                   pl.BlockSpec((1, 1, 2, CONV_DIM), lambda i, j: (0, i, 0, 0)),
                   pl.BlockSpec((1, 1, GLA_HEADS, GLA_DK, GLA_DV), lambda i, j: (0, i, 0, 0, 0))],
        out_shape=[jax.ShapeDtypeStruct((b, s, D_MODEL), F32),
                   jax.ShapeDtypeStruct((1, b, 2, CONV_DIM), F32),
                   jax.ShapeDtypeStruct((1, b, GLA_HEADS, GLA_DK, GLA_DV), F32)],
        scratch_shapes=[pltpu.VMEM((t // c, 2, c, QK_DIM), F32)],
        compiler_params=pltpu.CompilerParams(
            dimension_semantics=("arbitrary", "arbitrary"), vmem_limit_bytes=VMEM_LIMIT_BYTES),
        name="prompt_mixer",
    )(x, _level_map(c // 2), tri, *mixer_weights)


def _sample_mixer(x, conv_buf, state, mixer_weights):
    n = x.shape[0]
    nb = SAMPLE_BLOCK
    state_spec = pl.BlockSpec((nb, GLA_HEADS, GLA_DK, GLA_DV), lambda i: (i, 0, 0, 0))
    whole = lambda shape: pl.BlockSpec(shape, lambda i: (0,) * len(shape))
    return pl.pallas_call(
        _sample_mixer_kernel,
        grid=(n // nb,),
        in_specs=[_resident((n, D_MODEL)), _resident((n, 2 * CONV_DIM)), state_spec] + _mixer_weight_specs(),
        out_specs=[whole((n, D_MODEL)), whole((n, 2 * CONV_DIM)), state_spec],
        out_shape=[jax.ShapeDtypeStruct((n, D_MODEL), F32),
                   jax.ShapeDtypeStruct((n, 2 * CONV_DIM), F32),
                   jax.ShapeDtypeStruct((n, GLA_HEADS, GLA_DK, GLA_DV), F32)],
        scratch_shapes=[pltpu.VMEM((n // nb, nb, QK_DIM), F32), pltpu.VMEM((n // nb, nb, QK_DIM), F32),
                        pltpu.VMEM((n // nb, nb, QK_DIM), F32), pltpu.VMEM((n // nb, nb, V_DIM), F32),
                        pltpu.VMEM((n // nb, nb, V_DIM), F32), pltpu.VMEM((n, D_MODEL), F32),
                        pltpu.VMEM((n, V_DIM), F32), pltpu.VMEM((n, D_MODEL), F32),
                        pltpu.VMEM((n, D_MODEL), F32)],
        compiler_params=pltpu.CompilerParams(
            dimension_semantics=("arbitrary",), vmem_limit_bytes=VMEM_LIMIT_BYTES),
        name="sample_mixer",
    )(x, conv_buf, state, *mixer_weights)


def _ffn(h, p, ffn_weights, tile):
    rows = h.shape[0]
    return pl.pallas_call(
        _ffn_kernel,
        grid=(rows // tile,),
        in_specs=[pl.BlockSpec((tile, D_MODEL), lambda i: (i, 0)),
                  pl.BlockSpec((tile, PLE_DIM), lambda i: (i, 0)),
                  _resident((1, D_MODEL)), _resident((D_MODEL, D_FF)), _resident((D_MODEL, D_FF)),
                  _resident((D_FF, D_MODEL)), _resident((1, D_MODEL)), _resident((PLE_DIM, D_MODEL)),
                  _resident((D_MODEL, D_MODEL)), _resident((1, D_MODEL))],
        out_specs=pl.BlockSpec((tile, D_MODEL), lambda i: (i, 0)),
        out_shape=jax.ShapeDtypeStruct((rows, D_MODEL), F32),
        compiler_params=pltpu.CompilerParams(
            dimension_semantics=("arbitrary",), vmem_limit_bytes=VMEM_LIMIT_BYTES),
        name="ffn_ple",
    )(h, p, *ffn_weights)


def kernel(x_prompt, x_sample, state_conv, state_gla, p_prompt, p_sample, w_norm_mix_pre, w_in, w_conv, w_a_out, w_gk, b_gk, w_gla_norm, w_b_out, w_o, w_norm_mix_post, w_norm_ffn_pre, w_ffn_gate, w_ffn_up, w_ffn_down, w_norm_ffn_post, w_ple_proj, w_ple_gate, w_norm_ple_post):
    depth = w_in.shape[0]
    batch, seq, _ = x_prompt.shape
    n_dec = x_sample.shape[0]
    assert x_sample.shape[1] == 1, "the sample group carries one new token per sequence"
    assert seq % SEQ_TILE == 0 and (batch * seq) % FFN_TILE == 0 and n_dec % SAMPLE_BLOCK == 0

    hp = x_prompt
    hs = x_sample.reshape(n_dec, D_MODEL)
    conv_p, gla_p, conv_s, gla_s = [], [], [], []
    for i in range(depth):
        row = lambda w: w[i].reshape(1, -1)
        (w_head, w_tail, w_gklr, w_a, w_b, w_o_b, w_gate, w_up, w_down, w_pp, w_pg) = _cast_weights(
            w_in[i], (w_a_out[i], w_b_out[i], w_o[i], w_ffn_gate[i], w_ffn_up[i], w_ffn_down[i],
                      w_ple_proj[i], w_ple_gate[i]))
        w_gk_pad = jnp.pad(w_gk[i], ((0, LANES - GATE_RANK), (0, 0))).astype(BF16)
        mixer_weights = (row(w_norm_mix_pre), w_head, w_tail, w_gklr, w_gk_pad, row(b_gk), w_conv[i],
                         w_a, row(w_gla_norm), w_b, w_o_b, row(w_norm_mix_post))
        ffn_weights = (row(w_norm_ffn_pre), w_gate, w_up, w_down, row(w_norm_ffn_post), w_pp, w_pg,
                       row(w_norm_ple_post))

        hp_mid, cbp, sp = _prompt_mixer(hp, mixer_weights)
        hp = _ffn(hp_mid.reshape(batch * seq, D_MODEL), p_prompt[i].reshape(batch * seq, PLE_DIM),
                  ffn_weights, FFN_TILE).reshape(batch, seq, D_MODEL)

        hs_mid, cbs, ss = _sample_mixer(hs, state_conv[i].reshape(n_dec, 2 * CONV_DIM), state_gla[i],
                                        mixer_weights)
        hs = _ffn(hs_mid, p_sample[i].reshape(n_dec, PLE_DIM), ffn_weights, n_dec)

        conv_p.append(cbp[0]); gla_p.append(sp[0])
        conv_s.append(cbs.reshape(n_dec, 2, CONV_DIM)); gla_s.append(ss)
    return (hp, hs.reshape(n_dec, 1, D_MODEL), jnp.stack(conv_p), jnp.stack(gla_p),
            jnp.stack(conv_s), jnp.stack(gla_s))
```

```python
import functools

import numpy as np
import jax
import jax.numpy as jnp
from jax import lax
from jax.experimental import pallas as pl
from jax.experimental.pallas import tpu as pltpu

D_MODEL = 1024
CONV_DIM = D_MODEL
GLA_HEADS = 4
GLA_DK = 128
GLA_DV = 256
QK_DIM = GLA_HEADS * GLA_DK
V_DIM = GLA_HEADS * GLA_DV
GATE_RANK = 16
GATE_NORMALIZER = 16.0
D_FF = 2816
PLE_DIM = 256
EPS = 1e-6
LOG2_E = 1.4426950408889634

LANES = 128
SUBLANES = 8
VMEM_LIMIT_BYTES = 56 * 1024 * 1024

OFF_B, OFF_C, OFF_X = 0, 1024, 2048
OFF_Q, OFF_K, OFF_V, OFF_G = 3072, 3584, 4096, 5120
OFF_GA, OFF_GB = 6144, 7168
N_HEAD = 6144
N_TAIL = 2048
GKLR_START = 6144

GLA_CHUNK = 256
N_LEVELS = 8
SEQ_TILE = 512
FFN_TILE = 512
FFN_CHUNK = 1408
SAMPLE_BLOCK = 8
CAST_STEPS = 8
W_IN_BLOCK = 512

F32 = jnp.float32
BF16 = jnp.bfloat16


def _rms(x, w):
    return x * lax.rsqrt(jnp.mean(x * x, axis=-1, keepdims=True) + EPS) * w


def _sigmoid(x):
    return 1.0 / (1.0 + jnp.exp2(x * -LOG2_E))


def _log2_sigmoid(x, scale):
    return (jnp.minimum(x, 0.0) - jnp.log(1.0 + jnp.exp2(jnp.abs(x) * -LOG2_E))) * (scale * LOG2_E)


def _dot(a, b):
    return jnp.dot(a, b, preferred_element_type=F32)


def _dot_nt(a, b):
    return lax.dot_general(a, b, (((1,), (1,)), ((), ())), preferred_element_type=F32)


def _column_broadcast(row):
    return jnp.broadcast_to(row, (LANES, LANES)).T


def _projections(hn, w_head_ref, w_tail_ref, w_gklr_ref, w_gk_ref, b_gk_ref):
    def proj(off, width):
        if off < N_HEAD:
            return _dot(hn, w_head_ref[:, off:off + width])
        return _dot(hn, w_tail_ref[:, off - N_HEAD:off - N_HEAD + width])

    gk_lr = _dot(hn, w_gklr_ref[...])
    gk = _dot(gk_lr.astype(BF16), w_gk_ref[...]) + b_gk_ref[...]
    return proj, _log2_sigmoid(gk, 1.0 / GATE_NORMALIZER)


def _mix_out(x, o_heads, g, gate_a, gate_b, y_a, w_gn, w_b_ref, w_o_ref, wn_post):
    normed = []
    for h in range(GLA_HEADS):
        o = o_heads[h]
        gh = g[:, h * GLA_DV:(h + 1) * GLA_DV]
        o = o * lax.rsqrt(jnp.mean(o * o, axis=-1, keepdims=True) + EPS) * w_gn
        normed.append((o * (gh * _sigmoid(gh))).astype(BF16))
    y_b = _dot(jnp.concatenate(normed, axis=1), w_b_ref[...])
    merged = _sigmoid(gate_a) * y_a + _sigmoid(gate_b) * y_b
    mix = _dot(merged.astype(BF16), w_o_ref[...])
    return x + _rms(mix, wn_post)


def _prompt_mixer_kernel(x_ref, lev_ref, tri_ref, wn_pre_ref, w_head_ref, w_tail_ref, w_gklr_ref,
                         w_gk_ref, b_gk_ref, w_conv_ref, w_a_ref, w_gn_ref, w_b_ref, w_o_ref,
                         wn_post_ref, h_ref, conv_ref, state_ref, dec_ref):
    @pl.when(pl.program_id(1) == 0)
    def _():
        conv_ref[...] = jnp.zeros_like(conv_ref)
        state_ref[...] = jnp.zeros_like(state_ref)

    for c in range(SEQ_TILE // GLA_CHUNK):
        rows = slice(c * GLA_CHUNK, (c + 1) * GLA_CHUNK)
        h_ref[0, rows] = _prompt_mixer_chunk(
            x_ref[0, rows], dec_ref.at[c], lev_ref, tri_ref, wn_pre_ref, w_head_ref, w_tail_ref,
            w_gklr_ref, w_gk_ref, b_gk_ref, w_conv_ref, w_a_ref, w_gn_ref, w_b_ref, w_o_ref,
            wn_post_ref, conv_ref, state_ref)


def _prompt_mixer_chunk(x, dec_ref, lev_ref, tri_ref, wn_pre_ref, w_head_ref, w_tail_ref, w_gklr_ref,
                        w_gk_ref, b_gk_ref, w_conv_ref, w_a_ref, w_gn_ref, w_b_ref, w_o_ref,
                        wn_post_ref, conv_ref, state_ref):
    t = GLA_CHUNK
    hn = _rms(x, wn_pre_ref[...]).astype(BF16)
    proj, logw2 = _projections(hn, w_head_ref, w_tail_ref, w_gklr_ref, w_gk_ref, b_gk_ref)

    u = proj(OFF_C, CONV_DIM) * proj(OFF_X, CONV_DIM)
    prev2 = conv_ref[0, 0, 0:1, :]
    prev1 = conv_ref[0, 0, 1:2, :]
    row = lax.broadcasted_iota(jnp.int32, (t, CONV_DIM), 0)
    u1 = jnp.where(row == 0, prev1, pltpu.roll(u, 1, 0))
    u2 = jnp.where(row == 0, prev2, jnp.where(row == 1, prev1, pltpu.roll(u, 2, 0)))
    wc = w_conv_ref[...]
    y_conv = wc[0:1] * u2 + wc[1:2] * u1 + wc[2:3] * u
    conv_ref[0, 0] = u[t - 2:t]
    y_a = _dot((proj(OFF_B, CONV_DIM) * y_conv).astype(BF16), w_a_ref[...])

    dec_ref[0] = logw2
    logw2 = dec_ref[0]
    hi = logw2.astype(BF16)
    lo = (logw2 - hi.astype(F32)).astype(BF16)
    cum = _dot(tri_ref[...], hi) + _dot(tri_ref[...], lo)
    dec_ref[1] = cum
    cum_last = cum[t - 1:t]

    q = proj(OFF_Q, QK_DIM) * (GLA_DK ** -0.5)
    k = proj(OFF_K, QK_DIM)
    v = proj(OFF_V, V_DIM).astype(BF16)

    half = t // 2
    lev = lev_ref[...]
    rowq = lax.broadcasted_iota(jnp.int32, (t, QK_DIM), 0)

    def level_log2_factor(i):
        m = 1 << i
        if i == 0:
            return jnp.where((rowq & 1) == 1, logw2, 0.0)
        if i == 1:
            up = pltpu.roll(logw2, t - 1, 0)
            dn = pltpu.roll(logw2, 1, 0)
            r4 = rowq & 3
            return jnp.where(r4 == 0, up, jnp.where(r4 == 1, 0.0, jnp.where(r4 == 2, logw2, logw2 + dn)))
        pieces = []
        for blk in range(t // (2 * m)):
            r = blk * 2 * m + m - 1
            pieces.append(jnp.broadcast_to(dec_ref[1, r:r + 1, :], (2 * m, QK_DIM)))
        return -jnp.abs(cum - jnp.concatenate(pieces, axis=0))

    def head(a, h):
        return a[:, h * GLA_DK:(h + 1) * GLA_DK]

    q_lv = [q.astype(BF16)]
    k_lv = [k.astype(BF16)]
    for i in range(N_LEVELS - 1):
        e = jnp.exp2(level_log2_factor(i))
        q_lv.append((q * e).astype(BF16))
        k_lv.append((k * e).astype(BF16))
    cum_mid = dec_ref[1, half - 1:half, :]
    q_top = (q[half:] * jnp.exp2(cum[half:] - cum_mid)).astype(BF16)
    k_top = (k[:half] * jnp.exp2(cum_mid - cum[:half])).astype(BF16)

    q_in = (q * jnp.exp2(cum)).astype(BF16)
    k_out = k * jnp.exp2(cum_last - cum)
    a_last = jnp.exp2(cum_last)

    o_heads = []
    for h in range(GLA_HEADS):
        vh = v[:, h * GLA_DV:(h + 1) * GLA_DV]
        diag = [0.0, 0.0]
        for i in range(N_LEVELS):
            s_i = _dot_nt(head(q_lv[i], h), head(k_lv[i], h))
            diag = [jnp.where(lev == i - 1, s_i[r0:r0 + half, r0:r0 + half], diag[j])
                    for j, r0 in enumerate((0, half))]
        diag = [d.astype(BF16) for d in diag]
        p_low = jnp.concatenate([_dot_nt(head(q_top, h), head(k_top, h)).astype(BF16), diag[1]], axis=1)
        s_old = state_ref[0, 0, h]
        o = jnp.concatenate([_dot(diag[0], vh[:half]), _dot(p_low, vh)], axis=0)
        o_heads.append(o + _dot(head(q_in, h), s_old.astype(BF16)))
        a_col = _column_broadcast(head(a_last, h))
        a_col = jnp.concatenate([a_col, a_col], axis=1)
        state_ref[0, 0, h] = a_col * s_old + _dot(head(k_out, h).T.astype(BF16), vh)

    return _mix_out(x, o_heads, proj(OFF_G, V_DIM), proj(OFF_GA, D_MODEL), proj(OFF_GB, D_MODEL),
                    y_a, w_gn_ref[...], w_b_ref, w_o_ref, wn_post_ref[...])


def _sample_mixer_kernel(x_ref, cbuf_ref, st_ref, wn_pre_ref, w_head_ref, w_tail_ref, w_gklr_ref,
                         w_gk_ref, b_gk_ref, w_conv_ref, w_a_ref, w_gn_ref, w_b_ref, w_o_ref, wn_post_ref,
                         h_ref, conv_ref, st_out_ref,
                         q_scr, k_scr, a_scr, v_scr, o_scr, ya_scr, g_scr, ga_scr, gb_scr):
    step = pl.program_id(0)

    @pl.when(step == 0)
    def _():
        x = x_ref[...]
        hn = _rms(x, wn_pre_ref[...]).astype(BF16)
        proj, logw2 = _projections(hn, w_head_ref, w_tail_ref, w_gklr_ref, w_gk_ref, b_gk_ref)
        u = proj(OFF_C, CONV_DIM) * proj(OFF_X, CONV_DIM)
        buf0 = cbuf_ref[:, 0:CONV_DIM]
        buf1 = cbuf_ref[:, CONV_DIM:2 * CONV_DIM]
        wc = w_conv_ref[...]
        y_conv = wc[0:1] * buf0 + wc[1:2] * buf1 + wc[2:3] * u
        conv_ref[:, 0:CONV_DIM] = buf1
        conv_ref[:, CONV_DIM:2 * CONV_DIM] = u
        ya_scr[...] = _dot((proj(OFF_B, CONV_DIM) * y_conv).astype(BF16), w_a_ref[...])
        blocked = lambda a: a.reshape(a.shape[0] // SAMPLE_BLOCK, SAMPLE_BLOCK, a.shape[1])
        q_scr[...] = blocked(proj(OFF_Q, QK_DIM) * (GLA_DK ** -0.5))
        k_scr[...] = blocked(proj(OFF_K, QK_DIM))
        a_scr[...] = blocked(jnp.exp2(logw2))
        v_scr[...] = blocked(proj(OFF_V, V_DIM))
        g_scr[...] = proj(OFF_G, V_DIM)
        ga_scr[...] = proj(OFF_GA, D_MODEL)
        gb_scr[...] = proj(OFF_GB, D_MODEL)

    for n in range(SAMPLE_BLOCK):
        q_row = q_scr[step, n:n + 1, :]
        k_row = k_scr[step, n:n + 1, :]
        a_row = a_scr[step, n:n + 1, :]
        v_row = v_scr[step, n:n + 1, :]
        for h in range(GLA_HEADS):
            sl = slice(h * GLA_DK, (h + 1) * GLA_DK)
            a_col = _column_broadcast(a_row[:, sl])
            k_col = _column_broadcast(k_row[:, sl])
            q_col = _column_broadcast(q_row[:, sl])
            for half in range(GLA_DV // LANES):
                c0 = half * LANES
                s_old = st_ref[n, h, :, c0:c0 + LANES]
                vv = v_row[:, h * GLA_DV + c0:h * GLA_DV + c0 + LANES]
                s_new = a_col * s_old + k_col * vv
                st_out_ref[n, h, :, c0:c0 + LANES] = s_new
                o_scr[step, n:n + 1, h * GLA_DV + c0:h * GLA_DV + c0 + LANES] = jnp.sum(
                    q_col * s_new, axis=0, keepdims=True)

    @pl.when(step == pl.num_programs(0) - 1)
    def _():
        o = o_scr[...].reshape(x_ref.shape[0], V_DIM)
        o_heads = [o[:, h * GLA_DV:(h + 1) * GLA_DV] for h in range(GLA_HEADS)]
        h_ref[...] = _mix_out(x_ref[...], o_heads, g_scr[...], ga_scr[...], gb_scr[...], ya_scr[...],
                              w_gn_ref[...], w_b_ref, w_o_ref, wn_post_ref[...])


def _ffn_kernel(h_ref, p_ref, wn_pre_ref, w_gate_ref, w_up_ref, w_down_ref, wn_post_ref,
                w_pp_ref, w_pg_ref, wn_ple_ref, out_ref):
    h = h_ref[...]
    f = _rms(h, wn_pre_ref[...]).astype(BF16)
    acc = None
    for c in range(D_FF // FFN_CHUNK):
        sl = slice(c * FFN_CHUNK, (c + 1) * FFN_CHUNK)
        gate = _dot(f, w_gate_ref[:, sl])
        up = _dot(f, w_up_ref[:, sl])
        act = (gate * _sigmoid(gate) * up).astype(BF16)
        part = _dot(act, w_down_ref[sl, :])
        acc = part if acc is None else acc + part
    h = h + _rms(acc, wn_post_ref[...])
    e = _dot(p_ref[...].astype(BF16), w_pp_ref[...]) * _sigmoid(_dot(h.astype(BF16), w_pg_ref[...]))
    out_ref[...] = h + _rms(e, wn_ple_ref[...])


def _cast_w_in_kernel(wt_ref, gk_ref, head_ref, tail_ref, gklr_ref):
    j = pl.program_id(0)
    blk = wt_ref[...].T.astype(BF16)

    @pl.when(j < N_HEAD // W_IN_BLOCK)
    def _():
        head_ref[...] = blk

    @pl.when(j >= N_HEAD // W_IN_BLOCK)
    def _():
        tail_ref[...] = blk

    @pl.when(j == 0)
    def _():
        rows = jnp.concatenate([gk_ref[...], jnp.zeros((LANES - GATE_RANK, D_MODEL), F32)], axis=0)
        gklr_ref[...] = rows.T.astype(BF16)


def _cast_w_in(w_in_t):
    n_head = N_HEAD // W_IN_BLOCK
    n_tail = N_TAIL // W_IN_BLOCK

    def src_row(j):
        row = jnp.where(j < n_head, j * W_IN_BLOCK, GKLR_START + GATE_RANK + (j - n_head) * W_IN_BLOCK)
        return pl.multiple_of(row, GATE_RANK)

    return pl.pallas_call(
        _cast_w_in_kernel,
        grid=(n_head + n_tail,),
        in_specs=[pl.BlockSpec((pl.Element(W_IN_BLOCK), pl.Element(D_MODEL)), lambda j: (src_row(j), 0)),
                  pl.BlockSpec((pl.Element(GATE_RANK), pl.Element(D_MODEL)), lambda j: (GKLR_START, 0))],
        out_specs=[pl.BlockSpec((D_MODEL, W_IN_BLOCK), lambda j: (0, jnp.minimum(j, n_head - 1))),
                   pl.BlockSpec((D_MODEL, W_IN_BLOCK), lambda j: (0, jnp.maximum(j - n_head, 0))),
                   pl.BlockSpec((D_MODEL, LANES), lambda j: (0, 0))],
        out_shape=[jax.ShapeDtypeStruct((D_MODEL, N_HEAD), BF16),
                   jax.ShapeDtypeStruct((D_MODEL, N_TAIL), BF16),
                   jax.ShapeDtypeStruct((D_MODEL, LANES), BF16)],
        compiler_params=pltpu.CompilerParams(
            dimension_semantics=("arbitrary",), vmem_limit_bytes=VMEM_LIMIT_BYTES),
        name="cast_w_in",
    )(w_in_t, w_in_t)


def _cast_weights_kernel(*refs):
    n = len(refs) // 2
    for src, dst in zip(refs[:n], refs[n:]):
        dst[...] = src[...].astype(BF16)


def _cast_weights(weights):
    def spec(shape):
        return pl.BlockSpec((shape[0] // CAST_STEPS, shape[1]), lambda i: (i, 0))

    return pl.pallas_call(
        _cast_weights_kernel,
        grid=(CAST_STEPS,),
        in_specs=[spec(w.shape) for w in weights],
        out_specs=[spec(w.shape) for w in weights],
        out_shape=[jax.ShapeDtypeStruct(w.shape, BF16) for w in weights],
        compiler_params=pltpu.CompilerParams(
            dimension_semantics=("arbitrary",), vmem_limit_bytes=VMEM_LIMIT_BYTES),
        name="cast_weights",
    )(*weights)


def _resident(shape):
    return pl.BlockSpec(shape, lambda *_: (0,) * len(shape), pipeline_mode=pl.Buffered(1))


def _level_map(t):
    idx = np.arange(t)
    xor = idx[:, None] ^ idx[None, :]
    lev = np.floor(np.log2(np.maximum(xor, 1))).astype(np.int32)
    lev = np.where(idx[:, None] > idx[None, :], lev, -2)
    lev = np.where(idx[:, None] == idx[None, :], -1, lev)
    return jnp.asarray(lev, dtype=jnp.int32)


def _mixer_weight_specs():
    return [
        _resident((1, D_MODEL)),
        _resident((D_MODEL, N_HEAD)),
        _resident((D_MODEL, N_TAIL)),
        _resident((D_MODEL, LANES)),
        _resident((LANES, QK_DIM)),
        _resident((1, QK_DIM)),
        _resident((3, CONV_DIM)),
        _resident((CONV_DIM, D_MODEL)),
        _resident((1, GLA_DV)),
        _resident((V_DIM, D_MODEL)),
        _resident((D_MODEL, D_MODEL)),
        _resident((1, D_MODEL)),
    ]


def _prompt_mixer(x, mixer_weights):
    b, s, _ = x.shape
    t = SEQ_TILE
    c = GLA_CHUNK
    tri = jnp.asarray(np.tril(np.ones((c, c), np.float32)), dtype=BF16)
    return pl.pallas_call(
        _prompt_mixer_kernel,
        grid=(b, s // t),
        in_specs=[pl.BlockSpec((1, t, D_MODEL), lambda i, j: (i, j, 0)),
                  _resident((c // 2, c // 2)), _resident((c, c))] + _mixer_weight_specs(),
        out_specs=[pl.BlockSpec((1, t, D_MODEL), lambda i, j: (i, j, 0)),
                   pl.BlockSpec((1, 1, 2, CONV_DIM), lambda i, j: (0, i, 0, 0)),
                   pl.BlockSpec((1, 1, GLA_HEADS, GLA_DK, GLA_DV), lambda i, j: (0, i, 0, 0, 0))],
        out_shape=[jax.ShapeDtypeStruct((b, s, D_MODEL), F32),
                   jax.ShapeDtypeStruct((1, b, 2, CONV_DIM), F32),
                   jax.ShapeDtypeStruct((1, b, GLA_HEADS, GLA_DK, GLA_DV), F32)],
        scratch_shapes=[pltpu.VMEM((t // c, 2, c, QK_DIM), F32)],
        compiler_params=pltpu.CompilerParams(
            dimension_semantics=("arbitrary", "arbitrary"), vmem_limit_bytes=VMEM_LIMIT_BYTES),
        name="prompt_mixer",
    )(x, _level_map(c // 2), tri, *mixer_weights)


def _sample_mixer(x, conv_buf, state, mixer_weights):
    n = x.shape[0]
    nb = SAMPLE_BLOCK
    state_spec = pl.BlockSpec((nb, GLA_HEADS, GLA_DK, GLA_DV), lambda i: (i, 0, 0, 0))
    whole = lambda shape: pl.BlockSpec(shape, lambda i: (0,) * len(shape))
    return pl.pallas_call(
        _sample_mixer_kernel,
        grid=(n // nb,),
        in_specs=[_resident((n, D_MODEL)), _resident((n, 2 * CONV_DIM)), state_spec] + _mixer_weight_specs(),
        out_specs=[whole((n, D_MODEL)), whole((n, 2 * CONV_DIM)), state_spec],
        out_shape=[jax.ShapeDtypeStruct((n, D_MODEL), F32),
                   jax.ShapeDtypeStruct((n, 2 * CONV_DIM), F32),
                   jax.ShapeDtypeStruct((n, GLA_HEADS, GLA_DK, GLA_DV), F32)],
        scratch_shapes=[pltpu.VMEM((n // nb, nb, QK_DIM), F32), pltpu.VMEM((n // nb, nb, QK_DIM), F32),
                        pltpu.VMEM((n // nb, nb, QK_DIM), F32), pltpu.VMEM((n // nb, nb, V_DIM), F32),
                        pltpu.VMEM((n // nb, nb, V_DIM), F32), pltpu.VMEM((n, D_MODEL), F32),
                        pltpu.VMEM((n, V_DIM), F32), pltpu.VMEM((n, D_MODEL), F32),
                        pltpu.VMEM((n, D_MODEL), F32)],
        compiler_params=pltpu.CompilerParams(
            dimension_semantics=("arbitrary",), vmem_limit_bytes=VMEM_LIMIT_BYTES),
        name="sample_mixer",
    )(x, conv_buf, state, *mixer_weights)


def _ffn(h, p, ffn_weights, tile):
    rows = h.shape[0]
    return pl.pallas_call(
        _ffn_kernel,
        grid=(rows // tile,),
        in_specs=[pl.BlockSpec((tile, D_MODEL), lambda i: (i, 0)),
                  pl.BlockSpec((tile, PLE_DIM), lambda i: (i, 0)),
                  _resident((1, D_MODEL)), _resident((D_MODEL, D_FF)), _resident((D_MODEL, D_FF)),
                  _resident((D_FF, D_MODEL)), _resident((1, D_MODEL)), _resident((PLE_DIM, D_MODEL)),
                  _resident((D_MODEL, D_MODEL)), _resident((1, D_MODEL))],
        out_specs=pl.BlockSpec((tile, D_MODEL), lambda i: (i, 0)),
        out_shape=jax.ShapeDtypeStruct((rows, D_MODEL), F32),
        compiler_params=pltpu.CompilerParams(
            dimension_semantics=("arbitrary",), vmem_limit_bytes=VMEM_LIMIT_BYTES),
        name="ffn_ple",
    )(h, p, *ffn_weights)


def kernel(x_prompt, x_sample, state_conv, state_gla, p_prompt, p_sample, w_norm_mix_pre, w_in, w_conv, w_a_out, w_gk, b_gk, w_gla_norm, w_b_out, w_o, w_norm_mix_post, w_norm_ffn_pre, w_ffn_gate, w_ffn_up, w_ffn_down, w_norm_ffn_post, w_ple_proj, w_ple_gate, w_norm_ple_post):
    depth = w_in.shape[0]
    batch, seq, _ = x_prompt.shape
    n_dec = x_sample.shape[0]
    assert x_sample.shape[1] == 1, "the sample group carries one new token per sequence"
    assert seq % SEQ_TILE == 0 and (batch * seq) % FFN_TILE == 0 and n_dec % SAMPLE_BLOCK == 0

    hp = x_prompt
    hs = x_sample.reshape(n_dec, D_MODEL)
    conv_p, gla_p, conv_s, gla_s = [], [], [], []
    for i in range(depth):
        row = lambda w: w[i].reshape(1, -1)
        w_head, w_tail, w_gklr = _cast_w_in(jnp.swapaxes(w_in[i], 0, 1))
        w_a, w_b, w_o_b, w_gate, w_up, w_down, w_pp, w_pg = _cast_weights(
            (w_a_out[i], w_b_out[i], w_o[i], w_ffn_gate[i], w_ffn_up[i], w_ffn_down[i],
             w_ple_proj[i], w_ple_gate[i]))
        w_gk_pad = jnp.pad(w_gk[i], ((0, LANES - GATE_RANK), (0, 0))).astype(BF16)
        mixer_weights = (row(w_norm_mix_pre), w_head, w_tail, w_gklr, w_gk_pad, row(b_gk), w_conv[i],
                         w_a, row(w_gla_norm), w_b, w_o_b, row(w_norm_mix_post))
        ffn_weights = (row(w_norm_ffn_pre), w_gate, w_up, w_down, row(w_norm_ffn_post), w_pp, w_pg,
                       row(w_norm_ple_post))

        hp_mid, cbp, sp = _prompt_mixer(hp, mixer_weights)
        hp = _ffn(hp_mid.reshape(batch * seq, D_MODEL), p_prompt[i].reshape(batch * seq, PLE_DIM),
                  ffn_weights, FFN_TILE).reshape(batch, seq, D_MODEL)

        hs_mid, cbs, ss = _sample_mixer(hs, state_conv[i].reshape(n_dec, 2 * CONV_DIM), state_gla[i],
                                        mixer_weights)
        hs = _ffn(hs_mid, p_sample[i].reshape(n_dec, PLE_DIM), ffn_weights, n_dec)

        conv_p.append(cbp[0]); gla_p.append(sp[0])
        conv_s.append(cbs.reshape(n_dec, 2, CONV_DIM)); gla_s.append(ss)
    return (hp, hs.reshape(n_dec, 1, D_MODEL), jnp.stack(conv_p), jnp.stack(gla_p),
            jnp.stack(conv_s), jnp.stack(gla_s))
```

```python
import functools

import numpy as np
import jax
import jax.numpy as jnp
from jax import lax
from jax.experimental import pallas as pl
from jax.experimental.pallas import tpu as pltpu

D_MODEL = 1024
CONV_DIM = D_MODEL
GLA_HEADS = 4
GLA_DK = 128
GLA_DV = 256
QK_DIM = GLA_HEADS * GLA_DK
V_DIM = GLA_HEADS * GLA_DV
GATE_RANK = 16
GATE_NORMALIZER = 16.0
D_FF = 2816
PLE_DIM = 256
EPS = 1e-6
LOG2_E = 1.4426950408889634

LANES = 128
SUBLANES = 8
VMEM_LIMIT_BYTES = 56 * 1024 * 1024

OFF_B, OFF_C, OFF_X = 0, 1024, 2048
OFF_Q, OFF_K, OFF_V, OFF_G = 3072, 3584, 4096, 5120
OFF_GA, OFF_GB = 6144, 7168
N_HEAD = 6144
N_TAIL = 2048
GKLR_START = 6144

GLA_CHUNK = 256
N_LEVELS = 8
SEQ_TILE = 512
FFN_TILE = 512
FFN_SUBTILE = 256
FFN_CHUNK = 1408
SAMPLE_BLOCK = 4
CAST_STEPS = 8
W_IN_BLOCK = 512

F32 = jnp.float32
BF16 = jnp.bfloat16


def _rms(x, w):
    return x * lax.rsqrt(jnp.mean(x * x, axis=-1, keepdims=True) + EPS) * w


def _sigmoid(x):
    return 1.0 / (1.0 + jnp.exp2(x * -LOG2_E))


def _log2_sigmoid(x, scale):
    return (jnp.minimum(x, 0.0) - jnp.log(1.0 + jnp.exp2(jnp.abs(x) * -LOG2_E))) * (scale * LOG2_E)


def _dot(a, b):
    return jnp.dot(a, b, preferred_element_type=F32)


def _dot_nt(a, b):
    return lax.dot_general(a, b, (((1,), (1,)), ((), ())), preferred_element_type=F32)


def _column_broadcast(row):
    return jnp.broadcast_to(row, (LANES, LANES)).T


def _projections(hn, w_head_ref, w_tail_ref, w_gklr_ref, w_gk_ref, b_gk_ref):
    def proj(off, width):
        if off < N_HEAD:
            return _dot(hn, w_head_ref[:, off:off + width])
        return _dot(hn, w_tail_ref[:, off - N_HEAD:off - N_HEAD + width])

    gk_lr = _dot(hn, w_gklr_ref[...])
    gk = _dot(gk_lr.astype(BF16), w_gk_ref[...]) + b_gk_ref[...]
    return proj, _log2_sigmoid(gk, 1.0 / GATE_NORMALIZER)


def _mix_out(x, o_heads, g, gate_a, gate_b, y_a, w_gn, w_b_ref, w_o_ref, wn_post):
    normed = []
    for h in range(GLA_HEADS):
        o = o_heads[h]
        gh = g[:, h * GLA_DV:(h + 1) * GLA_DV]
        o = o * lax.rsqrt(jnp.mean(o * o, axis=-1, keepdims=True) + EPS) * w_gn
        normed.append((o * (gh * _sigmoid(gh))).astype(BF16))
    y_b = _dot(jnp.concatenate(normed, axis=1), w_b_ref[...])
    merged = _sigmoid(gate_a) * y_a + _sigmoid(gate_b) * y_b
    mix = _dot(merged.astype(BF16), w_o_ref[...])
    return x + _rms(mix, wn_post)


def _prompt_mixer_kernel(x_ref, lev_ref, tri_ref, wn_pre_ref, w_head_ref, w_tail_ref, w_gklr_ref,
                         w_gk_ref, b_gk_ref, w_conv_ref, w_a_ref, w_gn_ref, w_b_ref, w_o_ref,
                         wn_post_ref, h_ref, conv_ref, state_ref, dec_ref):
    @pl.when(pl.program_id(1) == 0)
    def _():
        conv_ref[...] = jnp.zeros_like(conv_ref)
        state_ref[...] = jnp.zeros_like(state_ref)

    for c in range(SEQ_TILE // GLA_CHUNK):
        rows = slice(c * GLA_CHUNK, (c + 1) * GLA_CHUNK)
        h_ref[0, rows] = _prompt_mixer_chunk(
            x_ref[0, rows], dec_ref.at[c], lev_ref, tri_ref, wn_pre_ref, w_head_ref, w_tail_ref,
            w_gklr_ref, w_gk_ref, b_gk_ref, w_conv_ref, w_a_ref, w_gn_ref, w_b_ref, w_o_ref,
            wn_post_ref, conv_ref, state_ref)


def _prompt_mixer_chunk(x, dec_ref, lev_ref, tri_ref, wn_pre_ref, w_head_ref, w_tail_ref, w_gklr_ref,
                        w_gk_ref, b_gk_ref, w_conv_ref, w_a_ref, w_gn_ref, w_b_ref, w_o_ref,
                        wn_post_ref, conv_ref, state_ref):
    t = GLA_CHUNK
    hn = _rms(x, wn_pre_ref[...]).astype(BF16)
    proj, logw2 = _projections(hn, w_head_ref, w_tail_ref, w_gklr_ref, w_gk_ref, b_gk_ref)

    u = proj(OFF_C, CONV_DIM) * proj(OFF_X, CONV_DIM)
    prev2 = conv_ref[0, 0, 0:1, :]
    prev1 = conv_ref[0, 0, 1:2, :]
    row = lax.broadcasted_iota(jnp.int32, (t, CONV_DIM), 0)
    u1 = jnp.where(row == 0, prev1, pltpu.roll(u, 1, 0))
    u2 = jnp.where(row == 0, prev2, jnp.where(row == 1, prev1, pltpu.roll(u, 2, 0)))
    wc = w_conv_ref[...]
    y_conv = wc[0:1] * u2 + wc[1:2] * u1 + wc[2:3] * u
    conv_ref[0, 0] = u[t - 2:t]
    y_a = _dot((proj(OFF_B, CONV_DIM) * y_conv).astype(BF16), w_a_ref[...])

    dec_ref[0] = logw2
    logw2 = dec_ref[0]
    hi = logw2.astype(BF16)
    lo = (logw2 - hi.astype(F32)).astype(BF16)
    cum = _dot(tri_ref[...], hi) + _dot(tri_ref[...], lo)
    dec_ref[1] = cum
    cum_last = cum[t - 1:t]

    q = proj(OFF_Q, QK_DIM) * (GLA_DK ** -0.5)
    k = proj(OFF_K, QK_DIM)
    v = proj(OFF_V, V_DIM).astype(BF16)

    half = t // 2
    lev = lev_ref[...]
    rowq = lax.broadcasted_iota(jnp.int32, (t, QK_DIM), 0)

    def level_log2_factor(i):
        m = 1 << i
        if i == 0:
            return jnp.where((rowq & 1) == 1, logw2, 0.0)
        if i == 1:
            up = pltpu.roll(logw2, t - 1, 0)
            dn = pltpu.roll(logw2, 1, 0)
            r4 = rowq & 3
            return jnp.where(r4 == 0, up, jnp.where(r4 == 1, 0.0, jnp.where(r4 == 2, logw2, logw2 + dn)))
        pieces = []
        for blk in range(t // (2 * m)):
            r = blk * 2 * m + m - 1
            pieces.append(jnp.broadcast_to(dec_ref[1, r:r + 1, :], (2 * m, QK_DIM)))
        return -jnp.abs(cum - jnp.concatenate(pieces, axis=0))

    def head(a, h):
        return a[:, h * GLA_DK:(h + 1) * GLA_DK]

    q_lv = [q.astype(BF16)]
    k_lv = [k.astype(BF16)]
    for i in range(N_LEVELS - 1):
        e = jnp.exp2(level_log2_factor(i))
        q_lv.append((q * e).astype(BF16))
        k_lv.append((k * e).astype(BF16))
    cum_mid = dec_ref[1, half - 1:half, :]
    q_top = (q[half:] * jnp.exp2(cum[half:] - cum_mid)).astype(BF16)
    k_top = (k[:half] * jnp.exp2(cum_mid - cum[:half])).astype(BF16)

    q_in = (q * jnp.exp2(cum)).astype(BF16)
    k_out = k * jnp.exp2(cum_last - cum)
    a_last = jnp.exp2(cum_last)

    o_heads = []
    for h in range(GLA_HEADS):
        vh = v[:, h * GLA_DV:(h + 1) * GLA_DV]
        diag = [0.0, 0.0]
        for i in range(N_LEVELS):
            s_i = _dot_nt(head(q_lv[i], h), head(k_lv[i], h))
            diag = [jnp.where(lev == i - 1, s_i[r0:r0 + half, r0:r0 + half], diag[j])
                    for j, r0 in enumerate((0, half))]
        diag = [d.astype(BF16) for d in diag]
        p_low = jnp.concatenate([_dot_nt(head(q_top, h), head(k_top, h)).astype(BF16), diag[1]], axis=1)
        s_old = state_ref[0, 0, h]
        o = jnp.concatenate([_dot(diag[0], vh[:half]), _dot(p_low, vh)], axis=0)
        o_heads.append(o + _dot(head(q_in, h), s_old.astype(BF16)))
        a_col = _column_broadcast(head(a_last, h))
        a_col = jnp.concatenate([a_col, a_col], axis=1)
        state_ref[0, 0, h] = a_col * s_old + _dot(head(k_out, h).T.astype(BF16), vh)

    return _mix_out(x, o_heads, proj(OFF_G, V_DIM), proj(OFF_GA, D_MODEL), proj(OFF_GB, D_MODEL),
                    y_a, w_gn_ref[...], w_b_ref, w_o_ref, wn_post_ref[...])


def _sample_proj_kernel(x_ref, cbuf_ref, wn_pre_ref, w_head_ref, w_tail_ref, w_gklr_ref, w_gk_ref,
                        b_gk_ref, w_conv_ref, w_a_ref,
                        conv_ref, q_ref, k_ref, a_ref, v_ref, ya_ref, g_ref, ga_ref, gb_ref):
    x = x_ref[...]
    hn = _rms(x, wn_pre_ref[...]).astype(BF16)
    proj, logw2 = _projections(hn, w_head_ref, w_tail_ref, w_gklr_ref, w_gk_ref, b_gk_ref)
    u = proj(OFF_C, CONV_DIM) * proj(OFF_X, CONV_DIM)
    buf0 = cbuf_ref[:, 0:CONV_DIM]
    buf1 = cbuf_ref[:, CONV_DIM:2 * CONV_DIM]
    wc = w_conv_ref[...]
    y_conv = wc[0:1] * buf0 + wc[1:2] * buf1 + wc[2:3] * u
    conv_ref[:, 0:CONV_DIM] = buf1
    conv_ref[:, CONV_DIM:2 * CONV_DIM] = u
    ya_ref[...] = _dot((proj(OFF_B, CONV_DIM) * y_conv).astype(BF16), w_a_ref[...])
    grouped = lambda a: a.reshape(a.shape[0] // SUBLANES, SUBLANES, a.shape[1])
    q_ref[...] = grouped(proj(OFF_Q, QK_DIM) * (GLA_DK ** -0.5))
    k_ref[...] = grouped(proj(OFF_K, QK_DIM))
    a_ref[...] = grouped(jnp.exp2(logw2))
    v_ref[...] = grouped(proj(OFF_V, V_DIM))
    g_ref[...] = proj(OFF_G, V_DIM)
    ga_ref[...] = proj(OFF_GA, D_MODEL)
    gb_ref[...] = proj(OFF_GB, D_MODEL)


def _sample_state_update(blk, st_ref, st_out_ref, q_ref, k_ref, a_ref, v_ref, o_scr):
    per_group = SUBLANES // SAMPLE_BLOCK
    group = blk // per_group
    first = (blk % per_group) * SAMPLE_BLOCK
    row_id = lax.broadcasted_iota(jnp.int32, (SUBLANES, V_DIM), 0)
    o_tile = o_scr[group]
    for n in range(SAMPLE_BLOCK):
        r = first + n
        q_row = q_ref[group, pl.ds(r, 1), :]
        k_row = k_ref[group, pl.ds(r, 1), :]
        a_row = a_ref[group, pl.ds(r, 1), :]
        v_row = v_ref[group, pl.ds(r, 1), :]
        o_parts = []
        for h in range(GLA_HEADS):
            sl = slice(h * GLA_DK, (h + 1) * GLA_DK)
            a_col = _column_broadcast(a_row[:, sl])
            k_col = _column_broadcast(k_row[:, sl])
            q_col = _column_broadcast(q_row[:, sl])
            for half in range(GLA_DV // LANES):
                c0 = half * LANES
                s_old = st_ref[n, h, :, c0:c0 + LANES]
                vv = v_row[:, h * GLA_DV + c0:h * GLA_DV + c0 + LANES]
                s_new = a_col * s_old + k_col * vv
                st_out_ref[n, h, :, c0:c0 + LANES] = s_new
                o_parts.append(jnp.sum(q_col * s_new, axis=0, keepdims=True))
        o_row = jnp.concatenate(o_parts, axis=1)
        o_tile = jnp.where(row_id == r, o_row, o_tile)
    o_scr[group] = o_tile


def _ffn_rows(h, p, wn_pre_ref, w_gate_ref, w_up_ref, w_down_ref, wn_post_ref, w_pp_ref, w_pg_ref,
              wn_ple_ref):
    f = _rms(h, wn_pre_ref[...]).astype(BF16)
    acc = None
    for c in range(D_FF // FFN_CHUNK):
        sl = slice(c * FFN_CHUNK, (c + 1) * FFN_CHUNK)
        gate = _dot(f, w_gate_ref[:, sl])
        up = _dot(f, w_up_ref[:, sl])
        act = (gate * _sigmoid(gate) * up).astype(BF16)
        part = _dot(act, w_down_ref[sl, :])
        acc = part if acc is None else acc + part
    h = h + _rms(acc, wn_post_ref[...])
    e = _dot(p.astype(BF16), w_pp_ref[...]) * _sigmoid(_dot(h.astype(BF16), w_pg_ref[...]))
    return h + _rms(e, wn_ple_ref[...])


def _ffn_kernel(h_ref, p_ref, wn_pre_ref, w_gate_ref, w_up_ref, w_down_ref, wn_post_ref,
                w_pp_ref, w_pg_ref, wn_ple_ref,
                st_ref, q_ref, k_ref, a_ref, v_ref, xs_ref, ps_ref, ya_ref, g_ref, ga_ref, gb_ref,
                w_gn_ref, w_b_ref, w_o_ref, wn_mix_post_ref,
                out_ref, st_out_ref, outs_ref, o_scr):
    step = pl.program_id(0)
    n_tiles = pl.num_programs(0) - 1
    ffn_refs = (wn_pre_ref, w_gate_ref, w_up_ref, w_down_ref, wn_post_ref, w_pp_ref, w_pg_ref, wn_ple_ref)

    @pl.when(step == 0)
    def _():
        o_scr[...] = jnp.zeros_like(o_scr)

    @pl.when(step < n_tiles)
    def _():
        for r in range(FFN_TILE // FFN_SUBTILE):
            rows = slice(r * FFN_SUBTILE, (r + 1) * FFN_SUBTILE)
            out_ref[rows, :] = _ffn_rows(h_ref[rows, :], p_ref[rows, :], *ffn_refs)
        _sample_state_update(step, st_ref, st_out_ref, q_ref, k_ref, a_ref, v_ref, o_scr)

    @pl.when(step == n_tiles)
    def _():
        o = o_scr[...].reshape(xs_ref.shape[0], V_DIM)
        o_heads = [o[:, h * GLA_DV:(h + 1) * GLA_DV] for h in range(GLA_HEADS)]
        hs = _mix_out(xs_ref[...], o_heads, g_ref[...], ga_ref[...], gb_ref[...], ya_ref[...],
                      w_gn_ref[...], w_b_ref, w_o_ref, wn_mix_post_ref[...])
        outs_ref[...] = _ffn_rows(hs, ps_ref[...], *ffn_refs)


def _cast_w_in_kernel(wt_ref, gk_ref, head_ref, tail_ref, gklr_ref):
    j = pl.program_id(0)
    blk = wt_ref[...].T.astype(BF16)

    @pl.when(j < N_HEAD // W_IN_BLOCK)
    def _():
        head_ref[...] = blk

    @pl.when(j >= N_HEAD // W_IN_BLOCK)
    def _():
        tail_ref[...] = blk

    @pl.when(j == 0)
    def _():
        rows = jnp.concatenate([gk_ref[...], jnp.zeros((LANES - GATE_RANK, D_MODEL), F32)], axis=0)
        gklr_ref[...] = rows.T.astype(BF16)


def _cast_w_in(w_in_t):
    n_head = N_HEAD // W_IN_BLOCK
    n_tail = N_TAIL // W_IN_BLOCK

    def src_row(j):
        row = jnp.where(j < n_head, j * W_IN_BLOCK, GKLR_START + GATE_RANK + (j - n_head) * W_IN_BLOCK)
        return pl.multiple_of(row, GATE_RANK)

    return pl.pallas_call(
        _cast_w_in_kernel,
        grid=(n_head + n_tail,),
        in_specs=[pl.BlockSpec((pl.Element(W_IN_BLOCK), pl.Element(D_MODEL)), lambda j: (src_row(j), 0)),
                  pl.BlockSpec((pl.Element(GATE_RANK), pl.Element(D_MODEL)), lambda j: (GKLR_START, 0))],
        out_specs=[pl.BlockSpec((D_MODEL, W_IN_BLOCK), lambda j: (0, jnp.minimum(j, n_head - 1))),
                   pl.BlockSpec((D_MODEL, W_IN_BLOCK), lambda j: (0, jnp.maximum(j - n_head, 0))),
                   pl.BlockSpec((D_MODEL, LANES), lambda j: (0, 0))],
        out_shape=[jax.ShapeDtypeStruct((D_MODEL, N_HEAD), BF16),
                   jax.ShapeDtypeStruct((D_MODEL, N_TAIL), BF16),
                   jax.ShapeDtypeStruct((D_MODEL, LANES), BF16)],
        compiler_params=pltpu.CompilerParams(
            dimension_semantics=("arbitrary",), vmem_limit_bytes=VMEM_LIMIT_BYTES),
        name="cast_w_in",
    )(w_in_t, w_in_t)


def _cast_weights_kernel(*refs):
    n = len(refs) // 2
    for src, dst in zip(refs[:n], refs[n:]):
        dst[...] = src[...].astype(BF16)


def _cast_weights(weights):
    def spec(shape):
        return pl.BlockSpec((shape[0] // CAST_STEPS, shape[1]), lambda i: (i, 0))

    return pl.pallas_call(
        _cast_weights_kernel,
        grid=(CAST_STEPS,),
        in_specs=[spec(w.shape) for w in weights],
        out_specs=[spec(w.shape) for w in weights],
        out_shape=[jax.ShapeDtypeStruct(w.shape, BF16) for w in weights],
        compiler_params=pltpu.CompilerParams(
            dimension_semantics=("arbitrary",), vmem_limit_bytes=VMEM_LIMIT_BYTES),
        name="cast_weights",
    )(*weights)


def _resident(shape):
    return pl.BlockSpec(shape, lambda *_: (0,) * len(shape), pipeline_mode=pl.Buffered(1))


def _level_map(t):
    idx = np.arange(t)
    xor = idx[:, None] ^ idx[None, :]
    lev = np.floor(np.log2(np.maximum(xor, 1))).astype(np.int32)
    lev = np.where(idx[:, None] > idx[None, :], lev, -2)
    lev = np.where(idx[:, None] == idx[None, :], -1, lev)
    return jnp.asarray(lev, dtype=jnp.int32)


def _mixer_weight_specs():
    return [
        _resident((1, D_MODEL)),
        _resident((D_MODEL, N_HEAD)),
        _resident((D_MODEL, N_TAIL)),
        _resident((D_MODEL, LANES)),
        _resident((LANES, QK_DIM)),
        _resident((1, QK_DIM)),
        _resident((3, CONV_DIM)),
        _resident((CONV_DIM, D_MODEL)),
        _resident((1, GLA_DV)),
        _resident((V_DIM, D_MODEL)),
        _resident((D_MODEL, D_MODEL)),
        _resident((1, D_MODEL)),
    ]


def _prompt_mixer(x, mixer_weights):
    b, s, _ = x.shape
    t = SEQ_TILE
    c = GLA_CHUNK
    tri = jnp.asarray(np.tril(np.ones((c, c), np.float32)), dtype=BF16)
    return pl.pallas_call(
        _prompt_mixer_kernel,
        grid=(b, s // t),
        in_specs=[pl.BlockSpec((1, t, D_MODEL), lambda i, j: (i, j, 0)),
                  _resident((c // 2, c // 2)), _resident((c, c))] + _mixer_weight_specs(),
        out_specs=[pl.BlockSpec((1, t, D_MODEL), lambda i, j: (i, j, 0)),
                   pl.BlockSpec((1, 1, 2, CONV_DIM), lambda i, j: (0, i, 0, 0)),
                   pl.BlockSpec((1, 1, GLA_HEADS, GLA_DK, GLA_DV), lambda i, j: (0, i, 0, 0, 0))],
        out_shape=[jax.ShapeDtypeStruct((b, s, D_MODEL), F32),
                   jax.ShapeDtypeStruct((1, b, 2, CONV_DIM), F32),
                   jax.ShapeDtypeStruct((1, b, GLA_HEADS, GLA_DK, GLA_DV), F32)],
        scratch_shapes=[pltpu.VMEM((t // c, 2, c, QK_DIM), F32)],
        compiler_params=pltpu.CompilerParams(
            dimension_semantics=("arbitrary", "arbitrary"), vmem_limit_bytes=VMEM_LIMIT_BYTES),
        name="prompt_mixer",
    )(x, _level_map(c // 2), tri, *mixer_weights)


def _sample_proj(x, conv_buf, proj_weights):
    n = x.shape[0]
    grouped = lambda w: (n // SUBLANES, SUBLANES, w)
    out_shapes = [(n, 2 * CONV_DIM), grouped(QK_DIM), grouped(QK_DIM), grouped(QK_DIM), grouped(V_DIM),
                  (n, D_MODEL), (n, V_DIM), (n, D_MODEL), (n, D_MODEL)]
    whole = lambda shape: pl.BlockSpec(shape, lambda i: (0,) * len(shape))
    return pl.pallas_call(
        _sample_proj_kernel,
        grid=(1,),
        in_specs=[_resident((n, D_MODEL)), _resident((n, 2 * CONV_DIM))] + _mixer_weight_specs()[:8],
        out_specs=[whole(s) for s in out_shapes],
        out_shape=[jax.ShapeDtypeStruct(s, F32) for s in out_shapes],
        compiler_params=pltpu.CompilerParams(
            dimension_semantics=("arbitrary",), vmem_limit_bytes=VMEM_LIMIT_BYTES),
        name="sample_proj",
    )(x, conv_buf, *proj_weights)


def _ffn_and_sample_state(h, p, ffn_weights, state, sample_proj, x_s, p_s, out_weights):
    rows = h.shape[0]
    n = x_s.shape[0]
    n_tiles = rows // FFN_TILE
    assert n_tiles * SAMPLE_BLOCK == n, "one sample state block per prompt row tile"
    q, k, a, v, y_a, g, gate_a, gate_b = sample_proj
    tile = lambda i: jnp.minimum(i, n_tiles - 1)
    state_spec = pl.BlockSpec((SAMPLE_BLOCK, GLA_HEADS, GLA_DK, GLA_DV), lambda i: (tile(i), 0, 0, 0))
    return pl.pallas_call(
        _ffn_kernel,
        grid=(n_tiles + 1,),
        in_specs=[pl.BlockSpec((FFN_TILE, D_MODEL), lambda i: (tile(i), 0)),
                  pl.BlockSpec((FFN_TILE, PLE_DIM), lambda i: (tile(i), 0)),
                  _resident((1, D_MODEL)), _resident((D_MODEL, D_FF)), _resident((D_MODEL, D_FF)),
                  _resident((D_FF, D_MODEL)), _resident((1, D_MODEL)), _resident((PLE_DIM, D_MODEL)),
                  _resident((D_MODEL, D_MODEL)), _resident((1, D_MODEL)),
                  state_spec, _resident(q.shape), _resident(k.shape), _resident(a.shape),
                  _resident(v.shape), _resident((n, D_MODEL)), _resident((n, PLE_DIM)),
                  _resident((n, D_MODEL)), _resident((n, V_DIM)), _resident((n, D_MODEL)),
                  _resident((n, D_MODEL)),
                  _resident((1, GLA_DV)), _resident((V_DIM, D_MODEL)), _resident((D_MODEL, D_MODEL)),
                  _resident((1, D_MODEL))],
        out_specs=[pl.BlockSpec((FFN_TILE, D_MODEL), lambda i: (tile(i), 0)),
                   state_spec,
                   pl.BlockSpec((n, D_MODEL), lambda i: (0, 0))],
        out_shape=[jax.ShapeDtypeStruct((rows, D_MODEL), F32),
                   jax.ShapeDtypeStruct(state.shape, F32),
                   jax.ShapeDtypeStruct((n, D_MODEL), F32)],
        scratch_shapes=[pltpu.VMEM((n // SUBLANES, SUBLANES, V_DIM), F32)],
        compiler_params=pltpu.CompilerParams(
            dimension_semantics=("arbitrary",), vmem_limit_bytes=VMEM_LIMIT_BYTES),
        name="ffn_ple",
    )(h, p, *ffn_weights, state, q, k, a, v, x_s, p_s, y_a, g, gate_a, gate_b, *out_weights)


def kernel(x_prompt, x_sample, state_conv, state_gla, p_prompt, p_sample, w_norm_mix_pre, w_in, w_conv, w_a_out, w_gk, b_gk, w_gla_norm, w_b_out, w_o, w_norm_mix_post, w_norm_ffn_pre, w_ffn_gate, w_ffn_up, w_ffn_down, w_norm_ffn_post, w_ple_proj, w_ple_gate, w_norm_ple_post):
    depth = w_in.shape[0]
    batch, seq, _ = x_prompt.shape
    n_dec = x_sample.shape[0]
    assert x_sample.shape[1] == 1, "the sample group carries one new token per sequence"
    assert seq % SEQ_TILE == 0 and (batch * seq) % FFN_TILE == 0 and n_dec % SAMPLE_BLOCK == 0

    hp = x_prompt
    hs = x_sample.reshape(n_dec, D_MODEL)
    conv_p, gla_p, conv_s, gla_s = [], [], [], []
    for i in range(depth):
        row = lambda w: w[i].reshape(1, -1)
        w_head, w_tail, w_gklr = _cast_w_in(jnp.swapaxes(w_in[i], 0, 1))
        w_a, w_b, w_o_b, w_gate, w_up, w_down, w_pp, w_pg = _cast_weights(
            (w_a_out[i], w_b_out[i], w_o[i], w_ffn_gate[i], w_ffn_up[i], w_ffn_down[i],
             w_ple_proj[i], w_ple_gate[i]))
        w_gk_pad = jnp.pad(w_gk[i], ((0, LANES - GATE_RANK), (0, 0))).astype(BF16)
        mixer_weights = (row(w_norm_mix_pre), w_head, w_tail, w_gklr, w_gk_pad, row(b_gk), w_conv[i],
                         w_a, row(w_gla_norm), w_b, w_o_b, row(w_norm_mix_post))
        ffn_weights = (row(w_norm_ffn_pre), w_gate, w_up, w_down, row(w_norm_ffn_post), w_pp, w_pg,
                       row(w_norm_ple_post))

        hp_mid, cbp, sp = _prompt_mixer(hp, mixer_weights)
        cbs, *sample_proj = _sample_proj(hs, state_conv[i].reshape(n_dec, 2 * CONV_DIM), mixer_weights[:8])
        hp, ss, hs = _ffn_and_sample_state(
            hp_mid.reshape(batch * seq, D_MODEL), p_prompt[i].reshape(batch * seq, PLE_DIM), ffn_weights,
            state_gla[i], sample_proj, hs, p_sample[i].reshape(n_dec, PLE_DIM), mixer_weights[8:])
        hp = hp.reshape(batch, seq, D_MODEL)

        conv_p.append(cbp[0]); gla_p.append(sp[0])
        conv_s.append(cbs.reshape(n_dec, 2, CONV_DIM)); gla_s.append(ss)
    return (hp, hs.reshape(n_dec, 1, D_MODEL), jnp.stack(conv_p), jnp.stack(gla_p),
            jnp.stack(conv_s), jnp.stack(gla_s))
```

```python
import functools

import numpy as np
import jax
import jax.numpy as jnp
from jax import lax
from jax.experimental import pallas as pl
from jax.experimental.pallas import tpu as pltpu

D_MODEL = 1024
CONV_DIM = D_MODEL
GLA_HEADS = 4
GLA_DK = 128
GLA_DV = 256
QK_DIM = GLA_HEADS * GLA_DK
V_DIM = GLA_HEADS * GLA_DV
GATE_RANK = 16
GATE_NORMALIZER = 16.0
D_FF = 2816
PLE_DIM = 256
EPS = 1e-6
LOG2_E = 1.4426950408889634

LANES = 128
SUBLANES = 8
BF16_ROWS = 16
VMEM_LIMIT_BYTES = 56 * 1024 * 1024

OFF_B, OFF_C, OFF_X = 0, 1024, 2048
OFF_Q, OFF_K, OFF_V, OFF_G = 3072, 3584, 4096, 5120
OFF_GA, OFF_GB = 6144, 7168
N_HEAD = 6144
N_TAIL = 2048
GKLR_START = 6144

GLA_CHUNK = 256
N_LEVELS = 8
SEQ_TILE = 512
FFN_TILE = 512
FFN_SUBTILE = 256
FFN_CHUNK = 1408
SAMPLE_BLOCK = 4
CAST_STEPS = 8
W_IN_BLOCK = 512

F32 = jnp.float32
BF16 = jnp.bfloat16


def _rms(x, w):
    return x * lax.rsqrt(jnp.mean(x * x, axis=-1, keepdims=True) + EPS) * w


def _sigmoid(x):
    return 1.0 / (1.0 + jnp.exp2(x * -LOG2_E))


def _log2_sigmoid(x, scale):
    return (jnp.minimum(x, 0.0) - jnp.log(1.0 + jnp.exp2(jnp.abs(x) * -LOG2_E))) * (scale * LOG2_E)


def _dot(a, b):
    return jnp.dot(a, b, preferred_element_type=F32)


def _dot_nt(a, b):
    return lax.dot_general(a, b, (((1,), (1,)), ((), ())), preferred_element_type=F32)


def _column_broadcast(row):
    return jnp.broadcast_to(row, (LANES, LANES)).T


def _projections(hn, w_head_ref, w_tail_ref, w_gklr_ref, w_gk_ref, b_gk_ref):
    def proj(off, width):
        if off < N_HEAD:
            return _dot(hn, w_head_ref[:, off:off + width])
        return _dot(hn, w_tail_ref[:, off - N_HEAD:off - N_HEAD + width])

    gk_lr = _dot(hn, w_gklr_ref[...])
    gk = _dot(gk_lr.astype(BF16), w_gk_ref[...]) + b_gk_ref[...]
    return proj, _log2_sigmoid(gk, 1.0 / GATE_NORMALIZER)


def _mix_out(x, o_heads, g, gate_a, gate_b, y_a, w_gn, w_b_ref, w_o_ref, wn_post):
    normed = []
    for h in range(GLA_HEADS):
        o = o_heads[h]
        gh = g[:, h * GLA_DV:(h + 1) * GLA_DV]
        o = o * lax.rsqrt(jnp.mean(o * o, axis=-1, keepdims=True) + EPS) * w_gn
        normed.append((o * (gh * _sigmoid(gh))).astype(BF16))
    y_b = _dot(jnp.concatenate(normed, axis=1), w_b_ref[...])
    merged = _sigmoid(gate_a) * y_a + _sigmoid(gate_b) * y_b
    mix = _dot(merged.astype(BF16), w_o_ref[...])
    return x + _rms(mix, wn_post)


def _prompt_mixer_kernel(x_ref, lev_ref, tri_ref, wn_pre_ref, w_head_ref, w_tail_ref, w_gklr_ref,
                         w_gk_ref, b_gk_ref, w_conv_ref, w_a_ref, w_gn_ref, w_b_ref, w_o_ref,
                         wn_post_ref, *rest):
    n_side = (len(rest) - 4) // 2
    side_src = rest[:n_side]
    h_ref, conv_ref, state_ref = rest[n_side:n_side + 3]
    side_dst = rest[n_side + 3:2 * n_side + 3]
    dec_ref = rest[-1]

    @pl.when(pl.program_id(1) == 0)
    def _():
        conv_ref[...] = jnp.zeros_like(conv_ref)
        state_ref[...] = jnp.zeros_like(state_ref)

    for src, dst in zip(side_src, side_dst):
        dst[...] = src[...].astype(BF16)

    for c in range(SEQ_TILE // GLA_CHUNK):
        rows = slice(c * GLA_CHUNK, (c + 1) * GLA_CHUNK)
        h_ref[0, rows] = _prompt_mixer_chunk(
            x_ref[0, rows], dec_ref.at[c], lev_ref, tri_ref, wn_pre_ref, w_head_ref, w_tail_ref,
            w_gklr_ref, w_gk_ref, b_gk_ref, w_conv_ref, w_a_ref, w_gn_ref, w_b_ref, w_o_ref,
            wn_post_ref, conv_ref, state_ref)


def _prompt_mixer_chunk(x, dec_ref, lev_ref, tri_ref, wn_pre_ref, w_head_ref, w_tail_ref, w_gklr_ref,
                        w_gk_ref, b_gk_ref, w_conv_ref, w_a_ref, w_gn_ref, w_b_ref, w_o_ref,
                        wn_post_ref, conv_ref, state_ref):
    t = GLA_CHUNK
    hn = _rms(x, wn_pre_ref[...]).astype(BF16)
    proj, logw2 = _projections(hn, w_head_ref, w_tail_ref, w_gklr_ref, w_gk_ref, b_gk_ref)

    u = proj(OFF_C, CONV_DIM) * proj(OFF_X, CONV_DIM)
    prev2 = conv_ref[0, 0, 0:1, :]
    prev1 = conv_ref[0, 0, 1:2, :]
    row = lax.broadcasted_iota(jnp.int32, (t, CONV_DIM), 0)
    u1 = jnp.where(row == 0, prev1, pltpu.roll(u, 1, 0))
    u2 = jnp.where(row == 0, prev2, jnp.where(row == 1, prev1, pltpu.roll(u, 2, 0)))
    wc = w_conv_ref[...]
    y_conv = wc[0:1] * u2 + wc[1:2] * u1 + wc[2:3] * u
    conv_ref[0, 0] = u[t - 2:t]
    y_a = _dot((proj(OFF_B, CONV_DIM) * y_conv).astype(BF16), w_a_ref[...])

    dec_ref[0] = logw2
    logw2 = dec_ref[0]
    hi = logw2.astype(BF16)
    lo = (logw2 - hi.astype(F32)).astype(BF16)
    cum = _dot(tri_ref[...], hi) + _dot(tri_ref[...], lo)
    dec_ref[1] = cum
    cum_last = cum[t - 1:t]

    q = proj(OFF_Q, QK_DIM) * (GLA_DK ** -0.5)
    k = proj(OFF_K, QK_DIM)
    v = proj(OFF_V, V_DIM).astype(BF16)

    half = t // 2
    lev = lev_ref[...]
    rowq = lax.broadcasted_iota(jnp.int32, (t, QK_DIM), 0)

    def level_log2_factor(i):
        m = 1 << i
        if i == 0:
            return jnp.where((rowq & 1) == 1, logw2, 0.0)
        if i == 1:
            up = pltpu.roll(logw2, t - 1, 0)
            dn = pltpu.roll(logw2, 1, 0)
            r4 = rowq & 3
            return jnp.where(r4 == 0, up, jnp.where(r4 == 1, 0.0, jnp.where(r4 == 2, logw2, logw2 + dn)))
        pieces = []
        for blk in range(t // (2 * m)):
            r = blk * 2 * m + m - 1
            pieces.append(jnp.broadcast_to(dec_ref[1, r:r + 1, :], (2 * m, QK_DIM)))
        return -jnp.abs(cum - jnp.concatenate(pieces, axis=0))

    def head(a, h):
        return a[:, h * GLA_DK:(h + 1) * GLA_DK]

    q_lv = [q.astype(BF16)]
    k_lv = [k.astype(BF16)]
    for i in range(N_LEVELS - 1):
        e = jnp.exp2(level_log2_factor(i))
        q_lv.append((q * e).astype(BF16))
        k_lv.append((k * e).astype(BF16))
    cum_mid = dec_ref[1, half - 1:half, :]
    q_top = (q[half:] * jnp.exp2(cum[half:] - cum_mid)).astype(BF16)
    k_top = (k[:half] * jnp.exp2(cum_mid - cum[:half])).astype(BF16)

    q_in = (q * jnp.exp2(cum)).astype(BF16)
    k_out = k * jnp.exp2(cum_last - cum)
    a_last = jnp.exp2(cum_last)

    o_heads = []
    for h in range(GLA_HEADS):
        vh = v[:, h * GLA_DV:(h + 1) * GLA_DV]
        diag = [0.0, 0.0]
        for i in range(N_LEVELS):
            s_i = _dot_nt(head(q_lv[i], h), head(k_lv[i], h))
            diag = [jnp.where(lev == i - 1, s_i[r0:r0 + half, r0:r0 + half], diag[j])
                    for j, r0 in enumerate((0, half))]
        diag = [d.astype(BF16) for d in diag]
        p_low = jnp.concatenate([_dot_nt(head(q_top, h), head(k_top, h)).astype(BF16), diag[1]], axis=1)
        s_old = state_ref[0, 0, h]
        o = jnp.concatenate([_dot(diag[0], vh[:half]), _dot(p_low, vh)], axis=0)
        o_heads.append(o + _dot(head(q_in, h), s_old.astype(BF16)))
        a_col = _column_broadcast(head(a_last, h))
        a_col = jnp.concatenate([a_col, a_col], axis=1)
        state_ref[0, 0, h] = a_col * s_old + _dot(head(k_out, h).T.astype(BF16), vh)

    return _mix_out(x, o_heads, proj(OFF_G, V_DIM), proj(OFF_GA, D_MODEL), proj(OFF_GB, D_MODEL),
                    y_a, w_gn_ref[...], w_b_ref, w_o_ref, wn_post_ref[...])


def _sample_proj_kernel(x_ref, cbuf_ref, wn_pre_ref, w_head_ref, w_tail_ref, w_gklr_ref, w_gk_ref,
                        b_gk_ref, w_conv_ref, w_a_ref,
                        conv_ref, q_ref, k_ref, a_ref, v_ref, ya_ref, g_ref, ga_ref, gb_ref):
    x = x_ref[...]
    hn = _rms(x, wn_pre_ref[...]).astype(BF16)
    proj, logw2 = _projections(hn, w_head_ref, w_tail_ref, w_gklr_ref, w_gk_ref, b_gk_ref)
    u = proj(OFF_C, CONV_DIM) * proj(OFF_X, CONV_DIM)
    buf0 = cbuf_ref[:, 0:CONV_DIM]
    buf1 = cbuf_ref[:, CONV_DIM:2 * CONV_DIM]
    wc = w_conv_ref[...]
    y_conv = wc[0:1] * buf0 + wc[1:2] * buf1 + wc[2:3] * u
    conv_ref[:, 0:CONV_DIM] = buf1
    conv_ref[:, CONV_DIM:2 * CONV_DIM] = u
    ya_ref[...] = _dot((proj(OFF_B, CONV_DIM) * y_conv).astype(BF16), w_a_ref[...])
    grouped = lambda a: a.reshape(a.shape[0] // SUBLANES, SUBLANES, a.shape[1])
    q_ref[...] = grouped(proj(OFF_Q, QK_DIM) * (GLA_DK ** -0.5))
    k_ref[...] = grouped(proj(OFF_K, QK_DIM))
    a_ref[...] = grouped(jnp.exp2(logw2))
    v_ref[...] = grouped(proj(OFF_V, V_DIM))
    g_ref[...] = proj(OFF_G, V_DIM)
    ga_ref[...] = proj(OFF_GA, D_MODEL)
    gb_ref[...] = proj(OFF_GB, D_MODEL)


def _sample_state_update(blk, st_ref, st_out_ref, q_ref, k_ref, a_ref, v_ref, o_scr):
    per_group = SUBLANES // SAMPLE_BLOCK
    group = blk // per_group
    first = (blk % per_group) * SAMPLE_BLOCK
    row_id = lax.broadcasted_iota(jnp.int32, (SUBLANES, V_DIM), 0)
    o_tile = o_scr[group]
    for n in range(SAMPLE_BLOCK):
        r = first + n
        q_row = q_ref[group, pl.ds(r, 1), :]
        k_row = k_ref[group, pl.ds(r, 1), :]
        a_row = a_ref[group, pl.ds(r, 1), :]
        v_row = v_ref[group, pl.ds(r, 1), :]
        o_parts = []
        for h in range(GLA_HEADS):
            sl = slice(h * GLA_DK, (h + 1) * GLA_DK)
            a_col = _column_broadcast(a_row[:, sl])
            k_col = _column_broadcast(k_row[:, sl])
            q_col = _column_broadcast(q_row[:, sl])
            for half in range(GLA_DV // LANES):
                c0 = half * LANES
                s_old = st_ref[n, h, :, c0:c0 + LANES]
                vv = v_row[:, h * GLA_DV + c0:h * GLA_DV + c0 + LANES]
                s_new = a_col * s_old + k_col * vv
                st_out_ref[n, h, :, c0:c0 + LANES] = s_new
                o_parts.append(jnp.sum(q_col * s_new, axis=0, keepdims=True))
        o_row = jnp.concatenate(o_parts, axis=1)
        o_tile = jnp.where(row_id == r, o_row, o_tile)
    o_scr[group] = o_tile


def _ffn_rows(h, p, wn_pre_ref, w_gate_ref, w_up_ref, w_down_ref, wn_post_ref, w_pp_ref, w_pg_ref,
              wn_ple_ref):
    f = _rms(h, wn_pre_ref[...]).astype(BF16)
    acc = None
    for c in range(D_FF // FFN_CHUNK):
        sl = slice(c * FFN_CHUNK, (c + 1) * FFN_CHUNK)
        gate = _dot(f, w_gate_ref[:, sl])
        up = _dot(f, w_up_ref[:, sl])
        act = (gate * _sigmoid(gate) * up).astype(BF16)
        part = _dot(act, w_down_ref[sl, :])
        acc = part if acc is None else acc + part
    h = h + _rms(acc, wn_post_ref[...])
    e = _dot(p.astype(BF16), w_pp_ref[...]) * _sigmoid(_dot(h.astype(BF16), w_pg_ref[...]))
    return h + _rms(e, wn_ple_ref[...])


def _ffn_kernel(h_ref, p_ref, wn_pre_ref, w_gate_ref, w_up_ref, w_down_ref, wn_post_ref,
                w_pp_ref, w_pg_ref, wn_ple_ref,
                st_ref, q_ref, k_ref, a_ref, v_ref, xs_ref, ps_ref, ya_ref, g_ref, ga_ref, gb_ref,
                w_gn_ref, w_b_ref, w_o_ref, wn_mix_post_ref,
                out_ref, st_out_ref, outs_ref, o_scr):
    step = pl.program_id(0)
    n_tiles = pl.num_programs(0) - 1
    ffn_refs = (wn_pre_ref, w_gate_ref, w_up_ref, w_down_ref, wn_post_ref, w_pp_ref, w_pg_ref, wn_ple_ref)

    @pl.when(step == 0)
    def _():
        o_scr[...] = jnp.zeros_like(o_scr)

    @pl.when(step < n_tiles)
    def _():
        for r in range(FFN_TILE // FFN_SUBTILE):
            rows = slice(r * FFN_SUBTILE, (r + 1) * FFN_SUBTILE)
            out_ref[rows, :] = _ffn_rows(h_ref[rows, :], p_ref[rows, :], *ffn_refs)
        _sample_state_update(step, st_ref, st_out_ref, q_ref, k_ref, a_ref, v_ref, o_scr)

    @pl.when(step == n_tiles)
    def _():
        o = o_scr[...].reshape(xs_ref.shape[0], V_DIM)
        o_heads = [o[:, h * GLA_DV:(h + 1) * GLA_DV] for h in range(GLA_HEADS)]
        hs = _mix_out(xs_ref[...], o_heads, g_ref[...], ga_ref[...], gb_ref[...], ya_ref[...],
                      w_gn_ref[...], w_b_ref, w_o_ref, wn_mix_post_ref[...])
        outs_ref[...] = _ffn_rows(hs, ps_ref[...], *ffn_refs)


def _cast_w_in_kernel(wt_ref, gk_ref, head_ref, tail_ref, gklr_ref):
    j = pl.program_id(0)
    blk = wt_ref[...].T.astype(BF16)

    @pl.when(j < N_HEAD // W_IN_BLOCK)
    def _():
        head_ref[...] = blk

    @pl.when(j >= N_HEAD // W_IN_BLOCK)
    def _():
        tail_ref[...] = blk

    @pl.when(j == 0)
    def _():
        rows = jnp.concatenate([gk_ref[...], jnp.zeros((LANES - GATE_RANK, D_MODEL), F32)], axis=0)
        gklr_ref[...] = rows.T.astype(BF16)


def _cast_w_in(w_in_t):
    n_head = N_HEAD // W_IN_BLOCK
    n_tail = N_TAIL // W_IN_BLOCK

    def src_row(j):
        row = jnp.where(j < n_head, j * W_IN_BLOCK, GKLR_START + GATE_RANK + (j - n_head) * W_IN_BLOCK)
        return pl.multiple_of(row, GATE_RANK)

    return pl.pallas_call(
        _cast_w_in_kernel,
        grid=(n_head + n_tail,),
        in_specs=[pl.BlockSpec((pl.Element(W_IN_BLOCK), pl.Element(D_MODEL)), lambda j: (src_row(j), 0)),
                  pl.BlockSpec((pl.Element(GATE_RANK), pl.Element(D_MODEL)), lambda j: (GKLR_START, 0))],
        out_specs=[pl.BlockSpec((D_MODEL, W_IN_BLOCK), lambda j: (0, jnp.minimum(j, n_head - 1))),
                   pl.BlockSpec((D_MODEL, W_IN_BLOCK), lambda j: (0, jnp.maximum(j - n_head, 0))),
                   pl.BlockSpec((D_MODEL, LANES), lambda j: (0, 0))],
        out_shape=[jax.ShapeDtypeStruct((D_MODEL, N_HEAD), BF16),
                   jax.ShapeDtypeStruct((D_MODEL, N_TAIL), BF16),
                   jax.ShapeDtypeStruct((D_MODEL, LANES), BF16)],
        compiler_params=pltpu.CompilerParams(
            dimension_semantics=("arbitrary",), vmem_limit_bytes=VMEM_LIMIT_BYTES),
        name="cast_w_in",
    )(w_in_t, w_in_t)


def _cast_weights_kernel(*refs):
    n = len(refs) // 2
    for src, dst in zip(refs[:n], refs[n:]):
        dst[...] = src[...].astype(BF16)


def _cast_weights(weights):
    def spec(shape):
        return pl.BlockSpec((shape[0] // CAST_STEPS, shape[1]), lambda i: (i, 0))

    return pl.pallas_call(
        _cast_weights_kernel,
        grid=(CAST_STEPS,),
        in_specs=[spec(w.shape) for w in weights],
        out_specs=[spec(w.shape) for w in weights],
        out_shape=[jax.ShapeDtypeStruct(w.shape, BF16) for w in weights],
        compiler_params=pltpu.CompilerParams(
            dimension_semantics=("arbitrary",), vmem_limit_bytes=VMEM_LIMIT_BYTES),
        name="cast_weights",
    )(*weights)


def _resident(shape):
    return pl.BlockSpec(shape, lambda *_: (0,) * len(shape), pipeline_mode=pl.Buffered(1))


def _level_map(t):
    idx = np.arange(t)
    xor = idx[:, None] ^ idx[None, :]
    lev = np.floor(np.log2(np.maximum(xor, 1))).astype(np.int32)
    lev = np.where(idx[:, None] > idx[None, :], lev, -2)
    lev = np.where(idx[:, None] == idx[None, :], -1, lev)
    return jnp.asarray(lev, dtype=jnp.int32)


def _mixer_weight_specs():
    return [
        _resident((1, D_MODEL)),
        _resident((D_MODEL, N_HEAD)),
        _resident((D_MODEL, N_TAIL)),
        _resident((D_MODEL, LANES)),
        _resident((LANES, QK_DIM)),
        _resident((1, QK_DIM)),
        _resident((3, CONV_DIM)),
        _resident((CONV_DIM, D_MODEL)),
        _resident((1, GLA_DV)),
        _resident((V_DIM, D_MODEL)),
        _resident((D_MODEL, D_MODEL)),
        _resident((1, D_MODEL)),
    ]


def _prompt_mixer(x, mixer_weights, later_weights):
    b, s, _ = x.shape
    t = SEQ_TILE
    c = GLA_CHUNK
    n_steps = b * (s // t)
    tri = jnp.asarray(np.tril(np.ones((c, c), np.float32)), dtype=BF16)

    def side_spec(shape):
        n_blocks = n_steps
        while shape[0] % (n_blocks * BF16_ROWS):
            n_blocks //= 2
        rep = n_steps // n_blocks
        return pl.BlockSpec((shape[0] // n_blocks, shape[1]),
                            lambda i, j: ((i * (s // t) + j) // rep, 0))

    side_specs = [side_spec(w.shape) for w in later_weights]
    return pl.pallas_call(
        _prompt_mixer_kernel,
        grid=(b, s // t),
        in_specs=[pl.BlockSpec((1, t, D_MODEL), lambda i, j: (i, j, 0)),
                  _resident((c // 2, c // 2)), _resident((c, c))] + _mixer_weight_specs() + side_specs,
        out_specs=[pl.BlockSpec((1, t, D_MODEL), lambda i, j: (i, j, 0)),
                   pl.BlockSpec((1, 1, 2, CONV_DIM), lambda i, j: (0, i, 0, 0)),
                   pl.BlockSpec((1, 1, GLA_HEADS, GLA_DK, GLA_DV), lambda i, j: (0, i, 0, 0, 0))]
                  + side_specs,
        out_shape=[jax.ShapeDtypeStruct((b, s, D_MODEL), F32),
                   jax.ShapeDtypeStruct((1, b, 2, CONV_DIM), F32),
                   jax.ShapeDtypeStruct((1, b, GLA_HEADS, GLA_DK, GLA_DV), F32)]
                  + [jax.ShapeDtypeStruct(w.shape, BF16) for w in later_weights],
        scratch_shapes=[pltpu.VMEM((t // c, 2, c, QK_DIM), F32)],
        compiler_params=pltpu.CompilerParams(
            dimension_semantics=("arbitrary", "arbitrary"), vmem_limit_bytes=VMEM_LIMIT_BYTES),
        name="prompt_mixer",
    )(x, _level_map(c // 2), tri, *mixer_weights, *later_weights)


def _sample_proj(x, conv_buf, proj_weights):
    n = x.shape[0]
    grouped = lambda w: (n // SUBLANES, SUBLANES, w)
    out_shapes = [(n, 2 * CONV_DIM), grouped(QK_DIM), grouped(QK_DIM), grouped(QK_DIM), grouped(V_DIM),
                  (n, D_MODEL), (n, V_DIM), (n, D_MODEL), (n, D_MODEL)]
    whole = lambda shape: pl.BlockSpec(shape, lambda i: (0,) * len(shape))
    return pl.pallas_call(
        _sample_proj_kernel,
        grid=(1,),
        in_specs=[_resident((n, D_MODEL)), _resident((n, 2 * CONV_DIM))] + _mixer_weight_specs()[:8],
        out_specs=[whole(s) for s in out_shapes],
        out_shape=[jax.ShapeDtypeStruct(s, F32) for s in out_shapes],
        compiler_params=pltpu.CompilerParams(
            dimension_semantics=("arbitrary",), vmem_limit_bytes=VMEM_LIMIT_BYTES),
        name="sample_proj",
    )(x, conv_buf, *proj_weights)


def _ffn_and_sample_state(h, p, ffn_weights, state, sample_proj, x_s, p_s, out_weights):
    rows = h.shape[0]
    n = x_s.shape[0]
    n_tiles = rows // FFN_TILE
    assert n_tiles * SAMPLE_BLOCK == n, "one sample state block per prompt row tile"
    q, k, a, v, y_a, g, gate_a, gate_b = sample_proj
    tile = lambda i: jnp.minimum(i, n_tiles - 1)
    state_spec = pl.BlockSpec((SAMPLE_BLOCK, GLA_HEADS, GLA_DK, GLA_DV), lambda i: (tile(i), 0, 0, 0))
    return pl.pallas_call(
        _ffn_kernel,
        grid=(n_tiles + 1,),
        in_specs=[pl.BlockSpec((FFN_TILE, D_MODEL), lambda i: (tile(i), 0)),
                  pl.BlockSpec((FFN_TILE, PLE_DIM), lambda i: (tile(i), 0)),
                  _resident((1, D_MODEL)), _resident((D_MODEL, D_FF)), _resident((D_MODEL, D_FF)),
                  _resident((D_FF, D_MODEL)), _resident((1, D_MODEL)), _resident((PLE_DIM, D_MODEL)),
                  _resident((D_MODEL, D_MODEL)), _resident((1, D_MODEL)),
                  state_spec, _resident(q.shape), _resident(k.shape), _resident(a.shape),
                  _resident(v.shape), _resident((n, D_MODEL)), _resident((n, PLE_DIM)),
                  _resident((n, D_MODEL)), _resident((n, V_DIM)), _resident((n, D_MODEL)),
                  _resident((n, D_MODEL)),
                  _resident((1, GLA_DV)), _resident((V_DIM, D_MODEL)), _resident((D_MODEL, D_MODEL)),
                  _resident((1, D_MODEL))],
        out_specs=[pl.BlockSpec((FFN_TILE, D_MODEL), lambda i: (tile(i), 0)),
                   state_spec,
                   pl.BlockSpec((n, D_MODEL), lambda i: (0, 0))],
        out_shape=[jax.ShapeDtypeStruct((rows, D_MODEL), F32),
                   jax.ShapeDtypeStruct(state.shape, F32),
                   jax.ShapeDtypeStruct((n, D_MODEL), F32)],
        scratch_shapes=[pltpu.VMEM((n // SUBLANES, SUBLANES, V_DIM), F32)],
        compiler_params=pltpu.CompilerParams(
            dimension_semantics=("arbitrary",), vmem_limit_bytes=VMEM_LIMIT_BYTES),
        name="ffn_ple",
    )(h, p, *ffn_weights, state, q, k, a, v, x_s, p_s, y_a, g, gate_a, gate_b, *out_weights)


def kernel(x_prompt, x_sample, state_conv, state_gla, p_prompt, p_sample, w_norm_mix_pre, w_in, w_conv, w_a_out, w_gk, b_gk, w_gla_norm, w_b_out, w_o, w_norm_mix_post, w_norm_ffn_pre, w_ffn_gate, w_ffn_up, w_ffn_down, w_norm_ffn_post, w_ple_proj, w_ple_gate, w_norm_ple_post):
    depth = w_in.shape[0]
    batch, seq, _ = x_prompt.shape
    n_dec = x_sample.shape[0]
    assert x_sample.shape[1] == 1, "the sample group carries one new token per sequence"
    assert seq % SEQ_TILE == 0 and (batch * seq) % FFN_TILE == 0 and n_dec % SAMPLE_BLOCK == 0

    hp = x_prompt
    hs = x_sample.reshape(n_dec, D_MODEL)
    conv_p, gla_p, conv_s, gla_s = [], [], [], []
    for i in range(depth):
        row = lambda w: w[i].reshape(1, -1)
        w_head, w_tail, w_gklr = _cast_w_in(jnp.swapaxes(w_in[i], 0, 1))
        w_a, w_b, w_o_b = _cast_weights((w_a_out[i], w_b_out[i], w_o[i]))
        w_gk_pad = jnp.pad(w_gk[i], ((0, LANES - GATE_RANK), (0, 0))).astype(BF16)
        mixer_weights = (row(w_norm_mix_pre), w_head, w_tail, w_gklr, w_gk_pad, row(b_gk), w_conv[i],
                         w_a, row(w_gla_norm), w_b, w_o_b, row(w_norm_mix_post))

        hp_mid, cbp, sp, w_gate, w_up, w_down, w_pp, w_pg = _prompt_mixer(
            hp, mixer_weights, (w_ffn_gate[i], w_ffn_up[i], w_ffn_down[i], w_ple_proj[i], w_ple_gate[i]))
        ffn_weights = (row(w_norm_ffn_pre), w_gate, w_up, w_down, row(w_norm_ffn_post), w_pp, w_pg,
                       row(w_norm_ple_post))
        cbs, *sample_proj = _sample_proj(hs, state_conv[i].reshape(n_dec, 2 * CONV_DIM), mixer_weights[:8])
        hp, ss, hs = _ffn_and_sample_state(
            hp_mid.reshape(batch * seq, D_MODEL), p_prompt[i].reshape(batch * seq, PLE_DIM), ffn_weights,
            state_gla[i], sample_proj, hs, p_sample[i].reshape(n_dec, PLE_DIM), mixer_weights[8:])
        hp = hp.reshape(batch, seq, D_MODEL)

        conv_p.append(cbp[0]); gla_p.append(sp[0])
        conv_s.append(cbs.reshape(n_dec, 2, CONV_DIM)); gla_s.append(ss)
    return (hp, hs.reshape(n_dec, 1, D_MODEL), jnp.stack(conv_p), jnp.stack(gla_p),
            jnp.stack(conv_s), jnp.stack(gla_s))
```

```python
import functools

import numpy as np
import jax
import jax.numpy as jnp
from jax import lax
from jax.experimental import pallas as pl
from jax.experimental.pallas import tpu as pltpu

D_MODEL = 1024
CONV_DIM = D_MODEL
GLA_HEADS = 4
GLA_DK = 128
GLA_DV = 256
QK_DIM = GLA_HEADS * GLA_DK
V_DIM = GLA_HEADS * GLA_DV
GATE_RANK = 16
GATE_NORMALIZER = 16.0
D_FF = 2816
PLE_DIM = 256
EPS = 1e-6
LOG2_E = 1.4426950408889634

LANES = 128
SUBLANES = 8
BF16_ROWS = 16
VMEM_LIMIT_BYTES = 56 * 1024 * 1024

OFF_B, OFF_C, OFF_X = 0, 1024, 2048
OFF_Q, OFF_K, OFF_V, OFF_G = 3072, 3584, 4096, 5120
OFF_GA, OFF_GB = 6144, 7168
N_HEAD = 6144
N_TAIL = 2048
GKLR_START = 6144

GLA_CHUNK = 256
N_LEVELS = 8
SEQ_TILE = 512
FFN_TILE = 512
FFN_SUBTILE = 256
FFN_CHUNK = 1408
SAMPLE_BLOCK = 4
W_IN_BLOCK = 1024

F32 = jnp.float32
BF16 = jnp.bfloat16


def _rms(x, w):
    return x * lax.rsqrt(jnp.mean(x * x, axis=-1, keepdims=True) + EPS) * w


def _sigmoid(x):
    return 1.0 / (1.0 + jnp.exp2(x * -LOG2_E))


def _log2_sigmoid(x, scale):
    return (jnp.minimum(x, 0.0) - jnp.log(1.0 + jnp.exp2(jnp.abs(x) * -LOG2_E))) * (scale * LOG2_E)


def _dot(a, b):
    return jnp.dot(a, b, preferred_element_type=F32)


def _dot_nt(a, b):
    return lax.dot_general(a, b, (((1,), (1,)), ((), ())), preferred_element_type=F32)


def _column_broadcast(row):
    return jnp.broadcast_to(row, (LANES, LANES)).T


def _projections(hn, w_head_ref, w_tail_ref, w_gklr_ref, w_gk_ref, b_gk_ref):
    def proj(off, width):
        if off < N_HEAD:
            return _dot(hn, w_head_ref[:, off:off + width])
        return _dot(hn, w_tail_ref[:, off - N_HEAD:off - N_HEAD + width])

    gk_lr = _dot(hn, w_gklr_ref[...])
    gk = _dot(gk_lr.astype(BF16), w_gk_ref[...]) + b_gk_ref[...]
    return proj, _log2_sigmoid(gk, 1.0 / GATE_NORMALIZER)


def _mix_out(x, o_heads, g, gate_a, gate_b, y_a, w_gn, w_b_ref, w_o_ref, wn_post):
    normed = []
    for h in range(GLA_HEADS):
        o = o_heads[h]
        gh = g[:, h * GLA_DV:(h + 1) * GLA_DV]
        o = o * lax.rsqrt(jnp.mean(o * o, axis=-1, keepdims=True) + EPS) * w_gn
        normed.append((o * (gh * _sigmoid(gh))).astype(BF16))
    y_b = _dot(jnp.concatenate(normed, axis=1), w_b_ref[...])
    merged = _sigmoid(gate_a) * y_a + _sigmoid(gate_b) * y_b
    mix = _dot(merged.astype(BF16), w_o_ref[...])
    return x + _rms(mix, wn_post)


def _prompt_mixer_kernel(x_ref, lev_ref, tri_ref, wn_pre_ref, w_head_ref, w_tail_ref, w_gklr_ref,
                         w_gk_ref, b_gk_ref, w_conv_ref, w_a_ref, w_gn_ref, w_b_ref, w_o_ref,
                         wn_post_ref, *rest):
    n_side = (len(rest) - 4) // 2
    side_src = rest[:n_side]
    h_ref, conv_ref, state_ref = rest[n_side:n_side + 3]
    side_dst = rest[n_side + 3:2 * n_side + 3]
    dec_ref = rest[-1]

    @pl.when(pl.program_id(1) == 0)
    def _():
        conv_ref[...] = jnp.zeros_like(conv_ref)
        state_ref[...] = jnp.zeros_like(state_ref)

    for src, dst in zip(side_src, side_dst):
        dst[...] = src[...].astype(BF16)

    for c in range(SEQ_TILE // GLA_CHUNK):
        rows = slice(c * GLA_CHUNK, (c + 1) * GLA_CHUNK)
        h_ref[0, rows] = _prompt_mixer_chunk(
            x_ref[0, rows], dec_ref.at[c], lev_ref, tri_ref, wn_pre_ref, w_head_ref, w_tail_ref,
            w_gklr_ref, w_gk_ref, b_gk_ref, w_conv_ref, w_a_ref, w_gn_ref, w_b_ref, w_o_ref,
            wn_post_ref, conv_ref, state_ref)


def _prompt_mixer_chunk(x, dec_ref, lev_ref, tri_ref, wn_pre_ref, w_head_ref, w_tail_ref, w_gklr_ref,
                        w_gk_ref, b_gk_ref, w_conv_ref, w_a_ref, w_gn_ref, w_b_ref, w_o_ref,
                        wn_post_ref, conv_ref, state_ref):
    t = GLA_CHUNK
    hn = _rms(x, wn_pre_ref[...]).astype(BF16)
    proj, logw2 = _projections(hn, w_head_ref, w_tail_ref, w_gklr_ref, w_gk_ref, b_gk_ref)

    u = proj(OFF_C, CONV_DIM) * proj(OFF_X, CONV_DIM)
    prev2 = conv_ref[0, 0, 0:1, :]
    prev1 = conv_ref[0, 0, 1:2, :]
    row = lax.broadcasted_iota(jnp.int32, (t, CONV_DIM), 0)
    u1 = jnp.where(row == 0, prev1, pltpu.roll(u, 1, 0))
    u2 = jnp.where(row == 0, prev2, jnp.where(row == 1, prev1, pltpu.roll(u, 2, 0)))
    wc = w_conv_ref[...]
    y_conv = wc[0:1] * u2 + wc[1:2] * u1 + wc[2:3] * u
    conv_ref[0, 0] = u[t - 2:t]
    y_a = _dot((proj(OFF_B, CONV_DIM) * y_conv).astype(BF16), w_a_ref[...])

    dec_ref[0] = logw2
    logw2 = dec_ref[0]
    hi = logw2.astype(BF16)
    lo = (logw2 - hi.astype(F32)).astype(BF16)
    cum = _dot(tri_ref[...], hi) + _dot(tri_ref[...], lo)
    dec_ref[1] = cum
    cum_last = cum[t - 1:t]

    q = proj(OFF_Q, QK_DIM) * (GLA_DK ** -0.5)
    k = proj(OFF_K, QK_DIM)
    v = proj(OFF_V, V_DIM).astype(BF16)

    half = t // 2
    lev = lev_ref[...]
    rowq = lax.broadcasted_iota(jnp.int32, (t, QK_DIM), 0)

    def level_log2_factor(i):
        m = 1 << i
        if i == 0:
            return jnp.where((rowq & 1) == 1, logw2, 0.0)
        if i == 1:
            up = pltpu.roll(logw2, t - 1, 0)
            dn = pltpu.roll(logw2, 1, 0)
            r4 = rowq & 3
            return jnp.where(r4 == 0, up, jnp.where(r4 == 1, 0.0, jnp.where(r4 == 2, logw2, logw2 + dn)))
        pieces = []
        for blk in range(t // (2 * m)):
            r = blk * 2 * m + m - 1
            pieces.append(jnp.broadcast_to(dec_ref[1, r:r + 1, :], (2 * m, QK_DIM)))
        return -jnp.abs(cum - jnp.concatenate(pieces, axis=0))

    def head(a, h):
        return a[:, h * GLA_DK:(h + 1) * GLA_DK]

    q_lv = [q.astype(BF16)]
    k_lv = [k.astype(BF16)]
    for i in range(N_LEVELS - 1):
        e = jnp.exp2(level_log2_factor(i))
        q_lv.append((q * e).astype(BF16))
        k_lv.append((k * e).astype(BF16))
    cum_mid = dec_ref[1, half - 1:half, :]
    q_top = (q[half:] * jnp.exp2(cum[half:] - cum_mid)).astype(BF16)
    k_top = (k[:half] * jnp.exp2(cum_mid - cum[:half])).astype(BF16)

    q_in = (q * jnp.exp2(cum)).astype(BF16)
    k_out = k * jnp.exp2(cum_last - cum)
    a_last = jnp.exp2(cum_last)

    o_heads = []
    for h in range(GLA_HEADS):
        vh = v[:, h * GLA_DV:(h + 1) * GLA_DV]
        diag = [0.0, 0.0]
        for i in range(N_LEVELS):
            s_i = _dot_nt(head(q_lv[i], h), head(k_lv[i], h))
            diag = [jnp.where(lev == i - 1, s_i[r0:r0 + half, r0:r0 + half], diag[j])
                    for j, r0 in enumerate((0, half))]
        diag = [d.astype(BF16) for d in diag]
        p_low = jnp.concatenate([_dot_nt(head(q_top, h), head(k_top, h)).astype(BF16), diag[1]], axis=1)
        s_old = state_ref[0, 0, h]
        o = jnp.concatenate([_dot(diag[0], vh[:half]), _dot(p_low, vh)], axis=0)
        o_heads.append(o + _dot(head(q_in, h), s_old.astype(BF16)))
        a_col = _column_broadcast(head(a_last, h))
        a_col = jnp.concatenate([a_col, a_col], axis=1)
        state_ref[0, 0, h] = a_col * s_old + _dot(head(k_out, h).T.astype(BF16), vh)

    return _mix_out(x, o_heads, proj(OFF_G, V_DIM), proj(OFF_GA, D_MODEL), proj(OFF_GB, D_MODEL),
                    y_a, w_gn_ref[...], w_b_ref, w_o_ref, wn_post_ref[...])


def _sample_proj_kernel(x_ref, cbuf_ref, wn_pre_ref, w_head_ref, w_tail_ref, w_gklr_ref, w_gk_ref,
                        b_gk_ref, w_conv_ref, w_a_ref,
                        ps_ref, conv_ref, q_ref, k_ref, a_ref, v_ref, ya_ref, g_ref, ga_ref, gb_ref,
                        xs2_ref, ps2_ref):
    x = x_ref[:, 0, :]
    xs2_ref[...] = x
    ps2_ref[...] = ps_ref[:, 0, :]
    hn = _rms(x, wn_pre_ref[...]).astype(BF16)
    proj, logw2 = _projections(hn, w_head_ref, w_tail_ref, w_gklr_ref, w_gk_ref, b_gk_ref)
    u = proj(OFF_C, CONV_DIM) * proj(OFF_X, CONV_DIM)
    buf0 = cbuf_ref[:, 0, :]
    buf1 = cbuf_ref[:, 1, :]
    wc = w_conv_ref[...]
    y_conv = wc[0:1] * buf0 + wc[1:2] * buf1 + wc[2:3] * u
    conv_ref[:, 0, :] = buf1
    conv_ref[:, 1, :] = u
    ya_ref[...] = _dot((proj(OFF_B, CONV_DIM) * y_conv).astype(BF16), w_a_ref[...])
    grouped = lambda a: a.reshape(a.shape[0] // SUBLANES, SUBLANES, a.shape[1])
    q_ref[...] = grouped(proj(OFF_Q, QK_DIM) * (GLA_DK ** -0.5))
    k_ref[...] = grouped(proj(OFF_K, QK_DIM))
    a_ref[...] = grouped(jnp.exp2(logw2))
    v_ref[...] = grouped(proj(OFF_V, V_DIM))
    g_ref[...] = proj(OFF_G, V_DIM)
    ga_ref[...] = proj(OFF_GA, D_MODEL)
    gb_ref[...] = proj(OFF_GB, D_MODEL)


def _sample_state_update(blk, st_ref, st_out_ref, q_ref, k_ref, a_ref, v_ref, o_scr):
    per_group = SUBLANES // SAMPLE_BLOCK
    group = blk // per_group
    first = (blk % per_group) * SAMPLE_BLOCK
    row_id = lax.broadcasted_iota(jnp.int32, (SUBLANES, V_DIM), 0)
    o_tile = o_scr[group]
    for n in range(SAMPLE_BLOCK):
        r = first + n
        q_row = q_ref[group, pl.ds(r, 1), :]
        k_row = k_ref[group, pl.ds(r, 1), :]
        a_row = a_ref[group, pl.ds(r, 1), :]
        v_row = v_ref[group, pl.ds(r, 1), :]
        o_parts = []
        for h in range(GLA_HEADS):
            sl = slice(h * GLA_DK, (h + 1) * GLA_DK)
            a_col = _column_broadcast(a_row[:, sl])
            k_col = _column_broadcast(k_row[:, sl])
            q_col = _column_broadcast(q_row[:, sl])
            for half in range(GLA_DV // LANES):
                c0 = half * LANES
                s_old = st_ref[n, h, :, c0:c0 + LANES]
                vv = v_row[:, h * GLA_DV + c0:h * GLA_DV + c0 + LANES]
                s_new = a_col * s_old + k_col * vv
                st_out_ref[n, h, :, c0:c0 + LANES] = s_new
                o_parts.append(jnp.sum(q_col * s_new, axis=0, keepdims=True))
        o_row = jnp.concatenate(o_parts, axis=1)
        o_tile = jnp.where(row_id == r, o_row, o_tile)
    o_scr[group] = o_tile


def _ffn_rows(h, p, wn_pre_ref, w_gate_ref, w_up_ref, w_down_ref, wn_post_ref, w_pp_ref, w_pg_ref,
              wn_ple_ref):
    f = _rms(h, wn_pre_ref[...]).astype(BF16)
    acc = None
    for c in range(D_FF // FFN_CHUNK):
        sl = slice(c * FFN_CHUNK, (c + 1) * FFN_CHUNK)
        gate = _dot(f, w_gate_ref[:, sl])
        up = _dot(f, w_up_ref[:, sl])
        act = (gate * _sigmoid(gate) * up).astype(BF16)
        part = _dot(act, w_down_ref[sl, :])
        acc = part if acc is None else acc + part
    h = h + _rms(acc, wn_post_ref[...])
    e = _dot(p.astype(BF16), w_pp_ref[...]) * _sigmoid(_dot(h.astype(BF16), w_pg_ref[...]))
    return h + _rms(e, wn_ple_ref[...])


def _ffn_kernel(h_ref, p_ref, wn_pre_ref, w_gate_ref, w_up_ref, w_down_ref, wn_post_ref,
                w_pp_ref, w_pg_ref, wn_ple_ref,
                st_ref, q_ref, k_ref, a_ref, v_ref, xs_ref, ps_ref, ya_ref, g_ref, ga_ref, gb_ref,
                w_gn_ref, w_b_ref, w_o_ref, wn_mix_post_ref,
                out_ref, st_out_ref, outs_ref, o_scr):
    step = pl.program_id(0)
    n_tiles = pl.num_programs(0) - 1
    ffn_refs = (wn_pre_ref, w_gate_ref, w_up_ref, w_down_ref, wn_post_ref, w_pp_ref, w_pg_ref, wn_ple_ref)

    @pl.when(step == 0)
    def _():
        o_scr[...] = jnp.zeros_like(o_scr)

    @pl.when(step < n_tiles)
    def _():
        for r in range(FFN_TILE // FFN_SUBTILE):
            rows = slice(r * FFN_SUBTILE, (r + 1) * FFN_SUBTILE)
            out_ref[rows, :] = _ffn_rows(h_ref[rows, :], p_ref[rows, :], *ffn_refs)
        _sample_state_update(step, st_ref, st_out_ref, q_ref, k_ref, a_ref, v_ref, o_scr)

    @pl.when(step == n_tiles)
    def _():
        o = o_scr[...].reshape(xs_ref.shape[0], V_DIM)
        o_heads = [o[:, h * GLA_DV:(h + 1) * GLA_DV] for h in range(GLA_HEADS)]
        hs = _mix_out(xs_ref[...], o_heads, g_ref[...], ga_ref[...], gb_ref[...], ya_ref[...],
                      w_gn_ref[...], w_b_ref, w_o_ref, wn_mix_post_ref[...])
        outs_ref[...] = _ffn_rows(hs, ps_ref[...], *ffn_refs)


def _cast_w_in_kernel(wt_ref, gk_ref, *refs):
    n = (len(refs) - 3) // 2
    head_ref, tail_ref, gklr_ref = refs[n:n + 3]
    for src, dst in zip(refs[:n], refs[n + 3:]):
        dst[...] = src[...].astype(BF16)
    j = pl.program_id(0)
    blk = wt_ref[...].T.astype(BF16)

    @pl.when(j < N_HEAD // W_IN_BLOCK)
    def _():
        head_ref[...] = blk

    @pl.when(j >= N_HEAD // W_IN_BLOCK)
    def _():
        tail_ref[...] = blk

    @pl.when(j == 0)
    def _():
        rows = jnp.concatenate([gk_ref[...], jnp.zeros((LANES - GATE_RANK, D_MODEL), F32)], axis=0)
        gklr_ref[...] = rows.T.astype(BF16)


def _cast_w_in(w_in_t, others):
    n_head = N_HEAD // W_IN_BLOCK
    n_tail = N_TAIL // W_IN_BLOCK
    other_specs = [pl.BlockSpec((w.shape[0] // (n_head + n_tail), w.shape[1]), lambda j: (j, 0))
                   for w in others]

    def src_row(j):
        row = jnp.where(j < n_head, j * W_IN_BLOCK, GKLR_START + GATE_RANK + (j - n_head) * W_IN_BLOCK)
        return pl.multiple_of(row, GATE_RANK)

    return pl.pallas_call(
        _cast_w_in_kernel,
        grid=(n_head + n_tail,),
        in_specs=[pl.BlockSpec((pl.Element(W_IN_BLOCK), pl.Element(D_MODEL)), lambda j: (src_row(j), 0)),
                  pl.BlockSpec((pl.Element(GATE_RANK), pl.Element(D_MODEL)), lambda j: (GKLR_START, 0))]
                 + other_specs,
        out_specs=[pl.BlockSpec((D_MODEL, W_IN_BLOCK), lambda j: (0, jnp.minimum(j, n_head - 1))),
                   pl.BlockSpec((D_MODEL, W_IN_BLOCK), lambda j: (0, jnp.maximum(j - n_head, 0))),
                   pl.BlockSpec((D_MODEL, LANES), lambda j: (0, 0))] + other_specs,
        out_shape=[jax.ShapeDtypeStruct((D_MODEL, N_HEAD), BF16),
                   jax.ShapeDtypeStruct((D_MODEL, N_TAIL), BF16),
                   jax.ShapeDtypeStruct((D_MODEL, LANES), BF16)]
                  + [jax.ShapeDtypeStruct(w.shape, BF16) for w in others],
        compiler_params=pltpu.CompilerParams(
            dimension_semantics=("arbitrary",), vmem_limit_bytes=VMEM_LIMIT_BYTES),
        name="cast_weights",
    )(w_in_t, w_in_t, *others)


def _resident(shape):
    return pl.BlockSpec(shape, lambda *_: (0,) * len(shape), pipeline_mode=pl.Buffered(1))


def _level_map(t):
    idx = np.arange(t)
    xor = idx[:, None] ^ idx[None, :]
    lev = np.floor(np.log2(np.maximum(xor, 1))).astype(np.int32)
    lev = np.where(idx[:, None] > idx[None, :], lev, -2)
    lev = np.where(idx[:, None] == idx[None, :], -1, lev)
    return jnp.asarray(lev, dtype=jnp.int32)


def _mixer_weight_specs():
    return [
        _resident((1, D_MODEL)),
        _resident((D_MODEL, N_HEAD)),
        _resident((D_MODEL, N_TAIL)),
        _resident((D_MODEL, LANES)),
        _resident((LANES, QK_DIM)),
        _resident((1, QK_DIM)),
        _resident((3, CONV_DIM)),
        _resident((CONV_DIM, D_MODEL)),
        _resident((1, GLA_DV)),
        _resident((V_DIM, D_MODEL)),
        _resident((D_MODEL, D_MODEL)),
        _resident((1, D_MODEL)),
    ]


def _prompt_mixer(x, mixer_weights, later_weights):
    b, s, _ = x.shape
    t = SEQ_TILE
    c = GLA_CHUNK
    n_steps = b * (s // t)
    tri = jnp.asarray(np.tril(np.ones((c, c), np.float32)), dtype=BF16)

    def side_spec(shape):
        n_blocks = n_steps
        while shape[0] % (n_blocks * BF16_ROWS):
            n_blocks //= 2
        rep = n_steps // n_blocks
        return pl.BlockSpec((shape[0] // n_blocks, shape[1]),
                            lambda i, j: ((i * (s // t) + j) // rep, 0))

    side_specs = [side_spec(w.shape) for w in later_weights]
    return pl.pallas_call(
        _prompt_mixer_kernel,
        grid=(b, s // t),
        in_specs=[pl.BlockSpec((1, t, D_MODEL), lambda i, j: (i, j, 0)),
                  _resident((c // 2, c // 2)), _resident((c, c))] + _mixer_weight_specs() + side_specs,
        out_specs=[pl.BlockSpec((1, t, D_MODEL), lambda i, j: (i, j, 0)),
                   pl.BlockSpec((1, 1, 2, CONV_DIM), lambda i, j: (0, i, 0, 0)),
                   pl.BlockSpec((1, 1, GLA_HEADS, GLA_DK, GLA_DV), lambda i, j: (0, i, 0, 0, 0))]
                  + side_specs,
        out_shape=[jax.ShapeDtypeStruct((b, s, D_MODEL), F32),
                   jax.ShapeDtypeStruct((1, b, 2, CONV_DIM), F32),
                   jax.ShapeDtypeStruct((1, b, GLA_HEADS, GLA_DK, GLA_DV), F32)]
                  + [jax.ShapeDtypeStruct(w.shape, BF16) for w in later_weights],
        scratch_shapes=[pltpu.VMEM((t // c, 2, c, QK_DIM), F32)],
        compiler_params=pltpu.CompilerParams(
            dimension_semantics=("arbitrary", "arbitrary"), vmem_limit_bytes=VMEM_LIMIT_BYTES),
        name="prompt_mixer",
    )(x, _level_map(c // 2), tri, *mixer_weights, *later_weights)


def _sample_proj(x, conv_buf, p, proj_weights):
    n = x.shape[0]
    grouped = lambda w: (n // SUBLANES, SUBLANES, w)
    out_shapes = [(n, 2, CONV_DIM), grouped(QK_DIM), grouped(QK_DIM), grouped(QK_DIM), grouped(V_DIM),
                  (n, D_MODEL), (n, V_DIM), (n, D_MODEL), (n, D_MODEL), (n, D_MODEL), (n, PLE_DIM)]
    whole = lambda shape: pl.BlockSpec(shape, lambda i: (0,) * len(shape))
    return pl.pallas_call(
        _sample_proj_kernel,
        grid=(1,),
        in_specs=[_resident((n, 1, D_MODEL)), _resident((n, 2, CONV_DIM))] + _mixer_weight_specs()[:8]
                 + [_resident((n, 1, PLE_DIM))],
        out_specs=[whole(s) for s in out_shapes],
        out_shape=[jax.ShapeDtypeStruct(s, F32) for s in out_shapes],
        compiler_params=pltpu.CompilerParams(
            dimension_semantics=("arbitrary",), vmem_limit_bytes=VMEM_LIMIT_BYTES),
        name="sample_proj",
    )(x, conv_buf, *proj_weights, p)


def _ffn_and_sample_state(h, p, ffn_weights, state, sample_proj, x_s, p_s, out_weights):
    rows = h.shape[0]
    n = x_s.shape[0]
    n_tiles = rows // FFN_TILE
    assert n_tiles * SAMPLE_BLOCK == n, "one sample state block per prompt row tile"
    q, k, a, v, y_a, g, gate_a, gate_b = sample_proj
    tile = lambda i: jnp.minimum(i, n_tiles - 1)
    state_spec = pl.BlockSpec((SAMPLE_BLOCK, GLA_HEADS, GLA_DK, GLA_DV), lambda i: (tile(i), 0, 0, 0))
    return pl.pallas_call(
        _ffn_kernel,
        grid=(n_tiles + 1,),
        in_specs=[pl.BlockSpec((FFN_TILE, D_MODEL), lambda i: (tile(i), 0)),
                  pl.BlockSpec((FFN_TILE, PLE_DIM), lambda i: (tile(i), 0)),
                  _resident((1, D_MODEL)), _resident((D_MODEL, D_FF)), _resident((D_MODEL, D_FF)),
                  _resident((D_FF, D_MODEL)), _resident((1, D_MODEL)), _resident((PLE_DIM, D_MODEL)),
                  _resident((D_MODEL, D_MODEL)), _resident((1, D_MODEL)),
                  state_spec, _resident(q.shape), _resident(k.shape), _resident(a.shape),
                  _resident(v.shape), _resident((n, D_MODEL)), _resident((n, PLE_DIM)),
                  _resident((n, D_MODEL)), _resident((n, V_DIM)), _resident((n, D_MODEL)),
                  _resident((n, D_MODEL)),
                  _resident((1, GLA_DV)), _resident((V_DIM, D_MODEL)), _resident((D_MODEL, D_MODEL)),
                  _resident((1, D_MODEL))],
        out_specs=[pl.BlockSpec((FFN_TILE, D_MODEL), lambda i: (tile(i), 0)),
                   state_spec,
                   pl.BlockSpec((n, D_MODEL), lambda i: (0, 0))],
        out_shape=[jax.ShapeDtypeStruct((rows, D_MODEL), F32),
                   jax.ShapeDtypeStruct(state.shape, F32),
                   jax.ShapeDtypeStruct((n, D_MODEL), F32)],
        scratch_shapes=[pltpu.VMEM((n // SUBLANES, SUBLANES, V_DIM), F32)],
        compiler_params=pltpu.CompilerParams(
            dimension_semantics=("arbitrary",), vmem_limit_bytes=VMEM_LIMIT_BYTES),
        name="ffn_ple",
    )(h, p, *ffn_weights, state, q, k, a, v, x_s, p_s, y_a, g, gate_a, gate_b, *out_weights)


def kernel(x_prompt, x_sample, state_conv, state_gla, p_prompt, p_sample, w_norm_mix_pre, w_in, w_conv, w_a_out, w_gk, b_gk, w_gla_norm, w_b_out, w_o, w_norm_mix_post, w_norm_ffn_pre, w_ffn_gate, w_ffn_up, w_ffn_down, w_norm_ffn_post, w_ple_proj, w_ple_gate, w_norm_ple_post):
    depth = w_in.shape[0]
    batch, seq, _ = x_prompt.shape
    n_dec = x_sample.shape[0]
    assert x_sample.shape[1] == 1, "the sample group carries one new token per sequence"
    assert seq % SEQ_TILE == 0 and (batch * seq) % FFN_TILE == 0 and n_dec % SAMPLE_BLOCK == 0

    hp = x_prompt
    hs = x_sample
    conv_p, gla_p, conv_s, gla_s = [], [], [], []
    for i in range(depth):
        row = lambda w: w[i].reshape(1, -1)
        w_head, w_tail, w_gklr, w_a, w_b, w_o_b = _cast_w_in(
            jnp.swapaxes(w_in[i], 0, 1), (w_a_out[i], w_b_out[i], w_o[i]))
        w_gk_pad = jnp.pad(w_gk[i], ((0, LANES - GATE_RANK), (0, 0))).astype(BF16)
        mixer_weights = (row(w_norm_mix_pre), w_head, w_tail, w_gklr, w_gk_pad, row(b_gk), w_conv[i],
                         w_a, row(w_gla_norm), w_b, w_o_b, row(w_norm_mix_post))

        hp_mid, cbp, sp, w_gate, w_up, w_down, w_pp, w_pg = _prompt_mixer(
            hp, mixer_weights, (w_ffn_gate[i], w_ffn_up[i], w_ffn_down[i], w_ple_proj[i], w_ple_gate[i]))
        ffn_weights = (row(w_norm_ffn_pre), w_gate, w_up, w_down, row(w_norm_ffn_post), w_pp, w_pg,
                       row(w_norm_ple_post))
        cbs, *sample_proj, xs2d, ps2d = _sample_proj(hs, state_conv[i], p_sample[i], mixer_weights[:8])
        hp, ss, hs = _ffn_and_sample_state(
            hp_mid.reshape(batch * seq, D_MODEL), p_prompt[i].reshape(batch * seq, PLE_DIM), ffn_weights,
            state_gla[i], sample_proj, xs2d, ps2d, mixer_weights[8:])
        hp = hp.reshape(batch, seq, D_MODEL)
        hs = hs.reshape(n_dec, 1, D_MODEL)

        conv_p.append(cbp[0]); gla_p.append(sp[0])
        conv_s.append(cbs); gla_s.append(ss)
    return (hp, hs, jnp.stack(conv_p), jnp.stack(gla_p),
            jnp.stack(conv_s), jnp.stack(gla_s))
```

```python
import functools

import numpy as np
import jax
import jax.numpy as jnp
from jax import lax
from jax.experimental import pallas as pl
from jax.experimental.pallas import tpu as pltpu

D_MODEL = 1024
CONV_DIM = D_MODEL
GLA_HEADS = 4
GLA_DK = 128
GLA_DV = 256
QK_DIM = GLA_HEADS * GLA_DK
V_DIM = GLA_HEADS * GLA_DV
GATE_RANK = 16
GATE_NORMALIZER = 16.0
D_FF = 2816
PLE_DIM = 256
EPS = 1e-6
LOG2_E = 1.4426950408889634

LANES = 128
SUBLANES = 8
BF16_ROWS = 16
VMEM_LIMIT_BYTES = 56 * 1024 * 1024

OFF_B, OFF_C, OFF_X = 0, 1024, 2048
OFF_Q, OFF_K, OFF_V, OFF_G = 3072, 3584, 4096, 5120
OFF_GA, OFF_GB = 6144, 7168
N_HEAD = 6144
N_TAIL = 2048
GKLR_START = 6144

GLA_CHUNK = 256
N_LEVELS = 8
SEQ_TILE = 512
FFN_TILE = 512
FFN_SUBTILE = 512
FFN_CHUNK = 1408
SAMPLE_BLOCK = 4
W_IN_BLOCK = 1024

F32 = jnp.float32
BF16 = jnp.bfloat16


def _rms(x, w):
    return x * lax.rsqrt(jnp.mean(x * x, axis=-1, keepdims=True) + EPS) * w


def _sigmoid(x):
    return 1.0 / (1.0 + jnp.exp2(x * -LOG2_E))


def _log2_sigmoid(x, scale):
    return (jnp.minimum(x, 0.0) - jnp.log(1.0 + jnp.exp2(jnp.abs(x) * -LOG2_E))) * (scale * LOG2_E)


def _dot(a, b):
    return jnp.dot(a, b, preferred_element_type=F32)


def _dot_nt(a, b):
    return lax.dot_general(a, b, (((1,), (1,)), ((), ())), preferred_element_type=F32)


def _column_broadcast(row):
    return jnp.broadcast_to(row, (LANES, LANES)).T


def _projections(hn, w_head_ref, w_tail_ref, w_gklr_ref, w_gk_ref, b_gk_ref):
    def proj(off, width):
        if off < N_HEAD:
            return _dot(hn, w_head_ref[:, off:off + width])
        return _dot(hn, w_tail_ref[:, off - N_HEAD:off - N_HEAD + width])

    gk_lr = _dot(hn, w_gklr_ref[...])
    gk = _dot(gk_lr.astype(BF16), w_gk_ref[...]) + b_gk_ref[...]
    return proj, _log2_sigmoid(gk, 1.0 / GATE_NORMALIZER)


def _mix_out(x, o_heads, g, gate_a, gate_b, y_a, w_gn, w_b_ref, w_o_ref, wn_post):
    normed = []
    for h in range(GLA_HEADS):
        o = o_heads[h]
        gh = g[:, h * GLA_DV:(h + 1) * GLA_DV]
        o = o * lax.rsqrt(jnp.mean(o * o, axis=-1, keepdims=True) + EPS) * w_gn
        normed.append((o * (gh * _sigmoid(gh))).astype(BF16))
    y_b = _dot(jnp.concatenate(normed, axis=1), w_b_ref[...])
    merged = _sigmoid(gate_a) * y_a + _sigmoid(gate_b) * y_b
    mix = _dot(merged.astype(BF16), w_o_ref[...])
    return x + _rms(mix, wn_post)


def _prompt_mixer_kernel(x_ref, lev_ref, tri_ref, wn_pre_ref, w_head_ref, w_tail_ref, w_gklr_ref,
                         w_gk_ref, b_gk_ref, w_conv_ref, w_a_ref, w_gn_ref, w_b_ref, w_o_ref,
                         wn_post_ref, *rest):
    n_side = (len(rest) - 4) // 2
    side_src = rest[:n_side]
    h_ref, conv_ref, state_ref = rest[n_side:n_side + 3]
    side_dst = rest[n_side + 3:2 * n_side + 3]
    dec_ref = rest[-1]

    @pl.when(pl.program_id(1) == 0)
    def _():
        conv_ref[...] = jnp.zeros_like(conv_ref)
        state_ref[...] = jnp.zeros_like(state_ref)

    for src, dst in zip(side_src, side_dst):
        dst[...] = src[...].astype(BF16)

    for c in range(SEQ_TILE // GLA_CHUNK):
        rows = slice(c * GLA_CHUNK, (c + 1) * GLA_CHUNK)
        h_ref[0, rows] = _prompt_mixer_chunk(
            x_ref[0, rows], dec_ref.at[c], lev_ref, tri_ref, wn_pre_ref, w_head_ref, w_tail_ref,
            w_gklr_ref, w_gk_ref, b_gk_ref, w_conv_ref, w_a_ref, w_gn_ref, w_b_ref, w_o_ref,
            wn_post_ref, conv_ref, state_ref)


def _prompt_mixer_chunk(x, dec_ref, lev_ref, tri_ref, wn_pre_ref, w_head_ref, w_tail_ref, w_gklr_ref,
                        w_gk_ref, b_gk_ref, w_conv_ref, w_a_ref, w_gn_ref, w_b_ref, w_o_ref,
                        wn_post_ref, conv_ref, state_ref):
    t = GLA_CHUNK
    hn = _rms(x, wn_pre_ref[...]).astype(BF16)
    proj, logw2 = _projections(hn, w_head_ref, w_tail_ref, w_gklr_ref, w_gk_ref, b_gk_ref)

    u = proj(OFF_C, CONV_DIM) * proj(OFF_X, CONV_DIM)
    prev2 = conv_ref[0, 0, 0:1, :]
    prev1 = conv_ref[0, 0, 1:2, :]
    row = lax.broadcasted_iota(jnp.int32, (t, CONV_DIM), 0)
    u1 = jnp.where(row == 0, prev1, pltpu.roll(u, 1, 0))
    u2 = jnp.where(row == 0, prev2, jnp.where(row == 1, prev1, pltpu.roll(u, 2, 0)))
    wc = w_conv_ref[...]
    y_conv = wc[0:1] * u2 + wc[1:2] * u1 + wc[2:3] * u
    conv_ref[0, 0] = u[t - 2:t]
    y_a = _dot((proj(OFF_B, CONV_DIM) * y_conv).astype(BF16), w_a_ref[...])

    dec_ref[0] = logw2
    logw2 = dec_ref[0]
    hi = logw2.astype(BF16)
    lo = (logw2 - hi.astype(F32)).astype(BF16)
    cum = _dot(tri_ref[...], hi) + _dot(tri_ref[...], lo)
    dec_ref[1] = cum
    cum_last = cum[t - 1:t]

    q = proj(OFF_Q, QK_DIM) * (GLA_DK ** -0.5)
    k = proj(OFF_K, QK_DIM)
    v = proj(OFF_V, V_DIM).astype(BF16)

    half = t // 2
    lev = lev_ref[...]
    rowq = lax.broadcasted_iota(jnp.int32, (t, QK_DIM), 0)

    def level_log2_factor(i):
        m = 1 << i
        if i == 0:
            return jnp.where((rowq & 1) == 1, logw2, 0.0)
        if i == 1:
            up = pltpu.roll(logw2, t - 1, 0)
            dn = pltpu.roll(logw2, 1, 0)
            r4 = rowq & 3
            return jnp.where(r4 == 0, up, jnp.where(r4 == 1, 0.0, jnp.where(r4 == 2, logw2, logw2 + dn)))
        pieces = []
        for blk in range(t // (2 * m)):
            r = blk * 2 * m + m - 1
            pieces.append(jnp.broadcast_to(dec_ref[1, r:r + 1, :], (2 * m, QK_DIM)))
        return -jnp.abs(cum - jnp.concatenate(pieces, axis=0))

    def head(a, h):
        return a[:, h * GLA_DK:(h + 1) * GLA_DK]

    q_lv = [q.astype(BF16)]
    k_lv = [k.astype(BF16)]
    for i in range(N_LEVELS - 1):
        e = jnp.exp2(level_log2_factor(i))
        q_lv.append((q * e).astype(BF16))
        k_lv.append((k * e).astype(BF16))
    cum_mid = dec_ref[1, half - 1:half, :]
    q_top = (q[half:] * jnp.exp2(cum[half:] - cum_mid)).astype(BF16)
    k_top = (k[:half] * jnp.exp2(cum_mid - cum[:half])).astype(BF16)

    q_in = (q * jnp.exp2(cum)).astype(BF16)
    k_out = k * jnp.exp2(cum_last - cum)
    a_last = jnp.exp2(cum_last)

    o_heads = []
    for h in range(GLA_HEADS):
        vh = v[:, h * GLA_DV:(h + 1) * GLA_DV]
        diag = [0.0, 0.0]
        for i in range(N_LEVELS):
            s_i = _dot_nt(head(q_lv[i], h), head(k_lv[i], h))
            diag = [jnp.where(lev == i - 1, s_i[r0:r0 + half, r0:r0 + half], diag[j])
                    for j, r0 in enumerate((0, half))]
        diag = [d.astype(BF16) for d in diag]
        p_low = jnp.concatenate([_dot_nt(head(q_top, h), head(k_top, h)).astype(BF16), diag[1]], axis=1)
        s_old = state_ref[0, 0, h]
        o = jnp.concatenate([_dot(diag[0], vh[:half]), _dot(p_low, vh)], axis=0)
        o_heads.append(o + _dot(head(q_in, h), s_old.astype(BF16)))
        a_col = _column_broadcast(head(a_last, h))
        a_col = jnp.concatenate([a_col, a_col], axis=1)
        state_ref[0, 0, h] = a_col * s_old + _dot(head(k_out, h).T.astype(BF16), vh)

    return _mix_out(x, o_heads, proj(OFF_G, V_DIM), proj(OFF_GA, D_MODEL), proj(OFF_GB, D_MODEL),
                    y_a, w_gn_ref[...], w_b_ref, w_o_ref, wn_post_ref[...])


def _sample_proj_kernel(x_ref, cbuf_ref, wn_pre_ref, w_head_ref, w_tail_ref, w_gklr_ref, w_gk_ref,
                        b_gk_ref, w_conv_ref, w_a_ref,
                        ps_ref, conv_ref, q_ref, k_ref, a_ref, v_ref, ya_ref, g_ref, ga_ref, gb_ref,
                        xs2_ref, ps2_ref):
    x = x_ref[:, 0, :]
    xs2_ref[...] = x
    ps2_ref[...] = ps_ref[:, 0, :]
    hn = _rms(x, wn_pre_ref[...]).astype(BF16)
    proj, logw2 = _projections(hn, w_head_ref, w_tail_ref, w_gklr_ref, w_gk_ref, b_gk_ref)
    u = proj(OFF_C, CONV_DIM) * proj(OFF_X, CONV_DIM)
    buf0 = cbuf_ref[:, 0, :]
    buf1 = cbuf_ref[:, 1, :]
    wc = w_conv_ref[...]
    y_conv = wc[0:1] * buf0 + wc[1:2] * buf1 + wc[2:3] * u
    conv_ref[:, 0, :] = buf1
    conv_ref[:, 1, :] = u
    ya_ref[...] = _dot((proj(OFF_B, CONV_DIM) * y_conv).astype(BF16), w_a_ref[...])
    grouped = lambda a: a.reshape(a.shape[0] // SUBLANES, SUBLANES, a.shape[1])
    q_ref[...] = grouped(proj(OFF_Q, QK_DIM) * (GLA_DK ** -0.5))
    k_ref[...] = grouped(proj(OFF_K, QK_DIM))
    a_ref[...] = grouped(jnp.exp2(logw2))
    v_ref[...] = grouped(proj(OFF_V, V_DIM))
    g_ref[...] = proj(OFF_G, V_DIM)
    ga_ref[...] = proj(OFF_GA, D_MODEL)
    gb_ref[...] = proj(OFF_GB, D_MODEL)


def _sample_state_update(blk, st_ref, st_out_ref, q_ref, k_ref, a_ref, v_ref, o_scr):
    per_group = SUBLANES // SAMPLE_BLOCK
    group = blk // per_group
    first = (blk % per_group) * SAMPLE_BLOCK
    row_id = lax.broadcasted_iota(jnp.int32, (SUBLANES, V_DIM), 0)
    o_tile = o_scr[group]
    for n in range(SAMPLE_BLOCK):
        r = first + n
        q_row = q_ref[group, pl.ds(r, 1), :]
        k_row = k_ref[group, pl.ds(r, 1), :]
        a_row = a_ref[group, pl.ds(r, 1), :]
        v_row = v_ref[group, pl.ds(r, 1), :]
        o_parts = []
        for h in range(GLA_HEADS):
            sl = slice(h * GLA_DK, (h + 1) * GLA_DK)
            a_col = _column_broadcast(a_row[:, sl])
            k_col = _column_broadcast(k_row[:, sl])
            q_col = _column_broadcast(q_row[:, sl])
            for half in range(GLA_DV // LANES):
                c0 = half * LANES
                s_old = st_ref[n, h, :, c0:c0 + LANES]
                vv = v_row[:, h * GLA_DV + c0:h * GLA_DV + c0 + LANES]
                s_new = a_col * s_old + k_col * vv
                st_out_ref[n, h, :, c0:c0 + LANES] = s_new
                o_parts.append(jnp.sum(q_col * s_new, axis=0, keepdims=True))
        o_row = jnp.concatenate(o_parts, axis=1)
        o_tile = jnp.where(row_id == r, o_row, o_tile)
    o_scr[group] = o_tile


def _ffn_rows(h, p, wn_pre_ref, w_gate_ref, w_up_ref, w_down_ref, wn_post_ref, w_pp_ref, w_pg_ref,
              wn_ple_ref):
    f = _rms(h, wn_pre_ref[...]).astype(BF16)
    acc = None
    for c in range(D_FF // FFN_CHUNK):
        sl = slice(c * FFN_CHUNK, (c + 1) * FFN_CHUNK)
        gate = _dot(f, w_gate_ref[:, sl])
        up = _dot(f, w_up_ref[:, sl])
        act = (gate * _sigmoid(gate) * up).astype(BF16)
        part = _dot(act, w_down_ref[sl, :])
        acc = part if acc is None else acc + part
    h = h + _rms(acc, wn_post_ref[...])
    e = _dot(p.astype(BF16), w_pp_ref[...]) * _sigmoid(_dot(h.astype(BF16), w_pg_ref[...]))
    return h + _rms(e, wn_ple_ref[...])


def _ffn_kernel(h_ref, p_ref, wn_pre_ref, w_gate_ref, w_up_ref, w_down_ref, wn_post_ref,
                w_pp_ref, w_pg_ref, wn_ple_ref,
                st_ref, q_ref, k_ref, a_ref, v_ref, xs_ref, ps_ref, ya_ref, g_ref, ga_ref, gb_ref,
                w_gn_ref, w_b_ref, w_o_ref, wn_mix_post_ref,
                out_ref, st_out_ref, outs_ref, o_scr):
    step = pl.program_id(0)
    n_tiles = pl.num_programs(0) - 1
    ffn_refs = (wn_pre_ref, w_gate_ref, w_up_ref, w_down_ref, wn_post_ref, w_pp_ref, w_pg_ref, wn_ple_ref)

    @pl.when(step == 0)
    def _():
        o_scr[...] = jnp.zeros_like(o_scr)

    @pl.when(step < n_tiles)
    def _():
        for r in range(FFN_TILE // FFN_SUBTILE):
            rows = slice(r * FFN_SUBTILE, (r + 1) * FFN_SUBTILE)
            out_ref[rows, :] = _ffn_rows(h_ref[rows, :], p_ref[rows, :], *ffn_refs)
        _sample_state_update(step, st_ref, st_out_ref, q_ref, k_ref, a_ref, v_ref, o_scr)

    @pl.when(step == n_tiles)
    def _():
        o = o_scr[...].reshape(xs_ref.shape[0], V_DIM)
        o_heads = [o[:, h * GLA_DV:(h + 1) * GLA_DV] for h in range(GLA_HEADS)]
        hs = _mix_out(xs_ref[...], o_heads, g_ref[...], ga_ref[...], gb_ref[...], ya_ref[...],
                      w_gn_ref[...], w_b_ref, w_o_ref, wn_mix_post_ref[...])
        outs_ref[...] = _ffn_rows(hs, ps_ref[...], *ffn_refs)


def _cast_w_in_kernel(wt_ref, gk_ref, *refs):
    n = (len(refs) - 3) // 2
    head_ref, tail_ref, gklr_ref = refs[n:n + 3]
    for src, dst in zip(refs[:n], refs[n + 3:]):
        dst[...] = src[...].astype(BF16)
    j = pl.program_id(0)
    blk = wt_ref[...].T.astype(BF16)

    @pl.when(j < N_HEAD // W_IN_BLOCK)
    def _():
        head_ref[...] = blk

    @pl.when(j >= N_HEAD // W_IN_BLOCK)
    def _():
        tail_ref[...] = blk

    @pl.when(j == 0)
    def _():
        rows = jnp.concatenate([gk_ref[...], jnp.zeros((LANES - GATE_RANK, D_MODEL), F32)], axis=0)
        gklr_ref[...] = rows.T.astype(BF16)


def _cast_w_in(w_in_t, others):
    n_head = N_HEAD // W_IN_BLOCK
    n_tail = N_TAIL // W_IN_BLOCK
    other_specs = [pl.BlockSpec((w.shape[0] // (n_head + n_tail), w.shape[1]), lambda j: (j, 0))
                   for w in others]

    def src_row(j):
        row = jnp.where(j < n_head, j * W_IN_BLOCK, GKLR_START + GATE_RANK + (j - n_head) * W_IN_BLOCK)
        return pl.multiple_of(row, GATE_RANK)

    return pl.pallas_call(
        _cast_w_in_kernel,
        grid=(n_head + n_tail,),
        in_specs=[pl.BlockSpec((pl.Element(W_IN_BLOCK), pl.Element(D_MODEL)), lambda j: (src_row(j), 0)),
                  pl.BlockSpec((pl.Element(GATE_RANK), pl.Element(D_MODEL)), lambda j: (GKLR_START, 0))]
                 + other_specs,
        out_specs=[pl.BlockSpec((D_MODEL, W_IN_BLOCK), lambda j: (0, jnp.minimum(j, n_head - 1))),
                   pl.BlockSpec((D_MODEL, W_IN_BLOCK), lambda j: (0, jnp.maximum(j - n_head, 0))),
                   pl.BlockSpec((D_MODEL, LANES), lambda j: (0, 0))] + other_specs,
        out_shape=[jax.ShapeDtypeStruct((D_MODEL, N_HEAD), BF16),
                   jax.ShapeDtypeStruct((D_MODEL, N_TAIL), BF16),
                   jax.ShapeDtypeStruct((D_MODEL, LANES), BF16)]
                  + [jax.ShapeDtypeStruct(w.shape, BF16) for w in others],
        compiler_params=pltpu.CompilerParams(
            dimension_semantics=("arbitrary",), vmem_limit_bytes=VMEM_LIMIT_BYTES),
        name="cast_weights",
    )(w_in_t, w_in_t, *others)


def _resident(shape):
    return pl.BlockSpec(shape, lambda *_: (0,) * len(shape), pipeline_mode=pl.Buffered(1))


def _level_map(t):
    idx = np.arange(t)
    xor = idx[:, None] ^ idx[None, :]
    lev = np.floor(np.log2(np.maximum(xor, 1))).astype(np.int32)
    lev = np.where(idx[:, None] > idx[None, :], lev, -2)
    lev = np.where(idx[:, None] == idx[None, :], -1, lev)
    return jnp.asarray(lev, dtype=jnp.int32)


def _mixer_weight_specs():
    return [
        _resident((1, D_MODEL)),
        _resident((D_MODEL, N_HEAD)),
        _resident((D_MODEL, N_TAIL)),
        _resident((D_MODEL, LANES)),
        _resident((LANES, QK_DIM)),
        _resident((1, QK_DIM)),
        _resident((3, CONV_DIM)),
        _resident((CONV_DIM, D_MODEL)),
        _resident((1, GLA_DV)),
        _resident((V_DIM, D_MODEL)),
        _resident((D_MODEL, D_MODEL)),
        _resident((1, D_MODEL)),
    ]


def _prompt_mixer(x, mixer_weights, later_weights):
    b, s, _ = x.shape
    t = SEQ_TILE
    c = GLA_CHUNK
    n_steps = b * (s // t)
    tri = jnp.asarray(np.tril(np.ones((c, c), np.float32)), dtype=BF16)

    def side_spec(shape):
        n_blocks = n_steps
        while shape[0] % (n_blocks * BF16_ROWS):
            n_blocks //= 2
        rep = n_steps // n_blocks
        return pl.BlockSpec((shape[0] // n_blocks, shape[1]),
                            lambda i, j: ((i * (s // t) + j) // rep, 0))

    side_specs = [side_spec(w.shape) for w in later_weights]
    return pl.pallas_call(
        _prompt_mixer_kernel,
        grid=(b, s // t),
        in_specs=[pl.BlockSpec((1, t, D_MODEL), lambda i, j: (i, j, 0)),
                  _resident((c // 2, c // 2)), _resident((c, c))] + _mixer_weight_specs() + side_specs,
        out_specs=[pl.BlockSpec((1, t, D_MODEL), lambda i, j: (i, j, 0)),
                   pl.BlockSpec((1, 1, 2, CONV_DIM), lambda i, j: (0, i, 0, 0)),
                   pl.BlockSpec((1, 1, GLA_HEADS, GLA_DK, GLA_DV), lambda i, j: (0, i, 0, 0, 0))]
                  + side_specs,
        out_shape=[jax.ShapeDtypeStruct((b, s, D_MODEL), F32),
                   jax.ShapeDtypeStruct((1, b, 2, CONV_DIM), F32),
                   jax.ShapeDtypeStruct((1, b, GLA_HEADS, GLA_DK, GLA_DV), F32)]
                  + [jax.ShapeDtypeStruct(w.shape, BF16) for w in later_weights],
        scratch_shapes=[pltpu.VMEM((t // c, 2, c, QK_DIM), F32)],
        compiler_params=pltpu.CompilerParams(
            dimension_semantics=("arbitrary", "arbitrary"), vmem_limit_bytes=VMEM_LIMIT_BYTES),
        name="prompt_mixer",
    )(x, _level_map(c // 2), tri, *mixer_weights, *later_weights)


def _sample_proj(x, conv_buf, p, proj_weights):
    n = x.shape[0]
    grouped = lambda w: (n // SUBLANES, SUBLANES, w)
    out_shapes = [(n, 2, CONV_DIM), grouped(QK_DIM), grouped(QK_DIM), grouped(QK_DIM), grouped(V_DIM),
                  (n, D_MODEL), (n, V_DIM), (n, D_MODEL), (n, D_MODEL), (n, D_MODEL), (n, PLE_DIM)]
    whole = lambda shape: pl.BlockSpec(shape, lambda i: (0,) * len(shape))
    return pl.pallas_call(
        _sample_proj_kernel,
        grid=(1,),
        in_specs=[_resident((n, 1, D_MODEL)), _resident((n, 2, CONV_DIM))] + _mixer_weight_specs()[:8]
                 + [_resident((n, 1, PLE_DIM))],
        out_specs=[whole(s) for s in out_shapes],
        out_shape=[jax.ShapeDtypeStruct(s, F32) for s in out_shapes],
        compiler_params=pltpu.CompilerParams(
            dimension_semantics=("arbitrary",), vmem_limit_bytes=VMEM_LIMIT_BYTES),
        name="sample_proj",
    )(x, conv_buf, *proj_weights, p)


def _ffn_and_sample_state(h, p, ffn_weights, state, sample_proj, x_s, p_s, out_weights):
    rows = h.shape[0]
    n = x_s.shape[0]
    n_tiles = rows // FFN_TILE
    assert n_tiles * SAMPLE_BLOCK == n, "one sample state block per prompt row tile"
    q, k, a, v, y_a, g, gate_a, gate_b = sample_proj
    tile = lambda i: jnp.minimum(i, n_tiles - 1)
    state_spec = pl.BlockSpec((SAMPLE_BLOCK, GLA_HEADS, GLA_DK, GLA_DV), lambda i: (tile(i), 0, 0, 0))
    return pl.pallas_call(
        _ffn_kernel,
        grid=(n_tiles + 1,),
        in_specs=[pl.BlockSpec((FFN_TILE, D_MODEL), lambda i: (tile(i), 0)),
                  pl.BlockSpec((FFN_TILE, PLE_DIM), lambda i: (tile(i), 0)),
                  _resident((1, D_MODEL)), _resident((D_MODEL, D_FF)), _resident((D_MODEL, D_FF)),
                  _resident((D_FF, D_MODEL)), _resident((1, D_MODEL)), _resident((PLE_DIM, D_MODEL)),
                  _resident((D_MODEL, D_MODEL)), _resident((1, D_MODEL)),
                  state_spec, _resident(q.shape), _resident(k.shape), _resident(a.shape),
                  _resident(v.shape), _resident((n, D_MODEL)), _resident((n, PLE_DIM)),
                  _resident((n, D_MODEL)), _resident((n, V_DIM)), _resident((n, D_MODEL)),
                  _resident((n, D_MODEL)),
                  _resident((1, GLA_DV)), _resident((V_DIM, D_MODEL)), _resident((D_MODEL, D_MODEL)),
                  _resident((1, D_MODEL))],
        out_specs=[pl.BlockSpec((FFN_TILE, D_MODEL), lambda i: (tile(i), 0)),
                   state_spec,
                   pl.BlockSpec((n, D_MODEL), lambda i: (0, 0))],
        out_shape=[jax.ShapeDtypeStruct((rows, D_MODEL), F32),
                   jax.ShapeDtypeStruct(state.shape, F32),
                   jax.ShapeDtypeStruct((n, D_MODEL), F32)],
        scratch_shapes=[pltpu.VMEM((n // SUBLANES, SUBLANES, V_DIM), F32)],
        compiler_params=pltpu.CompilerParams(
            dimension_semantics=("arbitrary",), vmem_limit_bytes=VMEM_LIMIT_BYTES),
        name="ffn_ple",
    )(h, p, *ffn_weights, state, q, k, a, v, x_s, p_s, y_a, g, gate_a, gate_b, *out_weights)


def kernel(x_prompt, x_sample, state_conv, state_gla, p_prompt, p_sample, w_norm_mix_pre, w_in, w_conv, w_a_out, w_gk, b_gk, w_gla_norm, w_b_out, w_o, w_norm_mix_post, w_norm_ffn_pre, w_ffn_gate, w_ffn_up, w_ffn_down, w_norm_ffn_post, w_ple_proj, w_ple_gate, w_norm_ple_post):
    depth = w_in.shape[0]
    batch, seq, _ = x_prompt.shape
    n_dec = x_sample.shape[0]
    assert x_sample.shape[1] == 1, "the sample group carries one new token per sequence"
    assert seq % SEQ_TILE == 0 and (batch * seq) % FFN_TILE == 0 and n_dec % SAMPLE_BLOCK == 0

    hp = x_prompt
    hs = x_sample
    conv_p, gla_p, conv_s, gla_s = [], [], [], []
    for i in range(depth):
        row = lambda w: w[i].reshape(1, -1)
        w_head, w_tail, w_gklr, w_a, w_b, w_o_b = _cast_w_in(
            jnp.swapaxes(w_in[i], 0, 1), (w_a_out[i], w_b_out[i], w_o[i]))
        w_gk_pad = jnp.pad(w_gk[i], ((0, LANES - GATE_RANK), (0, 0))).astype(BF16)
        mixer_weights = (row(w_norm_mix_pre), w_head, w_tail, w_gklr, w_gk_pad, row(b_gk), w_conv[i],
                         w_a, row(w_gla_norm), w_b, w_o_b, row(w_norm_mix_post))

        hp_mid, cbp, sp, w_gate, w_up, w_down, w_pp, w_pg = _prompt_mixer(
            hp, mixer_weights, (w_ffn_gate[i], w_ffn_up[i], w_ffn_down[i], w_ple_proj[i], w_ple_gate[i]))
        ffn_weights = (row(w_norm_ffn_pre), w_gate, w_up, w_down, row(w_norm_ffn_post), w_pp, w_pg,
                       row(w_norm_ple_post))
        cbs, *sample_proj, xs2d, ps2d = _sample_proj(hs, state_conv[i], p_sample[i], mixer_weights[:8])
        hp, ss, hs = _ffn_and_sample_state(
            hp_mid.reshape(batch * seq, D_MODEL), p_prompt[i].reshape(batch * seq, PLE_DIM), ffn_weights,
            state_gla[i], sample_proj, xs2d, ps2d, mixer_weights[8:])
        hp = hp.reshape(batch, seq, D_MODEL)
        hs = hs.reshape(n_dec, 1, D_MODEL)

        conv_p.append(cbp[0]); gla_p.append(sp[0])
        conv_s.append(cbs); gla_s.append(ss)
    return (hp, hs, jnp.stack(conv_p), jnp.stack(gla_p),
            jnp.stack(conv_s), jnp.stack(gla_s))
```

```python
import functools

import numpy as np
import jax
import jax.numpy as jnp
from jax import lax
from jax.experimental import pallas as pl
from jax.experimental.pallas import tpu as pltpu

D_MODEL = 1024
CONV_DIM = D_MODEL
GLA_HEADS = 4
GLA_DK = 128
GLA_DV = 256
QK_DIM = GLA_HEADS * GLA_DK
V_DIM = GLA_HEADS * GLA_DV
GATE_RANK = 16
GATE_NORMALIZER = 16.0
D_FF = 2816
PLE_DIM = 256
EPS = 1e-6
LOG2_E = 1.4426950408889634

LANES = 128
SUBLANES = 8
BF16_ROWS = 16
VMEM_LIMIT_BYTES = 56 * 1024 * 1024

OFF_B, OFF_C, OFF_X = 0, 1024, 2048
OFF_Q, OFF_K, OFF_V, OFF_G = 3072, 3584, 4096, 5120
OFF_GA, OFF_GB = 6144, 7168
N_HEAD = 6144
N_TAIL = 2048
GKLR_START = 6144

GLA_CHUNK = 256
N_LEVELS = 8
SEQ_TILE = 512
FFN_TILE = 512
FFN_SUBTILE = 512
FFN_CHUNK = 1408
SAMPLE_BLOCK = 4
W_IN_BLOCK = 1024

F32 = jnp.float32
BF16 = jnp.bfloat16


def _rms(x, w):
    return x * lax.rsqrt(jnp.mean(x * x, axis=-1, keepdims=True) + EPS) * w


def _sigmoid(x):
    return 1.0 / (1.0 + jnp.exp2(x * -LOG2_E))


def _log2_sigmoid(x, scale):
    return (jnp.minimum(x, 0.0) - jnp.log(1.0 + jnp.exp2(jnp.abs(x) * -LOG2_E))) * (scale * LOG2_E)


def _dot(a, b):
    return jnp.dot(a, b, preferred_element_type=F32)


def _dot_nt(a, b):
    return lax.dot_general(a, b, (((1,), (1,)), ((), ())), preferred_element_type=F32)


def _column_broadcast(row):
    return jnp.broadcast_to(row, (LANES, LANES)).T


def _projections(hn, w_head_ref, w_tail_ref, w_gklr_ref, w_gk_ref, b_gk_ref):
    def proj(off, width):
        if off < N_HEAD:
            return _dot(hn, w_head_ref[:, off:off + width])
        return _dot(hn, w_tail_ref[:, off - N_HEAD:off - N_HEAD + width])

    gk_lr = _dot(hn, w_gklr_ref[...])
    gk = _dot(gk_lr.astype(BF16), w_gk_ref[...]) + b_gk_ref[...]
    return proj, _log2_sigmoid(gk, 1.0 / GATE_NORMALIZER)


def _mix_out(x, o_heads, g, gate_a, gate_b, y_a, w_gn, w_b_ref, w_o_ref, wn_post):
    normed = []
    for h in range(GLA_HEADS):
        o = o_heads[h]
        gh = g[:, h * GLA_DV:(h + 1) * GLA_DV]
        o = o * lax.rsqrt(jnp.mean(o * o, axis=-1, keepdims=True) + EPS) * w_gn
        normed.append((o * (gh * _sigmoid(gh))).astype(BF16))
    y_b = _dot(jnp.concatenate(normed, axis=1), w_b_ref[...])
    merged = _sigmoid(gate_a) * y_a + _sigmoid(gate_b) * y_b
    mix = _dot(merged.astype(BF16), w_o_ref[...])
    return x + _rms(mix, wn_post)


def _prompt_mixer_kernel(x_ref, lev_ref, tri_ref, wn_pre_ref, w_head_ref, w_tail_ref, w_gklr_ref,
                         w_gk_ref, b_gk_ref, w_conv_ref, w_a_ref, w_gn_ref, w_b_ref, w_o_ref,
                         wn_post_ref, *rest):
    n_side = (len(rest) - 4) // 2
    side_src = rest[:n_side]
    h_ref, conv_ref, state_ref = rest[n_side:n_side + 3]
    side_dst = rest[n_side + 3:2 * n_side + 3]
    dec_ref = rest[-1]

    @pl.when(pl.program_id(1) == 0)
    def _():
        conv_ref[...] = jnp.zeros_like(conv_ref)
        state_ref[...] = jnp.zeros_like(state_ref)

    for src, dst in zip(side_src, side_dst):
        dst[...] = src[...].astype(BF16)

    t = SEQ_TILE
    x = x_ref[0]
    hn = _rms(x, wn_pre_ref[...]).astype(BF16)
    proj, logw2 = _projections(hn, w_head_ref, w_tail_ref, w_gklr_ref, w_gk_ref, b_gk_ref)

    u = proj(OFF_C, CONV_DIM) * proj(OFF_X, CONV_DIM)
    prev2 = conv_ref[0, 0, 0:1, :]
    prev1 = conv_ref[0, 0, 1:2, :]
    row = lax.broadcasted_iota(jnp.int32, (t, CONV_DIM), 0)
    u1 = jnp.where(row == 0, prev1, pltpu.roll(u, 1, 0))
    u2 = jnp.where(row == 0, prev2, jnp.where(row == 1, prev1, pltpu.roll(u, 2, 0)))
    wc = w_conv_ref[...]
    y_conv = wc[0:1] * u2 + wc[1:2] * u1 + wc[2:3] * u
    conv_ref[0, 0] = u[t - 2:t]
    y_a = _dot((proj(OFF_B, CONV_DIM) * y_conv).astype(BF16), w_a_ref[...])

    q = proj(OFF_Q, QK_DIM) * (GLA_DK ** -0.5)
    k = proj(OFF_K, QK_DIM)
    v = proj(OFF_V, V_DIM).astype(BF16)

    o_chunks = []
    for c in range(SEQ_TILE // GLA_CHUNK):
        rows = slice(c * GLA_CHUNK, (c + 1) * GLA_CHUNK)
        o_chunks.append(_gla_chunk(q[rows], k[rows], v[rows], logw2[rows], dec_ref.at[c], lev_ref,
                                   tri_ref, state_ref))
    o_heads = [jnp.concatenate([o[h] for o in o_chunks], axis=0) for h in range(GLA_HEADS)]

    h_ref[0] = _mix_out(x, o_heads, proj(OFF_G, V_DIM), proj(OFF_GA, D_MODEL), proj(OFF_GB, D_MODEL),
                        y_a, w_gn_ref[...], w_b_ref, w_o_ref, wn_post_ref[...])


def _gla_chunk(q, k, v, logw2, dec_ref, lev_ref, tri_ref, state_ref):
    t = GLA_CHUNK
    dec_ref[0] = logw2
    logw2 = dec_ref[0]
    hi = logw2.astype(BF16)
    lo = (logw2 - hi.astype(F32)).astype(BF16)
    cum = _dot(tri_ref[...], hi) + _dot(tri_ref[...], lo)
    dec_ref[1] = cum
    cum_last = cum[t - 1:t]

    half = t // 2
    lev = lev_ref[...]
    rowq = lax.broadcasted_iota(jnp.int32, (t, QK_DIM), 0)

    def level_log2_factor(i):
        m = 1 << i
        if i == 0:
            return jnp.where((rowq & 1) == 1, logw2, 0.0)
        if i == 1:
            up = pltpu.roll(logw2, t - 1, 0)
            dn = pltpu.roll(logw2, 1, 0)
            r4 = rowq & 3
            return jnp.where(r4 == 0, up, jnp.where(r4 == 1, 0.0, jnp.where(r4 == 2, logw2, logw2 + dn)))
        pieces = []
        for blk in range(t // (2 * m)):
            r = blk * 2 * m + m - 1
            pieces.append(jnp.broadcast_to(dec_ref[1, r:r + 1, :], (2 * m, QK_DIM)))
        return -jnp.abs(cum - jnp.concatenate(pieces, axis=0))

    def head(a, h):
        return a[:, h * GLA_DK:(h + 1) * GLA_DK]

    q_lv = [q.astype(BF16)]
    k_lv = [k.astype(BF16)]
    for i in range(N_LEVELS - 1):
        e = jnp.exp2(level_log2_factor(i))
        q_lv.append((q * e).astype(BF16))
        k_lv.append((k * e).astype(BF16))
    cum_mid = dec_ref[1, half - 1:half, :]
    q_top = (q[half:] * jnp.exp2(cum[half:] - cum_mid)).astype(BF16)
    k_top = (k[:half] * jnp.exp2(cum_mid - cum[:half])).astype(BF16)

    q_in = (q * jnp.exp2(cum)).astype(BF16)
    k_out = k * jnp.exp2(cum_last - cum)
    a_last = jnp.exp2(cum_last)

    o_heads = []
    for h in range(GLA_HEADS):
        vh = v[:, h * GLA_DV:(h + 1) * GLA_DV]
        diag = [0.0, 0.0]
        for i in range(N_LEVELS):
            s_i = _dot_nt(head(q_lv[i], h), head(k_lv[i], h))
            diag = [jnp.where(lev == i - 1, s_i[r0:r0 + half, r0:r0 + half], diag[j])
                    for j, r0 in enumerate((0, half))]
        diag = [d.astype(BF16) for d in diag]
        p_low = jnp.concatenate([_dot_nt(head(q_top, h), head(k_top, h)).astype(BF16), diag[1]], axis=1)
        s_old = state_ref[0, 0, h]
        o = jnp.concatenate([_dot(diag[0], vh[:half]), _dot(p_low, vh)], axis=0)
        o_heads.append(o + _dot(head(q_in, h), s_old.astype(BF16)))
        a_col = _column_broadcast(head(a_last, h))
        a_col = jnp.concatenate([a_col, a_col], axis=1)
        state_ref[0, 0, h] = a_col * s_old + _dot(head(k_out, h).T.astype(BF16), vh)
    return o_heads


def _sample_proj_kernel(x_ref, cbuf_ref, wn_pre_ref, w_head_ref, w_tail_ref, w_gklr_ref, w_gk_ref,
                        b_gk_ref, w_conv_ref, w_a_ref,
                        ps_ref, conv_ref, q_ref, k_ref, a_ref, v_ref, ya_ref, g_ref, ga_ref, gb_ref,
                        xs2_ref, ps2_ref):
    x = x_ref[:, 0, :]
    xs2_ref[...] = x
    ps2_ref[...] = ps_ref[:, 0, :]
    hn = _rms(x, wn_pre_ref[...]).astype(BF16)
    proj, logw2 = _projections(hn, w_head_ref, w_tail_ref, w_gklr_ref, w_gk_ref, b_gk_ref)
    u = proj(OFF_C, CONV_DIM) * proj(OFF_X, CONV_DIM)
    buf0 = cbuf_ref[:, 0, :]
    buf1 = cbuf_ref[:, 1, :]
    wc = w_conv_ref[...]
    y_conv = wc[0:1] * buf0 + wc[1:2] * buf1 + wc[2:3] * u
    conv_ref[:, 0, :] = buf1
    conv_ref[:, 1, :] = u
    ya_ref[...] = _dot((proj(OFF_B, CONV_DIM) * y_conv).astype(BF16), w_a_ref[...])
    grouped = lambda a: a.reshape(a.shape[0] // SUBLANES, SUBLANES, a.shape[1])
    q_ref[...] = grouped(proj(OFF_Q, QK_DIM) * (GLA_DK ** -0.5))
    k_ref[...] = grouped(proj(OFF_K, QK_DIM))
    a_ref[...] = grouped(jnp.exp2(logw2))
    v_ref[...] = grouped(proj(OFF_V, V_DIM))
    g_ref[...] = proj(OFF_G, V_DIM)
    ga_ref[...] = proj(OFF_GA, D_MODEL)
    gb_ref[...] = proj(OFF_GB, D_MODEL)


def _sample_state_update(blk, st_ref, st_out_ref, q_ref, k_ref, a_ref, v_ref, o_scr):
    per_group = SUBLANES // SAMPLE_BLOCK
    group = blk // per_group
    first = (blk % per_group) * SAMPLE_BLOCK
    row_id = lax.broadcasted_iota(jnp.int32, (SUBLANES, V_DIM), 0)
    o_tile = o_scr[group]
    for n in range(SAMPLE_BLOCK):
        r = first + n
        q_row = q_ref[group, pl.ds(r, 1), :]
        k_row = k_ref[group, pl.ds(r, 1), :]
        a_row = a_ref[group, pl.ds(r, 1), :]
        v_row = v_ref[group, pl.ds(r, 1), :]
        o_parts = []
        for h in range(GLA_HEADS):
            sl = slice(h * GLA_DK, (h + 1) * GLA_DK)
            a_col = _column_broadcast(a_row[:, sl])
            k_col = _column_broadcast(k_row[:, sl])
            q_col = _column_broadcast(q_row[:, sl])
            for half in range(GLA_DV // LANES):
                c0 = half * LANES
                s_old = st_ref[n, h, :, c0:c0 + LANES]
                vv = v_row[:, h * GLA_DV + c0:h * GLA_DV + c0 + LANES]
                s_new = a_col * s_old + k_col * vv
                st_out_ref[n, h, :, c0:c0 + LANES] = s_new
                o_parts.append(jnp.sum(q_col * s_new, axis=0, keepdims=True))
        o_row = jnp.concatenate(o_parts, axis=1)
        o_tile = jnp.where(row_id == r, o_row, o_tile)
    o_scr[group] = o_tile


def _ffn_rows(h, p, wn_pre_ref, w_gate_ref, w_up_ref, w_down_ref, wn_post_ref, w_pp_ref, w_pg_ref,
              wn_ple_ref):
    f = _rms(h, wn_pre_ref[...]).astype(BF16)
    acc = None
    for c in range(D_FF // FFN_CHUNK):
        sl = slice(c * FFN_CHUNK, (c + 1) * FFN_CHUNK)
        gate = _dot(f, w_gate_ref[:, sl])
        up = _dot(f, w_up_ref[:, sl])
        act = (gate * _sigmoid(gate) * up).astype(BF16)
        part = _dot(act, w_down_ref[sl, :])
        acc = part if acc is None else acc + part
    h = h + _rms(acc, wn_post_ref[...])
    e = _dot(p.astype(BF16), w_pp_ref[...]) * _sigmoid(_dot(h.astype(BF16), w_pg_ref[...]))
    return h + _rms(e, wn_ple_ref[...])


def _ffn_kernel(h_ref, p_ref, wn_pre_ref, w_gate_ref, w_up_ref, w_down_ref, wn_post_ref,
                w_pp_ref, w_pg_ref, wn_ple_ref,
                st_ref, q_ref, k_ref, a_ref, v_ref, xs_ref, ps_ref, ya_ref, g_ref, ga_ref, gb_ref,
                w_gn_ref, w_b_ref, w_o_ref, wn_mix_post_ref,
                out_ref, st_out_ref, outs_ref, o_scr):
    step = pl.program_id(0)
    n_tiles = pl.num_programs(0) - 1
    ffn_refs = (wn_pre_ref, w_gate_ref, w_up_ref, w_down_ref, wn_post_ref, w_pp_ref, w_pg_ref, wn_ple_ref)

    @pl.when(step == 0)
    def _():
        o_scr[...] = jnp.zeros_like(o_scr)

    @pl.when(step < n_tiles)
    def _():
        for r in range(FFN_TILE // FFN_SUBTILE):
            rows = slice(r * FFN_SUBTILE, (r + 1) * FFN_SUBTILE)
            out_ref[rows, :] = _ffn_rows(h_ref[rows, :], p_ref[rows, :], *ffn_refs)
        _sample_state_update(step, st_ref, st_out_ref, q_ref, k_ref, a_ref, v_ref, o_scr)

    @pl.when(step == n_tiles)
    def _():
        o = o_scr[...].reshape(xs_ref.shape[0], V_DIM)
        o_heads = [o[:, h * GLA_DV:(h + 1) * GLA_DV] for h in range(GLA_HEADS)]
        hs = _mix_out(xs_ref[...], o_heads, g_ref[...], ga_ref[...], gb_ref[...], ya_ref[...],
                      w_gn_ref[...], w_b_ref, w_o_ref, wn_mix_post_ref[...])
        outs_ref[...] = _ffn_rows(hs, ps_ref[...], *ffn_refs)


def _cast_w_in_kernel(wt_ref, gk_ref, *refs):
    n = (len(refs) - 3) // 2
    head_ref, tail_ref, gklr_ref = refs[n:n + 3]
    for src, dst in zip(refs[:n], refs[n + 3:]):
        dst[...] = src[...].astype(BF16)
    j = pl.program_id(0)
    blk = wt_ref[...].T.astype(BF16)

    @pl.when(j < N_HEAD // W_IN_BLOCK)
    def _():
        head_ref[...] = blk

    @pl.when(j >= N_HEAD // W_IN_BLOCK)
    def _():
        tail_ref[...] = blk

    @pl.when(j == 0)
    def _():
        rows = jnp.concatenate([gk_ref[...], jnp.zeros((LANES - GATE_RANK, D_MODEL), F32)], axis=0)
        gklr_ref[...] = rows.T.astype(BF16)


def _cast_w_in(w_in_t, others):
    n_head = N_HEAD // W_IN_BLOCK
    n_tail = N_TAIL // W_IN_BLOCK
    other_specs = [pl.BlockSpec((w.shape[0] // (n_head + n_tail), w.shape[1]), lambda j: (j, 0))
                   for w in others]

    def src_row(j):
        row = jnp.where(j < n_head, j * W_IN_BLOCK, GKLR_START + GATE_RANK + (j - n_head) * W_IN_BLOCK)
        return pl.multiple_of(row, GATE_RANK)

    return pl.pallas_call(
        _cast_w_in_kernel,
        grid=(n_head + n_tail,),
        in_specs=[pl.BlockSpec((pl.Element(W_IN_BLOCK), pl.Element(D_MODEL)), lambda j: (src_row(j), 0)),
                  pl.BlockSpec((pl.Element(GATE_RANK), pl.Element(D_MODEL)), lambda j: (GKLR_START, 0))]
                 + other_specs,
        out_specs=[pl.BlockSpec((D_MODEL, W_IN_BLOCK), lambda j: (0, jnp.minimum(j, n_head - 1))),
                   pl.BlockSpec((D_MODEL, W_IN_BLOCK), lambda j: (0, jnp.maximum(j - n_head, 0))),
                   pl.BlockSpec((D_MODEL, LANES), lambda j: (0, 0))] + other_specs,
        out_shape=[jax.ShapeDtypeStruct((D_MODEL, N_HEAD), BF16),
                   jax.ShapeDtypeStruct((D_MODEL, N_TAIL), BF16),
                   jax.ShapeDtypeStruct((D_MODEL, LANES), BF16)]
                  + [jax.ShapeDtypeStruct(w.shape, BF16) for w in others],
        compiler_params=pltpu.CompilerParams(
            dimension_semantics=("arbitrary",), vmem_limit_bytes=VMEM_LIMIT_BYTES),
        name="cast_weights",
    )(w_in_t, w_in_t, *others)


def _resident(shape):
    return pl.BlockSpec(shape, lambda *_: (0,) * len(shape), pipeline_mode=pl.Buffered(1))


def _level_map(t):
    idx = np.arange(t)
    xor = idx[:, None] ^ idx[None, :]
    lev = np.floor(np.log2(np.maximum(xor, 1))).astype(np.int32)
    lev = np.where(idx[:, None] > idx[None, :], lev, -2)
    lev = np.where(idx[:, None] == idx[None, :], -1, lev)
    return jnp.asarray(lev, dtype=jnp.int32)


def _mixer_weight_specs():
    return [
        _resident((1, D_MODEL)),
        _resident((D_MODEL, N_HEAD)),
        _resident((D_MODEL, N_TAIL)),
        _resident((D_MODEL, LANES)),
        _resident((LANES, QK_DIM)),
        _resident((1, QK_DIM)),
        _resident((3, CONV_DIM)),
        _resident((CONV_DIM, D_MODEL)),
        _resident((1, GLA_DV)),
        _resident((V_DIM, D_MODEL)),
        _resident((D_MODEL, D_MODEL)),
        _resident((1, D_MODEL)),
    ]


def _prompt_mixer(x, mixer_weights, later_weights):
    b, s, _ = x.shape
    t = SEQ_TILE
    c = GLA_CHUNK
    n_steps = b * (s // t)
    tri = jnp.asarray(np.tril(np.ones((c, c), np.float32)), dtype=BF16)

    def side_spec(shape):
        n_blocks = n_steps
        while shape[0] % (n_blocks * BF16_ROWS):
            n_blocks //= 2
        rep = n_steps // n_blocks
        return pl.BlockSpec((shape[0] // n_blocks, shape[1]),
                            lambda i, j: ((i * (s // t) + j) // rep, 0))

    side_specs = [side_spec(w.shape) for w in later_weights]
    return pl.pallas_call(
        _prompt_mixer_kernel,
        grid=(b, s // t),
        in_specs=[pl.BlockSpec((1, t, D_MODEL), lambda i, j: (i, j, 0)),
                  _resident((c // 2, c // 2)), _resident((c, c))] + _mixer_weight_specs() + side_specs,
        out_specs=[pl.BlockSpec((1, t, D_MODEL), lambda i, j: (i, j, 0)),
                   pl.BlockSpec((1, 1, 2, CONV_DIM), lambda i, j: (0, i, 0, 0)),
                   pl.BlockSpec((1, 1, GLA_HEADS, GLA_DK, GLA_DV), lambda i, j: (0, i, 0, 0, 0))]
                  + side_specs,
        out_shape=[jax.ShapeDtypeStruct((b, s, D_MODEL), F32),
                   jax.ShapeDtypeStruct((1, b, 2, CONV_DIM), F32),
                   jax.ShapeDtypeStruct((1, b, GLA_HEADS, GLA_DK, GLA_DV), F32)]
                  + [jax.ShapeDtypeStruct(w.shape, BF16) for w in later_weights],
        scratch_shapes=[pltpu.VMEM((t // c, 2, c, QK_DIM), F32)],
        compiler_params=pltpu.CompilerParams(
            dimension_semantics=("arbitrary", "arbitrary"), vmem_limit_bytes=VMEM_LIMIT_BYTES),
        name="prompt_mixer",
    )(x, _level_map(c // 2), tri, *mixer_weights, *later_weights)


def _sample_proj(x, conv_buf, p, proj_weights):
    n = x.shape[0]
    grouped = lambda w: (n // SUBLANES, SUBLANES, w)
    out_shapes = [(n, 2, CONV_DIM), grouped(QK_DIM), grouped(QK_DIM), grouped(QK_DIM), grouped(V_DIM),
                  (n, D_MODEL), (n, V_DIM), (n, D_MODEL), (n, D_MODEL), (n, D_MODEL), (n, PLE_DIM)]
    whole = lambda shape: pl.BlockSpec(shape, lambda i: (0,) * len(shape))
    return pl.pallas_call(
        _sample_proj_kernel,
        grid=(1,),
        in_specs=[_resident((n, 1, D_MODEL)), _resident((n, 2, CONV_DIM))] + _mixer_weight_specs()[:8]
                 + [_resident((n, 1, PLE_DIM))],
        out_specs=[whole(s) for s in out_shapes],
        out_shape=[jax.ShapeDtypeStruct(s, F32) for s in out_shapes],
        compiler_params=pltpu.CompilerParams(
            dimension_semantics=("arbitrary",), vmem_limit_bytes=VMEM_LIMIT_BYTES),
        name="sample_proj",
    )(x, conv_buf, *proj_weights, p)


def _ffn_and_sample_state(h, p, ffn_weights, state, sample_proj, x_s, p_s, out_weights):
    rows = h.shape[0]
    n = x_s.shape[0]
    n_tiles = rows // FFN_TILE
    assert n_tiles * SAMPLE_BLOCK == n, "one sample state block per prompt row tile"
    q, k, a, v, y_a, g, gate_a, gate_b = sample_proj
    tile = lambda i: jnp.minimum(i, n_tiles - 1)
    state_spec = pl.BlockSpec((SAMPLE_BLOCK, GLA_HEADS, GLA_DK, GLA_DV), lambda i: (tile(i), 0, 0, 0))
    return pl.pallas_call(
        _ffn_kernel,
        grid=(n_tiles + 1,),
        in_specs=[pl.BlockSpec((FFN_TILE, D_MODEL), lambda i: (tile(i), 0)),
                  pl.BlockSpec((FFN_TILE, PLE_DIM), lambda i: (tile(i), 0)),
                  _resident((1, D_MODEL)), _resident((D_MODEL, D_FF)), _resident((D_MODEL, D_FF)),
                  _resident((D_FF, D_MODEL)), _resident((1, D_MODEL)), _resident((PLE_DIM, D_MODEL)),
                  _resident((D_MODEL, D_MODEL)), _resident((1, D_MODEL)),
                  state_spec, _resident(q.shape), _resident(k.shape), _resident(a.shape),
                  _resident(v.shape), _resident((n, D_MODEL)), _resident((n, PLE_DIM)),
                  _resident((n, D_MODEL)), _resident((n, V_DIM)), _resident((n, D_MODEL)),
                  _resident((n, D_MODEL)),
                  _resident((1, GLA_DV)), _resident((V_DIM, D_MODEL)), _resident((D_MODEL, D_MODEL)),
                  _resident((1, D_MODEL))],
        out_specs=[pl.BlockSpec((FFN_TILE, D_MODEL), lambda i: (tile(i), 0)),
                   state_spec,
                   pl.BlockSpec((n, D_MODEL), lambda i: (0, 0))],
        out_shape=[jax.ShapeDtypeStruct((rows, D_MODEL), F32),
                   jax.ShapeDtypeStruct(state.shape, F32),
                   jax.ShapeDtypeStruct((n, D_MODEL), F32)],
        scratch_shapes=[pltpu.VMEM((n // SUBLANES, SUBLANES, V_DIM), F32)],
        compiler_params=pltpu.CompilerParams(
            dimension_semantics=("arbitrary",), vmem_limit_bytes=VMEM_LIMIT_BYTES),
        name="ffn_ple",
    )(h, p, *ffn_weights, state, q, k, a, v, x_s, p_s, y_a, g, gate_a, gate_b, *out_weights)


def kernel(x_prompt, x_sample, state_conv, state_gla, p_prompt, p_sample, w_norm_mix_pre, w_in, w_conv, w_a_out, w_gk, b_gk, w_gla_norm, w_b_out, w_o, w_norm_mix_post, w_norm_ffn_pre, w_ffn_gate, w_ffn_up, w_ffn_down, w_norm_ffn_post, w_ple_proj, w_ple_gate, w_norm_ple_post):
    depth = w_in.shape[0]
    batch, seq, _ = x_prompt.shape
    n_dec = x_sample.shape[0]
    assert x_sample.shape[1] == 1, "the sample group carries one new token per sequence"
    assert seq % SEQ_TILE == 0 and (batch * seq) % FFN_TILE == 0 and n_dec % SAMPLE_BLOCK == 0

    hp = x_prompt
    hs = x_sample
    conv_p, gla_p, conv_s, gla_s = [], [], [], []
    for i in range(depth):
        row = lambda w: w[i].reshape(1, -1)
        w_head, w_tail, w_gklr, w_a, w_b, w_o_b = _cast_w_in(
            jnp.swapaxes(w_in[i], 0, 1), (w_a_out[i], w_b_out[i], w_o[i]))
        w_gk_pad = jnp.pad(w_gk[i], ((0, LANES - GATE_RANK), (0, 0))).astype(BF16)
        mixer_weights = (row(w_norm_mix_pre), w_head, w_tail, w_gklr, w_gk_pad, row(b_gk), w_conv[i],
                         w_a, row(w_gla_norm), w_b, w_o_b, row(w_norm_mix_post))

        hp_mid, cbp, sp, w_gate, w_up, w_down, w_pp, w_pg = _prompt_mixer(
            hp, mixer_weights, (w_ffn_gate[i], w_ffn_up[i], w_ffn_down[i], w_ple_proj[i], w_ple_gate[i]))
        ffn_weights = (row(w_norm_ffn_pre), w_gate, w_up, w_down, row(w_norm_ffn_post), w_pp, w_pg,
                       row(w_norm_ple_post))
        cbs, *sample_proj, xs2d, ps2d = _sample_proj(hs, state_conv[i], p_sample[i], mixer_weights[:8])
        hp, ss, hs = _ffn_and_sample_state(
            hp_mid.reshape(batch * seq, D_MODEL), p_prompt[i].reshape(batch * seq, PLE_DIM), ffn_weights,
            state_gla[i], sample_proj, xs2d, ps2d, mixer_weights[8:])
        hp = hp.reshape(batch, seq, D_MODEL)
        hs = hs.reshape(n_dec, 1, D_MODEL)

        conv_p.append(cbp[0]); gla_p.append(sp[0])
        conv_s.append(cbs); gla_s.append(ss)
    return (hp, hs, jnp.stack(conv_p), jnp.stack(gla_p),
            jnp.stack(conv_s), jnp.stack(gla_s))
```

```python
import functools

import numpy as np
import jax
import jax.numpy as jnp
from jax import lax
from jax.experimental import pallas as pl
from jax.experimental.pallas import tpu as pltpu

D_MODEL = 1024
CONV_DIM = D_MODEL
GLA_HEADS = 4
GLA_DK = 128
GLA_DV = 256
QK_DIM = GLA_HEADS * GLA_DK
V_DIM = GLA_HEADS * GLA_DV
GATE_RANK = 16
GATE_NORMALIZER = 16.0
D_FF = 2816
PLE_DIM = 256
EPS = 1e-6
LOG2_E = 1.4426950408889634

LANES = 128
SUBLANES = 8
BF16_ROWS = 16
VMEM_LIMIT_BYTES = 56 * 1024 * 1024

OFF_B, OFF_C, OFF_X = 0, 1024, 2048
OFF_Q, OFF_K, OFF_V, OFF_G = 3072, 3584, 4096, 5120
OFF_GA, OFF_GB = 6144, 7168
N_HEAD = 6144
N_TAIL = 2048
GKLR_START = 6144

GLA_CHUNK = 256
N_LEVELS = 8
SEQ_TILE = 512
FFN_TILE = 512
FFN_SUBTILE = 512
FFN_CHUNK = 1408
SAMPLE_BLOCK = 4
W_IN_BLOCK = 1024

F32 = jnp.float32
BF16 = jnp.bfloat16


def _rms(x, w):
    return x * lax.rsqrt(jnp.mean(x * x, axis=-1, keepdims=True) + EPS) * w


def _sigmoid(x):
    return 1.0 / (1.0 + jnp.exp2(x * -LOG2_E))


def _log2_sigmoid(x, scale):
    return (jnp.minimum(x, 0.0) - jnp.log(1.0 + jnp.exp2(jnp.abs(x) * -LOG2_E))) * (scale * LOG2_E)


def _dot(a, b):
    return jnp.dot(a, b, preferred_element_type=F32)


def _dot_nt(a, b):
    return lax.dot_general(a, b, (((1,), (1,)), ((), ())), preferred_element_type=F32)


def _column_broadcast(row):
    return jnp.broadcast_to(row, (LANES, LANES)).T


def _projections(hn, w_head_ref, w_tail_ref, w_gklr_ref, w_gk_ref, b_gk_ref):
    def proj(off, width):
        if off < N_HEAD:
            return _dot(hn, w_head_ref[:, off:off + width])
        return _dot(hn, w_tail_ref[:, off - N_HEAD:off - N_HEAD + width])

    gk_lr = _dot(hn, w_gklr_ref[...])
    gk = _dot(gk_lr.astype(BF16), w_gk_ref[...]) + b_gk_ref[...]
    return proj, _log2_sigmoid(gk, 1.0 / GATE_NORMALIZER)


def _mix_out(x, o_heads, g, gate_a, gate_b, y_a, w_gn, w_b_ref, w_o_ref, wn_post):
    normed = []
    for h in range(GLA_HEADS):
        o = o_heads[h]
        gh = g[:, h * GLA_DV:(h + 1) * GLA_DV]
        o = o * lax.rsqrt(jnp.mean(o * o, axis=-1, keepdims=True) + EPS) * w_gn
        normed.append((o * (gh * _sigmoid(gh))).astype(BF16))
    y_b = _dot(jnp.concatenate(normed, axis=1), w_b_ref[...])
    merged = _sigmoid(gate_a) * y_a + _sigmoid(gate_b) * y_b
    mix = _dot(merged.astype(BF16), w_o_ref[...])
    return x + _rms(mix, wn_post)


def _prompt_mixer_kernel(x_ref, lev_ref, tri_ref, wn_pre_ref, w_head_ref, w_tail_ref, w_gklr_ref,
                         w_gk_ref, b_gk_ref, w_conv_ref, w_a_ref, w_gn_ref, w_b_ref, w_o_ref,
                         wn_post_ref, *rest):
    n_side = (len(rest) - 4) // 2
    side_src = rest[:n_side]
    h_ref, conv_ref, state_ref = rest[n_side:n_side + 3]
    side_dst = rest[n_side + 3:2 * n_side + 3]
    dec_ref = rest[-1]

    @pl.when(pl.program_id(1) == 0)
    def _():
        conv_ref[...] = jnp.zeros_like(conv_ref)
        state_ref[...] = jnp.zeros_like(state_ref)

    for src, dst in zip(side_src, side_dst):
        dst[...] = src[...].astype(BF16)

    t = SEQ_TILE
    x = x_ref[0]
    hn = _rms(x, wn_pre_ref[...]).astype(BF16)
    proj, logw2 = _projections(hn, w_head_ref, w_tail_ref, w_gklr_ref, w_gk_ref, b_gk_ref)

    u = proj(OFF_C, CONV_DIM) * proj(OFF_X, CONV_DIM)
    prev2 = conv_ref[0, 0, 0:1, :]
    prev1 = conv_ref[0, 0, 1:2, :]
    row = lax.broadcasted_iota(jnp.int32, (t, CONV_DIM), 0)
    u1 = jnp.where(row == 0, prev1, pltpu.roll(u, 1, 0))
    u2 = jnp.where(row == 0, prev2, jnp.where(row == 1, prev1, pltpu.roll(u, 2, 0)))
    wc = w_conv_ref[...]
    y_conv = wc[0:1] * u2 + wc[1:2] * u1 + wc[2:3] * u
    conv_ref[0, 0] = u[t - 2:t]
    y_a = _dot((proj(OFF_B, CONV_DIM) * y_conv).astype(BF16), w_a_ref[...])

    q = proj(OFF_Q, QK_DIM) * (GLA_DK ** -0.5)
    k = proj(OFF_K, QK_DIM)
    v = proj(OFF_V, V_DIM).astype(BF16)

    o_chunks = []
    for c in range(SEQ_TILE // GLA_CHUNK):
        rows = slice(c * GLA_CHUNK, (c + 1) * GLA_CHUNK)
        o_chunks.append(_gla_chunk(q[rows], k[rows], v[rows], logw2[rows], dec_ref.at[c], lev_ref,
                                   tri_ref, state_ref))
    o_heads = [jnp.concatenate([o[h] for o in o_chunks], axis=0) for h in range(GLA_HEADS)]

    h_ref[0] = _mix_out(x, o_heads, proj(OFF_G, V_DIM), proj(OFF_GA, D_MODEL), proj(OFF_GB, D_MODEL),
                        y_a, w_gn_ref[...], w_b_ref, w_o_ref, wn_post_ref[...])


def _gla_chunk(q, k, v, logw2, dec_ref, lev_ref, tri_ref, state_ref):
    t = GLA_CHUNK
    dec_ref[0] = logw2
    logw2 = dec_ref[0]
    hi = logw2.astype(BF16)
    lo = (logw2 - hi.astype(F32)).astype(BF16)
    cum = _dot(tri_ref[...], hi) + _dot(tri_ref[...], lo)
    dec_ref[1] = cum
    cum_last = cum[t - 1:t]

    half = t // 2
    lev = lev_ref[...]
    rowq = lax.broadcasted_iota(jnp.int32, (t, QK_DIM), 0)

    def level_log2_factor(i):
        m = 1 << i
        if i == 0:
            return jnp.where((rowq & 1) == 1, logw2, 0.0)
        if i == 1:
            up = pltpu.roll(logw2, t - 1, 0)
            dn = pltpu.roll(logw2, 1, 0)
            r4 = rowq & 3
            return jnp.where(r4 == 0, up, jnp.where(r4 == 1, 0.0, jnp.where(r4 == 2, logw2, logw2 + dn)))
        pieces = []
        for blk in range(t // (2 * m)):
            r = blk * 2 * m + m - 1
            pieces.append(jnp.broadcast_to(dec_ref[1, r:r + 1, :], (2 * m, QK_DIM)))
        return -jnp.abs(cum - jnp.concatenate(pieces, axis=0))

    def head(a, h):
        return a[:, h * GLA_DK:(h + 1) * GLA_DK]

    def split_level(i):
        m = 1 << i
        qs, ks = [], []
        for blk in range(t // (2 * m)):
            first = slice(blk * 2 * m, blk * 2 * m + m)
            second = slice(blk * 2 * m + m, (blk + 1) * 2 * m)
            ref_row = dec_ref[1, blk * 2 * m + m - 1:blk * 2 * m + m, :]
            ks += [(k[first] * jnp.exp2(ref_row - cum[first])).astype(BF16), k_lv[0][second]]
            qs += [q_lv[0][first], (q[second] * jnp.exp2(cum[second] - ref_row)).astype(BF16)]
        return jnp.concatenate(qs, axis=0), jnp.concatenate(ks, axis=0)

    q_lv = [q.astype(BF16)]
    k_lv = [k.astype(BF16)]
    for i in range(N_LEVELS - 1):
        if (1 << i) % BF16_ROWS == 0:
            q_i, k_i = split_level(i)
        else:
            e = jnp.exp2(level_log2_factor(i))
            q_i, k_i = (q * e).astype(BF16), (k * e).astype(BF16)
        q_lv.append(q_i)
        k_lv.append(k_i)
    cum_mid = dec_ref[1, half - 1:half, :]
    q_top = (q[half:] * jnp.exp2(cum[half:] - cum_mid)).astype(BF16)
    k_top = (k[:half] * jnp.exp2(cum_mid - cum[:half])).astype(BF16)

    q_in = (q * jnp.exp2(cum)).astype(BF16)
    k_out = k * jnp.exp2(cum_last - cum)
    a_last = jnp.exp2(cum_last)

    o_heads = []
    for h in range(GLA_HEADS):
        vh = v[:, h * GLA_DV:(h + 1) * GLA_DV]
        diag = [0.0, 0.0]
        for i in range(N_LEVELS):
            s_i = _dot_nt(head(q_lv[i], h), head(k_lv[i], h))
            diag = [jnp.where(lev == i - 1, s_i[r0:r0 + half, r0:r0 + half], diag[j])
                    for j, r0 in enumerate((0, half))]
        diag = [d.astype(BF16) for d in diag]
        p_low = jnp.concatenate([_dot_nt(head(q_top, h), head(k_top, h)).astype(BF16), diag[1]], axis=1)
        s_old = state_ref[0, 0, h]
        o = jnp.concatenate([_dot(diag[0], vh[:half]), _dot(p_low, vh)], axis=0)
        o_heads.append(o + _dot(head(q_in, h), s_old.astype(BF16)))
        a_col = _column_broadcast(head(a_last, h))
        a_col = jnp.concatenate([a_col, a_col], axis=1)
        state_ref[0, 0, h] = a_col * s_old + _dot(head(k_out, h).T.astype(BF16), vh)
    return o_heads


def _sample_proj_kernel(x_ref, cbuf_ref, wn_pre_ref, w_head_ref, w_tail_ref, w_gklr_ref, w_gk_ref,
                        b_gk_ref, w_conv_ref, w_a_ref,
                        ps_ref, conv_ref, q_ref, k_ref, a_ref, v_ref, ya_ref, g_ref, ga_ref, gb_ref,
                        xs2_ref, ps2_ref):
    x = x_ref[:, 0, :]
    xs2_ref[...] = x
    ps2_ref[...] = ps_ref[:, 0, :]
    hn = _rms(x, wn_pre_ref[...]).astype(BF16)
    proj, logw2 = _projections(hn, w_head_ref, w_tail_ref, w_gklr_ref, w_gk_ref, b_gk_ref)
    u = proj(OFF_C, CONV_DIM) * proj(OFF_X, CONV_DIM)
    buf0 = cbuf_ref[:, 0, :]
    buf1 = cbuf_ref[:, 1, :]
    wc = w_conv_ref[...]
    y_conv = wc[0:1] * buf0 + wc[1:2] * buf1 + wc[2:3] * u
    conv_ref[:, 0, :] = buf1
    conv_ref[:, 1, :] = u
    ya_ref[...] = _dot((proj(OFF_B, CONV_DIM) * y_conv).astype(BF16), w_a_ref[...])
    grouped = lambda a: a.reshape(a.shape[0] // SUBLANES, SUBLANES, a.shape[1])
    q_ref[...] = grouped(proj(OFF_Q, QK_DIM) * (GLA_DK ** -0.5))
    k_ref[...] = grouped(proj(OFF_K, QK_DIM))
    a_ref[...] = grouped(jnp.exp2(logw2))
    v_ref[...] = grouped(proj(OFF_V, V_DIM))
    g_ref[...] = proj(OFF_G, V_DIM)
    ga_ref[...] = proj(OFF_GA, D_MODEL)
    gb_ref[...] = proj(OFF_GB, D_MODEL)


def _sample_state_update(blk, st_ref, st_out_ref, q_ref, k_ref, a_ref, v_ref, o_scr):
    per_group = SUBLANES // SAMPLE_BLOCK
    group = blk // per_group
    first = (blk % per_group) * SAMPLE_BLOCK
    row_id = lax.broadcasted_iota(jnp.int32, (SUBLANES, V_DIM), 0)
    o_tile = o_scr[group]
    for n in range(SAMPLE_BLOCK):
        r = first + n
        q_row = q_ref[group, pl.ds(r, 1), :]
        k_row = k_ref[group, pl.ds(r, 1), :]
        a_row = a_ref[group, pl.ds(r, 1), :]
        v_row = v_ref[group, pl.ds(r, 1), :]
        o_parts = []
        for h in range(GLA_HEADS):
            sl = slice(h * GLA_DK, (h + 1) * GLA_DK)
            a_col = _column_broadcast(a_row[:, sl])
            k_col = _column_broadcast(k_row[:, sl])
            q_col = _column_broadcast(q_row[:, sl])
            for half in range(GLA_DV // LANES):
                c0 = half * LANES
                s_old = st_ref[n, h, :, c0:c0 + LANES]
                vv = v_row[:, h * GLA_DV + c0:h * GLA_DV + c0 + LANES]
                s_new = a_col * s_old + k_col * vv
                st_out_ref[n, h, :, c0:c0 + LANES] = s_new
                o_parts.append(jnp.sum(q_col * s_new, axis=0, keepdims=True))
        o_row = jnp.concatenate(o_parts, axis=1)
        o_tile = jnp.where(row_id == r, o_row, o_tile)
    o_scr[group] = o_tile


def _ffn_rows(h, p, wn_pre_ref, w_gate_ref, w_up_ref, w_down_ref, wn_post_ref, w_pp_ref, w_pg_ref,
              wn_ple_ref):
    f = _rms(h, wn_pre_ref[...]).astype(BF16)
    acc = None
    for c in range(D_FF // FFN_CHUNK):
        sl = slice(c * FFN_CHUNK, (c + 1) * FFN_CHUNK)
        gate = _dot(f, w_gate_ref[:, sl])
        up = _dot(f, w_up_ref[:, sl])
        act = (gate * _sigmoid(gate) * up).astype(BF16)
        part = _dot(act, w_down_ref[sl, :])
        acc = part if acc is None else acc + part
    h = h + _rms(acc, wn_post_ref[...])
    e = _dot(p.astype(BF16), w_pp_ref[...]) * _sigmoid(_dot(h.astype(BF16), w_pg_ref[...]))
    return h + _rms(e, wn_ple_ref[...])


def _ffn_kernel(h_ref, p_ref, wn_pre_ref, w_gate_ref, w_up_ref, w_down_ref, wn_post_ref,
                w_pp_ref, w_pg_ref, wn_ple_ref,
                st_ref, q_ref, k_ref, a_ref, v_ref, xs_ref, ps_ref, ya_ref, g_ref, ga_ref, gb_ref,
                w_gn_ref, w_b_ref, w_o_ref, wn_mix_post_ref,
                out_ref, st_out_ref, outs_ref, o_scr):
    step = pl.program_id(0)
    n_tiles = pl.num_programs(0) - 1
    ffn_refs = (wn_pre_ref, w_gate_ref, w_up_ref, w_down_ref, wn_post_ref, w_pp_ref, w_pg_ref, wn_ple_ref)

    @pl.when(step == 0)
    def _():
        o_scr[...] = jnp.zeros_like(o_scr)

    @pl.when(step < n_tiles)
    def _():
        for r in range(FFN_TILE // FFN_SUBTILE):
            rows = slice(r * FFN_SUBTILE, (r + 1) * FFN_SUBTILE)
            out_ref[rows, :] = _ffn_rows(h_ref[rows, :], p_ref[rows, :], *ffn_refs)
        _sample_state_update(step, st_ref, st_out_ref, q_ref, k_ref, a_ref, v_ref, o_scr)

    @pl.when(step == n_tiles)
    def _():
        o = o_scr[...].reshape(xs_ref.shape[0], V_DIM)
        o_heads = [o[:, h * GLA_DV:(h + 1) * GLA_DV] for h in range(GLA_HEADS)]
        hs = _mix_out(xs_ref[...], o_heads, g_ref[...], ga_ref[...], gb_ref[...], ya_ref[...],
                      w_gn_ref[...], w_b_ref, w_o_ref, wn_mix_post_ref[...])
        outs_ref[...] = _ffn_rows(hs, ps_ref[...], *ffn_refs)


def _cast_w_in_kernel(wt_ref, gk_ref, *refs):
    n = (len(refs) - 3) // 2
    head_ref, tail_ref, gklr_ref = refs[n:n + 3]
    for src, dst in zip(refs[:n], refs[n + 3:]):
        dst[...] = src[...].astype(BF16)
    j = pl.program_id(0)
    blk = wt_ref[...].T.astype(BF16)

    @pl.when(j < N_HEAD // W_IN_BLOCK)
    def _():
        head_ref[...] = blk

    @pl.when(j >= N_HEAD // W_IN_BLOCK)
    def _():
        tail_ref[...] = blk

    @pl.when(j == 0)
    def _():
        rows = jnp.concatenate([gk_ref[...], jnp.zeros((LANES - GATE_RANK, D_MODEL), F32)], axis=0)
        gklr_ref[...] = rows.T.astype(BF16)


def _cast_w_in(w_in_t, others):
    n_head = N_HEAD // W_IN_BLOCK
    n_tail = N_TAIL // W_IN_BLOCK
    other_specs = [pl.BlockSpec((w.shape[0] // (n_head + n_tail), w.shape[1]), lambda j: (j, 0))
                   for w in others]

    def src_row(j):
        row = jnp.where(j < n_head, j * W_IN_BLOCK, GKLR_START + GATE_RANK + (j - n_head) * W_IN_BLOCK)
        return pl.multiple_of(row, GATE_RANK)

    return pl.pallas_call(
        _cast_w_in_kernel,
        grid=(n_head + n_tail,),
        in_specs=[pl.BlockSpec((pl.Element(W_IN_BLOCK), pl.Element(D_MODEL)), lambda j: (src_row(j), 0)),
                  pl.BlockSpec((pl.Element(GATE_RANK), pl.Element(D_MODEL)), lambda j: (GKLR_START, 0))]
                 + other_specs,
        out_specs=[pl.BlockSpec((D_MODEL, W_IN_BLOCK), lambda j: (0, jnp.minimum(j, n_head - 1))),
                   pl.BlockSpec((D_MODEL, W_IN_BLOCK), lambda j: (0, jnp.maximum(j - n_head, 0))),
                   pl.BlockSpec((D_MODEL, LANES), lambda j: (0, 0))] + other_specs,
        out_shape=[jax.ShapeDtypeStruct((D_MODEL, N_HEAD), BF16),
                   jax.ShapeDtypeStruct((D_MODEL, N_TAIL), BF16),
                   jax.ShapeDtypeStruct((D_MODEL, LANES), BF16)]
                  + [jax.ShapeDtypeStruct(w.shape, BF16) for w in others],
        compiler_params=pltpu.CompilerParams(
            dimension_semantics=("arbitrary",), vmem_limit_bytes=VMEM_LIMIT_BYTES),
        name="cast_weights",
    )(w_in_t, w_in_t, *others)


def _resident(shape):
    return pl.BlockSpec(shape, lambda *_: (0,) * len(shape), pipeline_mode=pl.Buffered(1))


def _level_map(t):
    idx = np.arange(t)
    xor = idx[:, None] ^ idx[None, :]
    lev = np.floor(np.log2(np.maximum(xor, 1))).astype(np.int32)
    lev = np.where(idx[:, None] > idx[None, :], lev, -2)
    lev = np.where(idx[:, None] == idx[None, :], -1, lev)
    return jnp.asarray(lev, dtype=jnp.int32)


def _mixer_weight_specs():
    return [
        _resident((1, D_MODEL)),
        _resident((D_MODEL, N_HEAD)),
        _resident((D_MODEL, N_TAIL)),
        _resident((D_MODEL, LANES)),
        _resident((LANES, QK_DIM)),
        _resident((1, QK_DIM)),
        _resident((3, CONV_DIM)),
        _resident((CONV_DIM, D_MODEL)),
        _resident((1, GLA_DV)),
        _resident((V_DIM, D_MODEL)),
        _resident((D_MODEL, D_MODEL)),
        _resident((1, D_MODEL)),
    ]


def _prompt_mixer(x, mixer_weights, later_weights):
    b, s, _ = x.shape
    t = SEQ_TILE
    c = GLA_CHUNK
    n_steps = b * (s // t)
    tri = jnp.asarray(np.tril(np.ones((c, c), np.float32)), dtype=BF16)

    def side_spec(shape):
        n_blocks = n_steps
        while shape[0] % (n_blocks * BF16_ROWS):
            n_blocks //= 2
        rep = n_steps // n_blocks
        return pl.BlockSpec((shape[0] // n_blocks, shape[1]),
                            lambda i, j: ((i * (s // t) + j) // rep, 0))

    side_specs = [side_spec(w.shape) for w in later_weights]
    return pl.pallas_call(
        _prompt_mixer_kernel,
        grid=(b, s // t),
        in_specs=[pl.BlockSpec((1, t, D_MODEL), lambda i, j: (i, j, 0)),
                  _resident((c // 2, c // 2)), _resident((c, c))] + _mixer_weight_specs() + side_specs,
        out_specs=[pl.BlockSpec((1, t, D_MODEL), lambda i, j: (i, j, 0)),
                   pl.BlockSpec((1, 1, 2, CONV_DIM), lambda i, j: (0, i, 0, 0)),
                   pl.BlockSpec((1, 1, GLA_HEADS, GLA_DK, GLA_DV), lambda i, j: (0, i, 0, 0, 0))]
                  + side_specs,
        out_shape=[jax.ShapeDtypeStruct((b, s, D_MODEL), F32),
                   jax.ShapeDtypeStruct((1, b, 2, CONV_DIM), F32),
                   jax.ShapeDtypeStruct((1, b, GLA_HEADS, GLA_DK, GLA_DV), F32)]
                  + [jax.ShapeDtypeStruct(w.shape, BF16) for w in later_weights],
        scratch_shapes=[pltpu.VMEM((t // c, 2, c, QK_DIM), F32)],
        compiler_params=pltpu.CompilerParams(
            dimension_semantics=("arbitrary", "arbitrary"), vmem_limit_bytes=VMEM_LIMIT_BYTES),
        name="prompt_mixer",
    )(x, _level_map(c // 2), tri, *mixer_weights, *later_weights)


def _sample_proj(x, conv_buf, p, proj_weights):
    n = x.shape[0]
    grouped = lambda w: (n // SUBLANES, SUBLANES, w)
    out_shapes = [(n, 2, CONV_DIM), grouped(QK_DIM), grouped(QK_DIM), grouped(QK_DIM), grouped(V_DIM),
                  (n, D_MODEL), (n, V_DIM), (n, D_MODEL), (n, D_MODEL), (n, D_MODEL), (n, PLE_DIM)]
    whole = lambda shape: pl.BlockSpec(shape, lambda i: (0,) * len(shape))
    return pl.pallas_call(
        _sample_proj_kernel,
        grid=(1,),
        in_specs=[_resident((n, 1, D_MODEL)), _resident((n, 2, CONV_DIM))] + _mixer_weight_specs()[:8]
                 + [_resident((n, 1, PLE_DIM))],
        out_specs=[whole(s) for s in out_shapes],
        out_shape=[jax.ShapeDtypeStruct(s, F32) for s in out_shapes],
        compiler_params=pltpu.CompilerParams(
            dimension_semantics=("arbitrary",), vmem_limit_bytes=VMEM_LIMIT_BYTES),
        name="sample_proj",
    )(x, conv_buf, *proj_weights, p)


def _ffn_and_sample_state(h, p, ffn_weights, state, sample_proj, x_s, p_s, out_weights):
    rows = h.shape[0]
    n = x_s.shape[0]
    n_tiles = rows // FFN_TILE
    assert n_tiles * SAMPLE_BLOCK == n, "one sample state block per prompt row tile"
    q, k, a, v, y_a, g, gate_a, gate_b = sample_proj
    tile = lambda i: jnp.minimum(i, n_tiles - 1)
    state_spec = pl.BlockSpec((SAMPLE_BLOCK, GLA_HEADS, GLA_DK, GLA_DV), lambda i: (tile(i), 0, 0, 0))
    return pl.pallas_call(
        _ffn_kernel,
        grid=(n_tiles + 1,),
        in_specs=[pl.BlockSpec((FFN_TILE, D_MODEL), lambda i: (tile(i), 0)),
                  pl.BlockSpec((FFN_TILE, PLE_DIM), lambda i: (tile(i), 0)),
                  _resident((1, D_MODEL)), _resident((D_MODEL, D_FF)), _resident((D_MODEL, D_FF)),
                  _resident((D_FF, D_MODEL)), _resident((1, D_MODEL)), _resident((PLE_DIM, D_MODEL)),
                  _resident((D_MODEL, D_MODEL)), _resident((1, D_MODEL)),
                  state_spec, _resident(q.shape), _resident(k.shape), _resident(a.shape),
                  _resident(v.shape), _resident((n, D_MODEL)), _resident((n, PLE_DIM)),
                  _resident((n, D_MODEL)), _resident((n, V_DIM)), _resident((n, D_MODEL)),
                  _resident((n, D_MODEL)),
                  _resident((1, GLA_DV)), _resident((V_DIM, D_MODEL)), _resident((D_MODEL, D_MODEL)),
                  _resident((1, D_MODEL))],
        out_specs=[pl.BlockSpec((FFN_TILE, D_MODEL), lambda i: (tile(i), 0)),
                   state_spec,
                   pl.BlockSpec((n, D_MODEL), lambda i: (0, 0))],
        out_shape=[jax.ShapeDtypeStruct((rows, D_MODEL), F32),
                   jax.ShapeDtypeStruct(state.shape, F32),
                   jax.ShapeDtypeStruct((n, D_MODEL), F32)],
        scratch_shapes=[pltpu.VMEM((n // SUBLANES, SUBLANES, V_DIM), F32)],
        compiler_params=pltpu.CompilerParams(
            dimension_semantics=("arbitrary",), vmem_limit_bytes=VMEM_LIMIT_BYTES),
        name="ffn_ple",
    )(h, p, *ffn_weights, state, q, k, a, v, x_s, p_s, y_a, g, gate_a, gate_b, *out_weights)


def kernel(x_prompt, x_sample, state_conv, state_gla, p_prompt, p_sample, w_norm_mix_pre, w_in, w_conv, w_a_out, w_gk, b_gk, w_gla_norm, w_b_out, w_o, w_norm_mix_post, w_norm_ffn_pre, w_ffn_gate, w_ffn_up, w_ffn_down, w_norm_ffn_post, w_ple_proj, w_ple_gate, w_norm_ple_post):
    depth = w_in.shape[0]
    batch, seq, _ = x_prompt.shape
    n_dec = x_sample.shape[0]
    assert x_sample.shape[1] == 1, "the sample group carries one new token per sequence"
    assert seq % SEQ_TILE == 0 and (batch * seq) % FFN_TILE == 0 and n_dec % SAMPLE_BLOCK == 0

    hp = x_prompt
    hs = x_sample
    conv_p, gla_p, conv_s, gla_s = [], [], [], []
    for i in range(depth):
        row = lambda w: w[i].reshape(1, -1)
        w_head, w_tail, w_gklr, w_a, w_b, w_o_b = _cast_w_in(
            jnp.swapaxes(w_in[i], 0, 1), (w_a_out[i], w_b_out[i], w_o[i]))
        w_gk_pad = jnp.pad(w_gk[i], ((0, LANES - GATE_RANK), (0, 0))).astype(BF16)
        mixer_weights = (row(w_norm_mix_pre), w_head, w_tail, w_gklr, w_gk_pad, row(b_gk), w_conv[i],
                         w_a, row(w_gla_norm), w_b, w_o_b, row(w_norm_mix_post))

        hp_mid, cbp, sp, w_gate, w_up, w_down, w_pp, w_pg = _prompt_mixer(
            hp, mixer_weights, (w_ffn_gate[i], w_ffn_up[i], w_ffn_down[i], w_ple_proj[i], w_ple_gate[i]))
        ffn_weights = (row(w_norm_ffn_pre), w_gate, w_up, w_down, row(w_norm_ffn_post), w_pp, w_pg,
                       row(w_norm_ple_post))
        cbs, *sample_proj, xs2d, ps2d = _sample_proj(hs, state_conv[i], p_sample[i], mixer_weights[:8])
        hp, ss, hs = _ffn_and_sample_state(
            hp_mid.reshape(batch * seq, D_MODEL), p_prompt[i].reshape(batch * seq, PLE_DIM), ffn_weights,
            state_gla[i], sample_proj, xs2d, ps2d, mixer_weights[8:])
        hp = hp.reshape(batch, seq, D_MODEL)
        hs = hs.reshape(n_dec, 1, D_MODEL)

        conv_p.append(cbp[0]); gla_p.append(sp[0])
        conv_s.append(cbs); gla_s.append(ss)
    return (hp, hs, jnp.stack(conv_p), jnp.stack(gla_p),
            jnp.stack(conv_s), jnp.stack(gla_s))
```

```python
import functools

import numpy as np
import jax
import jax.numpy as jnp
from jax import lax
from jax.experimental import pallas as pl
from jax.experimental.pallas import tpu as pltpu

D_MODEL = 1024
CONV_DIM = D_MODEL
GLA_HEADS = 4
GLA_DK = 128
GLA_DV = 256
QK_DIM = GLA_HEADS * GLA_DK
V_DIM = GLA_HEADS * GLA_DV
GATE_RANK = 16
GATE_NORMALIZER = 16.0
D_FF = 2816
PLE_DIM = 256
EPS = 1e-6
LOG2_E = 1.4426950408889634

LANES = 128
SUBLANES = 8
BF16_ROWS = 16
VMEM_LIMIT_BYTES = 56 * 1024 * 1024

OFF_B, OFF_C, OFF_X = 0, 1024, 2048
OFF_Q, OFF_K, OFF_V, OFF_G = 3072, 3584, 4096, 5120
OFF_GA, OFF_GB = 6144, 7168
N_HEAD = 6144
N_TAIL = 2048
GKLR_START = 6144

GLA_CHUNK = 256
N_LEVELS = 8
SEQ_TILE = 512
FFN_TILE = 512
FFN_SUBTILE = 512
FFN_CHUNK = 1408
SAMPLE_BLOCK = 4
W_IN_BLOCK = 1024

F32 = jnp.float32
BF16 = jnp.bfloat16


def _rms(x, w):
    return x * lax.rsqrt(jnp.mean(x * x, axis=-1, keepdims=True) + EPS) * w


def _sigmoid(x):
    return 1.0 / (1.0 + jnp.exp2(x * -LOG2_E))


def _log2_sigmoid(x, scale):
    return (jnp.minimum(x, 0.0) - jnp.log(1.0 + jnp.exp2(jnp.abs(x) * -LOG2_E))) * (scale * LOG2_E)


def _dot(a, b):
    return jnp.dot(a, b, preferred_element_type=F32)


def _dot_nt(a, b):
    return lax.dot_general(a, b, (((1,), (1,)), ((), ())), preferred_element_type=F32)


def _column_broadcast(row):
    return jnp.broadcast_to(row, (LANES, LANES)).T


def _projections(hn, w_head_ref, w_tail_ref, w_gklr_ref, w_gk_ref, b_gk_ref):
    def proj(off, width):
        if off < N_HEAD:
            return _dot(hn, w_head_ref[:, off:off + width])
        return _dot(hn, w_tail_ref[:, off - N_HEAD:off - N_HEAD + width])

    gk_lr = _dot(hn, w_gklr_ref[...])
    gk = _dot(gk_lr.astype(BF16), w_gk_ref[...]) + b_gk_ref[...]
    return proj, _log2_sigmoid(gk, 1.0 / GATE_NORMALIZER)


def _mix_out(x, o_heads, g, gate_a, gate_b, y_a, w_gn, w_b_ref, w_o_ref, wn_post):
    normed = []
    for h in range(GLA_HEADS):
        o = o_heads[h]
        gh = g[:, h * GLA_DV:(h + 1) * GLA_DV]
        o = o * lax.rsqrt(jnp.mean(o * o, axis=-1, keepdims=True) + EPS) * w_gn
        normed.append((o * (gh * _sigmoid(gh))).astype(BF16))
    y_b = _dot(jnp.concatenate(normed, axis=1), w_b_ref[...])
    merged = _sigmoid(gate_a) * y_a + _sigmoid(gate_b) * y_b
    mix = _dot(merged.astype(BF16), w_o_ref[...])
    return x + _rms(mix, wn_post)


def _prompt_mixer_kernel(x_ref, lev_ref, tri_ref, wn_pre_ref, w_head_ref, w_tail_ref, w_gklr_ref,
                         w_gk_ref, b_gk_ref, w_conv_ref, w_a_ref, w_gn_ref, w_b_ref, w_o_ref,
                         wn_post_ref, *rest):
    n_side = (len(rest) - 4) // 2
    side_src = rest[:n_side]
    h_ref, conv_ref, state_ref = rest[n_side:n_side + 3]
    side_dst = rest[n_side + 3:2 * n_side + 3]
    dec_ref = rest[-1]

    @pl.when(pl.program_id(1) == 0)
    def _():
        conv_ref[...] = jnp.zeros_like(conv_ref)
        state_ref[...] = jnp.zeros_like(state_ref)

    for src, dst in zip(side_src, side_dst):
        dst[...] = src[...].astype(BF16)

    t = SEQ_TILE
    x = x_ref[0]
    hn = _rms(x, wn_pre_ref[...]).astype(BF16)
    proj, logw2 = _projections(hn, w_head_ref, w_tail_ref, w_gklr_ref, w_gk_ref, b_gk_ref)

    u = proj(OFF_C, CONV_DIM) * proj(OFF_X, CONV_DIM)
    prev2 = conv_ref[0, 0, 0:1, :]
    prev1 = conv_ref[0, 0, 1:2, :]
    row = lax.broadcasted_iota(jnp.int32, (t, CONV_DIM), 0)
    u1 = jnp.where(row == 0, prev1, pltpu.roll(u, 1, 0))
    u2 = jnp.where(row == 0, prev2, jnp.where(row == 1, prev1, pltpu.roll(u, 2, 0)))
    wc = w_conv_ref[...]
    y_conv = wc[0:1] * u2 + wc[1:2] * u1 + wc[2:3] * u
    conv_ref[0, 0] = u[t - 2:t]
    y_a = _dot((proj(OFF_B, CONV_DIM) * y_conv).astype(BF16), w_a_ref[...])

    q = proj(OFF_Q, QK_DIM) * (GLA_DK ** -0.5)
    k = proj(OFF_K, QK_DIM)
    v = proj(OFF_V, V_DIM).astype(BF16)

    o_chunks = []
    for c in range(SEQ_TILE // GLA_CHUNK):
        rows = slice(c * GLA_CHUNK, (c + 1) * GLA_CHUNK)
        o_chunks.append(_gla_chunk(q[rows], k[rows], v[rows], logw2[rows], dec_ref.at[c], lev_ref,
                                   tri_ref, state_ref))
    o_heads = [jnp.concatenate([o[h] for o in o_chunks], axis=0) for h in range(GLA_HEADS)]

    h_ref[0] = _mix_out(x, o_heads, proj(OFF_G, V_DIM), proj(OFF_GA, D_MODEL), proj(OFF_GB, D_MODEL),
                        y_a, w_gn_ref[...], w_b_ref, w_o_ref, wn_post_ref[...])


def _gla_chunk(q, k, v, logw2, dec_ref, lev_ref, tri_ref, state_ref):
    t = GLA_CHUNK
    dec_ref[0] = logw2
    logw2 = dec_ref[0]
    hi = logw2.astype(BF16)
    lo = (logw2 - hi.astype(F32)).astype(BF16)
    cum = _dot(tri_ref[...], hi) + _dot(tri_ref[...], lo)
    dec_ref[1] = cum
    cum_last = cum[t - 1:t]

    half = t // 2
    lev = lev_ref[...]
    rowq = lax.broadcasted_iota(jnp.int32, (t, QK_DIM), 0)

    def level_log2_factor(i):
        m = 1 << i
        pieces = []
        for blk in range(t // (2 * m)):
            r = blk * 2 * m + m - 1
            pieces.append(jnp.broadcast_to(dec_ref[1, r:r + 1, :], (2 * m, QK_DIM)))
        return -jnp.abs(cum - jnp.concatenate(pieces, axis=0))

    def head(a, h):
        return a[:, h * GLA_DK:(h + 1) * GLA_DK]

    def split_level(i):
        m = 1 << i
        qs, ks = [], []
        for blk in range(t // (2 * m)):
            first = slice(blk * 2 * m, blk * 2 * m + m)
            second = slice(blk * 2 * m + m, (blk + 1) * 2 * m)
            ref_row = dec_ref[1, blk * 2 * m + m - 1:blk * 2 * m + m, :]
            ks += [(k[first] * jnp.exp2(ref_row - cum[first])).astype(BF16), k_lv[0][second]]
            qs += [q_lv[0][first], (q[second] * jnp.exp2(cum[second] - ref_row)).astype(BF16)]
        return jnp.concatenate(qs, axis=0), jnp.concatenate(ks, axis=0)

    q_lv = [q.astype(BF16)]
    k_lv = [k.astype(BF16)]
    step_decay = jnp.exp2(logw2)
    r4 = rowq & 3
    q0 = q * step_decay
    q_lv += [q0.astype(BF16),
             (q0 * jnp.where(r4 == 3, pltpu.roll(step_decay, 1, 0), 1.0)).astype(BF16)]
    k_lv += [k_lv[0], (k * jnp.where(r4 == 0, pltpu.roll(step_decay, t - 1, 0), 1.0)).astype(BF16)]
    for i in range(2, N_LEVELS - 1):
        if (1 << i) % BF16_ROWS == 0:
            q_i, k_i = split_level(i)
        else:
            e = jnp.exp2(level_log2_factor(i))
            q_i, k_i = (q * e).astype(BF16), (k * e).astype(BF16)
        q_lv.append(q_i)
        k_lv.append(k_i)
    cum_mid = dec_ref[1, half - 1:half, :]
    q_top = (q[half:] * jnp.exp2(cum[half:] - cum_mid)).astype(BF16)
    k_top = (k[:half] * jnp.exp2(cum_mid - cum[:half])).astype(BF16)

    q_in = (q * jnp.exp2(cum)).astype(BF16)
    k_out = k * jnp.exp2(cum_last - cum)
    a_last = jnp.exp2(cum_last)

    o_heads = []
    for h in range(GLA_HEADS):
        vh = v[:, h * GLA_DV:(h + 1) * GLA_DV]
        diag = [0.0, 0.0]
        for i in range(N_LEVELS):
            s_i = _dot_nt(head(q_lv[i], h), head(k_lv[i], h))
            diag = [jnp.where(lev == i - 1, s_i[r0:r0 + half, r0:r0 + half], diag[j])
                    for j, r0 in enumerate((0, half))]
        diag = [d.astype(BF16) for d in diag]
        p_low = jnp.concatenate([_dot_nt(head(q_top, h), head(k_top, h)).astype(BF16), diag[1]], axis=1)
        s_old = state_ref[0, 0, h]
        o = jnp.concatenate([_dot(diag[0], vh[:half]), _dot(p_low, vh)], axis=0)
        o_heads.append(o + _dot(head(q_in, h), s_old.astype(BF16)))
        a_col = _column_broadcast(head(a_last, h))
        a_col = jnp.concatenate([a_col, a_col], axis=1)
        state_ref[0, 0, h] = a_col * s_old + _dot(head(k_out, h).T.astype(BF16), vh)
    return o_heads


def _sample_proj_kernel(x_ref, cbuf_ref, wn_pre_ref, w_head_ref, w_tail_ref, w_gklr_ref, w_gk_ref,
                        b_gk_ref, w_conv_ref, w_a_ref,
                        ps_ref, conv_ref, q_ref, k_ref, a_ref, v_ref, ya_ref, g_ref, ga_ref, gb_ref,
                        xs2_ref, ps2_ref):
    x = x_ref[:, 0, :]
    xs2_ref[...] = x
    ps2_ref[...] = ps_ref[:, 0, :]
    hn = _rms(x, wn_pre_ref[...]).astype(BF16)
    proj, logw2 = _projections(hn, w_head_ref, w_tail_ref, w_gklr_ref, w_gk_ref, b_gk_ref)
    u = proj(OFF_C, CONV_DIM) * proj(OFF_X, CONV_DIM)
    buf0 = cbuf_ref[:, 0, :]
    buf1 = cbuf_ref[:, 1, :]
    wc = w_conv_ref[...]
    y_conv = wc[0:1] * buf0 + wc[1:2] * buf1 + wc[2:3] * u
    conv_ref[:, 0, :] = buf1
    conv_ref[:, 1, :] = u
    ya_ref[...] = _dot((proj(OFF_B, CONV_DIM) * y_conv).astype(BF16), w_a_ref[...])
    grouped = lambda a: a.reshape(a.shape[0] // SUBLANES, SUBLANES, a.shape[1])
    q_ref[...] = grouped(proj(OFF_Q, QK_DIM) * (GLA_DK ** -0.5))
    k_ref[...] = grouped(proj(OFF_K, QK_DIM))
    a_ref[...] = grouped(jnp.exp2(logw2))
    v_ref[...] = grouped(proj(OFF_V, V_DIM))
    g_ref[...] = proj(OFF_G, V_DIM)
    ga_ref[...] = proj(OFF_GA, D_MODEL)
    gb_ref[...] = proj(OFF_GB, D_MODEL)


def _sample_state_update(blk, st_ref, st_out_ref, q_ref, k_ref, a_ref, v_ref, o_scr):
    per_group = SUBLANES // SAMPLE_BLOCK
    group = blk // per_group
    first = (blk % per_group) * SAMPLE_BLOCK
    row_id = lax.broadcasted_iota(jnp.int32, (SUBLANES, V_DIM), 0)
    o_tile = o_scr[group]
    for n in range(SAMPLE_BLOCK):
        r = first + n
        q_row = q_ref[group, pl.ds(r, 1), :]
        k_row = k_ref[group, pl.ds(r, 1), :]
        a_row = a_ref[group, pl.ds(r, 1), :]
        v_row = v_ref[group, pl.ds(r, 1), :]
        o_parts = []
        for h in range(GLA_HEADS):
            sl = slice(h * GLA_DK, (h + 1) * GLA_DK)
            a_col = _column_broadcast(a_row[:, sl])
            k_col = _column_broadcast(k_row[:, sl])
            q_col = _column_broadcast(q_row[:, sl])
            for half in range(GLA_DV // LANES):
                c0 = half * LANES
                s_old = st_ref[n, h, :, c0:c0 + LANES]
                vv = v_row[:, h * GLA_DV + c0:h * GLA_DV + c0 + LANES]
                s_new = a_col * s_old + k_col * vv
                st_out_ref[n, h, :, c0:c0 + LANES] = s_new
                o_parts.append(jnp.sum(q_col * s_new, axis=0, keepdims=True))
        o_row = jnp.concatenate(o_parts, axis=1)
        o_tile = jnp.where(row_id == r, o_row, o_tile)
    o_scr[group] = o_tile


def _ffn_rows(h, p, wn_pre_ref, w_gate_ref, w_up_ref, w_down_ref, wn_post_ref, w_pp_ref, w_pg_ref,
              wn_ple_ref):
    f = _rms(h, wn_pre_ref[...]).astype(BF16)
    acc = None
    for c in range(D_FF // FFN_CHUNK):
        sl = slice(c * FFN_CHUNK, (c + 1) * FFN_CHUNK)
        gate = _dot(f, w_gate_ref[:, sl])
        up = _dot(f, w_up_ref[:, sl])
        act = (gate * _sigmoid(gate) * up).astype(BF16)
        part = _dot(act, w_down_ref[sl, :])
        acc = part if acc is None else acc + part
    h = h + _rms(acc, wn_post_ref[...])
    e = _dot(p.astype(BF16), w_pp_ref[...]) * _sigmoid(_dot(h.astype(BF16), w_pg_ref[...]))
    return h + _rms(e, wn_ple_ref[...])


def _ffn_kernel(h_ref, p_ref, wn_pre_ref, w_gate_ref, w_up_ref, w_down_ref, wn_post_ref,
                w_pp_ref, w_pg_ref, wn_ple_ref,
                st_ref, q_ref, k_ref, a_ref, v_ref, xs_ref, ps_ref, ya_ref, g_ref, ga_ref, gb_ref,
                w_gn_ref, w_b_ref, w_o_ref, wn_mix_post_ref,
                out_ref, st_out_ref, outs_ref, o_scr):
    step = pl.program_id(0)
    n_tiles = pl.num_programs(0) - 1
    ffn_refs = (wn_pre_ref, w_gate_ref, w_up_ref, w_down_ref, wn_post_ref, w_pp_ref, w_pg_ref, wn_ple_ref)

    @pl.when(step == 0)
    def _():
        o_scr[...] = jnp.zeros_like(o_scr)

    @pl.when(step < n_tiles)
    def _():
        for r in range(FFN_TILE // FFN_SUBTILE):
            rows = slice(r * FFN_SUBTILE, (r + 1) * FFN_SUBTILE)
            out_ref[rows, :] = _ffn_rows(h_ref[rows, :], p_ref[rows, :], *ffn_refs)
        _sample_state_update(step, st_ref, st_out_ref, q_ref, k_ref, a_ref, v_ref, o_scr)

    @pl.when(step == n_tiles)
    def _():
        o = o_scr[...].reshape(xs_ref.shape[0], V_DIM)
        o_heads = [o[:, h * GLA_DV:(h + 1) * GLA_DV] for h in range(GLA_HEADS)]
        hs = _mix_out(xs_ref[...], o_heads, g_ref[...], ga_ref[...], gb_ref[...], ya_ref[...],
                      w_gn_ref[...], w_b_ref, w_o_ref, wn_mix_post_ref[...])
        outs_ref[...] = _ffn_rows(hs, ps_ref[...], *ffn_refs)


def _cast_w_in_kernel(wt_ref, gk_ref, *refs):
    n = (len(refs) - 3) // 2
    head_ref, tail_ref, gklr_ref = refs[n:n + 3]
    for src, dst in zip(refs[:n], refs[n + 3:]):
        dst[...] = src[...].astype(BF16)
    j = pl.program_id(0)
    blk = wt_ref[...].T.astype(BF16)

    @pl.when(j < N_HEAD // W_IN_BLOCK)
    def _():
        head_ref[...] = blk

    @pl.when(j >= N_HEAD // W_IN_BLOCK)
    def _():
        tail_ref[...] = blk

    @pl.when(j == 0)
    def _():
        rows = jnp.concatenate([gk_ref[...], jnp.zeros((LANES - GATE_RANK, D_MODEL), F32)], axis=0)
        gklr_ref[...] = rows.T.astype(BF16)


def _cast_w_in(w_in_t, others):
    n_head = N_HEAD // W_IN_BLOCK
    n_tail = N_TAIL // W_IN_BLOCK
    other_specs = [pl.BlockSpec((w.shape[0] // (n_head + n_tail), w.shape[1]), lambda j: (j, 0))
                   for w in others]

    def src_row(j):
        row = jnp.where(j < n_head, j * W_IN_BLOCK, GKLR_START + GATE_RANK + (j - n_head) * W_IN_BLOCK)
        return pl.multiple_of(row, GATE_RANK)

    return pl.pallas_call(
        _cast_w_in_kernel,
        grid=(n_head + n_tail,),
        in_specs=[pl.BlockSpec((pl.Element(W_IN_BLOCK), pl.Element(D_MODEL)), lambda j: (src_row(j), 0)),
                  pl.BlockSpec((pl.Element(GATE_RANK), pl.Element(D_MODEL)), lambda j: (GKLR_START, 0))]
                 + other_specs,
        out_specs=[pl.BlockSpec((D_MODEL, W_IN_BLOCK), lambda j: (0, jnp.minimum(j, n_head - 1))),
                   pl.BlockSpec((D_MODEL, W_IN_BLOCK), lambda j: (0, jnp.maximum(j - n_head, 0))),
                   pl.BlockSpec((D_MODEL, LANES), lambda j: (0, 0))] + other_specs,
        out_shape=[jax.ShapeDtypeStruct((D_MODEL, N_HEAD), BF16),
                   jax.ShapeDtypeStruct((D_MODEL, N_TAIL), BF16),
                   jax.ShapeDtypeStruct((D_MODEL, LANES), BF16)]
                  + [jax.ShapeDtypeStruct(w.shape, BF16) for w in others],
        compiler_params=pltpu.CompilerParams(
            dimension_semantics=("arbitrary",), vmem_limit_bytes=VMEM_LIMIT_BYTES),
        name="cast_weights",
    )(w_in_t, w_in_t, *others)


def _resident(shape):
    return pl.BlockSpec(shape, lambda *_: (0,) * len(shape), pipeline_mode=pl.Buffered(1))


def _level_map(t):
    idx = np.arange(t)
    xor = idx[:, None] ^ idx[None, :]
    lev = np.floor(np.log2(np.maximum(xor, 1))).astype(np.int32)
    lev = np.where(idx[:, None] > idx[None, :], lev, -2)
    lev = np.where(idx[:, None] == idx[None, :], -1, lev)
    return jnp.asarray(lev, dtype=jnp.int32)


def _mixer_weight_specs():
    return [
        _resident((1, D_MODEL)),
        _resident((D_MODEL, N_HEAD)),
        _resident((D_MODEL, N_TAIL)),
        _resident((D_MODEL, LANES)),
        _resident((LANES, QK_DIM)),
        _resident((1, QK_DIM)),
        _resident((3, CONV_DIM)),
        _resident((CONV_DIM, D_MODEL)),
        _resident((1, GLA_DV)),
        _resident((V_DIM, D_MODEL)),
        _resident((D_MODEL, D_MODEL)),
        _resident((1, D_MODEL)),
    ]


def _prompt_mixer(x, mixer_weights, later_weights):
    b, s, _ = x.shape
    t = SEQ_TILE
    c = GLA_CHUNK
    n_steps = b * (s // t)
    tri = jnp.asarray(np.tril(np.ones((c, c), np.float32)), dtype=BF16)

    def side_spec(shape):
        n_blocks = n_steps
        while shape[0] % (n_blocks * BF16_ROWS):
            n_blocks //= 2
        rep = n_steps // n_blocks
        return pl.BlockSpec((shape[0] // n_blocks, shape[1]),
                            lambda i, j: ((i * (s // t) + j) // rep, 0))

    side_specs = [side_spec(w.shape) for w in later_weights]
    return pl.pallas_call(
        _prompt_mixer_kernel,
        grid=(b, s // t),
        in_specs=[pl.BlockSpec((1, t, D_MODEL), lambda i, j: (i, j, 0)),
                  _resident((c // 2, c // 2)), _resident((c, c))] + _mixer_weight_specs() + side_specs,
        out_specs=[pl.BlockSpec((1, t, D_MODEL), lambda i, j: (i, j, 0)),
                   pl.BlockSpec((1, 1, 2, CONV_DIM), lambda i, j: (0, i, 0, 0)),
                   pl.BlockSpec((1, 1, GLA_HEADS, GLA_DK, GLA_DV), lambda i, j: (0, i, 0, 0, 0))]
                  + side_specs,
        out_shape=[jax.ShapeDtypeStruct((b, s, D_MODEL), F32),
                   jax.ShapeDtypeStruct((1, b, 2, CONV_DIM), F32),
                   jax.ShapeDtypeStruct((1, b, GLA_HEADS, GLA_DK, GLA_DV), F32)]
                  + [jax.ShapeDtypeStruct(w.shape, BF16) for w in later_weights],
        scratch_shapes=[pltpu.VMEM((t // c, 2, c, QK_DIM), F32)],
        compiler_params=pltpu.CompilerParams(
            dimension_semantics=("arbitrary", "arbitrary"), vmem_limit_bytes=VMEM_LIMIT_BYTES),
        name="prompt_mixer",
    )(x, _level_map(c // 2), tri, *mixer_weights, *later_weights)


def _sample_proj(x, conv_buf, p, proj_weights):
    n = x.shape[0]
    grouped = lambda w: (n // SUBLANES, SUBLANES, w)
    out_shapes = [(n, 2, CONV_DIM), grouped(QK_DIM), grouped(QK_DIM), grouped(QK_DIM), grouped(V_DIM),
                  (n, D_MODEL), (n, V_DIM), (n, D_MODEL), (n, D_MODEL), (n, D_MODEL), (n, PLE_DIM)]
    whole = lambda shape: pl.BlockSpec(shape, lambda i: (0,) * len(shape))
    return pl.pallas_call(
        _sample_proj_kernel,
        grid=(1,),
        in_specs=[_resident((n, 1, D_MODEL)), _resident((n, 2, CONV_DIM))] + _mixer_weight_specs()[:8]
                 + [_resident((n, 1, PLE_DIM))],
        out_specs=[whole(s) for s in out_shapes],
        out_shape=[jax.ShapeDtypeStruct(s, F32) for s in out_shapes],
        compiler_params=pltpu.CompilerParams(
            dimension_semantics=("arbitrary",), vmem_limit_bytes=VMEM_LIMIT_BYTES),
        name="sample_proj",
    )(x, conv_buf, *proj_weights, p)


def _ffn_and_sample_state(h, p, ffn_weights, state, sample_proj, x_s, p_s, out_weights):
    rows = h.shape[0]
    n = x_s.shape[0]
    n_tiles = rows // FFN_TILE
    assert n_tiles * SAMPLE_BLOCK == n, "one sample state block per prompt row tile"
    q, k, a, v, y_a, g, gate_a, gate_b = sample_proj
    tile = lambda i: jnp.minimum(i, n_tiles - 1)
    state_spec = pl.BlockSpec((SAMPLE_BLOCK, GLA_HEADS, GLA_DK, GLA_DV), lambda i: (tile(i), 0, 0, 0))
    return pl.pallas_call(
        _ffn_kernel,
        grid=(n_tiles + 1,),
        in_specs=[pl.BlockSpec((FFN_TILE, D_MODEL), lambda i: (tile(i), 0)),
                  pl.BlockSpec((FFN_TILE, PLE_DIM), lambda i: (tile(i), 0)),
                  _resident((1, D_MODEL)), _resident((D_MODEL, D_FF)), _resident((D_MODEL, D_FF)),
                  _resident((D_FF, D_MODEL)), _resident((1, D_MODEL)), _resident((PLE_DIM, D_MODEL)),
                  _resident((D_MODEL, D_MODEL)), _resident((1, D_MODEL)),
                  state_spec, _resident(q.shape), _resident(k.shape), _resident(a.shape),
                  _resident(v.shape), _resident((n, D_MODEL)), _resident((n, PLE_DIM)),
                  _resident((n, D_MODEL)), _resident((n, V_DIM)), _resident((n, D_MODEL)),
                  _resident((n, D_MODEL)),
                  _resident((1, GLA_DV)), _resident((V_DIM, D_MODEL)), _resident((D_MODEL, D_MODEL)),
                  _resident((1, D_MODEL))],
        out_specs=[pl.BlockSpec((FFN_TILE, D_MODEL), lambda i: (tile(i), 0)),
                   state_spec,
                   pl.BlockSpec((n, D_MODEL), lambda i: (0, 0))],
        out_shape=[jax.ShapeDtypeStruct((rows, D_MODEL), F32),
                   jax.ShapeDtypeStruct(state.shape, F32),
                   jax.ShapeDtypeStruct((n, D_MODEL), F32)],
        scratch_shapes=[pltpu.VMEM((n // SUBLANES, SUBLANES, V_DIM), F32)],
        compiler_params=pltpu.CompilerParams(
            dimension_semantics=("arbitrary",), vmem_limit_bytes=VMEM_LIMIT_BYTES),
        name="ffn_ple",
    )(h, p, *ffn_weights, state, q, k, a, v, x_s, p_s, y_a, g, gate_a, gate_b, *out_weights)


def kernel(x_prompt, x_sample, state_conv, state_gla, p_prompt, p_sample, w_norm_mix_pre, w_in, w_conv, w_a_out, w_gk, b_gk, w_gla_norm, w_b_out, w_o, w_norm_mix_post, w_norm_ffn_pre, w_ffn_gate, w_ffn_up, w_ffn_down, w_norm_ffn_post, w_ple_proj, w_ple_gate, w_norm_ple_post):
    depth = w_in.shape[0]
    batch, seq, _ = x_prompt.shape
    n_dec = x_sample.shape[0]
    assert x_sample.shape[1] == 1, "the sample group carries one new token per sequence"
    assert seq % SEQ_TILE == 0 and (batch * seq) % FFN_TILE == 0 and n_dec % SAMPLE_BLOCK == 0

    hp = x_prompt
    hs = x_sample
    conv_p, gla_p, conv_s, gla_s = [], [], [], []
    for i in range(depth):
        row = lambda w: w[i].reshape(1, -1)
        w_head, w_tail, w_gklr, w_a, w_b, w_o_b = _cast_w_in(
            jnp.swapaxes(w_in[i], 0, 1), (w_a_out[i], w_b_out[i], w_o[i]))
        w_gk_pad = jnp.pad(w_gk[i], ((0, LANES - GATE_RANK), (0, 0))).astype(BF16)
        mixer_weights = (row(w_norm_mix_pre), w_head, w_tail, w_gklr, w_gk_pad, row(b_gk), w_conv[i],
                         w_a, row(w_gla_norm), w_b, w_o_b, row(w_norm_mix_post))

        hp_mid, cbp, sp, w_gate, w_up, w_down, w_pp, w_pg = _prompt_mixer(
            hp, mixer_weights, (w_ffn_gate[i], w_ffn_up[i], w_ffn_down[i], w_ple_proj[i], w_ple_gate[i]))
        ffn_weights = (row(w_norm_ffn_pre), w_gate, w_up, w_down, row(w_norm_ffn_post), w_pp, w_pg,
                       row(w_norm_ple_post))
        cbs, *sample_proj, xs2d, ps2d = _sample_proj(hs, state_conv[i], p_sample[i], mixer_weights[:8])
        hp, ss, hs = _ffn_and_sample_state(
            hp_mid.reshape(batch * seq, D_MODEL), p_prompt[i].reshape(batch * seq, PLE_DIM), ffn_weights,
            state_gla[i], sample_proj, xs2d, ps2d, mixer_weights[8:])
        hp = hp.reshape(batch, seq, D_MODEL)
        hs = hs.reshape(n_dec, 1, D_MODEL)

        conv_p.append(cbp[0]); gla_p.append(sp[0])
        conv_s.append(cbs); gla_s.append(ss)
    return (hp, hs, jnp.stack(conv_p), jnp.stack(gla_p),
            jnp.stack(conv_s), jnp.stack(gla_s))
```

```python
import numpy as np
import jax
import jax.numpy as jnp
from jax import lax
from jax.experimental import pallas as pl
from jax.experimental.pallas import tpu as pltpu

D_MODEL = 1024
CONV_DIM = D_MODEL
GLA_HEADS = 4
GLA_DK = 128
GLA_DV = 256
QK_DIM = GLA_HEADS * GLA_DK
V_DIM = GLA_HEADS * GLA_DV
GATE_RANK = 16
GATE_NORMALIZER = 16.0
D_FF = 2816
PLE_DIM = 256
EPS = 1e-6
LOG2_E = 1.4426950408889634

LANES = 128
SUBLANES = 8
BF16_ROWS = 16
VMEM_LIMIT_BYTES = 56 * 1024 * 1024

OFF_B, OFF_C, OFF_X = 0, 1024, 2048
OFF_Q, OFF_K, OFF_V, OFF_G = 3072, 3584, 4096, 5120
OFF_GA, OFF_GB = 6144, 7168
N_HEAD = 6144
N_TAIL = 2048
GKLR_START = 6144

GLA_CHUNK = 256
N_LEVELS = 8
SEQ_TILE = 512
FFN_TILE = 512
FFN_SUBTILE = 512
FFN_CHUNK = 1408
SAMPLE_BLOCK = 4
W_IN_BLOCK = 2048

F32 = jnp.float32
BF16 = jnp.bfloat16


def _rms(x, w):
    return x * lax.rsqrt(jnp.mean(x * x, axis=-1, keepdims=True) + EPS) * w


def _sigmoid(x):
    return 1.0 / (1.0 + jnp.exp2(x * -LOG2_E))


def _log2_sigmoid(x, scale):
    return (jnp.minimum(x, 0.0) - jnp.log(1.0 + jnp.exp2(jnp.abs(x) * -LOG2_E))) * (scale * LOG2_E)


def _dot(a, b):
    return jnp.dot(a, b, preferred_element_type=F32)


def _dot_nt(a, b):
    return lax.dot_general(a, b, (((1,), (1,)), ((), ())), preferred_element_type=F32)


def _column_broadcast(row):
    return jnp.broadcast_to(row, (LANES, LANES)).T


def _projections(hn, w_head_ref, w_tail_ref, w_gklr_ref, w_gk_ref, b_gk_ref):
    def proj(off, width):
        if off < N_HEAD:
            return _dot(hn, w_head_ref[:, off:off + width])
        return _dot(hn, w_tail_ref[:, off - N_HEAD:off - N_HEAD + width])

    gk_lr = _dot(hn, w_gklr_ref[...])
    gk = _dot(gk_lr.astype(BF16), w_gk_ref[...]) + b_gk_ref[...]
    return proj, _log2_sigmoid(gk, 1.0 / GATE_NORMALIZER)


def _mix_out(x, o_heads, g, gate_a, gate_b, y_a, w_gn, w_b_ref, w_o_ref, wn_post):
    normed = []
    for h in range(GLA_HEADS):
        o = o_heads[h]
        gh = g[:, h * GLA_DV:(h + 1) * GLA_DV]
        o = o * lax.rsqrt(jnp.mean(o * o, axis=-1, keepdims=True) + EPS) * w_gn
        normed.append((o * (gh * _sigmoid(gh))).astype(BF16))
    y_b = _dot(jnp.concatenate(normed, axis=1), w_b_ref[...])
    merged = _sigmoid(gate_a) * y_a + _sigmoid(gate_b) * y_b
    mix = _dot(merged.astype(BF16), w_o_ref[...])
    return x + _rms(mix, wn_post)


def _prompt_mixer_kernel(x_ref, lev_ref, tri_ref, wn_pre_ref, w_head_ref, w_tail_ref, w_gklr_ref,
                         w_gk_ref, b_gk_ref, w_conv_ref, w_a_ref, w_gn_ref, w_b_ref, w_o_ref,
                         wn_post_ref, *rest):
    n_side = (len(rest) - 4) // 2
    side_src = rest[:n_side]
    h_ref, conv_ref, state_ref = rest[n_side:n_side + 3]
    side_dst = rest[n_side + 3:2 * n_side + 3]
    dec_ref = rest[-1]

    @pl.when(pl.program_id(1) == 0)
    def _():
        conv_ref[...] = jnp.zeros_like(conv_ref)
        state_ref[...] = jnp.zeros_like(state_ref)

    for src, dst in zip(side_src, side_dst):
        dst[...] = src[...].astype(BF16)

    t = SEQ_TILE
    x = x_ref[0]
    hn = _rms(x, wn_pre_ref[...]).astype(BF16)
    proj, logw2 = _projections(hn, w_head_ref, w_tail_ref, w_gklr_ref, w_gk_ref, b_gk_ref)

    u = proj(OFF_C, CONV_DIM) * proj(OFF_X, CONV_DIM)
    prev2 = conv_ref[0, 0, 0:1, :]
    prev1 = conv_ref[0, 0, 1:2, :]
    row = lax.broadcasted_iota(jnp.int32, (t, CONV_DIM), 0)
    u1 = jnp.where(row == 0, prev1, pltpu.roll(u, 1, 0))
    u2 = jnp.where(row == 0, prev2, jnp.where(row == 1, prev1, pltpu.roll(u, 2, 0)))
    wc = w_conv_ref[...]
    y_conv = wc[0:1] * u2 + wc[1:2] * u1 + wc[2:3] * u
    conv_ref[0, 0] = u[t - 2:t]
    y_a = _dot((proj(OFF_B, CONV_DIM) * y_conv).astype(BF16), w_a_ref[...])

    q = proj(OFF_Q, QK_DIM) * (GLA_DK ** -0.5)
    k = proj(OFF_K, QK_DIM)
    v = proj(OFF_V, V_DIM).astype(BF16)

    o_chunks = []
    for c in range(SEQ_TILE // GLA_CHUNK):
        rows = slice(c * GLA_CHUNK, (c + 1) * GLA_CHUNK)
        o_chunks.append(_gla_chunk(q[rows], k[rows], v[rows], logw2[rows], dec_ref.at[c], lev_ref,
                                   tri_ref, state_ref))
    o_heads = [jnp.concatenate([o[h] for o in o_chunks], axis=0) for h in range(GLA_HEADS)]

    h_ref[0] = _mix_out(x, o_heads, proj(OFF_G, V_DIM), proj(OFF_GA, D_MODEL), proj(OFF_GB, D_MODEL),
                        y_a, w_gn_ref[...], w_b_ref, w_o_ref, wn_post_ref[...])


def _gla_chunk(q, k, v, logw2, dec_ref, lev_ref, tri_ref, state_ref):
    t = GLA_CHUNK
    dec_ref[0] = logw2
    logw2 = dec_ref[0]
    hi = logw2.astype(BF16)
    lo = (logw2 - hi.astype(F32)).astype(BF16)
    cum = _dot(tri_ref[...], hi) + _dot(tri_ref[...], lo)
    dec_ref[1] = cum
    cum_last = cum[t - 1:t]

    half = t // 2
    lev = lev_ref[...]
    rowq = lax.broadcasted_iota(jnp.int32, (t, QK_DIM), 0)

    def level_log2_factor(i):
        m = 1 << i
        pieces = []
        for blk in range(t // (2 * m)):
            r = blk * 2 * m + m - 1
            pieces.append(jnp.broadcast_to(dec_ref[1, r:r + 1, :], (2 * m, QK_DIM)))
        return -jnp.abs(cum - jnp.concatenate(pieces, axis=0))

    def head(a, h):
        return a[:, h * GLA_DK:(h + 1) * GLA_DK]

    def split_level(i):
        m = 1 << i
        qs, ks = [], []
        for blk in range(t // (2 * m)):
            first = slice(blk * 2 * m, blk * 2 * m + m)
            second = slice(blk * 2 * m + m, (blk + 1) * 2 * m)
            ref_row = dec_ref[1, blk * 2 * m + m - 1:blk * 2 * m + m, :]
            ks += [(k[first] * jnp.exp2(ref_row - cum[first])).astype(BF16), k_lv[0][second]]
            qs += [q_lv[0][first], (q[second] * jnp.exp2(cum[second] - ref_row)).astype(BF16)]
        return jnp.concatenate(qs, axis=0), jnp.concatenate(ks, axis=0)

    q_lv = [q.astype(BF16)]
    k_lv = [k.astype(BF16)]
    step_decay = jnp.exp2(logw2)
    r4 = rowq & 3
    q0 = q * step_decay
    q_lv += [q0.astype(BF16),
             (q0 * jnp.where(r4 == 3, pltpu.roll(step_decay, 1, 0), 1.0)).astype(BF16)]
    k_lv += [k_lv[0], (k * jnp.where(r4 == 0, pltpu.roll(step_decay, t - 1, 0), 1.0)).astype(BF16)]
    for i in range(2, N_LEVELS - 1):
        if (1 << i) % BF16_ROWS == 0:
            q_i, k_i = split_level(i)
        else:
            e = jnp.exp2(level_log2_factor(i))
            q_i, k_i = (q * e).astype(BF16), (k * e).astype(BF16)
        q_lv.append(q_i)
        k_lv.append(k_i)
    cum_mid = dec_ref[1, half - 1:half, :]
    q_top = (q[half:] * jnp.exp2(cum[half:] - cum_mid)).astype(BF16)
    k_top = (k[:half] * jnp.exp2(cum_mid - cum[:half])).astype(BF16)

    q_in = (q * jnp.exp2(cum)).astype(BF16)
    k_out = k * jnp.exp2(cum_last - cum)
    a_last = jnp.exp2(cum_last)

    o_heads = []
    for h in range(GLA_HEADS):
        vh = v[:, h * GLA_DV:(h + 1) * GLA_DV]
        diag = [0.0, 0.0]
        for i in range(N_LEVELS):
            s_i = _dot_nt(head(q_lv[i], h), head(k_lv[i], h))
            diag = [jnp.where(lev == i - 1, s_i[r0:r0 + half, r0:r0 + half], diag[j])
                    for j, r0 in enumerate((0, half))]
        diag = [d.astype(BF16) for d in diag]
        p_low = jnp.concatenate([_dot_nt(head(q_top, h), head(k_top, h)).astype(BF16), diag[1]], axis=1)
        s_old = state_ref[0, 0, h]
        o = jnp.concatenate([_dot(diag[0], vh[:half]), _dot(p_low, vh)], axis=0)
        o_heads.append(o + _dot(head(q_in, h), s_old.astype(BF16)))
        a_col = _column_broadcast(head(a_last, h))
        a_col = jnp.concatenate([a_col, a_col], axis=1)
        state_ref[0, 0, h] = a_col * s_old + _dot(head(k_out, h).T.astype(BF16), vh)
    return o_heads


def _sample_proj_kernel(x_ref, cbuf_ref, wn_pre_ref, w_head_ref, w_tail_ref, w_gklr_ref, w_gk_ref,
                        b_gk_ref, w_conv_ref, w_a_ref,
                        ps_ref, conv_ref, q_ref, k_ref, a_ref, v_ref, ya_ref, g_ref, ga_ref, gb_ref,
                        xs2_ref, ps2_ref):
    x = x_ref[:, 0, :]
    xs2_ref[...] = x
    ps2_ref[...] = ps_ref[:, 0, :]
    hn = _rms(x, wn_pre_ref[...]).astype(BF16)
    proj, logw2 = _projections(hn, w_head_ref, w_tail_ref, w_gklr_ref, w_gk_ref, b_gk_ref)
    u = proj(OFF_C, CONV_DIM) * proj(OFF_X, CONV_DIM)
    buf0 = cbuf_ref[:, 0, :]
    buf1 = cbuf_ref[:, 1, :]
    wc = w_conv_ref[...]
    y_conv = wc[0:1] * buf0 + wc[1:2] * buf1 + wc[2:3] * u
    conv_ref[:, 0, :] = buf1
    conv_ref[:, 1, :] = u
    ya_ref[...] = _dot((proj(OFF_B, CONV_DIM) * y_conv).astype(BF16), w_a_ref[...])
    grouped = lambda a: a.reshape(a.shape[0] // SUBLANES, SUBLANES, a.shape[1])
    q_ref[...] = grouped(proj(OFF_Q, QK_DIM) * (GLA_DK ** -0.5))
    k_ref[...] = grouped(proj(OFF_K, QK_DIM))
    a_ref[...] = grouped(jnp.exp2(logw2))
    v_ref[...] = grouped(proj(OFF_V, V_DIM))
    g_ref[...] = proj(OFF_G, V_DIM)
    ga_ref[...] = proj(OFF_GA, D_MODEL)
    gb_ref[...] = proj(OFF_GB, D_MODEL)


def _sample_state_update(blk, st_ref, st_out_ref, q_ref, k_ref, a_ref, v_ref, o_scr):
    per_group = SUBLANES // SAMPLE_BLOCK
    group = blk // per_group
    first = (blk % per_group) * SAMPLE_BLOCK
    row_id = lax.broadcasted_iota(jnp.int32, (SUBLANES, V_DIM), 0)
    o_tile = o_scr[group]
    for n in range(SAMPLE_BLOCK):
        r = first + n
        q_row = q_ref[group, pl.ds(r, 1), :]
        k_row = k_ref[group, pl.ds(r, 1), :]
        a_row = a_ref[group, pl.ds(r, 1), :]
        v_row = v_ref[group, pl.ds(r, 1), :]
        o_parts = []
        for h in range(GLA_HEADS):
            sl = slice(h * GLA_DK, (h + 1) * GLA_DK)
            a_col = _column_broadcast(a_row[:, sl])
            k_col = _column_broadcast(k_row[:, sl])
            q_col = _column_broadcast(q_row[:, sl])
            for half in range(GLA_DV // LANES):
                c0 = half * LANES
                s_old = st_ref[n, h, :, c0:c0 + LANES]
                vv = v_row[:, h * GLA_DV + c0:h * GLA_DV + c0 + LANES]
                s_new = a_col * s_old + k_col * vv
                st_out_ref[n, h, :, c0:c0 + LANES] = s_new
                o_parts.append(jnp.sum(q_col * s_new, axis=0, keepdims=True))
        o_row = jnp.concatenate(o_parts, axis=1)
        o_tile = jnp.where(row_id == r, o_row, o_tile)
    o_scr[group] = o_tile


def _ffn_rows(h, p, wn_pre_ref, w_gate_ref, w_up_ref, w_down_ref, wn_post_ref, w_pp_ref, w_pg_ref,
              wn_ple_ref):
    f = _rms(h, wn_pre_ref[...]).astype(BF16)
    acc = None
    for c in range(D_FF // FFN_CHUNK):
        sl = slice(c * FFN_CHUNK, (c + 1) * FFN_CHUNK)
        gate = _dot(f, w_gate_ref[:, sl])
        up = _dot(f, w_up_ref[:, sl])
        act = (gate * _sigmoid(gate) * up).astype(BF16)
        part = _dot(act, w_down_ref[sl, :])
        acc = part if acc is None else acc + part
    h = h + _rms(acc, wn_post_ref[...])
    e = _dot(p.astype(BF16), w_pp_ref[...]) * _sigmoid(_dot(h.astype(BF16), w_pg_ref[...]))
    return h + _rms(e, wn_ple_ref[...])


def _ffn_kernel(h_ref, p_ref, wn_pre_ref, w_gate_ref, w_up_ref, w_down_ref, wn_post_ref,
                w_pp_ref, w_pg_ref, wn_ple_ref,
                st_ref, q_ref, k_ref, a_ref, v_ref, xs_ref, ps_ref, ya_ref, g_ref, ga_ref, gb_ref,
                w_gn_ref, w_b_ref, w_o_ref, wn_mix_post_ref,
                out_ref, st_out_ref, outs_ref, o_scr):
    step = pl.program_id(0)
    n_tiles = pl.num_programs(0) - 1
    ffn_refs = (wn_pre_ref, w_gate_ref, w_up_ref, w_down_ref, wn_post_ref, w_pp_ref, w_pg_ref, wn_ple_ref)

    @pl.when(step == 0)
    def _():
        o_scr[...] = jnp.zeros_like(o_scr)

    @pl.when(step < n_tiles)
    def _():
        for r in range(FFN_TILE // FFN_SUBTILE):
            rows = slice(r * FFN_SUBTILE, (r + 1) * FFN_SUBTILE)
            out_ref[rows, :] = _ffn_rows(h_ref[rows, :], p_ref[rows, :], *ffn_refs)
        _sample_state_update(step, st_ref, st_out_ref, q_ref, k_ref, a_ref, v_ref, o_scr)

    @pl.when(step == n_tiles)
    def _():
        o = o_scr[...].reshape(xs_ref.shape[0], V_DIM)
        o_heads = [o[:, h * GLA_DV:(h + 1) * GLA_DV] for h in range(GLA_HEADS)]
        hs = _mix_out(xs_ref[...], o_heads, g_ref[...], ga_ref[...], gb_ref[...], ya_ref[...],
                      w_gn_ref[...], w_b_ref, w_o_ref, wn_mix_post_ref[...])
        outs_ref[...] = _ffn_rows(hs, ps_ref[...], *ffn_refs)


def _cast_w_in_kernel(wt_ref, gk_ref, *refs):
    n = (len(refs) - 3) // 2
    head_ref, tail_ref, gklr_ref = refs[n:n + 3]
    for src, dst in zip(refs[:n], refs[n + 3:]):
        dst[...] = src[...].astype(BF16)
    j = pl.program_id(0)
    blk = wt_ref[...].T.astype(BF16)

    @pl.when(j < N_HEAD // W_IN_BLOCK)
    def _():
        head_ref[...] = blk

    @pl.when(j >= N_HEAD // W_IN_BLOCK)
    def _():
        tail_ref[...] = blk

    @pl.when(j == 0)
    def _():
        rows = jnp.concatenate([gk_ref[...], jnp.zeros((LANES - GATE_RANK, D_MODEL), F32)], axis=0)
        gklr_ref[...] = rows.T.astype(BF16)


def _cast_w_in(w_in_t, others):
    n_head = N_HEAD // W_IN_BLOCK
    n_tail = N_TAIL // W_IN_BLOCK
    other_specs = [pl.BlockSpec((w.shape[0] // (n_head + n_tail), w.shape[1]), lambda j: (j, 0))
                   for w in others]

    def src_row(j):
        row = jnp.where(j < n_head, j * W_IN_BLOCK, GKLR_START + GATE_RANK + (j - n_head) * W_IN_BLOCK)
        return pl.multiple_of(row, GATE_RANK)

    return pl.pallas_call(
        _cast_w_in_kernel,
        grid=(n_head + n_tail,),
        in_specs=[pl.BlockSpec((pl.Element(W_IN_BLOCK), pl.Element(D_MODEL)), lambda j: (src_row(j), 0)),
                  pl.BlockSpec((pl.Element(GATE_RANK), pl.Element(D_MODEL)), lambda j: (GKLR_START, 0))]
                 + other_specs,
        out_specs=[pl.BlockSpec((D_MODEL, W_IN_BLOCK), lambda j: (0, jnp.minimum(j, n_head - 1))),
                   pl.BlockSpec((D_MODEL, W_IN_BLOCK), lambda j: (0, jnp.maximum(j - n_head, 0))),
                   pl.BlockSpec((D_MODEL, LANES), lambda j: (0, 0))] + other_specs,
        out_shape=[jax.ShapeDtypeStruct((D_MODEL, N_HEAD), BF16),
                   jax.ShapeDtypeStruct((D_MODEL, N_TAIL), BF16),
                   jax.ShapeDtypeStruct((D_MODEL, LANES), BF16)]
                  + [jax.ShapeDtypeStruct(w.shape, BF16) for w in others],
        compiler_params=pltpu.CompilerParams(
            dimension_semantics=("arbitrary",), vmem_limit_bytes=VMEM_LIMIT_BYTES),
        name="cast_weights",
    )(w_in_t, w_in_t, *others)


def _resident(shape):
    return pl.BlockSpec(shape, lambda *_: (0,) * len(shape), pipeline_mode=pl.Buffered(1))


def _level_map(t):
    idx = np.arange(t)
    xor = idx[:, None] ^ idx[None, :]
    lev = np.floor(np.log2(np.maximum(xor, 1))).astype(np.int32)
    lev = np.where(idx[:, None] > idx[None, :], lev, -2)
    lev = np.where(idx[:, None] == idx[None, :], -1, lev)
    return jnp.asarray(lev, dtype=jnp.int32)


def _mixer_weight_specs():
    return [
        _resident((1, D_MODEL)),
        _resident((D_MODEL, N_HEAD)),
        _resident((D_MODEL, N_TAIL)),
        _resident((D_MODEL, LANES)),
        _resident((LANES, QK_DIM)),
        _resident((1, QK_DIM)),
        _resident((3, CONV_DIM)),
        _resident((CONV_DIM, D_MODEL)),
        _resident((1, GLA_DV)),
        _resident((V_DIM, D_MODEL)),
        _resident((D_MODEL, D_MODEL)),
        _resident((1, D_MODEL)),
    ]


def _prompt_mixer(x, mixer_weights, later_weights):
    b, s, _ = x.shape
    t = SEQ_TILE
    c = GLA_CHUNK
    n_steps = b * (s // t)
    tri = jnp.asarray(np.tril(np.ones((c, c), np.float32)), dtype=BF16)

    def side_spec(shape):
        n_blocks = n_steps
        while shape[0] % (n_blocks * BF16_ROWS):
            n_blocks //= 2
        rep = n_steps // n_blocks
        return pl.BlockSpec((shape[0] // n_blocks, shape[1]),
                            lambda i, j: ((i * (s // t) + j) // rep, 0))

    side_specs = [side_spec(w.shape) for w in later_weights]
    return pl.pallas_call(
        _prompt_mixer_kernel,
        grid=(b, s // t),
        in_specs=[pl.BlockSpec((1, t, D_MODEL), lambda i, j: (i, j, 0)),
                  _resident((c // 2, c // 2)), _resident((c, c))] + _mixer_weight_specs() + side_specs,
        out_specs=[pl.BlockSpec((1, t, D_MODEL), lambda i, j: (i, j, 0)),
                   pl.BlockSpec((1, 1, 2, CONV_DIM), lambda i, j: (0, i, 0, 0)),
                   pl.BlockSpec((1, 1, GLA_HEADS, GLA_DK, GLA_DV), lambda i, j: (0, i, 0, 0, 0))]
                  + side_specs,
        out_shape=[jax.ShapeDtypeStruct((b, s, D_MODEL), F32),
                   jax.ShapeDtypeStruct((1, b, 2, CONV_DIM), F32),
                   jax.ShapeDtypeStruct((1, b, GLA_HEADS, GLA_DK, GLA_DV), F32)]
                  + [jax.ShapeDtypeStruct(w.shape, BF16) for w in later_weights],
        scratch_shapes=[pltpu.VMEM((t // c, 2, c, QK_DIM), F32)],
        compiler_params=pltpu.CompilerParams(
            dimension_semantics=("arbitrary", "arbitrary"), vmem_limit_bytes=VMEM_LIMIT_BYTES),
        name="prompt_mixer",
    )(x, _level_map(c // 2), tri, *mixer_weights, *later_weights)


def _sample_proj(x, conv_buf, p, proj_weights):
    n = x.shape[0]
    grouped = lambda w: (n // SUBLANES, SUBLANES, w)
    out_shapes = [(n, 2, CONV_DIM), grouped(QK_DIM), grouped(QK_DIM), grouped(QK_DIM), grouped(V_DIM),
                  (n, D_MODEL), (n, V_DIM), (n, D_MODEL), (n, D_MODEL), (n, D_MODEL), (n, PLE_DIM)]
    whole = lambda shape: pl.BlockSpec(shape, lambda i: (0,) * len(shape))
    return pl.pallas_call(
        _sample_proj_kernel,
        grid=(1,),
        in_specs=[_resident((n, 1, D_MODEL)), _resident((n, 2, CONV_DIM))] + _mixer_weight_specs()[:8]
                 + [_resident((n, 1, PLE_DIM))],
        out_specs=[whole(s) for s in out_shapes],
        out_shape=[jax.ShapeDtypeStruct(s, F32) for s in out_shapes],
        compiler_params=pltpu.CompilerParams(
            dimension_semantics=("arbitrary",), vmem_limit_bytes=VMEM_LIMIT_BYTES),
        name="sample_proj",
    )(x, conv_buf, *proj_weights, p)


def _ffn_and_sample_state(h, p, ffn_weights, state, sample_proj, x_s, p_s, out_weights):
    rows = h.shape[0]
    n = x_s.shape[0]
    n_tiles = rows // FFN_TILE
    assert n_tiles * SAMPLE_BLOCK == n, "one sample state block per prompt row tile"
    q, k, a, v, y_a, g, gate_a, gate_b = sample_proj
    tile = lambda i: jnp.minimum(i, n_tiles - 1)
    state_spec = pl.BlockSpec((SAMPLE_BLOCK, GLA_HEADS, GLA_DK, GLA_DV), lambda i: (tile(i), 0, 0, 0))
    return pl.pallas_call(
        _ffn_kernel,
        grid=(n_tiles + 1,),
        in_specs=[pl.BlockSpec((FFN_TILE, D_MODEL), lambda i: (tile(i), 0)),
                  pl.BlockSpec((FFN_TILE, PLE_DIM), lambda i: (tile(i), 0)),
                  _resident((1, D_MODEL)), _resident((D_MODEL, D_FF)), _resident((D_MODEL, D_FF)),
                  _resident((D_FF, D_MODEL)), _resident((1, D_MODEL)), _resident((PLE_DIM, D_MODEL)),
                  _resident((D_MODEL, D_MODEL)), _resident((1, D_MODEL)),
                  state_spec, _resident(q.shape), _resident(k.shape), _resident(a.shape),
                  _resident(v.shape), _resident((n, D_MODEL)), _resident((n, PLE_DIM)),
                  _resident((n, D_MODEL)), _resident((n, V_DIM)), _resident((n, D_MODEL)),
                  _resident((n, D_MODEL)),
                  _resident((1, GLA_DV)), _resident((V_DIM, D_MODEL)), _resident((D_MODEL, D_MODEL)),
                  _resident((1, D_MODEL))],
        out_specs=[pl.BlockSpec((FFN_TILE, D_MODEL), lambda i: (tile(i), 0)),
                   state_spec,
                   pl.BlockSpec((n, D_MODEL), lambda i: (0, 0))],
        out_shape=[jax.ShapeDtypeStruct((rows, D_MODEL), F32),
                   jax.ShapeDtypeStruct(state.shape, F32),
                   jax.ShapeDtypeStruct((n, D_MODEL), F32)],
        scratch_shapes=[pltpu.VMEM((n // SUBLANES, SUBLANES, V_DIM), F32)],
        compiler_params=pltpu.CompilerParams(
            dimension_semantics=("arbitrary",), vmem_limit_bytes=VMEM_LIMIT_BYTES),
        name="ffn_ple",
    )(h, p, *ffn_weights, state, q, k, a, v, x_s, p_s, y_a, g, gate_a, gate_b, *out_weights)


def kernel(x_prompt, x_sample, state_conv, state_gla, p_prompt, p_sample, w_norm_mix_pre, w_in, w_conv, w_a_out, w_gk, b_gk, w_gla_norm, w_b_out, w_o, w_norm_mix_post, w_norm_ffn_pre, w_ffn_gate, w_ffn_up, w_ffn_down, w_norm_ffn_post, w_ple_proj, w_ple_gate, w_norm_ple_post):
    depth = w_in.shape[0]
    batch, seq, _ = x_prompt.shape
    n_dec = x_sample.shape[0]
    assert x_sample.shape[1] == 1, "the sample group carries one new token per sequence"
    assert seq % SEQ_TILE == 0 and (batch * seq) % FFN_TILE == 0 and n_dec % SAMPLE_BLOCK == 0

    hp = x_prompt
    hs = x_sample
    conv_p, gla_p, conv_s, gla_s = [], [], [], []
    for i in range(depth):
        row = lambda w: w[i].reshape(1, -1)
        w_head, w_tail, w_gklr, w_a, w_b, w_o_b = _cast_w_in(
            jnp.swapaxes(w_in[i], 0, 1), (w_a_out[i], w_b_out[i], w_o[i]))
        w_gk_pad = jnp.pad(w_gk[i], ((0, LANES - GATE_RANK), (0, 0))).astype(BF16)
        mixer_weights = (row(w_norm_mix_pre), w_head, w_tail, w_gklr, w_gk_pad, row(b_gk), w_conv[i],
                         w_a, row(w_gla_norm), w_b, w_o_b, row(w_norm_mix_post))

        hp_mid, cbp, sp, w_gate, w_up, w_down, w_pp, w_pg = _prompt_mixer(
            hp, mixer_weights, (w_ffn_gate[i], w_ffn_up[i], w_ffn_down[i], w_ple_proj[i], w_ple_gate[i]))
        ffn_weights = (row(w_norm_ffn_pre), w_gate, w_up, w_down, row(w_norm_ffn_post), w_pp, w_pg,
                       row(w_norm_ple_post))
        cbs, *sample_proj, xs2d, ps2d = _sample_proj(hs, state_conv[i], p_sample[i], mixer_weights[:8])
        hp, ss, hs = _ffn_and_sample_state(
            hp_mid.reshape(batch * seq, D_MODEL), p_prompt[i].reshape(batch * seq, PLE_DIM), ffn_weights,
            state_gla[i], sample_proj, xs2d, ps2d, mixer_weights[8:])
        hp = hp.reshape(batch, seq, D_MODEL)
        hs = hs.reshape(n_dec, 1, D_MODEL)

        conv_p.append(cbp[0]); gla_p.append(sp[0])
        conv_s.append(cbs); gla_s.append(ss)
    return (hp, hs, jnp.stack(conv_p), jnp.stack(gla_p),
            jnp.stack(conv_s), jnp.stack(gla_s))
```

```python
import numpy as np
import jax
import jax.numpy as jnp
from jax import lax
from jax.experimental import pallas as pl
from jax.experimental.pallas import tpu as pltpu

D_MODEL = 1024
CONV_DIM = D_MODEL
GLA_HEADS = 4
GLA_DK = 128
GLA_DV = 256
QK_DIM = GLA_HEADS * GLA_DK
V_DIM = GLA_HEADS * GLA_DV
GATE_RANK = 16
GATE_NORMALIZER = 16.0
D_FF = 2816
PLE_DIM = 256
EPS = 1e-6
LOG2_E = 1.4426950408889634

LANES = 128
SUBLANES = 8
BF16_ROWS = 16
VMEM_LIMIT_BYTES = 56 * 1024 * 1024

OFF_B, OFF_C, OFF_X = 0, 1024, 2048
OFF_Q, OFF_K, OFF_V, OFF_G = 3072, 3584, 4096, 5120
OFF_GA, OFF_GB = 6144, 7168
N_HEAD = 6144
N_TAIL = 2048
GKLR_START = 6144

GLA_CHUNK = 256
N_LEVELS = 8
SEQ_TILE = 512
FFN_TILE = 512
FFN_CHUNK = 1408
SAMPLE_BLOCK = 4
W_IN_BLOCK = 1024

F32 = jnp.float32
BF16 = jnp.bfloat16


def _rms(x, w):
    return x * lax.rsqrt(jnp.mean(x * x, axis=-1, keepdims=True) + EPS) * w


def _sigmoid(x):
    return 1.0 / (1.0 + jnp.exp2(x * -LOG2_E))


def _log2_sigmoid(x, scale):
    return (jnp.minimum(x, 0.0) - jnp.log(1.0 + jnp.exp2(jnp.abs(x) * -LOG2_E))) * (scale * LOG2_E)


def _dot(a, b):
    return jnp.dot(a, b, preferred_element_type=F32)


def _dot_nt(a, b):
    return lax.dot_general(a, b, (((1,), (1,)), ((), ())), preferred_element_type=F32)


def _column_broadcast(row):
    return jnp.broadcast_to(row, (LANES, LANES)).T


def _projections(hn, w_head_ref, w_tail_ref, w_gklr_ref, w_gk_ref, b_gk_ref):
    def proj(off, width):
        if off < N_HEAD:
            return _dot(hn, w_head_ref[:, off:off + width])
        return _dot(hn, w_tail_ref[:, off - N_HEAD:off - N_HEAD + width])

    gk_lr = _dot(hn, w_gklr_ref[...])
    gk = _dot(gk_lr.astype(BF16), w_gk_ref[...]) + b_gk_ref[...]
    return proj, _log2_sigmoid(gk, 1.0 / GATE_NORMALIZER)


def _mix_out(x, o_heads, g, gate_a, gate_b, y_a, w_gn, w_b_ref, w_o_ref, wn_post):
    normed = []
    for h in range(GLA_HEADS):
        o = o_heads[h]
        gh = g[:, h * GLA_DV:(h + 1) * GLA_DV]
        o = o * lax.rsqrt(jnp.mean(o * o, axis=-1, keepdims=True) + EPS) * w_gn
        normed.append((o * (gh * _sigmoid(gh))).astype(BF16))
    y_b = _dot(jnp.concatenate(normed, axis=1), w_b_ref[...])
    merged = _sigmoid(gate_a) * y_a + _sigmoid(gate_b) * y_b
    mix = _dot(merged.astype(BF16), w_o_ref[...])
    return x + _rms(mix, wn_post)


def _prompt_mixer_kernel(x_ref, lev_ref, tri_ref, wn_pre_ref, w_head_ref, w_tail_ref, w_gklr_ref,
                         w_gk_ref, b_gk_ref, w_conv_ref, w_a_ref, w_gn_ref, w_b_ref, w_o_ref,
                         wn_post_ref, *rest):
    n_side = (len(rest) - 4) // 2
    side_src = rest[:n_side]
    h_ref, conv_ref, state_ref = rest[n_side:n_side + 3]
    side_dst = rest[n_side + 3:2 * n_side + 3]
    dec_ref = rest[-1]

    @pl.when(pl.program_id(1) == 0)
    def _():
        conv_ref[...] = jnp.zeros_like(conv_ref)
        state_ref[...] = jnp.zeros_like(state_ref)

    for src, dst in zip(side_src, side_dst):
        dst[...] = src[...].astype(BF16)

    t = SEQ_TILE
    x = x_ref[0]
    hn = _rms(x, wn_pre_ref[...]).astype(BF16)
    proj, logw2 = _projections(hn, w_head_ref, w_tail_ref, w_gklr_ref, w_gk_ref, b_gk_ref)

    u = proj(OFF_C, CONV_DIM) * proj(OFF_X, CONV_DIM)
    prev2 = conv_ref[0, 0, 0:1, :]
    prev1 = conv_ref[0, 0, 1:2, :]
    row = lax.broadcasted_iota(jnp.int32, (t, CONV_DIM), 0)
    u1 = jnp.where(row == 0, prev1, pltpu.roll(u, 1, 0))
    u2 = jnp.where(row == 0, prev2, jnp.where(row == 1, prev1, pltpu.roll(u, 2, 0)))
    wc = w_conv_ref[...]
    y_conv = wc[0:1] * u2 + wc[1:2] * u1 + wc[2:3] * u
    conv_ref[0, 0] = u[t - 2:t]
    y_a = _dot((proj(OFF_B, CONV_DIM) * y_conv).astype(BF16), w_a_ref[...])

    q = proj(OFF_Q, QK_DIM) * (GLA_DK ** -0.5)
    k = proj(OFF_K, QK_DIM)
    v = proj(OFF_V, V_DIM).astype(BF16)

    o_chunks = []
    for c in range(SEQ_TILE // GLA_CHUNK):
        rows = slice(c * GLA_CHUNK, (c + 1) * GLA_CHUNK)
        o_chunks.append(_gla_chunk(q[rows], k[rows], v[rows], logw2[rows], dec_ref.at[c], lev_ref,
                                   tri_ref, state_ref))
    o_heads = [jnp.concatenate([o[h] for o in o_chunks], axis=0) for h in range(GLA_HEADS)]

    h_ref[0] = _mix_out(x, o_heads, proj(OFF_G, V_DIM), proj(OFF_GA, D_MODEL), proj(OFF_GB, D_MODEL),
                        y_a, w_gn_ref[...], w_b_ref, w_o_ref, wn_post_ref[...])


def _gla_chunk(q, k, v, logw2, dec_ref, lev_ref, tri_ref, state_ref):
    t = GLA_CHUNK
    dec_ref[0] = logw2
    logw2 = dec_ref[0]
    hi = logw2.astype(BF16)
    lo = (logw2 - hi.astype(F32)).astype(BF16)
    cum = _dot(tri_ref[...], hi) + _dot(tri_ref[...], lo)
    dec_ref[1] = cum
    cum_last = cum[t - 1:t]

    half = t // 2
    lev = lev_ref[...]
    rowq = lax.broadcasted_iota(jnp.int32, (t, QK_DIM), 0)

    def level_log2_factor(i):
        m = 1 << i
        pieces = []
        for blk in range(t // (2 * m)):
            r = blk * 2 * m + m - 1
            pieces.append(jnp.broadcast_to(dec_ref[1, r:r + 1, :], (2 * m, QK_DIM)))
        return -jnp.abs(cum - jnp.concatenate(pieces, axis=0))

    def head(a, h):
        return a[:, h * GLA_DK:(h + 1) * GLA_DK]

    def split_level(i):
        m = 1 << i
        qs, ks = [], []
        for blk in range(t // (2 * m)):
            first = slice(blk * 2 * m, blk * 2 * m + m)
            second = slice(blk * 2 * m + m, (blk + 1) * 2 * m)
            ref_row = dec_ref[1, blk * 2 * m + m - 1:blk * 2 * m + m, :]
            ks += [(k[first] * jnp.exp2(ref_row - cum[first])).astype(BF16), k_lv[0][second]]
            qs += [q_lv[0][first], (q[second] * jnp.exp2(cum[second] - ref_row)).astype(BF16)]
        return jnp.concatenate(qs, axis=0), jnp.concatenate(ks, axis=0)

    q_lv = [q.astype(BF16)]
    k_lv = [k.astype(BF16)]
    step_decay = jnp.exp2(logw2)
    r4 = rowq & 3
    q0 = q * step_decay
    q_lv += [q0.astype(BF16),
             (q0 * jnp.where(r4 == 3, pltpu.roll(step_decay, 1, 0), 1.0)).astype(BF16)]
    k_lv += [k_lv[0], (k * jnp.where(r4 == 0, pltpu.roll(step_decay, t - 1, 0), 1.0)).astype(BF16)]
    for i in range(2, N_LEVELS - 1):
        if (1 << i) % BF16_ROWS == 0:
            q_i, k_i = split_level(i)
        else:
            e = jnp.exp2(level_log2_factor(i))
            q_i, k_i = (q * e).astype(BF16), (k * e).astype(BF16)
        q_lv.append(q_i)
        k_lv.append(k_i)
    cum_mid = dec_ref[1, half - 1:half, :]
    q_top = (q[half:] * jnp.exp2(cum[half:] - cum_mid)).astype(BF16)
    k_top = (k[:half] * jnp.exp2(cum_mid - cum[:half])).astype(BF16)

    q_in = (q * jnp.exp2(cum)).astype(BF16)
    k_out = k * jnp.exp2(cum_last - cum)
    a_last = jnp.exp2(cum_last)

    o_heads = []
    for h in range(GLA_HEADS):
        vh = v[:, h * GLA_DV:(h + 1) * GLA_DV]
        diag = [0.0, 0.0]
        for i in range(N_LEVELS):
            s_i = _dot_nt(head(q_lv[i], h), head(k_lv[i], h))
            diag = [jnp.where(lev == i - 1, s_i[r0:r0 + half, r0:r0 + half], diag[j])
                    for j, r0 in enumerate((0, half))]
        diag = [d.astype(BF16) for d in diag]
        p_low = jnp.concatenate([_dot_nt(head(q_top, h), head(k_top, h)).astype(BF16), diag[1]], axis=1)
        s_old = state_ref[0, 0, h]
        o = jnp.concatenate([_dot(diag[0], vh[:half]), _dot(p_low, vh)], axis=0)
        o_heads.append(o + _dot(head(q_in, h), s_old.astype(BF16)))
        a_col = _column_broadcast(head(a_last, h))
        a_col = jnp.concatenate([a_col, a_col], axis=1)
        state_ref[0, 0, h] = a_col * s_old + _dot(head(k_out, h).T.astype(BF16), vh)
    return o_heads


def _sample_proj_kernel(x_ref, cbuf_ref, wn_pre_ref, w_head_ref, w_tail_ref, w_gklr_ref, w_gk_ref,
                        b_gk_ref, w_conv_ref, w_a_ref,
                        ps_ref, conv_ref, q_ref, k_ref, a_ref, v_ref, ya_ref, g_ref, ga_ref, gb_ref,
                        xs2_ref, ps2_ref):
    x = x_ref[:, 0, :]
    xs2_ref[...] = x
    ps2_ref[...] = ps_ref[:, 0, :]
    hn = _rms(x, wn_pre_ref[...]).astype(BF16)
    proj, logw2 = _projections(hn, w_head_ref, w_tail_ref, w_gklr_ref, w_gk_ref, b_gk_ref)
    u = proj(OFF_C, CONV_DIM) * proj(OFF_X, CONV_DIM)
    buf0 = cbuf_ref[:, 0, :]
    buf1 = cbuf_ref[:, 1, :]
    wc = w_conv_ref[...]
    y_conv = wc[0:1] * buf0 + wc[1:2] * buf1 + wc[2:3] * u
    conv_ref[:, 0, :] = buf1
    conv_ref[:, 1, :] = u
    ya_ref[...] = _dot((proj(OFF_B, CONV_DIM) * y_conv).astype(BF16), w_a_ref[...])
    grouped = lambda a: a.reshape(a.shape[0] // SUBLANES, SUBLANES, a.shape[1])
    q_ref[...] = grouped(proj(OFF_Q, QK_DIM) * (GLA_DK ** -0.5))
    k_ref[...] = grouped(proj(OFF_K, QK_DIM))
    a_ref[...] = grouped(jnp.exp2(logw2))
    v_ref[...] = grouped(proj(OFF_V, V_DIM))
    g_ref[...] = proj(OFF_G, V_DIM)
    ga_ref[...] = proj(OFF_GA, D_MODEL)
    gb_ref[...] = proj(OFF_GB, D_MODEL)


def _sample_state_update(blk, st_ref, st_out_ref, q_ref, k_ref, a_ref, v_ref, o_scr):
    per_group = SUBLANES // SAMPLE_BLOCK
    group = blk // per_group
    first = (blk % per_group) * SAMPLE_BLOCK
    row_id = lax.broadcasted_iota(jnp.int32, (SUBLANES, V_DIM), 0)
    o_tile = o_scr[group]
    for n in range(SAMPLE_BLOCK):
        r = first + n
        q_row = q_ref[group, pl.ds(r, 1), :]
        k_row = k_ref[group, pl.ds(r, 1), :]
        a_row = a_ref[group, pl.ds(r, 1), :]
        v_row = v_ref[group, pl.ds(r, 1), :]
        o_parts = []
        for h in range(GLA_HEADS):
            sl = slice(h * GLA_DK, (h + 1) * GLA_DK)
            a_col = _column_broadcast(a_row[:, sl])
            k_col = _column_broadcast(k_row[:, sl])
            q_col = _column_broadcast(q_row[:, sl])
            for half in range(GLA_DV // LANES):
                c0 = half * LANES
                s_old = st_ref[n, h, :, c0:c0 + LANES]
                vv = v_row[:, h * GLA_DV + c0:h * GLA_DV + c0 + LANES]
                s_new = a_col * s_old + k_col * vv
                st_out_ref[n, h, :, c0:c0 + LANES] = s_new
                o_parts.append(jnp.sum(q_col * s_new, axis=0, keepdims=True))
            yield
        o_row = jnp.concatenate(o_parts, axis=1)
        o_tile = jnp.where(row_id == r, o_row, o_tile)
    o_scr[group] = o_tile


def _weave(*gens):
    results = [None] * len(gens)
    live = list(range(len(gens)))
    while live:
        for i in list(live):
            try:
                next(gens[i])
            except StopIteration as done:
                results[i] = done.value
                live.remove(i)
    return results


def _ffn_main(h, wn_pre_ref, w_gate_ref, w_up_ref, w_down_ref, wn_post_ref):
    f = _rms(h, wn_pre_ref[...]).astype(BF16)
    acc = None
    for c in range(D_FF // FFN_CHUNK):
        sl = slice(c * FFN_CHUNK, (c + 1) * FFN_CHUNK)
        gate = _dot(f, w_gate_ref[:, sl])
        yield
        up = _dot(f, w_up_ref[:, sl])
        yield
        act = (gate * _sigmoid(gate) * up).astype(BF16)
        part = _dot(act, w_down_ref[sl, :])
        yield
        acc = part if acc is None else acc + part
    return h + _rms(acc, wn_post_ref[...])


def _ple_tail(h, p, w_pp_ref, w_pg_ref, wn_ple_ref):
    e_proj = _dot(p.astype(BF16), w_pp_ref[...])
    yield
    e_gate = _dot(h.astype(BF16), w_pg_ref[...])
    yield
    e = e_proj * _sigmoid(e_gate)
    yield
    return h + _rms(e, wn_ple_ref[...])


def _ffn_kernel(h_ref, p_ref, wn_pre_ref, w_gate_ref, w_up_ref, w_down_ref, wn_post_ref,
                w_pp_ref, w_pg_ref, wn_ple_ref,
                st_ref, q_ref, k_ref, a_ref, v_ref, xs_ref, ps_ref, ya_ref, g_ref, ga_ref, gb_ref,
                w_gn_ref, w_b_ref, w_o_ref, wn_mix_post_ref,
                out_ref, st_out_ref, outs_ref, o_scr, mid_scr):
    step = pl.program_id(0)
    n_tiles = pl.num_programs(0) - 1
    main_refs = (wn_pre_ref, w_gate_ref, w_up_ref, w_down_ref, wn_post_ref)
    tail_refs = (w_pp_ref, w_pg_ref, wn_ple_ref)
    slot = step % 2

    @pl.when(step == 0)
    def _():
        o_scr[...] = jnp.zeros_like(o_scr)
        mid_scr[...] = jnp.zeros_like(mid_scr)

    @pl.when(step < n_tiles)
    def _():
        mid, out, _ = _weave(
            _ffn_main(h_ref[...], *main_refs),
            _ple_tail(mid_scr[1 - slot], p_ref[...], *tail_refs),
            _sample_state_update(step, st_ref, st_out_ref, q_ref, k_ref, a_ref, v_ref, o_scr))
        mid_scr[slot] = mid
        out_ref[...] = out

    @pl.when(step == n_tiles)
    def _():
        out_ref[...], = _weave(_ple_tail(mid_scr[1 - slot], p_ref[...], *tail_refs))
        o = o_scr[...].reshape(xs_ref.shape[0], V_DIM)
        o_heads = [o[:, h * GLA_DV:(h + 1) * GLA_DV] for h in range(GLA_HEADS)]
        hs = _mix_out(xs_ref[...], o_heads, g_ref[...], ga_ref[...], gb_ref[...], ya_ref[...],
                      w_gn_ref[...], w_b_ref, w_o_ref, wn_mix_post_ref[...])
        hs, = _weave(_ffn_main(hs, *main_refs))
        outs_ref[...], = _weave(_ple_tail(hs, ps_ref[...], *tail_refs))


def _cast_w_in_kernel(wt_ref, gk_ref, *refs):
    n = (len(refs) - 3) // 2
    head_ref, tail_ref, gklr_ref = refs[n:n + 3]
    for src, dst in zip(refs[:n], refs[n + 3:]):
        dst[...] = src[...].astype(BF16)
    j = pl.program_id(0)
    blk = wt_ref[...].T.astype(BF16)

    @pl.when(j < N_HEAD // W_IN_BLOCK)
    def _():
        head_ref[...] = blk

    @pl.when(j >= N_HEAD // W_IN_BLOCK)
    def _():
        tail_ref[...] = blk

    @pl.when(j == 0)
    def _():
        rows = jnp.concatenate([gk_ref[...], jnp.zeros((LANES - GATE_RANK, D_MODEL), F32)], axis=0)
        gklr_ref[...] = rows.T.astype(BF16)


def _cast_w_in(w_in_t, others):
    n_head = N_HEAD // W_IN_BLOCK
    n_tail = N_TAIL // W_IN_BLOCK
    other_specs = [pl.BlockSpec((w.shape[0] // (n_head + n_tail), w.shape[1]), lambda j: (j, 0))
                   for w in others]

    def src_row(j):
        row = jnp.where(j < n_head, j * W_IN_BLOCK, GKLR_START + GATE_RANK + (j - n_head) * W_IN_BLOCK)
        return pl.multiple_of(row, GATE_RANK)

    return pl.pallas_call(
        _cast_w_in_kernel,
        grid=(n_head + n_tail,),
        in_specs=[pl.BlockSpec((pl.Element(W_IN_BLOCK), pl.Element(D_MODEL)), lambda j: (src_row(j), 0)),
                  pl.BlockSpec((pl.Element(GATE_RANK), pl.Element(D_MODEL)), lambda j: (GKLR_START, 0))]
                 + other_specs,
        out_specs=[pl.BlockSpec((D_MODEL, W_IN_BLOCK), lambda j: (0, jnp.minimum(j, n_head - 1))),
                   pl.BlockSpec((D_MODEL, W_IN_BLOCK), lambda j: (0, jnp.maximum(j - n_head, 0))),
                   pl.BlockSpec((D_MODEL, LANES), lambda j: (0, 0))] + other_specs,
        out_shape=[jax.ShapeDtypeStruct((D_MODEL, N_HEAD), BF16),
                   jax.ShapeDtypeStruct((D_MODEL, N_TAIL), BF16),
                   jax.ShapeDtypeStruct((D_MODEL, LANES), BF16)]
                  + [jax.ShapeDtypeStruct(w.shape, BF16) for w in others],
        compiler_params=pltpu.CompilerParams(
            dimension_semantics=("arbitrary",), vmem_limit_bytes=VMEM_LIMIT_BYTES),
        name="cast_weights",
    )(w_in_t, w_in_t, *others)


def _resident(shape):
    return pl.BlockSpec(shape, lambda *_: (0,) * len(shape), pipeline_mode=pl.Buffered(1))


def _level_map(t):
    idx = np.arange(t)
    xor = idx[:, None] ^ idx[None, :]
    lev = np.floor(np.log2(np.maximum(xor, 1))).astype(np.int32)
    lev = np.where(idx[:, None] > idx[None, :], lev, -2)
    lev = np.where(idx[:, None] == idx[None, :], -1, lev)
    return jnp.asarray(lev, dtype=jnp.int32)


def _mixer_weight_specs():
    return [
        _resident((1, D_MODEL)),
        _resident((D_MODEL, N_HEAD)),
        _resident((D_MODEL, N_TAIL)),
        _resident((D_MODEL, LANES)),
        _resident((LANES, QK_DIM)),
        _resident((1, QK_DIM)),
        _resident((3, CONV_DIM)),
        _resident((CONV_DIM, D_MODEL)),
        _resident((1, GLA_DV)),
        _resident((V_DIM, D_MODEL)),
        _resident((D_MODEL, D_MODEL)),
        _resident((1, D_MODEL)),
    ]


def _prompt_mixer(x, mixer_weights, later_weights):
    b, s, _ = x.shape
    t = SEQ_TILE
    c = GLA_CHUNK
    n_steps = b * (s // t)
    tri = jnp.asarray(np.tril(np.ones((c, c), np.float32)), dtype=BF16)

    def side_spec(shape):
        n_blocks = n_steps
        while shape[0] % (n_blocks * BF16_ROWS):
            n_blocks //= 2
        rep = n_steps // n_blocks
        return pl.BlockSpec((shape[0] // n_blocks, shape[1]),
                            lambda i, j: ((i * (s // t) + j) // rep, 0))

    side_specs = [side_spec(w.shape) for w in later_weights]
    return pl.pallas_call(
        _prompt_mixer_kernel,
        grid=(b, s // t),
        in_specs=[pl.BlockSpec((1, t, D_MODEL), lambda i, j: (i, j, 0)),
                  _resident((c // 2, c // 2)), _resident((c, c))] + _mixer_weight_specs() + side_specs,
        out_specs=[pl.BlockSpec((1, t, D_MODEL), lambda i, j: (i, j, 0)),
                   pl.BlockSpec((1, 1, 2, CONV_DIM), lambda i, j: (0, i, 0, 0)),
                   pl.BlockSpec((1, 1, GLA_HEADS, GLA_DK, GLA_DV), lambda i, j: (0, i, 0, 0, 0))]
                  + side_specs,
        out_shape=[jax.ShapeDtypeStruct((b, s, D_MODEL), F32),
                   jax.ShapeDtypeStruct((1, b, 2, CONV_DIM), F32),
                   jax.ShapeDtypeStruct((1, b, GLA_HEADS, GLA_DK, GLA_DV), F32)]
                  + [jax.ShapeDtypeStruct(w.shape, BF16) for w in later_weights],
        scratch_shapes=[pltpu.VMEM((t // c, 2, c, QK_DIM), F32)],
        compiler_params=pltpu.CompilerParams(
            dimension_semantics=("arbitrary", "arbitrary"), vmem_limit_bytes=VMEM_LIMIT_BYTES),
        name="prompt_mixer",
    )(x, _level_map(c // 2), tri, *mixer_weights, *later_weights)


def _sample_proj(x, conv_buf, p, proj_weights):
    n = x.shape[0]
    grouped = lambda w: (n // SUBLANES, SUBLANES, w)
    out_shapes = [(n, 2, CONV_DIM), grouped(QK_DIM), grouped(QK_DIM), grouped(QK_DIM), grouped(V_DIM),
                  (n, D_MODEL), (n, V_DIM), (n, D_MODEL), (n, D_MODEL), (n, D_MODEL), (n, PLE_DIM)]
    whole = lambda shape: pl.BlockSpec(shape, lambda i: (0,) * len(shape))
    return pl.pallas_call(
        _sample_proj_kernel,
        grid=(1,),
        in_specs=[_resident((n, 1, D_MODEL)), _resident((n, 2, CONV_DIM))] + _mixer_weight_specs()[:8]
                 + [_resident((n, 1, PLE_DIM))],
        out_specs=[whole(s) for s in out_shapes],
        out_shape=[jax.ShapeDtypeStruct(s, F32) for s in out_shapes],
        compiler_params=pltpu.CompilerParams(
            dimension_semantics=("arbitrary",), vmem_limit_bytes=VMEM_LIMIT_BYTES),
        name="sample_proj",
    )(x, conv_buf, *proj_weights, p)


def _ffn_and_sample_state(h, p, ffn_weights, state, sample_proj, x_s, p_s, out_weights):
    rows = h.shape[0]
    n = x_s.shape[0]
    n_tiles = rows // FFN_TILE
    assert n_tiles * SAMPLE_BLOCK == n, "one sample state block per prompt row tile"
    q, k, a, v, y_a, g, gate_a, gate_b = sample_proj
    tile = lambda i: jnp.minimum(i, n_tiles - 1)
    prev = lambda i: jnp.maximum(i - 1, 0)
    state_spec = pl.BlockSpec((SAMPLE_BLOCK, GLA_HEADS, GLA_DK, GLA_DV), lambda i: (tile(i), 0, 0, 0))
    return pl.pallas_call(
        _ffn_kernel,
        grid=(n_tiles + 1,),
        in_specs=[pl.BlockSpec((FFN_TILE, D_MODEL), lambda i: (tile(i), 0)),
                  pl.BlockSpec((FFN_TILE, PLE_DIM), lambda i: (prev(i), 0)),
                  _resident((1, D_MODEL)), _resident((D_MODEL, D_FF)), _resident((D_MODEL, D_FF)),
                  _resident((D_FF, D_MODEL)), _resident((1, D_MODEL)), _resident((PLE_DIM, D_MODEL)),
                  _resident((D_MODEL, D_MODEL)), _resident((1, D_MODEL)),
                  state_spec, _resident(q.shape), _resident(k.shape), _resident(a.shape),
                  _resident(v.shape), _resident((n, D_MODEL)), _resident((n, PLE_DIM)),
                  _resident((n, D_MODEL)), _resident((n, V_DIM)), _resident((n, D_MODEL)),
                  _resident((n, D_MODEL)),
                  _resident((1, GLA_DV)), _resident((V_DIM, D_MODEL)), _resident((D_MODEL, D_MODEL)),
                  _resident((1, D_MODEL))],
        out_specs=[pl.BlockSpec((FFN_TILE, D_MODEL), lambda i: (prev(i), 0)),
                   state_spec,
                   pl.BlockSpec((n, D_MODEL), lambda i: (0, 0))],
        out_shape=[jax.ShapeDtypeStruct((rows, D_MODEL), F32),
                   jax.ShapeDtypeStruct(state.shape, F32),
                   jax.ShapeDtypeStruct((n, D_MODEL), F32)],
        scratch_shapes=[pltpu.VMEM((n // SUBLANES, SUBLANES, V_DIM), F32),
                        pltpu.VMEM((2, FFN_TILE, D_MODEL), F32)],
        compiler_params=pltpu.CompilerParams(
            dimension_semantics=("arbitrary",), vmem_limit_bytes=VMEM_LIMIT_BYTES),
        name="ffn_ple",
    )(h, p, *ffn_weights, state, q, k, a, v, x_s, p_s, y_a, g, gate_a, gate_b, *out_weights)


def kernel(x_prompt, x_sample, state_conv, state_gla, p_prompt, p_sample, w_norm_mix_pre, w_in, w_conv, w_a_out, w_gk, b_gk, w_gla_norm, w_b_out, w_o, w_norm_mix_post, w_norm_ffn_pre, w_ffn_gate, w_ffn_up, w_ffn_down, w_norm_ffn_post, w_ple_proj, w_ple_gate, w_norm_ple_post):
    depth = w_in.shape[0]
    batch, seq, _ = x_prompt.shape
    n_dec = x_sample.shape[0]
    assert x_sample.shape[1] == 1, "the sample group carries one new token per sequence"
    assert seq % SEQ_TILE == 0 and (batch * seq) % FFN_TILE == 0 and n_dec % SAMPLE_BLOCK == 0

    hp = x_prompt
    hs = x_sample
    conv_p, gla_p, conv_s, gla_s = [], [], [], []
    for i in range(depth):
        row = lambda w: w[i].reshape(1, -1)
        w_head, w_tail, w_gklr, w_a, w_b, w_o_b = _cast_w_in(
            jnp.swapaxes(w_in[i], 0, 1), (w_a_out[i], w_b_out[i], w_o[i]))
        w_gk_pad = jnp.pad(w_gk[i], ((0, LANES - GATE_RANK), (0, 0))).astype(BF16)
        mixer_weights = (row(w_norm_mix_pre), w_head, w_tail, w_gklr, w_gk_pad, row(b_gk), w_conv[i],
                         w_a, row(w_gla_norm), w_b, w_o_b, row(w_norm_mix_post))

        hp_mid, cbp, sp, w_gate, w_up, w_down, w_pp, w_pg = _prompt_mixer(
            hp, mixer_weights, (w_ffn_gate[i], w_ffn_up[i], w_ffn_down[i], w_ple_proj[i], w_ple_gate[i]))
        ffn_weights = (row(w_norm_ffn_pre), w_gate, w_up, w_down, row(w_norm_ffn_post), w_pp, w_pg,
                       row(w_norm_ple_post))
        cbs, *sample_proj, xs2d, ps2d = _sample_proj(hs, state_conv[i], p_sample[i], mixer_weights[:8])
        hp, ss, hs = _ffn_and_sample_state(
            hp_mid.reshape(batch * seq, D_MODEL), p_prompt[i].reshape(batch * seq, PLE_DIM), ffn_weights,
            state_gla[i], sample_proj, xs2d, ps2d, mixer_weights[8:])
        hp = hp.reshape(batch, seq, D_MODEL)
        hs = hs.reshape(n_dec, 1, D_MODEL)

        conv_p.append(cbp[0]); gla_p.append(sp[0])
        conv_s.append(cbs); gla_s.append(ss)
    return (hp, hs, jnp.stack(conv_p), jnp.stack(gla_p),
            jnp.stack(conv_s), jnp.stack(gla_s))
```

```python
import numpy as np
import jax
import jax.numpy as jnp
from jax import lax
from jax.experimental import pallas as pl
from jax.experimental.pallas import tpu as pltpu

D_MODEL = 1024
CONV_DIM = D_MODEL
GLA_HEADS = 4
GLA_DK = 128
GLA_DV = 256
QK_DIM = GLA_HEADS * GLA_DK
V_DIM = GLA_HEADS * GLA_DV
GATE_RANK = 16
GATE_NORMALIZER = 16.0
D_FF = 2816
PLE_DIM = 256
EPS = 1e-6
LOG2_E = 1.4426950408889634

LANES = 128
SUBLANES = 8
BF16_ROWS = 16
VMEM_LIMIT_BYTES = 56 * 1024 * 1024

OFF_B, OFF_C, OFF_X = 0, 1024, 2048
OFF_Q, OFF_K, OFF_V, OFF_G = 3072, 3584, 4096, 5120
OFF_GA, OFF_GB = 6144, 7168
N_HEAD = 6144
N_TAIL = 2048
GKLR_START = 6144

GLA_CHUNK = 256
N_LEVELS = 8
SEQ_TILE = 512
FFN_TILE = 512
FFN_CHUNK = 1408
SAMPLE_BLOCK = 4
W_IN_BLOCK = 1024

F32 = jnp.float32
BF16 = jnp.bfloat16


def _rms(x, w):
    return x * lax.rsqrt(jnp.mean(x * x, axis=-1, keepdims=True) + EPS) * w


def _sigmoid(x):
    return 1.0 / (1.0 + jnp.exp2(x * -LOG2_E))


def _log2_sigmoid(x, scale):
    return (jnp.minimum(x, 0.0) - jnp.log(1.0 + jnp.exp2(jnp.abs(x) * -LOG2_E))) * (scale * LOG2_E)


def _dot(a, b):
    return jnp.dot(a, b, preferred_element_type=F32)


def _dot_nt(a, b):
    return lax.dot_general(a, b, (((1,), (1,)), ((), ())), preferred_element_type=F32)


def _column_broadcast(row):
    return jnp.broadcast_to(row, (LANES, LANES)).T


def _projections(hn, w_head_ref, w_tail_ref, w_gklr_ref, w_gk_ref, b_gk_ref):
    def proj(off, width):
        if off < N_HEAD:
            return _dot(hn, w_head_ref[:, off:off + width])
        return _dot(hn, w_tail_ref[:, off - N_HEAD:off - N_HEAD + width])

    gk_lr = _dot(hn, w_gklr_ref[...])
    gk = _dot(gk_lr.astype(BF16), w_gk_ref[...]) + b_gk_ref[...]
    return proj, _log2_sigmoid(gk, 1.0 / GATE_NORMALIZER)


def _mix_out(x, o_heads, g, gate_a, gate_b, y_a, w_gn, w_b_ref, w_o_ref, wn_post):
    normed = []
    for h in range(GLA_HEADS):
        o = o_heads[h]
        gh = g[:, h * GLA_DV:(h + 1) * GLA_DV]
        o = o * lax.rsqrt(jnp.mean(o * o, axis=-1, keepdims=True) + EPS) * w_gn
        normed.append((o * (gh * _sigmoid(gh))).astype(BF16))
    y_b = _dot(jnp.concatenate(normed, axis=1), w_b_ref[...])
    merged = _sigmoid(gate_a) * y_a + _sigmoid(gate_b) * y_b
    mix = _dot(merged.astype(BF16), w_o_ref[...])
    return x + _rms(mix, wn_post)


def _prompt_mixer_kernel(x_ref, lev_ref, tri_ref, wn_pre_ref, w_head_ref, w_tail_ref, w_gklr_ref,
                         w_gk_ref, b_gk_ref, w_conv_ref, w_a_ref, w_gn_ref, w_b_ref, w_o_ref,
                         wn_post_ref, *rest):
    n_side = (len(rest) - 4) // 2
    side_src = rest[:n_side]
    h_ref, conv_ref, state_ref = rest[n_side:n_side + 3]
    side_dst = rest[n_side + 3:2 * n_side + 3]
    dec_ref = rest[-1]

    @pl.when(pl.program_id(1) == 0)
    def _():
        conv_ref[...] = jnp.zeros_like(conv_ref)
        state_ref[...] = jnp.zeros_like(state_ref)

    for src, dst in zip(side_src, side_dst):
        dst[...] = src[...].astype(BF16)

    t = SEQ_TILE
    x = x_ref[0]
    hn = _rms(x, wn_pre_ref[...]).astype(BF16)
    proj, logw2 = _projections(hn, w_head_ref, w_tail_ref, w_gklr_ref, w_gk_ref, b_gk_ref)

    u = proj(OFF_C, CONV_DIM) * proj(OFF_X, CONV_DIM)
    prev2 = conv_ref[0, 0, 0:1, :]
    prev1 = conv_ref[0, 0, 1:2, :]
    row = lax.broadcasted_iota(jnp.int32, (t, CONV_DIM), 0)
    u1 = jnp.where(row == 0, prev1, pltpu.roll(u, 1, 0))
    u2 = jnp.where(row == 0, prev2, jnp.where(row == 1, prev1, pltpu.roll(u, 2, 0)))
    wc = w_conv_ref[...]
    y_conv = wc[0:1] * u2 + wc[1:2] * u1 + wc[2:3] * u
    conv_ref[0, 0] = u[t - 2:t]
    y_a = _dot((proj(OFF_B, CONV_DIM) * y_conv).astype(BF16), w_a_ref[...])

    q = proj(OFF_Q, QK_DIM) * (GLA_DK ** -0.5)
    k = proj(OFF_K, QK_DIM)
    v = proj(OFF_V, V_DIM).astype(BF16)

    o_chunks = []
    for c in range(SEQ_TILE // GLA_CHUNK):
        rows = slice(c * GLA_CHUNK, (c + 1) * GLA_CHUNK)
        o_chunks.append(_gla_chunk(q[rows], k[rows], v[rows], logw2[rows], dec_ref.at[c], lev_ref,
                                   tri_ref, state_ref))
    o_heads = [jnp.concatenate([o[h] for o in o_chunks], axis=0) for h in range(GLA_HEADS)]

    h_ref[0] = _mix_out(x, o_heads, proj(OFF_G, V_DIM), proj(OFF_GA, D_MODEL), proj(OFF_GB, D_MODEL),
                        y_a, w_gn_ref[...], w_b_ref, w_o_ref, wn_post_ref[...])


def _gla_chunk(q, k, v, logw2, dec_ref, lev_ref, tri_ref, state_ref):
    t = GLA_CHUNK
    dec_ref[0] = logw2
    logw2 = dec_ref[0]
    hi = logw2.astype(BF16)
    lo = (logw2 - hi.astype(F32)).astype(BF16)
    cum = _dot(tri_ref[...], hi) + _dot(tri_ref[...], lo)
    dec_ref[1] = cum
    cum_last = cum[t - 1:t]

    half = t // 2
    lev = lev_ref[...]
    rowq = lax.broadcasted_iota(jnp.int32, (t, QK_DIM), 0)

    def level_log2_factor(i):
        m = 1 << i
        pieces = []
        for blk in range(t // (2 * m)):
            r = blk * 2 * m + m - 1
            pieces.append(jnp.broadcast_to(dec_ref[1, r:r + 1, :], (2 * m, QK_DIM)))
        return -jnp.abs(cum - jnp.concatenate(pieces, axis=0))

    def head(a, h):
        return a[:, h * GLA_DK:(h + 1) * GLA_DK]

    def split_level(i):
        m = 1 << i
        qs, ks = [], []
        for blk in range(t // (2 * m)):
            first = slice(blk * 2 * m, blk * 2 * m + m)
            second = slice(blk * 2 * m + m, (blk + 1) * 2 * m)
            ref_row = dec_ref[1, blk * 2 * m + m - 1:blk * 2 * m + m, :]
            ks += [(k[first] * jnp.exp2(ref_row - cum[first])).astype(BF16), k_lv[0][second]]
            qs += [q_lv[0][first], (q[second] * jnp.exp2(cum[second] - ref_row)).astype(BF16)]
        return jnp.concatenate(qs, axis=0), jnp.concatenate(ks, axis=0)

    q_lv = [q.astype(BF16)]
    k_lv = [k.astype(BF16)]
    step_decay = jnp.exp2(logw2)
    r4 = rowq & 3
    q0 = q * step_decay
    q_lv += [q0.astype(BF16),
             (q0 * jnp.where(r4 == 3, pltpu.roll(step_decay, 1, 0), 1.0)).astype(BF16)]
    k_lv += [k_lv[0], (k * jnp.where(r4 == 0, pltpu.roll(step_decay, t - 1, 0), 1.0)).astype(BF16)]
    for i in range(2, N_LEVELS - 1):
        if (1 << i) % BF16_ROWS == 0:
            q_i, k_i = split_level(i)
        else:
            e = jnp.exp2(level_log2_factor(i))
            q_i, k_i = (q * e).astype(BF16), (k * e).astype(BF16)
        q_lv.append(q_i)
        k_lv.append(k_i)
    cum_mid = dec_ref[1, half - 1:half, :]
    q_top = (q[half:] * jnp.exp2(cum[half:] - cum_mid)).astype(BF16)
    k_top = (k[:half] * jnp.exp2(cum_mid - cum[:half])).astype(BF16)

    q_in = (q * jnp.exp2(cum)).astype(BF16)
    k_out = k * jnp.exp2(cum_last - cum)
    a_last = jnp.exp2(cum_last)

    o_heads = []
    for h in range(GLA_HEADS):
        vh = v[:, h * GLA_DV:(h + 1) * GLA_DV]
        diag = [0.0, 0.0]
        for i in range(N_LEVELS):
            s_i = _dot_nt(head(q_lv[i], h), head(k_lv[i], h))
            diag = [jnp.where(lev == i - 1, s_i[r0:r0 + half, r0:r0 + half], diag[j])
                    for j, r0 in enumerate((0, half))]
        diag = [d.astype(BF16) for d in diag]
        p_low = jnp.concatenate([_dot_nt(head(q_top, h), head(k_top, h)).astype(BF16), diag[1]], axis=1)
        s_old = state_ref[0, 0, h]
        o = jnp.concatenate([_dot(diag[0], vh[:half]), _dot(p_low, vh)], axis=0)
        o_heads.append(o + _dot(head(q_in, h), s_old.astype(BF16)))
        a_col = _column_broadcast(head(a_last, h))
        a_col = jnp.concatenate([a_col, a_col], axis=1)
        state_ref[0, 0, h] = a_col * s_old + _dot(head(k_out, h).T.astype(BF16), vh)
    return o_heads


def _sample_proj_kernel(x_ref, cbuf_ref, wn_pre_ref, w_head_ref, w_tail_ref, w_gklr_ref, w_gk_ref,
                        b_gk_ref, w_conv_ref, w_a_ref,
                        ps_ref, conv_ref, q_ref, k_ref, a_ref, v_ref, ya_ref, g_ref, ga_ref, gb_ref,
                        xs2_ref, ps2_ref):
    x = x_ref[:, 0, :]
    xs2_ref[...] = x
    ps2_ref[...] = ps_ref[:, 0, :]
    hn = _rms(x, wn_pre_ref[...]).astype(BF16)
    proj, logw2 = _projections(hn, w_head_ref, w_tail_ref, w_gklr_ref, w_gk_ref, b_gk_ref)
    u = proj(OFF_C, CONV_DIM) * proj(OFF_X, CONV_DIM)
    buf0 = cbuf_ref[:, 0, :]
    buf1 = cbuf_ref[:, 1, :]
    wc = w_conv_ref[...]
    y_conv = wc[0:1] * buf0 + wc[1:2] * buf1 + wc[2:3] * u
    conv_ref[:, 0, :] = buf1
    conv_ref[:, 1, :] = u
    ya_ref[...] = _dot((proj(OFF_B, CONV_DIM) * y_conv).astype(BF16), w_a_ref[...])
    grouped = lambda a: a.reshape(a.shape[0] // SUBLANES, SUBLANES, a.shape[1])
    q_ref[...] = grouped(proj(OFF_Q, QK_DIM) * (GLA_DK ** -0.5))
    k_ref[...] = grouped(proj(OFF_K, QK_DIM))
    a_ref[...] = grouped(jnp.exp2(logw2))
    v_ref[...] = grouped(proj(OFF_V, V_DIM))
    g_ref[...] = proj(OFF_G, V_DIM)
    ga_ref[...] = proj(OFF_GA, D_MODEL)
    gb_ref[...] = proj(OFF_GB, D_MODEL)


def _sample_state_update(blk, st_ref, st_out_ref, q_ref, k_ref, a_ref, v_ref, o_scr):
    per_group = SUBLANES // SAMPLE_BLOCK
    group = blk // per_group
    first = (blk % per_group) * SAMPLE_BLOCK
    row_id = lax.broadcasted_iota(jnp.int32, (SUBLANES, V_DIM), 0)
    o_tile = o_scr[group]
    for n in range(SAMPLE_BLOCK):
        r = first + n
        q_row = q_ref[group, pl.ds(r, 1), :]
        k_row = k_ref[group, pl.ds(r, 1), :]
        a_row = a_ref[group, pl.ds(r, 1), :]
        v_row = v_ref[group, pl.ds(r, 1), :]
        o_parts = []
        for h in range(GLA_HEADS):
            sl = slice(h * GLA_DK, (h + 1) * GLA_DK)
            a_col = _column_broadcast(a_row[:, sl])
            k_col = _column_broadcast(k_row[:, sl])
            q_col = _column_broadcast(q_row[:, sl])
            for half in range(GLA_DV // LANES):
                c0 = half * LANES
                s_old = st_ref[n, h, :, c0:c0 + LANES]
                vv = v_row[:, h * GLA_DV + c0:h * GLA_DV + c0 + LANES]
                s_new = a_col * s_old + k_col * vv
                st_out_ref[n, h, :, c0:c0 + LANES] = s_new
                o_parts.append(jnp.sum(q_col * s_new, axis=0, keepdims=True))
        o_row = jnp.concatenate(o_parts, axis=1)
        o_tile = jnp.where(row_id == r, o_row, o_tile)
    o_scr[group] = o_tile


def _ffn_rows(h, p, wn_pre_ref, w_gate_ref, w_up_ref, w_down_ref, wn_post_ref, w_pp_ref, w_pg_ref,
              wn_ple_ref):
    f = _rms(h, wn_pre_ref[...]).astype(BF16)
    acc = None
    for c in range(D_FF // FFN_CHUNK):
        sl = slice(c * FFN_CHUNK, (c + 1) * FFN_CHUNK)
        gate = _dot(f, w_gate_ref[:, sl])
        up = _dot(f, w_up_ref[:, sl])
        act = (gate * _sigmoid(gate) * up).astype(BF16)
        part = _dot(act, w_down_ref[sl, :])
        acc = part if acc is None else acc + part
    h = h + _rms(acc, wn_post_ref[...])
    e = _dot(p.astype(BF16), w_pp_ref[...]) * _sigmoid(_dot(h.astype(BF16), w_pg_ref[...]))
    return h + _rms(e, wn_ple_ref[...])


def _ffn_kernel(h_ref, p_ref, wn_pre_ref, w_gate_ref, w_up_ref, w_down_ref, wn_post_ref,
                w_pp_ref, w_pg_ref, wn_ple_ref,
                st_ref, q_ref, k_ref, a_ref, v_ref, xs_ref, ps_ref, ya_ref, g_ref, ga_ref, gb_ref,
                w_gn_ref, w_b_ref, w_o_ref, wn_mix_post_ref,
                out_ref, st_out_ref, outs_ref, o_scr):
    step = pl.program_id(0)
    n_tiles = pl.num_programs(0) - 1
    ffn_refs = (wn_pre_ref, w_gate_ref, w_up_ref, w_down_ref, wn_post_ref, w_pp_ref, w_pg_ref, wn_ple_ref)

    @pl.when(step == 0)
    def _():
        o_scr[...] = jnp.zeros_like(o_scr)

    @pl.when(step < n_tiles)
    def _():
        out_ref[...] = _ffn_rows(h_ref[...], p_ref[...], *ffn_refs)
        _sample_state_update(step, st_ref, st_out_ref, q_ref, k_ref, a_ref, v_ref, o_scr)

    @pl.when(step == n_tiles)
    def _():
        o = o_scr[...].reshape(xs_ref.shape[0], V_DIM)
        o_heads = [o[:, h * GLA_DV:(h + 1) * GLA_DV] for h in range(GLA_HEADS)]
        hs = _mix_out(xs_ref[...], o_heads, g_ref[...], ga_ref[...], gb_ref[...], ya_ref[...],
                      w_gn_ref[...], w_b_ref, w_o_ref, wn_mix_post_ref[...])
        outs_ref[...] = _ffn_rows(hs, ps_ref[...], *ffn_refs)


def _cast_w_in_kernel(wt_ref, gk_ref, *refs):
    n = (len(refs) - 3) // 2
    head_ref, tail_ref, gklr_ref = refs[n:n + 3]
    for src, dst in zip(refs[:n], refs[n + 3:]):
        dst[...] = src[...].astype(BF16)
    j = pl.program_id(0)
    blk = wt_ref[...].T.astype(BF16)

    @pl.when(j < N_HEAD // W_IN_BLOCK)
    def _():
        head_ref[...] = blk

    @pl.when(j >= N_HEAD // W_IN_BLOCK)
    def _():
        tail_ref[...] = blk

    @pl.when(j == 0)
    def _():
        rows = jnp.concatenate([gk_ref[...], jnp.zeros((LANES - GATE_RANK, D_MODEL), F32)], axis=0)
        gklr_ref[...] = rows.T.astype(BF16)


def _cast_w_in(w_in_t, others):
    n_head = N_HEAD // W_IN_BLOCK
    n_tail = N_TAIL // W_IN_BLOCK
    other_specs = [pl.BlockSpec((w.shape[0] // (n_head + n_tail), w.shape[1]), lambda j: (j, 0))
                   for w in others]

    def src_row(j):
        row = jnp.where(j < n_head, j * W_IN_BLOCK, GKLR_START + GATE_RANK + (j - n_head) * W_IN_BLOCK)
        return pl.multiple_of(row, GATE_RANK)

    return pl.pallas_call(
        _cast_w_in_kernel,
        grid=(n_head + n_tail,),
        in_specs=[pl.BlockSpec((pl.Element(W_IN_BLOCK), pl.Element(D_MODEL)), lambda j: (src_row(j), 0)),
                  pl.BlockSpec((pl.Element(GATE_RANK), pl.Element(D_MODEL)), lambda j: (GKLR_START, 0))]
                 + other_specs,
        out_specs=[pl.BlockSpec((D_MODEL, W_IN_BLOCK), lambda j: (0, jnp.minimum(j, n_head - 1))),
                   pl.BlockSpec((D_MODEL, W_IN_BLOCK), lambda j: (0, jnp.maximum(j - n_head, 0))),
                   pl.BlockSpec((D_MODEL, LANES), lambda j: (0, 0))] + other_specs,
        out_shape=[jax.ShapeDtypeStruct((D_MODEL, N_HEAD), BF16),
                   jax.ShapeDtypeStruct((D_MODEL, N_TAIL), BF16),
                   jax.ShapeDtypeStruct((D_MODEL, LANES), BF16)]
                  + [jax.ShapeDtypeStruct(w.shape, BF16) for w in others],
        compiler_params=pltpu.CompilerParams(
            dimension_semantics=("arbitrary",), vmem_limit_bytes=VMEM_LIMIT_BYTES),
        name="cast_weights",
    )(w_in_t, w_in_t, *others)


def _resident(shape):
    return pl.BlockSpec(shape, lambda *_: (0,) * len(shape), pipeline_mode=pl.Buffered(1))


def _level_map(t):
    idx = np.arange(t)
    xor = idx[:, None] ^ idx[None, :]
    lev = np.floor(np.log2(np.maximum(xor, 1))).astype(np.int32)
    lev = np.where(idx[:, None] > idx[None, :], lev, -2)
    lev = np.where(idx[:, None] == idx[None, :], -1, lev)
    return jnp.asarray(lev, dtype=jnp.int32)


def _mixer_weight_specs():
    return [
        _resident((1, D_MODEL)),
        _resident((D_MODEL, N_HEAD)),
        _resident((D_MODEL, N_TAIL)),
        _resident((D_MODEL, LANES)),
        _resident((LANES, QK_DIM)),
        _resident((1, QK_DIM)),
        _resident((3, CONV_DIM)),
        _resident((CONV_DIM, D_MODEL)),
        _resident((1, GLA_DV)),
        _resident((V_DIM, D_MODEL)),
        _resident((D_MODEL, D_MODEL)),
        _resident((1, D_MODEL)),
    ]


def _prompt_mixer(x, mixer_weights, later_weights):
    b, s, _ = x.shape
    t = SEQ_TILE
    c = GLA_CHUNK
    n_steps = b * (s // t)
    tri = jnp.asarray(np.tril(np.ones((c, c), np.float32)), dtype=BF16)

    def side_spec(shape):
        n_blocks = n_steps
        while shape[0] % (n_blocks * BF16_ROWS):
            n_blocks //= 2
        rep = n_steps // n_blocks
        return pl.BlockSpec((shape[0] // n_blocks, shape[1]),
                            lambda i, j: ((i * (s // t) + j) // rep, 0))

    side_specs = [side_spec(w.shape) for w in later_weights]
    return pl.pallas_call(
        _prompt_mixer_kernel,
        grid=(b, s // t),
        in_specs=[pl.BlockSpec((1, t, D_MODEL), lambda i, j: (i, j, 0)),
                  _resident((c // 2, c // 2)), _resident((c, c))] + _mixer_weight_specs() + side_specs,
        out_specs=[pl.BlockSpec((1, t, D_MODEL), lambda i, j: (i, j, 0)),
                   pl.BlockSpec((1, 1, 2, CONV_DIM), lambda i, j: (0, i, 0, 0)),
                   pl.BlockSpec((1, 1, GLA_HEADS, GLA_DK, GLA_DV), lambda i, j: (0, i, 0, 0, 0))]
                  + side_specs,
        out_shape=[jax.ShapeDtypeStruct((b, s, D_MODEL), F32),
                   jax.ShapeDtypeStruct((1, b, 2, CONV_DIM), F32),
                   jax.ShapeDtypeStruct((1, b, GLA_HEADS, GLA_DK, GLA_DV), F32)]
                  + [jax.ShapeDtypeStruct(w.shape, BF16) for w in later_weights],
        scratch_shapes=[pltpu.VMEM((t // c, 2, c, QK_DIM), F32)],
        compiler_params=pltpu.CompilerParams(
            dimension_semantics=("arbitrary", "arbitrary"), vmem_limit_bytes=VMEM_LIMIT_BYTES),
        name="prompt_mixer",
    )(x, _level_map(c // 2), tri, *mixer_weights, *later_weights)


def _sample_proj(x, conv_buf, p, proj_weights):
    n = x.shape[0]
    grouped = lambda w: (n // SUBLANES, SUBLANES, w)
    out_shapes = [(n, 2, CONV_DIM), grouped(QK_DIM), grouped(QK_DIM), grouped(QK_DIM), grouped(V_DIM),
                  (n, D_MODEL), (n, V_DIM), (n, D_MODEL), (n, D_MODEL), (n, D_MODEL), (n, PLE_DIM)]
    whole = lambda shape: pl.BlockSpec(shape, lambda i: (0,) * len(shape))
    return pl.pallas_call(
        _sample_proj_kernel,
        grid=(1,),
        in_specs=[_resident((n, 1, D_MODEL)), _resident((n, 2, CONV_DIM))] + _mixer_weight_specs()[:8]
                 + [_resident((n, 1, PLE_DIM))],
        out_specs=[whole(s) for s in out_shapes],
        out_shape=[jax.ShapeDtypeStruct(s, F32) for s in out_shapes],
        compiler_params=pltpu.CompilerParams(
            dimension_semantics=("arbitrary",), vmem_limit_bytes=VMEM_LIMIT_BYTES),
        name="sample_proj",
    )(x, conv_buf, *proj_weights, p)


def _ffn_and_sample_state(h, p, ffn_weights, state, sample_proj, x_s, p_s, out_weights):
    rows = h.shape[0]
    n = x_s.shape[0]
    n_tiles = rows // FFN_TILE
    assert n_tiles * SAMPLE_BLOCK == n, "one sample state block per prompt row tile"
    q, k, a, v, y_a, g, gate_a, gate_b = sample_proj
    tile = lambda i: jnp.minimum(i, n_tiles - 1)
    state_spec = pl.BlockSpec((SAMPLE_BLOCK, GLA_HEADS, GLA_DK, GLA_DV), lambda i: (tile(i), 0, 0, 0))
    return pl.pallas_call(
        _ffn_kernel,
        grid=(n_tiles + 1,),
        in_specs=[pl.BlockSpec((FFN_TILE, D_MODEL), lambda i: (tile(i), 0)),
                  pl.BlockSpec((FFN_TILE, PLE_DIM), lambda i: (tile(i), 0)),
                  _resident((1, D_MODEL)), _resident((D_MODEL, D_FF)), _resident((D_MODEL, D_FF)),
                  _resident((D_FF, D_MODEL)), _resident((1, D_MODEL)), _resident((PLE_DIM, D_MODEL)),
                  _resident((D_MODEL, D_MODEL)), _resident((1, D_MODEL)),
                  state_spec, _resident(q.shape), _resident(k.shape), _resident(a.shape),
                  _resident(v.shape), _resident((n, D_MODEL)), _resident((n, PLE_DIM)),
                  _resident((n, D_MODEL)), _resident((n, V_DIM)), _resident((n, D_MODEL)),
                  _resident((n, D_MODEL)),
                  _resident((1, GLA_DV)), _resident((V_DIM, D_MODEL)), _resident((D_MODEL, D_MODEL)),
                  _resident((1, D_MODEL))],
        out_specs=[pl.BlockSpec((FFN_TILE, D_MODEL), lambda i: (tile(i), 0)),
                   state_spec,
                   pl.BlockSpec((n, D_MODEL), lambda i: (0, 0))],
        out_shape=[jax.ShapeDtypeStruct((rows, D_MODEL), F32),
                   jax.ShapeDtypeStruct(state.shape, F32),
                   jax.ShapeDtypeStruct((n, D_MODEL), F32)],
        scratch_shapes=[pltpu.VMEM((n // SUBLANES, SUBLANES, V_DIM), F32)],
        compiler_params=pltpu.CompilerParams(
            dimension_semantics=("arbitrary",), vmem_limit_bytes=VMEM_LIMIT_BYTES),
        name="ffn_ple",
    )(h, p, *ffn_weights, state, q, k, a, v, x_s, p_s, y_a, g, gate_a, gate_b, *out_weights)


def kernel(x_prompt, x_sample, state_conv, state_gla, p_prompt, p_sample, w_norm_mix_pre, w_in, w_conv, w_a_out, w_gk, b_gk, w_gla_norm, w_b_out, w_o, w_norm_mix_post, w_norm_ffn_pre, w_ffn_gate, w_ffn_up, w_ffn_down, w_norm_ffn_post, w_ple_proj, w_ple_gate, w_norm_ple_post):
    depth = w_in.shape[0]
    batch, seq, _ = x_prompt.shape
    n_dec = x_sample.shape[0]
    assert x_sample.shape[1] == 1, "the sample group carries one new token per sequence"
    assert seq % SEQ_TILE == 0 and (batch * seq) % FFN_TILE == 0 and n_dec % SAMPLE_BLOCK == 0

    hp = x_prompt
    hs = x_sample
    conv_p, gla_p, conv_s, gla_s = [], [], [], []
    for i in range(depth):
        row = lambda w: w[i].reshape(1, -1)
        w_head, w_tail, w_gklr, w_a, w_b, w_o_b = _cast_w_in(
            jnp.swapaxes(w_in[i], 0, 1), (w_a_out[i], w_b_out[i], w_o[i]))
        w_gk_pad = jnp.pad(w_gk[i], ((0, LANES - GATE_RANK), (0, 0))).astype(BF16)
        mixer_weights = (row(w_norm_mix_pre), w_head, w_tail, w_gklr, w_gk_pad, row(b_gk), w_conv[i],
                         w_a, row(w_gla_norm), w_b, w_o_b, row(w_norm_mix_post))

        hp_mid, cbp, sp, w_gate, w_up, w_down, w_pp, w_pg = _prompt_mixer(
            hp, mixer_weights, (w_ffn_gate[i], w_ffn_up[i], w_ffn_down[i], w_ple_proj[i], w_ple_gate[i]))
        ffn_weights = (row(w_norm_ffn_pre), w_gate, w_up, w_down, row(w_norm_ffn_post), w_pp, w_pg,
                       row(w_norm_ple_post))
        cbs, *sample_proj, xs2d, ps2d = _sample_proj(hs, state_conv[i], p_sample[i], mixer_weights[:8])
        hp, ss, hs = _ffn_and_sample_state(
            hp_mid.reshape(batch * seq, D_MODEL), p_prompt[i].reshape(batch * seq, PLE_DIM), ffn_weights,
            state_gla[i], sample_proj, xs2d, ps2d, mixer_weights[8:])
        hp = hp.reshape(batch, seq, D_MODEL)
        hs = hs.reshape(n_dec, 1, D_MODEL)

        conv_p.append(cbp[0]); gla_p.append(sp[0])
        conv_s.append(cbs); gla_s.append(ss)
    return (hp, hs, jnp.stack(conv_p), jnp.stack(gla_p),
            jnp.stack(conv_s), jnp.stack(gla_s))
```

```python
import numpy as np
import jax
import jax.numpy as jnp
from jax import lax
from jax.experimental import pallas as pl
from jax.experimental.pallas import tpu as pltpu

D_MODEL = 1024
CONV_DIM = D_MODEL
GLA_HEADS = 4
GLA_DK = 128
GLA_DV = 256
QK_DIM = GLA_HEADS * GLA_DK
V_DIM = GLA_HEADS * GLA_DV
GATE_RANK = 16
GATE_NORMALIZER = 16.0
D_FF = 2816
PLE_DIM = 256
EPS = 1e-6
LOG2_E = 1.4426950408889634

LANES = 128
SUBLANES = 8
BF16_ROWS = 16
VMEM_LIMIT_BYTES = 56 * 1024 * 1024

OFF_B, OFF_C, OFF_X = 0, 1024, 2048
OFF_Q, OFF_K, OFF_V, OFF_G = 3072, 3584, 4096, 5120
OFF_GA, OFF_GB = 6144, 7168
N_HEAD = 6144
N_TAIL = 2048
GKLR_START = 6144

GLA_CHUNK = 256
N_LEVELS = 8
SEQ_TILE = 512
FFN_TILE = 512
FFN_CHUNK = 1408
SAMPLE_BLOCK = 4
W_IN_BLOCK = 1024

F32 = jnp.float32
BF16 = jnp.bfloat16


def _rms(x, w):
    return x * lax.rsqrt(jnp.mean(x * x, axis=-1, keepdims=True) + EPS) * w


def _sigmoid(x):
    return 1.0 / (1.0 + jnp.exp2(x * -LOG2_E))


def _log2_sigmoid(x, scale):
    return (jnp.minimum(x, 0.0) - jnp.log(1.0 + jnp.exp2(jnp.abs(x) * -LOG2_E))) * (scale * LOG2_E)


def _dot(a, b):
    return jnp.dot(a, b, preferred_element_type=F32)


def _dot_nt(a, b):
    return lax.dot_general(a, b, (((1,), (1,)), ((), ())), preferred_element_type=F32)


def _column_broadcast(row):
    return jnp.broadcast_to(row, (LANES, LANES)).T


def _projections(hn, w_head_ref, w_tail_ref, w_gklr_ref, w_gk_ref, b_gk_ref):
    def proj(off, width):
        if off < N_HEAD:
            return _dot(hn, w_head_ref[:, off:off + width])
        return _dot(hn, w_tail_ref[:, off - N_HEAD:off - N_HEAD + width])

    gk_lr = _dot(hn, w_gklr_ref[...])
    gk = _dot(gk_lr.astype(BF16), w_gk_ref[...]) + b_gk_ref[...]
    return proj, _log2_sigmoid(gk, 1.0 / GATE_NORMALIZER)


def _mix_out(x, o_heads, g, gate_a, gate_b, y_a, w_gn, w_b_ref, w_o_ref, wn_post):
    normed = []
    for h in range(GLA_HEADS):
        o = o_heads[h]
        gh = g[:, h * GLA_DV:(h + 1) * GLA_DV]
        o = o * lax.rsqrt(jnp.mean(o * o, axis=-1, keepdims=True) + EPS) * w_gn
        normed.append((o * (gh * _sigmoid(gh))).astype(BF16))
    y_b = _dot(jnp.concatenate(normed, axis=1), w_b_ref[...])
    merged = _sigmoid(gate_a) * y_a + _sigmoid(gate_b) * y_b
    mix = _dot(merged.astype(BF16), w_o_ref[...])
    return x + _rms(mix, wn_post)


def _prompt_mixer_kernel(x_ref, lev_ref, tri_ref, wn_pre_ref, w_head_ref, w_tail_ref, w_gklr_ref,
                         w_gk_ref, b_gk_ref, w_conv_ref, w_a_ref, w_gn_ref, w_b_ref, w_o_ref,
                         wn_post_ref, *rest):
    n_side = (len(rest) - 4) // 2
    side_src = rest[:n_side]
    h_ref, conv_ref, state_ref = rest[n_side:n_side + 3]
    side_dst = rest[n_side + 3:2 * n_side + 3]
    dec_ref = rest[-1]

    @pl.when(pl.program_id(1) == 0)
    def _():
        conv_ref[...] = jnp.zeros_like(conv_ref)
        state_ref[...] = jnp.zeros_like(state_ref)

    for src, dst in zip(side_src, side_dst):
        dst[...] = src[...].astype(BF16)

    t = SEQ_TILE
    x = x_ref[0]
    hn = _rms(x, wn_pre_ref[...]).astype(BF16)
    proj, logw2 = _projections(hn, w_head_ref, w_tail_ref, w_gklr_ref, w_gk_ref, b_gk_ref)

    u = proj(OFF_C, CONV_DIM) * proj(OFF_X, CONV_DIM)
    prev2 = conv_ref[0, 0, 0:1, :]
    prev1 = conv_ref[0, 0, 1:2, :]
    row = lax.broadcasted_iota(jnp.int32, (t, CONV_DIM), 0)
    u1 = jnp.where(row == 0, prev1, pltpu.roll(u, 1, 0))
    u2 = jnp.where(row == 0, prev2, jnp.where(row == 1, prev1, pltpu.roll(u, 2, 0)))
    wc = w_conv_ref[...]
    y_conv = wc[0:1] * u2 + wc[1:2] * u1 + wc[2:3] * u
    conv_ref[0, 0] = u[t - 2:t]
    y_a = _dot((proj(OFF_B, CONV_DIM) * y_conv).astype(BF16), w_a_ref[...])

    q = proj(OFF_Q, QK_DIM) * (GLA_DK ** -0.5)
    k = proj(OFF_K, QK_DIM)
    v = proj(OFF_V, V_DIM).astype(BF16)

    late = {}
    fillers = [[lambda: late.update(g=proj(OFF_G, V_DIM)), lambda: late.update(ga=proj(OFF_GA, D_MODEL))],
               [lambda: late.update(gb=proj(OFF_GB, D_MODEL))]]
    o_chunks = []
    for c in range(SEQ_TILE // GLA_CHUNK):
        rows = slice(c * GLA_CHUNK, (c + 1) * GLA_CHUNK)
        o_chunks.append(_gla_chunk(q[rows], k[rows], v[rows], logw2[rows], dec_ref.at[c], lev_ref,
                                   tri_ref, state_ref, fillers[c] if c < len(fillers) else []))
    o_heads = [jnp.concatenate([o[h] for o in o_chunks], axis=0) for h in range(GLA_HEADS)]

    h_ref[0] = _mix_out(x, o_heads, late["g"], late["ga"], late["gb"],
                        y_a, w_gn_ref[...], w_b_ref, w_o_ref, wn_post_ref[...])


def _gla_chunk(q, k, v, logw2, dec_ref, lev_ref, tri_ref, state_ref, fillers):
    fillers = list(fillers)
    t = GLA_CHUNK
    dec_ref[0] = logw2
    logw2 = dec_ref[0]
    hi = logw2.astype(BF16)
    lo = (logw2 - hi.astype(F32)).astype(BF16)
    cum = _dot(tri_ref[...], hi) + _dot(tri_ref[...], lo)
    dec_ref[1] = cum
    cum_last = cum[t - 1:t]
    if fillers:
        fillers.pop(0)()

    half = t // 2
    lev = lev_ref[...]
    rowq = lax.broadcasted_iota(jnp.int32, (t, QK_DIM), 0)

    def level_log2_factor(i):
        m = 1 << i
        pieces = []
        for blk in range(t // (2 * m)):
            r = blk * 2 * m + m - 1
            pieces.append(jnp.broadcast_to(dec_ref[1, r:r + 1, :], (2 * m, QK_DIM)))
        return -jnp.abs(cum - jnp.concatenate(pieces, axis=0))

    def head(a, h):
        return a[:, h * GLA_DK:(h + 1) * GLA_DK]

    def split_level(i):
        m = 1 << i
        qs, ks = [], []
        for blk in range(t // (2 * m)):
            first = slice(blk * 2 * m, blk * 2 * m + m)
            second = slice(blk * 2 * m + m, (blk + 1) * 2 * m)
            ref_row = dec_ref[1, blk * 2 * m + m - 1:blk * 2 * m + m, :]
            ks += [(k[first] * jnp.exp2(ref_row - cum[first])).astype(BF16), k_lv[0][second]]
            qs += [q_lv[0][first], (q[second] * jnp.exp2(cum[second] - ref_row)).astype(BF16)]
        return jnp.concatenate(qs, axis=0), jnp.concatenate(ks, axis=0)

    q_lv = [q.astype(BF16)]
    k_lv = [k.astype(BF16)]
    step_decay = jnp.exp2(logw2)
    r4 = rowq & 3
    q0 = q * step_decay
    q_lv += [q0.astype(BF16),
             (q0 * jnp.where(r4 == 3, pltpu.roll(step_decay, 1, 0), 1.0)).astype(BF16)]
    k_lv += [k_lv[0], (k * jnp.where(r4 == 0, pltpu.roll(step_decay, t - 1, 0), 1.0)).astype(BF16)]
    for i in range(2, N_LEVELS - 1):
        if (1 << i) % BF16_ROWS == 0:
            q_i, k_i = split_level(i)
        else:
            e = jnp.exp2(level_log2_factor(i))
            q_i, k_i = (q * e).astype(BF16), (k * e).astype(BF16)
        q_lv.append(q_i)
        k_lv.append(k_i)
        if i == 3 and fillers:
            fillers.pop(0)()
    cum_mid = dec_ref[1, half - 1:half, :]
    q_top = (q[half:] * jnp.exp2(cum[half:] - cum_mid)).astype(BF16)
    k_top = (k[:half] * jnp.exp2(cum_mid - cum[:half])).astype(BF16)

    q_in = (q * jnp.exp2(cum)).astype(BF16)
    k_out = k * jnp.exp2(cum_last - cum)
    a_last = jnp.exp2(cum_last)

    o_heads = []
    for h in range(GLA_HEADS):
        vh = v[:, h * GLA_DV:(h + 1) * GLA_DV]
        diag = [0.0, 0.0]
        for i in range(N_LEVELS):
            s_i = _dot_nt(head(q_lv[i], h), head(k_lv[i], h))
            diag = [jnp.where(lev == i - 1, s_i[r0:r0 + half, r0:r0 + half], diag[j])
                    for j, r0 in enumerate((0, half))]
        diag = [d.astype(BF16) for d in diag]
        p_low = jnp.concatenate([_dot_nt(head(q_top, h), head(k_top, h)).astype(BF16), diag[1]], axis=1)
        s_old = state_ref[0, 0, h]
        o = jnp.concatenate([_dot(diag[0], vh[:half]), _dot(p_low, vh)], axis=0)
        o_heads.append(o + _dot(head(q_in, h), s_old.astype(BF16)))
        a_col = _column_broadcast(head(a_last, h))
        a_col = jnp.concatenate([a_col, a_col], axis=1)
        state_ref[0, 0, h] = a_col * s_old + _dot(head(k_out, h).T.astype(BF16), vh)
    return o_heads


def _sample_proj_kernel(x_ref, cbuf_ref, wn_pre_ref, w_head_ref, w_tail_ref, w_gklr_ref, w_gk_ref,
                        b_gk_ref, w_conv_ref, w_a_ref,
                        ps_ref, conv_ref, q_ref, k_ref, a_ref, v_ref, ya_ref, g_ref, ga_ref, gb_ref,
                        xs2_ref, ps2_ref):
    x = x_ref[:, 0, :]
    xs2_ref[...] = x
    ps2_ref[...] = ps_ref[:, 0, :]
    hn = _rms(x, wn_pre_ref[...]).astype(BF16)
    proj, logw2 = _projections(hn, w_head_ref, w_tail_ref, w_gklr_ref, w_gk_ref, b_gk_ref)
    u = proj(OFF_C, CONV_DIM) * proj(OFF_X, CONV_DIM)
    buf0 = cbuf_ref[:, 0, :]
    buf1 = cbuf_ref[:, 1, :]
    wc = w_conv_ref[...]
    y_conv = wc[0:1] * buf0 + wc[1:2] * buf1 + wc[2:3] * u
    conv_ref[:, 0, :] = buf1
    conv_ref[:, 1, :] = u
    ya_ref[...] = _dot((proj(OFF_B, CONV_DIM) * y_conv).astype(BF16), w_a_ref[...])
    grouped = lambda a: a.reshape(a.shape[0] // SUBLANES, SUBLANES, a.shape[1])
    q_ref[...] = grouped(proj(OFF_Q, QK_DIM) * (GLA_DK ** -0.5))
    k_ref[...] = grouped(proj(OFF_K, QK_DIM))
    a_ref[...] = grouped(jnp.exp2(logw2))
    v_ref[...] = grouped(proj(OFF_V, V_DIM))
    g_ref[...] = proj(OFF_G, V_DIM)
    ga_ref[...] = proj(OFF_GA, D_MODEL)
    gb_ref[...] = proj(OFF_GB, D_MODEL)


def _sample_state_update(blk, st_ref, st_out_ref, q_ref, k_ref, a_ref, v_ref, o_scr):
    per_group = SUBLANES // SAMPLE_BLOCK
    group = blk // per_group
    first = (blk % per_group) * SAMPLE_BLOCK
    row_id = lax.broadcasted_iota(jnp.int32, (SUBLANES, V_DIM), 0)
    o_tile = o_scr[group]
    for n in range(SAMPLE_BLOCK):
        r = first + n
        q_row = q_ref[group, pl.ds(r, 1), :]
        k_row = k_ref[group, pl.ds(r, 1), :]
        a_row = a_ref[group, pl.ds(r, 1), :]
        v_row = v_ref[group, pl.ds(r, 1), :]
        o_parts = []
        for h in range(GLA_HEADS):
            sl = slice(h * GLA_DK, (h + 1) * GLA_DK)
            a_col = _column_broadcast(a_row[:, sl])
            k_col = _column_broadcast(k_row[:, sl])
            q_col = _column_broadcast(q_row[:, sl])
            for half in range(GLA_DV // LANES):
                c0 = half * LANES
                s_old = st_ref[n, h, :, c0:c0 + LANES]
                vv = v_row[:, h * GLA_DV + c0:h * GLA_DV + c0 + LANES]
                s_new = a_col * s_old + k_col * vv
                st_out_ref[n, h, :, c0:c0 + LANES] = s_new
                o_parts.append(jnp.sum(q_col * s_new, axis=0, keepdims=True))
        o_row = jnp.concatenate(o_parts, axis=1)
        o_tile = jnp.where(row_id == r, o_row, o_tile)
    o_scr[group] = o_tile


def _ffn_rows(h, p, wn_pre_ref, w_gate_ref, w_up_ref, w_down_ref, wn_post_ref, w_pp_ref, w_pg_ref,
              wn_ple_ref):
    f = _rms(h, wn_pre_ref[...]).astype(BF16)
    acc = None
    for c in range(D_FF // FFN_CHUNK):
        sl = slice(c * FFN_CHUNK, (c + 1) * FFN_CHUNK)
        gate = _dot(f, w_gate_ref[:, sl])
        up = _dot(f, w_up_ref[:, sl])
        act = (gate * _sigmoid(gate) * up).astype(BF16)
        part = _dot(act, w_down_ref[sl, :])
        acc = part if acc is None else acc + part
    h = h + _rms(acc, wn_post_ref[...])
    e = _dot(p.astype(BF16), w_pp_ref[...]) * _sigmoid(_dot(h.astype(BF16), w_pg_ref[...]))
    return h + _rms(e, wn_ple_ref[...])


def _ffn_kernel(h_ref, p_ref, wn_pre_ref, w_gate_ref, w_up_ref, w_down_ref, wn_post_ref,
                w_pp_ref, w_pg_ref, wn_ple_ref,
                st_ref, q_ref, k_ref, a_ref, v_ref, xs_ref, ps_ref, ya_ref, g_ref, ga_ref, gb_ref,
                w_gn_ref, w_b_ref, w_o_ref, wn_mix_post_ref,
                out_ref, st_out_ref, outs_ref, o_scr):
    step = pl.program_id(0)
    n_tiles = pl.num_programs(0) - 1
    ffn_refs = (wn_pre_ref, w_gate_ref, w_up_ref, w_down_ref, wn_post_ref, w_pp_ref, w_pg_ref, wn_ple_ref)

    @pl.when(step == 0)
    def _():
        o_scr[...] = jnp.zeros_like(o_scr)

    @pl.when(step < n_tiles)
    def _():
        out_ref[...] = _ffn_rows(h_ref[...], p_ref[...], *ffn_refs)
        _sample_state_update(step, st_ref, st_out_ref, q_ref, k_ref, a_ref, v_ref, o_scr)

    @pl.when(step == n_tiles)
    def _():
        o = o_scr[...].reshape(xs_ref.shape[0], V_DIM)
        o_heads = [o[:, h * GLA_DV:(h + 1) * GLA_DV] for h in range(GLA_HEADS)]
        hs = _mix_out(xs_ref[...], o_heads, g_ref[...], ga_ref[...], gb_ref[...], ya_ref[...],
                      w_gn_ref[...], w_b_ref, w_o_ref, wn_mix_post_ref[...])
        outs_ref[...] = _ffn_rows(hs, ps_ref[...], *ffn_refs)


def _cast_w_in_kernel(wt_ref, gk_ref, *refs):
    n = (len(refs) - 3) // 2
    head_ref, tail_ref, gklr_ref = refs[n:n + 3]
    for src, dst in zip(refs[:n], refs[n + 3:]):
        dst[...] = src[...].astype(BF16)
    j = pl.program_id(0)
    blk = wt_ref[...].T.astype(BF16)

    @pl.when(j < N_HEAD // W_IN_BLOCK)
    def _():
        head_ref[...] = blk

    @pl.when(j >= N_HEAD // W_IN_BLOCK)
    def _():
        tail_ref[...] = blk

    @pl.when(j == 0)
    def _():
        rows = jnp.concatenate([gk_ref[...], jnp.zeros((LANES - GATE_RANK, D_MODEL), F32)], axis=0)
        gklr_ref[...] = rows.T.astype(BF16)


def _cast_w_in(w_in_t, others):
    n_head = N_HEAD // W_IN_BLOCK
    n_tail = N_TAIL // W_IN_BLOCK
    other_specs = [pl.BlockSpec((w.shape[0] // (n_head + n_tail), w.shape[1]), lambda j: (j, 0))
                   for w in others]

    def src_row(j):
        row = jnp.where(j < n_head, j * W_IN_BLOCK, GKLR_START + GATE_RANK + (j - n_head) * W_IN_BLOCK)
        return pl.multiple_of(row, GATE_RANK)

    return pl.pallas_call(
        _cast_w_in_kernel,
        grid=(n_head + n_tail,),
        in_specs=[pl.BlockSpec((pl.Element(W_IN_BLOCK), pl.Element(D_MODEL)), lambda j: (src_row(j), 0)),
                  pl.BlockSpec((pl.Element(GATE_RANK), pl.Element(D_MODEL)), lambda j: (GKLR_START, 0))]
                 + other_specs,
        out_specs=[pl.BlockSpec((D_MODEL, W_IN_BLOCK), lambda j: (0, jnp.minimum(j, n_head - 1))),
                   pl.BlockSpec((D_MODEL, W_IN_BLOCK), lambda j: (0, jnp.maximum(j - n_head, 0))),
                   pl.BlockSpec((D_MODEL, LANES), lambda j: (0, 0))] + other_specs,
        out_shape=[jax.ShapeDtypeStruct((D_MODEL, N_HEAD), BF16),
                   jax.ShapeDtypeStruct((D_MODEL, N_TAIL), BF16),
                   jax.ShapeDtypeStruct((D_MODEL, LANES), BF16)]
                  + [jax.ShapeDtypeStruct(w.shape, BF16) for w in others],
        compiler_params=pltpu.CompilerParams(
            dimension_semantics=("arbitrary",), vmem_limit_bytes=VMEM_LIMIT_BYTES),
        name="cast_weights",
    )(w_in_t, w_in_t, *others)


def _resident(shape):
    return pl.BlockSpec(shape, lambda *_: (0,) * len(shape), pipeline_mode=pl.Buffered(1))


def _level_map(t):
    idx = np.arange(t)
    xor = idx[:, None] ^ idx[None, :]
    lev = np.floor(np.log2(np.maximum(xor, 1))).astype(np.int32)
    lev = np.where(idx[:, None] > idx[None, :], lev, -2)
    lev = np.where(idx[:, None] == idx[None, :], -1, lev)
    return jnp.asarray(lev, dtype=jnp.int32)


def _mixer_weight_specs():
    return [
        _resident((1, D_MODEL)),
        _resident((D_MODEL, N_HEAD)),
        _resident((D_MODEL, N_TAIL)),
        _resident((D_MODEL, LANES)),
        _resident((LANES, QK_DIM)),
        _resident((1, QK_DIM)),
        _resident((3, CONV_DIM)),
        _resident((CONV_DIM, D_MODEL)),
        _resident((1, GLA_DV)),
        _resident((V_DIM, D_MODEL)),
        _resident((D_MODEL, D_MODEL)),
        _resident((1, D_MODEL)),
    ]


def _prompt_mixer(x, mixer_weights, later_weights):
    b, s, _ = x.shape
    t = SEQ_TILE
    c = GLA_CHUNK
    n_steps = b * (s // t)
    tri = jnp.asarray(np.tril(np.ones((c, c), np.float32)), dtype=BF16)

    def side_spec(shape):
        n_blocks = n_steps
        while shape[0] % (n_blocks * BF16_ROWS):
            n_blocks //= 2
        rep = n_steps // n_blocks
        return pl.BlockSpec((shape[0] // n_blocks, shape[1]),
                            lambda i, j: ((i * (s // t) + j) // rep, 0))

    side_specs = [side_spec(w.shape) for w in later_weights]
    return pl.pallas_call(
        _prompt_mixer_kernel,
        grid=(b, s // t),
        in_specs=[pl.BlockSpec((1, t, D_MODEL), lambda i, j: (i, j, 0)),
                  _resident((c // 2, c // 2)), _resident((c, c))] + _mixer_weight_specs() + side_specs,
        out_specs=[pl.BlockSpec((1, t, D_MODEL), lambda i, j: (i, j, 0)),
                   pl.BlockSpec((1, 1, 2, CONV_DIM), lambda i, j: (0, i, 0, 0)),
                   pl.BlockSpec((1, 1, GLA_HEADS, GLA_DK, GLA_DV), lambda i, j: (0, i, 0, 0, 0))]
                  + side_specs,
        out_shape=[jax.ShapeDtypeStruct((b, s, D_MODEL), F32),
                   jax.ShapeDtypeStruct((1, b, 2, CONV_DIM), F32),
                   jax.ShapeDtypeStruct((1, b, GLA_HEADS, GLA_DK, GLA_DV), F32)]
                  + [jax.ShapeDtypeStruct(w.shape, BF16) for w in later_weights],
        scratch_shapes=[pltpu.VMEM((t // c, 2, c, QK_DIM), F32)],
        compiler_params=pltpu.CompilerParams(
            dimension_semantics=("arbitrary", "arbitrary"), vmem_limit_bytes=VMEM_LIMIT_BYTES),
        name="prompt_mixer",
    )(x, _level_map(c // 2), tri, *mixer_weights, *later_weights)


def _sample_proj(x, conv_buf, p, proj_weights):
    n = x.shape[0]
    grouped = lambda w: (n // SUBLANES, SUBLANES, w)
    out_shapes = [(n, 2, CONV_DIM), grouped(QK_DIM), grouped(QK_DIM), grouped(QK_DIM), grouped(V_DIM),
                  (n, D_MODEL), (n, V_DIM), (n, D_MODEL), (n, D_MODEL), (n, D_MODEL), (n, PLE_DIM)]
    whole = lambda shape: pl.BlockSpec(shape, lambda i: (0,) * len(shape))
    return pl.pallas_call(
        _sample_proj_kernel,
        grid=(1,),
        in_specs=[_resident((n, 1, D_MODEL)), _resident((n, 2, CONV_DIM))] + _mixer_weight_specs()[:8]
                 + [_resident((n, 1, PLE_DIM))],
        out_specs=[whole(s) for s in out_shapes],
        out_shape=[jax.ShapeDtypeStruct(s, F32) for s in out_shapes],
        compiler_params=pltpu.CompilerParams(
            dimension_semantics=("arbitrary",), vmem_limit_bytes=VMEM_LIMIT_BYTES),
        name="sample_proj",
    )(x, conv_buf, *proj_weights, p)


def _ffn_and_sample_state(h, p, ffn_weights, state, sample_proj, x_s, p_s, out_weights):
    rows = h.shape[0]
    n = x_s.shape[0]
    n_tiles = rows // FFN_TILE
    assert n_tiles * SAMPLE_BLOCK == n, "one sample state block per prompt row tile"
    q, k, a, v, y_a, g, gate_a, gate_b = sample_proj
    tile = lambda i: jnp.minimum(i, n_tiles - 1)
    state_spec = pl.BlockSpec((SAMPLE_BLOCK, GLA_HEADS, GLA_DK, GLA_DV), lambda i: (tile(i), 0, 0, 0))
    return pl.pallas_call(
        _ffn_kernel,
        grid=(n_tiles + 1,),
        in_specs=[pl.BlockSpec((FFN_TILE, D_MODEL), lambda i: (tile(i), 0)),
                  pl.BlockSpec((FFN_TILE, PLE_DIM), lambda i: (tile(i), 0)),
                  _resident((1, D_MODEL)), _resident((D_MODEL, D_FF)), _resident((D_MODEL, D_FF)),
                  _resident((D_FF, D_MODEL)), _resident((1, D_MODEL)), _resident((PLE_DIM, D_MODEL)),
                  _resident((D_MODEL, D_MODEL)), _resident((1, D_MODEL)),
                  state_spec, _resident(q.shape), _resident(k.shape), _resident(a.shape),
                  _resident(v.shape), _resident((n, D_MODEL)), _resident((n, PLE_DIM)),
                  _resident((n, D_MODEL)), _resident((n, V_DIM)), _resident((n, D_MODEL)),
                  _resident((n, D_MODEL)),
                  _resident((1, GLA_DV)), _resident((V_DIM, D_MODEL)), _resident((D_MODEL, D_MODEL)),
                  _resident((1, D_MODEL))],
        out_specs=[pl.BlockSpec((FFN_TILE, D_MODEL), lambda i: (tile(i), 0)),
                   state_spec,
                   pl.BlockSpec((n, D_MODEL), lambda i: (0, 0))],
        out_shape=[jax.ShapeDtypeStruct((rows, D_MODEL), F32),
                   jax.ShapeDtypeStruct(state.shape, F32),
                   jax.ShapeDtypeStruct((n, D_MODEL), F32)],
        scratch_shapes=[pltpu.VMEM((n // SUBLANES, SUBLANES, V_DIM), F32)],
        compiler_params=pltpu.CompilerParams(
            dimension_semantics=("arbitrary",), vmem_limit_bytes=VMEM_LIMIT_BYTES),
        name="ffn_ple",
    )(h, p, *ffn_weights, state, q, k, a, v, x_s, p_s, y_a, g, gate_a, gate_b, *out_weights)


def kernel(x_prompt, x_sample, state_conv, state_gla, p_prompt, p_sample, w_norm_mix_pre, w_in, w_conv, w_a_out, w_gk, b_gk, w_gla_norm, w_b_out, w_o, w_norm_mix_post, w_norm_ffn_pre, w_ffn_gate, w_ffn_up, w_ffn_down, w_norm_ffn_post, w_ple_proj, w_ple_gate, w_norm_ple_post):
    depth = w_in.shape[0]
    batch, seq, _ = x_prompt.shape
    n_dec = x_sample.shape[0]
    assert x_sample.shape[1] == 1, "the sample group carries one new token per sequence"
    assert seq % SEQ_TILE == 0 and (batch * seq) % FFN_TILE == 0 and n_dec % SAMPLE_BLOCK == 0

    hp = x_prompt
    hs = x_sample
    conv_p, gla_p, conv_s, gla_s = [], [], [], []
    for i in range(depth):
        row = lambda w: w[i].reshape(1, -1)
        w_head, w_tail, w_gklr, w_a, w_b, w_o_b = _cast_w_in(
            jnp.swapaxes(w_in[i], 0, 1), (w_a_out[i], w_b_out[i], w_o[i]))
        w_gk_pad = jnp.pad(w_gk[i], ((0, LANES - GATE_RANK), (0, 0))).astype(BF16)
        mixer_weights = (row(w_norm_mix_pre), w_head, w_tail, w_gklr, w_gk_pad, row(b_gk), w_conv[i],
                         w_a, row(w_gla_norm), w_b, w_o_b, row(w_norm_mix_post))

        hp_mid, cbp, sp, w_gate, w_up, w_down, w_pp, w_pg = _prompt_mixer(
            hp, mixer_weights, (w_ffn_gate[i], w_ffn_up[i], w_ffn_down[i], w_ple_proj[i], w_ple_gate[i]))
        ffn_weights = (row(w_norm_ffn_pre), w_gate, w_up, w_down, row(w_norm_ffn_post), w_pp, w_pg,
                       row(w_norm_ple_post))
        cbs, *sample_proj, xs2d, ps2d = _sample_proj(hs, state_conv[i], p_sample[i], mixer_weights[:8])
        hp, ss, hs = _ffn_and_sample_state(
            hp_mid.reshape(batch * seq, D_MODEL), p_prompt[i].reshape(batch * seq, PLE_DIM), ffn_weights,
            state_gla[i], sample_proj, xs2d, ps2d, mixer_weights[8:])
        hp = hp.reshape(batch, seq, D_MODEL)
        hs = hs.reshape(n_dec, 1, D_MODEL)

        conv_p.append(cbp[0]); gla_p.append(sp[0])
        conv_s.append(cbs); gla_s.append(ss)
    return (hp, hs, jnp.stack(conv_p), jnp.stack(gla_p),
            jnp.stack(conv_s), jnp.stack(gla_s))
```

```python
import numpy as np
import jax
import jax.numpy as jnp
from jax import lax
from jax.experimental import pallas as pl
from jax.experimental.pallas import tpu as pltpu

D_MODEL = 1024
CONV_DIM = D_MODEL
GLA_HEADS = 4
GLA_DK = 128
GLA_DV = 256
QK_DIM = GLA_HEADS * GLA_DK
V_DIM = GLA_HEADS * GLA_DV
GATE_RANK = 16
GATE_NORMALIZER = 16.0
D_FF = 2816
PLE_DIM = 256
EPS = 1e-6
LOG2_E = 1.4426950408889634

LANES = 128
SUBLANES = 8
BF16_ROWS = 16
VMEM_LIMIT_BYTES = 56 * 1024 * 1024

OFF_B, OFF_C, OFF_X = 0, 1024, 2048
OFF_Q, OFF_K, OFF_V, OFF_G = 3072, 3584, 4096, 5120
OFF_GA, OFF_GB = 6144, 7168
N_HEAD = 6144
N_TAIL = 2048
GKLR_START = 6144

GLA_CHUNK = 256
N_LEVELS = 8
SEQ_TILE = 512
FFN_TILE = 512
FFN_CHUNK = 1408
SAMPLE_BLOCK = 4
W_IN_BLOCK = 1024

F32 = jnp.float32
BF16 = jnp.bfloat16


def _rms(x, w):
    return x * lax.rsqrt(jnp.mean(x * x, axis=-1, keepdims=True) + EPS) * w


def _sigmoid(x):
    return 1.0 / (1.0 + jnp.exp2(x * -LOG2_E))


def _log2_sigmoid(x, scale):
    return (jnp.minimum(x, 0.0) - jnp.log(1.0 + jnp.exp2(jnp.abs(x) * -LOG2_E))) * (scale * LOG2_E)


def _dot(a, b):
    return jnp.dot(a, b, preferred_element_type=F32)


def _dot_nt(a, b):
    return lax.dot_general(a, b, (((1,), (1,)), ((), ())), preferred_element_type=F32)


def _column_broadcast(row):
    return jnp.broadcast_to(row, (LANES, LANES)).T


def _projections(hn, w_head_ref, w_tail_ref, w_gklr_ref, w_gk_ref, b_gk_ref):
    def proj(off, width):
        if off < N_HEAD:
            return _dot(hn, w_head_ref[:, off:off + width])
        return _dot(hn, w_tail_ref[:, off - N_HEAD:off - N_HEAD + width])

    gk_lr = _dot(hn, w_gklr_ref[...])
    gk = _dot(gk_lr.astype(BF16), w_gk_ref[...]) + b_gk_ref[...]
    return proj, _log2_sigmoid(gk, 1.0 / GATE_NORMALIZER)


def _mix_out(x, o_heads, g, gate_a, gate_b, y_a, w_gn, w_b_ref, w_o_ref, wn_post):
    normed = []
    for h in range(GLA_HEADS):
        o = o_heads[h]
        gh = g[:, h * GLA_DV:(h + 1) * GLA_DV]
        o = o * lax.rsqrt(jnp.mean(o * o, axis=-1, keepdims=True) + EPS) * w_gn
        normed.append((o * (gh * _sigmoid(gh))).astype(BF16))
    y_b = _dot(jnp.concatenate(normed, axis=1), w_b_ref[...])
    merged = _sigmoid(gate_a) * y_a + _sigmoid(gate_b) * y_b
    mix = _dot(merged.astype(BF16), w_o_ref[...])
    return x + _rms(mix, wn_post)


def _prompt_mixer_kernel(x_ref, lev_ref, tri_ref, wn_pre_ref, w_head_ref, w_tail_ref, w_gklr_ref,
                         w_gk_ref, b_gk_ref, w_conv_ref, w_a_ref, w_gn_ref, w_b_ref, w_o_ref,
                         wn_post_ref, *rest):
    n_side = (len(rest) - 4) // 2
    side_src = rest[:n_side]
    h_ref, conv_ref, state_ref = rest[n_side:n_side + 3]
    side_dst = rest[n_side + 3:2 * n_side + 3]
    dec_ref = rest[-1]

    @pl.when(pl.program_id(1) == 0)
    def _():
        conv_ref[...] = jnp.zeros_like(conv_ref)
        state_ref[...] = jnp.zeros_like(state_ref)

    for src, dst in zip(side_src, side_dst):
        dst[...] = src[...].astype(BF16)

    t = SEQ_TILE
    x = x_ref[0]
    hn = _rms(x, wn_pre_ref[...]).astype(BF16)
    proj, logw2 = _projections(hn, w_head_ref, w_tail_ref, w_gklr_ref, w_gk_ref, b_gk_ref)

    u = proj(OFF_C, CONV_DIM) * proj(OFF_X, CONV_DIM)
    prev2 = conv_ref[0, 0, 0:1, :]
    prev1 = conv_ref[0, 0, 1:2, :]
    row = lax.broadcasted_iota(jnp.int32, (t, CONV_DIM), 0)
    u1 = jnp.where(row == 0, prev1, pltpu.roll(u, 1, 0))
    u2 = jnp.where(row == 0, prev2, jnp.where(row == 1, prev1, pltpu.roll(u, 2, 0)))
    wc = w_conv_ref[...]
    y_conv = wc[0:1] * u2 + wc[1:2] * u1 + wc[2:3] * u
    conv_ref[0, 0] = u[t - 2:t]
    y_a = _dot((proj(OFF_B, CONV_DIM) * y_conv).astype(BF16), w_a_ref[...])

    q = proj(OFF_Q, QK_DIM) * (GLA_DK ** -0.5)
    k = proj(OFF_K, QK_DIM)
    v = proj(OFF_V, V_DIM).astype(BF16)

    o_chunks = []
    for c in range(SEQ_TILE // GLA_CHUNK):
        rows = slice(c * GLA_CHUNK, (c + 1) * GLA_CHUNK)
        o_chunks.append(_gla_chunk(q[rows], k[rows], v[rows], logw2[rows], dec_ref.at[c], lev_ref,
                                   tri_ref, state_ref))
    o_heads = [jnp.concatenate([o[h] for o in o_chunks], axis=0) for h in range(GLA_HEADS)]

    h_ref[0] = _mix_out(x, o_heads, proj(OFF_G, V_DIM), proj(OFF_GA, D_MODEL), proj(OFF_GB, D_MODEL),
                        y_a, w_gn_ref[...], w_b_ref, w_o_ref, wn_post_ref[...])


def _gla_chunk(q, k, v, logw2, dec_ref, lev_ref, tri_ref, state_ref):
    t = GLA_CHUNK
    dec_ref[0] = logw2
    logw2 = dec_ref[0]
    hi = logw2.astype(BF16)
    lo = (logw2 - hi.astype(F32)).astype(BF16)
    cum = _dot(tri_ref[...], hi) + _dot(tri_ref[...], lo)
    dec_ref[1] = cum
    cum_last = cum[t - 1:t]

    half = t // 2
    lev = lev_ref[...]
    rowq = lax.broadcasted_iota(jnp.int32, (t, QK_DIM), 0)

    def level_log2_factor(i):
        m = 1 << i
        pieces = []
        for blk in range(t // (2 * m)):
            r = blk * 2 * m + m - 1
            pieces.append(jnp.broadcast_to(dec_ref[1, r:r + 1, :], (2 * m, QK_DIM)))
        return -jnp.abs(cum - jnp.concatenate(pieces, axis=0))

    def head(a, h):
        return a[:, h * GLA_DK:(h + 1) * GLA_DK]

    def split_level(i):
        m = 1 << i
        qs, ks = [], []
        for blk in range(t // (2 * m)):
            first = slice(blk * 2 * m, blk * 2 * m + m)
            second = slice(blk * 2 * m + m, (blk + 1) * 2 * m)
            ref_row = dec_ref[1, blk * 2 * m + m - 1:blk * 2 * m + m, :]
            ks += [(k[first] * jnp.exp2(ref_row - cum[first])).astype(BF16), k_lv[0][second]]
            qs += [q_lv[0][first], (q[second] * jnp.exp2(cum[second] - ref_row)).astype(BF16)]
        return jnp.concatenate(qs, axis=0), jnp.concatenate(ks, axis=0)

    q_lv = [q.astype(BF16)]
    k_lv = [k.astype(BF16)]
    step_decay = jnp.exp2(logw2)
    r4 = rowq & 3
    q0 = q * step_decay
    q_lv += [q0.astype(BF16),
             (q0 * jnp.where(r4 == 3, pltpu.roll(step_decay, 1, 0), 1.0)).astype(BF16)]
    k_lv += [k_lv[0], (k * jnp.where(r4 == 0, pltpu.roll(step_decay, t - 1, 0), 1.0)).astype(BF16)]
    for i in range(2, N_LEVELS - 1):
        if (1 << i) % BF16_ROWS == 0:
            q_i, k_i = split_level(i)
        else:
            e = jnp.exp2(level_log2_factor(i))
            q_i, k_i = (q * e).astype(BF16), (k * e).astype(BF16)
        q_lv.append(q_i)
        k_lv.append(k_i)
    cum_mid = dec_ref[1, half - 1:half, :]
    q_top = (q[half:] * jnp.exp2(cum[half:] - cum_mid)).astype(BF16)
    k_top = (k[:half] * jnp.exp2(cum_mid - cum[:half])).astype(BF16)

    q_in = (q * jnp.exp2(cum)).astype(BF16)
    k_out = k * jnp.exp2(cum_last - cum)
    a_last = jnp.exp2(cum_last)

    o_heads = []
    for h in range(GLA_HEADS):
        vh = v[:, h * GLA_DV:(h + 1) * GLA_DV]
        diag = [0.0, 0.0]
        for i in range(N_LEVELS):
            s_i = _dot_nt(head(q_lv[i], h), head(k_lv[i], h))
            diag = [jnp.where(lev == i - 1, s_i[r0:r0 + half, r0:r0 + half], diag[j])
                    for j, r0 in enumerate((0, half))]
        diag = [d.astype(BF16) for d in diag]
        p_top = jnp.concatenate([diag[0], jnp.zeros((half, half), BF16)], axis=1)
        p_low = jnp.concatenate([_dot_nt(head(q_top, h), head(k_top, h)).astype(BF16), diag[1]], axis=1)
        s_old = state_ref[0, 0, h]
        lhs = jnp.concatenate([jnp.concatenate([p_top, p_low], axis=0), head(q_in, h)], axis=1)
        o_heads.append(_dot(lhs, jnp.concatenate([vh, s_old.astype(BF16)], axis=0)))
        a_col = _column_broadcast(head(a_last, h))
        a_col = jnp.concatenate([a_col, a_col], axis=1)
        state_ref[0, 0, h] = a_col * s_old + _dot(head(k_out, h).T.astype(BF16), vh)
    return o_heads


def _sample_proj_kernel(x_ref, cbuf_ref, wn_pre_ref, w_head_ref, w_tail_ref, w_gklr_ref, w_gk_ref,
                        b_gk_ref, w_conv_ref, w_a_ref,
                        ps_ref, conv_ref, q_ref, k_ref, a_ref, v_ref, ya_ref, g_ref, ga_ref, gb_ref,
                        xs2_ref, ps2_ref):
    x = x_ref[:, 0, :]
    xs2_ref[...] = x
    ps2_ref[...] = ps_ref[:, 0, :]
    hn = _rms(x, wn_pre_ref[...]).astype(BF16)
    proj, logw2 = _projections(hn, w_head_ref, w_tail_ref, w_gklr_ref, w_gk_ref, b_gk_ref)
    u = proj(OFF_C, CONV_DIM) * proj(OFF_X, CONV_DIM)
    buf0 = cbuf_ref[:, 0, :]
    buf1 = cbuf_ref[:, 1, :]
    wc = w_conv_ref[...]
    y_conv = wc[0:1] * buf0 + wc[1:2] * buf1 + wc[2:3] * u
    conv_ref[:, 0, :] = buf1
    conv_ref[:, 1, :] = u
    ya_ref[...] = _dot((proj(OFF_B, CONV_DIM) * y_conv).astype(BF16), w_a_ref[...])
    grouped = lambda a: a.reshape(a.shape[0] // SUBLANES, SUBLANES, a.shape[1])
    q_ref[...] = grouped(proj(OFF_Q, QK_DIM) * (GLA_DK ** -0.5))
    k_ref[...] = grouped(proj(OFF_K, QK_DIM))
    a_ref[...] = grouped(jnp.exp2(logw2))
    v_ref[...] = grouped(proj(OFF_V, V_DIM))
    g_ref[...] = proj(OFF_G, V_DIM)
    ga_ref[...] = proj(OFF_GA, D_MODEL)
    gb_ref[...] = proj(OFF_GB, D_MODEL)


def _sample_state_update(blk, st_ref, st_out_ref, q_ref, k_ref, a_ref, v_ref, o_scr):
    per_group = SUBLANES // SAMPLE_BLOCK
    group = blk // per_group
    first = (blk % per_group) * SAMPLE_BLOCK
    row_id = lax.broadcasted_iota(jnp.int32, (SUBLANES, V_DIM), 0)
    o_tile = o_scr[group]
    for n in range(SAMPLE_BLOCK):
        r = first + n
        q_row = q_ref[group, pl.ds(r, 1), :]
        k_row = k_ref[group, pl.ds(r, 1), :]
        a_row = a_ref[group, pl.ds(r, 1), :]
        v_row = v_ref[group, pl.ds(r, 1), :]
        o_parts = []
        for h in range(GLA_HEADS):
            sl = slice(h * GLA_DK, (h + 1) * GLA_DK)
            a_col = _column_broadcast(a_row[:, sl])
            k_col = _column_broadcast(k_row[:, sl])
            q_col = _column_broadcast(q_row[:, sl])
            for half in range(GLA_DV // LANES):
                c0 = half * LANES
                s_old = st_ref[n, h, :, c0:c0 + LANES]
                vv = v_row[:, h * GLA_DV + c0:h * GLA_DV + c0 + LANES]
                s_new = a_col * s_old + k_col * vv
                st_out_ref[n, h, :, c0:c0 + LANES] = s_new
                o_parts.append(jnp.sum(q_col * s_new, axis=0, keepdims=True))
        o_row = jnp.concatenate(o_parts, axis=1)
        o_tile = jnp.where(row_id == r, o_row, o_tile)
    o_scr[group] = o_tile


def _ffn_rows(h, p, wn_pre_ref, w_gate_ref, w_up_ref, w_down_ref, wn_post_ref, w_pp_ref, w_pg_ref,
              wn_ple_ref):
    f = _rms(h, wn_pre_ref[...]).astype(BF16)
    acc = None
    for c in range(D_FF // FFN_CHUNK):
        sl = slice(c * FFN_CHUNK, (c + 1) * FFN_CHUNK)
        gate = _dot(f, w_gate_ref[:, sl])
        up = _dot(f, w_up_ref[:, sl])
        act = (gate * _sigmoid(gate) * up).astype(BF16)
        part = _dot(act, w_down_ref[sl, :])
        acc = part if acc is None else acc + part
    h = h + _rms(acc, wn_post_ref[...])
    e = _dot(p.astype(BF16), w_pp_ref[...]) * _sigmoid(_dot(h.astype(BF16), w_pg_ref[...]))
    return h + _rms(e, wn_ple_ref[...])


def _ffn_kernel(h_ref, p_ref, wn_pre_ref, w_gate_ref, w_up_ref, w_down_ref, wn_post_ref,
                w_pp_ref, w_pg_ref, wn_ple_ref,
                st_ref, q_ref, k_ref, a_ref, v_ref, xs_ref, ps_ref, ya_ref, g_ref, ga_ref, gb_ref,
                w_gn_ref, w_b_ref, w_o_ref, wn_mix_post_ref,
                out_ref, st_out_ref, outs_ref, o_scr):
    step = pl.program_id(0)
    n_tiles = pl.num_programs(0) - 1
    ffn_refs = (wn_pre_ref, w_gate_ref, w_up_ref, w_down_ref, wn_post_ref, w_pp_ref, w_pg_ref, wn_ple_ref)

    @pl.when(step == 0)
    def _():
        o_scr[...] = jnp.zeros_like(o_scr)

    @pl.when(step < n_tiles)
    def _():
        out_ref[...] = _ffn_rows(h_ref[...], p_ref[...], *ffn_refs)
        _sample_state_update(step, st_ref, st_out_ref, q_ref, k_ref, a_ref, v_ref, o_scr)

    @pl.when(step == n_tiles)
    def _():
        o = o_scr[...].reshape(xs_ref.shape[0], V_DIM)
        o_heads = [o[:, h * GLA_DV:(h + 1) * GLA_DV] for h in range(GLA_HEADS)]
        hs = _mix_out(xs_ref[...], o_heads, g_ref[...], ga_ref[...], gb_ref[...], ya_ref[...],
                      w_gn_ref[...], w_b_ref, w_o_ref, wn_mix_post_ref[...])
        outs_ref[...] = _ffn_rows(hs, ps_ref[...], *ffn_refs)


def _cast_w_in_kernel(wt_ref, gk_ref, *refs):
    n = (len(refs) - 3) // 2
    head_ref, tail_ref, gklr_ref = refs[n:n + 3]
    for src, dst in zip(refs[:n], refs[n + 3:]):
        dst[...] = src[...].astype(BF16)
    j = pl.program_id(0)
    blk = wt_ref[...].T.astype(BF16)

    @pl.when(j < N_HEAD // W_IN_BLOCK)
    def _():
        head_ref[...] = blk

    @pl.when(j >= N_HEAD // W_IN_BLOCK)
    def _():
        tail_ref[...] = blk

    @pl.when(j == 0)
    def _():
        rows = jnp.concatenate([gk_ref[...], jnp.zeros((LANES - GATE_RANK, D_MODEL), F32)], axis=0)
        gklr_ref[...] = rows.T.astype(BF16)


def _cast_w_in(w_in_t, others):
    n_head = N_HEAD // W_IN_BLOCK
    n_tail = N_TAIL // W_IN_BLOCK
    other_specs = [pl.BlockSpec((w.shape[0] // (n_head + n_tail), w.shape[1]), lambda j: (j, 0))
                   for w in others]

    def src_row(j):
        row = jnp.where(j < n_head, j * W_IN_BLOCK, GKLR_START + GATE_RANK + (j - n_head) * W_IN_BLOCK)
        return pl.multiple_of(row, GATE_RANK)

    return pl.pallas_call(
        _cast_w_in_kernel,
        grid=(n_head + n_tail,),
        in_specs=[pl.BlockSpec((pl.Element(W_IN_BLOCK), pl.Element(D_MODEL)), lambda j: (src_row(j), 0)),
                  pl.BlockSpec((pl.Element(GATE_RANK), pl.Element(D_MODEL)), lambda j: (GKLR_START, 0))]
                 + other_specs,
        out_specs=[pl.BlockSpec((D_MODEL, W_IN_BLOCK), lambda j: (0, jnp.minimum(j, n_head - 1))),
                   pl.BlockSpec((D_MODEL, W_IN_BLOCK), lambda j: (0, jnp.maximum(j - n_head, 0))),
                   pl.BlockSpec((D_MODEL, LANES), lambda j: (0, 0))] + other_specs,
        out_shape=[jax.ShapeDtypeStruct((D_MODEL, N_HEAD), BF16),
                   jax.ShapeDtypeStruct((D_MODEL, N_TAIL), BF16),
                   jax.ShapeDtypeStruct((D_MODEL, LANES), BF16)]
                  + [jax.ShapeDtypeStruct(w.shape, BF16) for w in others],
        compiler_params=pltpu.CompilerParams(
            dimension_semantics=("arbitrary",), vmem_limit_bytes=VMEM_LIMIT_BYTES),
        name="cast_weights",
    )(w_in_t, w_in_t, *others)


def _resident(shape):
    return pl.BlockSpec(shape, lambda *_: (0,) * len(shape), pipeline_mode=pl.Buffered(1))


def _level_map(t):
    idx = np.arange(t)
    xor = idx[:, None] ^ idx[None, :]
    lev = np.floor(np.log2(np.maximum(xor, 1))).astype(np.int32)
    lev = np.where(idx[:, None] > idx[None, :], lev, -2)
    lev = np.where(idx[:, None] == idx[None, :], -1, lev)
    return jnp.asarray(lev, dtype=jnp.int32)


def _mixer_weight_specs():
    return [
        _resident((1, D_MODEL)),
        _resident((D_MODEL, N_HEAD)),
        _resident((D_MODEL, N_TAIL)),
        _resident((D_MODEL, LANES)),
        _resident((LANES, QK_DIM)),
        _resident((1, QK_DIM)),
        _resident((3, CONV_DIM)),
        _resident((CONV_DIM, D_MODEL)),
        _resident((1, GLA_DV)),
        _resident((V_DIM, D_MODEL)),
        _resident((D_MODEL, D_MODEL)),
        _resident((1, D_MODEL)),
    ]


def _prompt_mixer(x, mixer_weights, later_weights):
    b, s, _ = x.shape
    t = SEQ_TILE
    c = GLA_CHUNK
    n_steps = b * (s // t)
    tri = jnp.asarray(np.tril(np.ones((c, c), np.float32)), dtype=BF16)

    def side_spec(shape):
        n_blocks = n_steps
        while shape[0] % (n_blocks * BF16_ROWS):
            n_blocks //= 2
        rep = n_steps // n_blocks
        return pl.BlockSpec((shape[0] // n_blocks, shape[1]),
                            lambda i, j: ((i * (s // t) + j) // rep, 0))

    side_specs = [side_spec(w.shape) for w in later_weights]
    return pl.pallas_call(
        _prompt_mixer_kernel,
        grid=(b, s // t),
        in_specs=[pl.BlockSpec((1, t, D_MODEL), lambda i, j: (i, j, 0)),
                  _resident((c // 2, c // 2)), _resident((c, c))] + _mixer_weight_specs() + side_specs,
        out_specs=[pl.BlockSpec((1, t, D_MODEL), lambda i, j: (i, j, 0)),
                   pl.BlockSpec((1, 1, 2, CONV_DIM), lambda i, j: (0, i, 0, 0)),
                   pl.BlockSpec((1, 1, GLA_HEADS, GLA_DK, GLA_DV), lambda i, j: (0, i, 0, 0, 0))]
                  + side_specs,
        out_shape=[jax.ShapeDtypeStruct((b, s, D_MODEL), F32),
                   jax.ShapeDtypeStruct((1, b, 2, CONV_DIM), F32),
                   jax.ShapeDtypeStruct((1, b, GLA_HEADS, GLA_DK, GLA_DV), F32)]
                  + [jax.ShapeDtypeStruct(w.shape, BF16) for w in later_weights],
        scratch_shapes=[pltpu.VMEM((t // c, 2, c, QK_DIM), F32)],
        compiler_params=pltpu.CompilerParams(
            dimension_semantics=("arbitrary", "arbitrary"), vmem_limit_bytes=VMEM_LIMIT_BYTES),
        name="prompt_mixer",
    )(x, _level_map(c // 2), tri, *mixer_weights, *later_weights)


def _sample_proj(x, conv_buf, p, proj_weights):
    n = x.shape[0]
    grouped = lambda w: (n // SUBLANES, SUBLANES, w)
    out_shapes = [(n, 2, CONV_DIM), grouped(QK_DIM), grouped(QK_DIM), grouped(QK_DIM), grouped(V_DIM),
                  (n, D_MODEL), (n, V_DIM), (n, D_MODEL), (n, D_MODEL), (n, D_MODEL), (n, PLE_DIM)]
    whole = lambda shape: pl.BlockSpec(shape, lambda i: (0,) * len(shape))
    return pl.pallas_call(
        _sample_proj_kernel,
        grid=(1,),
        in_specs=[_resident((n, 1, D_MODEL)), _resident((n, 2, CONV_DIM))] + _mixer_weight_specs()[:8]
                 + [_resident((n, 1, PLE_DIM))],
        out_specs=[whole(s) for s in out_shapes],
        out_shape=[jax.ShapeDtypeStruct(s, F32) for s in out_shapes],
        compiler_params=pltpu.CompilerParams(
            dimension_semantics=("arbitrary",), vmem_limit_bytes=VMEM_LIMIT_BYTES),
        name="sample_proj",
    )(x, conv_buf, *proj_weights, p)


def _ffn_and_sample_state(h, p, ffn_weights, state, sample_proj, x_s, p_s, out_weights):
    rows = h.shape[0]
    n = x_s.shape[0]
    n_tiles = rows // FFN_TILE
    assert n_tiles * SAMPLE_BLOCK == n, "one sample state block per prompt row tile"
    q, k, a, v, y_a, g, gate_a, gate_b = sample_proj
    tile = lambda i: jnp.minimum(i, n_tiles - 1)
    state_spec = pl.BlockSpec((SAMPLE_BLOCK, GLA_HEADS, GLA_DK, GLA_DV), lambda i: (tile(i), 0, 0, 0))
    return pl.pallas_call(
        _ffn_kernel,
        grid=(n_tiles + 1,),
        in_specs=[pl.BlockSpec((FFN_TILE, D_MODEL), lambda i: (tile(i), 0)),
                  pl.BlockSpec((FFN_TILE, PLE_DIM), lambda i: (tile(i), 0)),
                  _resident((1, D_MODEL)), _resident((D_MODEL, D_FF)), _resident((D_MODEL, D_FF)),
                  _resident((D_FF, D_MODEL)), _resident((1, D_MODEL)), _resident((PLE_DIM, D_MODEL)),
                  _resident((D_MODEL, D_MODEL)), _resident((1, D_MODEL)),
                  state_spec, _resident(q.shape), _resident(k.shape), _resident(a.shape),
                  _resident(v.shape), _resident((n, D_MODEL)), _resident((n, PLE_DIM)),
                  _resident((n, D_MODEL)), _resident((n, V_DIM)), _resident((n, D_MODEL)),
                  _resident((n, D_MODEL)),
                  _resident((1, GLA_DV)), _resident((V_DIM, D_MODEL)), _resident((D_MODEL, D_MODEL)),
                  _resident((1, D_MODEL))],
        out_specs=[pl.BlockSpec((FFN_TILE, D_MODEL), lambda i: (tile(i), 0)),
                   state_spec,
                   pl.BlockSpec((n, D_MODEL), lambda i: (0, 0))],
        out_shape=[jax.ShapeDtypeStruct((rows, D_MODEL), F32),
                   jax.ShapeDtypeStruct(state.shape, F32),
                   jax.ShapeDtypeStruct((n, D_MODEL), F32)],
        scratch_shapes=[pltpu.VMEM((n // SUBLANES, SUBLANES, V_DIM), F32)],
        compiler_params=pltpu.CompilerParams(
            dimension_semantics=("arbitrary",), vmem_limit_bytes=VMEM_LIMIT_BYTES),
        name="ffn_ple",
    )(h, p, *ffn_weights, state, q, k, a, v, x_s, p_s, y_a, g, gate_a, gate_b, *out_weights)


def kernel(x_prompt, x_sample, state_conv, state_gla, p_prompt, p_sample, w_norm_mix_pre, w_in, w_conv, w_a_out, w_gk, b_gk, w_gla_norm, w_b_out, w_o, w_norm_mix_post, w_norm_ffn_pre, w_ffn_gate, w_ffn_up, w_ffn_down, w_norm_ffn_post, w_ple_proj, w_ple_gate, w_norm_ple_post):
    depth = w_in.shape[0]
    batch, seq, _ = x_prompt.shape
    n_dec = x_sample.shape[0]
    assert x_sample.shape[1] == 1, "the sample group carries one new token per sequence"
    assert seq % SEQ_TILE == 0 and (batch * seq) % FFN_TILE == 0 and n_dec % SAMPLE_BLOCK == 0

    hp = x_prompt
    hs = x_sample
    conv_p, gla_p, conv_s, gla_s = [], [], [], []
    for i in range(depth):
        row = lambda w: w[i].reshape(1, -1)
        w_head, w_tail, w_gklr, w_a, w_b, w_o_b = _cast_w_in(
            jnp.swapaxes(w_in[i], 0, 1), (w_a_out[i], w_b_out[i], w_o[i]))
        w_gk_pad = jnp.pad(w_gk[i], ((0, LANES - GATE_RANK), (0, 0))).astype(BF16)
        mixer_weights = (row(w_norm_mix_pre), w_head, w_tail, w_gklr, w_gk_pad, row(b_gk), w_conv[i],
                         w_a, row(w_gla_norm), w_b, w_o_b, row(w_norm_mix_post))

        hp_mid, cbp, sp, w_gate, w_up, w_down, w_pp, w_pg = _prompt_mixer(
            hp, mixer_weights, (w_ffn_gate[i], w_ffn_up[i], w_ffn_down[i], w_ple_proj[i], w_ple_gate[i]))
        ffn_weights = (row(w_norm_ffn_pre), w_gate, w_up, w_down, row(w_norm_ffn_post), w_pp, w_pg,
                       row(w_norm_ple_post))
        cbs, *sample_proj, xs2d, ps2d = _sample_proj(hs, state_conv[i], p_sample[i], mixer_weights[:8])
        hp, ss, hs = _ffn_and_sample_state(
            hp_mid.reshape(batch * seq, D_MODEL), p_prompt[i].reshape(batch * seq, PLE_DIM), ffn_weights,
            state_gla[i], sample_proj, xs2d, ps2d, mixer_weights[8:])
        hp = hp.reshape(batch, seq, D_MODEL)
        hs = hs.reshape(n_dec, 1, D_MODEL)

        conv_p.append(cbp[0]); gla_p.append(sp[0])
        conv_s.append(cbs); gla_s.append(ss)
    return (hp, hs, jnp.stack(conv_p), jnp.stack(gla_p),
            jnp.stack(conv_s), jnp.stack(gla_s))
```

```python
import numpy as np
import jax
import jax.numpy as jnp
from jax import lax
from jax.experimental import pallas as pl
from jax.experimental.pallas import tpu as pltpu

D_MODEL = 1024
CONV_DIM = D_MODEL
GLA_HEADS = 4
GLA_DK = 128
GLA_DV = 256
QK_DIM = GLA_HEADS * GLA_DK
V_DIM = GLA_HEADS * GLA_DV
GATE_RANK = 16
GATE_NORMALIZER = 16.0
D_FF = 2816
PLE_DIM = 256
EPS = 1e-6
LOG2_E = 1.4426950408889634

LANES = 128
SUBLANES = 8
BF16_ROWS = 16
VMEM_LIMIT_BYTES = 56 * 1024 * 1024

OFF_B, OFF_C, OFF_X = 0, 1024, 2048
OFF_Q, OFF_K, OFF_V, OFF_G = 3072, 3584, 4096, 5120
OFF_GA, OFF_GB = 6144, 7168
N_HEAD = 6144
N_TAIL = 2048
GKLR_START = 6144

GLA_CHUNK = 256
N_LEVELS = 8
SEQ_TILE = 512
FFN_TILE = 512
FFN_CHUNK = 1408
SAMPLE_BLOCK = 4
W_IN_BLOCK = 1024

F32 = jnp.float32
BF16 = jnp.bfloat16


def _rms(x, w):
    return x * lax.rsqrt(jnp.mean(x * x, axis=-1, keepdims=True) + EPS) * w


def _sigmoid(x):
    return 1.0 / (1.0 + jnp.exp2(x * -LOG2_E))


def _log2_sigmoid(x, scale):
    return (jnp.minimum(x, 0.0) - jnp.log(1.0 + jnp.exp2(jnp.abs(x) * -LOG2_E))) * (scale * LOG2_E)


def _dot(a, b):
    return jnp.dot(a, b, preferred_element_type=F32)


def _dot_nt(a, b):
    return lax.dot_general(a, b, (((1,), (1,)), ((), ())), preferred_element_type=F32)


def _column_broadcast(row):
    return jnp.broadcast_to(row, (LANES, LANES)).T


def _projections(hn, w_head_ref, w_tail_ref, w_gklr_ref, w_gk_ref, b_gk_ref):
    def proj(off, width):
        if off < N_HEAD:
            return _dot(hn, w_head_ref[:, off:off + width])
        return _dot(hn, w_tail_ref[:, off - N_HEAD:off - N_HEAD + width])

    gk_lr = _dot(hn, w_gklr_ref[...])
    gk = _dot(gk_lr.astype(BF16), w_gk_ref[...]) + b_gk_ref[...]
    return proj, _log2_sigmoid(gk, 1.0 / GATE_NORMALIZER)


def _mix_out(x, o_heads, g, gate_a, gate_b, y_a, w_gn, w_b_ref, w_o_ref, wn_post):
    normed = []
    for h in range(GLA_HEADS):
        o = o_heads[h]
        gh = g[:, h * GLA_DV:(h + 1) * GLA_DV]
        o = o * lax.rsqrt(jnp.mean(o * o, axis=-1, keepdims=True) + EPS) * w_gn
        normed.append((o * (gh * _sigmoid(gh))).astype(BF16))
    y_b = _dot(jnp.concatenate(normed, axis=1), w_b_ref[...])
    merged = _sigmoid(gate_a) * y_a + _sigmoid(gate_b) * y_b
    mix = _dot(merged.astype(BF16), w_o_ref[...])
    return x + _rms(mix, wn_post)


def _prompt_mixer_kernel(x_ref, lev_ref, tri_ref, wn_pre_ref, w_head_ref, w_tail_ref, w_gklr_ref,
                         w_gk_ref, b_gk_ref, w_conv_ref, w_a_ref, w_gn_ref, w_b_ref, w_o_ref,
                         wn_post_ref, *rest):
    n_side = (len(rest) - 4) // 2
    side_src = rest[:n_side]
    h_ref, conv_ref, state_ref = rest[n_side:n_side + 3]
    side_dst = rest[n_side + 3:2 * n_side + 3]
    dec_ref = rest[-1]

    @pl.when(pl.program_id(1) == 0)
    def _():
        conv_ref[...] = jnp.zeros_like(conv_ref)
        state_ref[...] = jnp.zeros_like(state_ref)

    for src, dst in zip(side_src, side_dst):
        dst[...] = src[...].astype(BF16)

    t = SEQ_TILE
    x = x_ref[0]
    hn = _rms(x, wn_pre_ref[...]).astype(BF16)
    proj, logw2 = _projections(hn, w_head_ref, w_tail_ref, w_gklr_ref, w_gk_ref, b_gk_ref)

    u = proj(OFF_C, CONV_DIM) * proj(OFF_X, CONV_DIM)
    prev2 = conv_ref[0, 0, 0:1, :]
    prev1 = conv_ref[0, 0, 1:2, :]
    row = lax.broadcasted_iota(jnp.int32, (t, CONV_DIM), 0)
    u1 = jnp.where(row == 0, prev1, pltpu.roll(u, 1, 0))
    u2 = jnp.where(row == 0, prev2, jnp.where(row == 1, prev1, pltpu.roll(u, 2, 0)))
    wc = w_conv_ref[...]
    y_conv = wc[0:1] * u2 + wc[1:2] * u1 + wc[2:3] * u
    conv_ref[0, 0] = u[t - 2:t]
    y_a = _dot((proj(OFF_B, CONV_DIM) * y_conv).astype(BF16), w_a_ref[...])

    q = proj(OFF_Q, QK_DIM) * (GLA_DK ** -0.5)
    k = proj(OFF_K, QK_DIM)
    v = proj(OFF_V, V_DIM).astype(BF16)

    o_chunks = []
    for c in range(SEQ_TILE // GLA_CHUNK):
        rows = slice(c * GLA_CHUNK, (c + 1) * GLA_CHUNK)
        o_chunks.append(_gla_chunk(q[rows], k[rows], v[rows], logw2[rows], dec_ref.at[c], lev_ref,
                                   tri_ref, state_ref))
    o_heads = [jnp.concatenate([o[h] for o in o_chunks], axis=0) for h in range(GLA_HEADS)]

    h_ref[0] = _mix_out(x, o_heads, proj(OFF_G, V_DIM), proj(OFF_GA, D_MODEL), proj(OFF_GB, D_MODEL),
                        y_a, w_gn_ref[...], w_b_ref, w_o_ref, wn_post_ref[...])


def _gla_chunk(q, k, v, logw2, dec_ref, lev_ref, tri_ref, state_ref):
    t = GLA_CHUNK
    dec_ref[0] = logw2
    logw2 = dec_ref[0]
    hi = logw2.astype(BF16)
    lo = (logw2 - hi.astype(F32)).astype(BF16)
    cum = _dot(tri_ref[...], jnp.concatenate([hi, lo], axis=0))
    dec_ref[1] = cum
    cum_last = cum[t - 1:t]

    half = t // 2
    lev = lev_ref[...]
    rowq = lax.broadcasted_iota(jnp.int32, (t, QK_DIM), 0)

    def level_log2_factor(i):
        m = 1 << i
        pieces = []
        for blk in range(t // (2 * m)):
            r = blk * 2 * m + m - 1
            pieces.append(jnp.broadcast_to(dec_ref[1, r:r + 1, :], (2 * m, QK_DIM)))
        return -jnp.abs(cum - jnp.concatenate(pieces, axis=0))

    def head(a, h):
        return a[:, h * GLA_DK:(h + 1) * GLA_DK]

    def split_level(i):
        m = 1 << i
        qs, ks = [], []
        for blk in range(t // (2 * m)):
            first = slice(blk * 2 * m, blk * 2 * m + m)
            second = slice(blk * 2 * m + m, (blk + 1) * 2 * m)
            ref_row = dec_ref[1, blk * 2 * m + m - 1:blk * 2 * m + m, :]
            ks += [(k[first] * jnp.exp2(ref_row - cum[first])).astype(BF16), k_lv[0][second]]
            qs += [q_lv[0][first], (q[second] * jnp.exp2(cum[second] - ref_row)).astype(BF16)]
        return jnp.concatenate(qs, axis=0), jnp.concatenate(ks, axis=0)

    q_lv = [q.astype(BF16)]
    k_lv = [k.astype(BF16)]
    step_decay = jnp.exp2(logw2)
    r4 = rowq & 3
    q0 = q * step_decay
    q_lv += [q0.astype(BF16),
             (q0 * jnp.where(r4 == 3, pltpu.roll(step_decay, 1, 0), 1.0)).astype(BF16)]
    k_lv += [k_lv[0], (k * jnp.where(r4 == 0, pltpu.roll(step_decay, t - 1, 0), 1.0)).astype(BF16)]
    for i in range(2, N_LEVELS - 1):
        if (1 << i) % BF16_ROWS == 0:
            q_i, k_i = split_level(i)
        else:
            e = jnp.exp2(level_log2_factor(i))
            q_i, k_i = (q * e).astype(BF16), (k * e).astype(BF16)
        q_lv.append(q_i)
        k_lv.append(k_i)
    cum_mid = dec_ref[1, half - 1:half, :]
    q_top = (q[half:] * jnp.exp2(cum[half:] - cum_mid)).astype(BF16)
    k_top = (k[:half] * jnp.exp2(cum_mid - cum[:half])).astype(BF16)

    q_in = (q * jnp.exp2(cum)).astype(BF16)
    k_out = k * jnp.exp2(cum_last - cum)
    a_last = jnp.exp2(cum_last)

    o_heads = []
    for h in range(GLA_HEADS):
        vh = v[:, h * GLA_DV:(h + 1) * GLA_DV]
        diag = [0.0, 0.0]
        for i in range(N_LEVELS):
            s_i = _dot_nt(head(q_lv[i], h), head(k_lv[i], h))
            diag = [jnp.where(lev == i - 1, s_i[r0:r0 + half, r0:r0 + half], diag[j])
                    for j, r0 in enumerate((0, half))]
        diag = [d.astype(BF16) for d in diag]
        p_top = jnp.concatenate([diag[0], jnp.zeros((half, half), BF16)], axis=1)
        p_low = jnp.concatenate([_dot_nt(head(q_top, h), head(k_top, h)).astype(BF16), diag[1]], axis=1)
        s_old = state_ref[0, 0, h]
        lhs = jnp.concatenate([jnp.concatenate([p_top, p_low], axis=0), head(q_in, h)], axis=1)
        o_heads.append(_dot(lhs, jnp.concatenate([vh, s_old.astype(BF16)], axis=0)))
        a_col = _column_broadcast(head(a_last, h))
        a_col = jnp.concatenate([a_col, a_col], axis=1)
        state_ref[0, 0, h] = a_col * s_old + _dot(head(k_out, h).T.astype(BF16), vh)
    return o_heads


def _sample_proj_kernel(x_ref, cbuf_ref, wn_pre_ref, w_head_ref, w_tail_ref, w_gklr_ref, w_gk_ref,
                        b_gk_ref, w_conv_ref, w_a_ref,
                        ps_ref, conv_ref, q_ref, k_ref, a_ref, v_ref, ya_ref, g_ref, ga_ref, gb_ref,
                        xs2_ref, ps2_ref):
    x = x_ref[:, 0, :]
    xs2_ref[...] = x
    ps2_ref[...] = ps_ref[:, 0, :]
    hn = _rms(x, wn_pre_ref[...]).astype(BF16)
    proj, logw2 = _projections(hn, w_head_ref, w_tail_ref, w_gklr_ref, w_gk_ref, b_gk_ref)
    u = proj(OFF_C, CONV_DIM) * proj(OFF_X, CONV_DIM)
    buf0 = cbuf_ref[:, 0, :]
    buf1 = cbuf_ref[:, 1, :]
    wc = w_conv_ref[...]
    y_conv = wc[0:1] * buf0 + wc[1:2] * buf1 + wc[2:3] * u
    conv_ref[:, 0, :] = buf1
    conv_ref[:, 1, :] = u
    ya_ref[...] = _dot((proj(OFF_B, CONV_DIM) * y_conv).astype(BF16), w_a_ref[...])
    grouped = lambda a: a.reshape(a.shape[0] // SUBLANES, SUBLANES, a.shape[1])
    q_ref[...] = grouped(proj(OFF_Q, QK_DIM) * (GLA_DK ** -0.5))
    k_ref[...] = grouped(proj(OFF_K, QK_DIM))
    a_ref[...] = grouped(jnp.exp2(logw2))
    v_ref[...] = grouped(proj(OFF_V, V_DIM))
    g_ref[...] = proj(OFF_G, V_DIM)
    ga_ref[...] = proj(OFF_GA, D_MODEL)
    gb_ref[...] = proj(OFF_GB, D_MODEL)


def _sample_state_update(blk, st_ref, st_out_ref, q_ref, k_ref, a_ref, v_ref, o_scr):
    per_group = SUBLANES // SAMPLE_BLOCK
    group = blk // per_group
    first = (blk % per_group) * SAMPLE_BLOCK
    row_id = lax.broadcasted_iota(jnp.int32, (SUBLANES, V_DIM), 0)
    o_tile = o_scr[group]
    for n in range(SAMPLE_BLOCK):
        r = first + n
        q_row = q_ref[group, pl.ds(r, 1), :]
        k_row = k_ref[group, pl.ds(r, 1), :]
        a_row = a_ref[group, pl.ds(r, 1), :]
        v_row = v_ref[group, pl.ds(r, 1), :]
        o_parts = []
        for h in range(GLA_HEADS):
            sl = slice(h * GLA_DK, (h + 1) * GLA_DK)
            a_col = _column_broadcast(a_row[:, sl])
            k_col = _column_broadcast(k_row[:, sl])
            q_col = _column_broadcast(q_row[:, sl])
            for half in range(GLA_DV // LANES):
                c0 = half * LANES
                s_old = st_ref[n, h, :, c0:c0 + LANES]
                vv = v_row[:, h * GLA_DV + c0:h * GLA_DV + c0 + LANES]
                s_new = a_col * s_old + k_col * vv
                st_out_ref[n, h, :, c0:c0 + LANES] = s_new
                o_parts.append(jnp.sum(q_col * s_new, axis=0, keepdims=True))
        o_row = jnp.concatenate(o_parts, axis=1)
        o_tile = jnp.where(row_id == r, o_row, o_tile)
    o_scr[group] = o_tile


def _ffn_rows(h, p, wn_pre_ref, w_gate_ref, w_up_ref, w_down_ref, wn_post_ref, w_pp_ref, w_pg_ref,
              wn_ple_ref):
    f = _rms(h, wn_pre_ref[...]).astype(BF16)
    acc = None
    for c in range(D_FF // FFN_CHUNK):
        sl = slice(c * FFN_CHUNK, (c + 1) * FFN_CHUNK)
        gate = _dot(f, w_gate_ref[:, sl])
        up = _dot(f, w_up_ref[:, sl])
        act = (gate * _sigmoid(gate) * up).astype(BF16)
        part = _dot(act, w_down_ref[sl, :])
        acc = part if acc is None else acc + part
    h = h + _rms(acc, wn_post_ref[...])
    e = _dot(p.astype(BF16), w_pp_ref[...]) * _sigmoid(_dot(h.astype(BF16), w_pg_ref[...]))
    return h + _rms(e, wn_ple_ref[...])


def _ffn_kernel(h_ref, p_ref, wn_pre_ref, w_gate_ref, w_up_ref, w_down_ref, wn_post_ref,
                w_pp_ref, w_pg_ref, wn_ple_ref,
                st_ref, q_ref, k_ref, a_ref, v_ref, xs_ref, ps_ref, ya_ref, g_ref, ga_ref, gb_ref,
                w_gn_ref, w_b_ref, w_o_ref, wn_mix_post_ref,
                out_ref, st_out_ref, outs_ref, o_scr):
    step = pl.program_id(0)
    n_tiles = pl.num_programs(0) - 1
    ffn_refs = (wn_pre_ref, w_gate_ref, w_up_ref, w_down_ref, wn_post_ref, w_pp_ref, w_pg_ref, wn_ple_ref)

    @pl.when(step == 0)
    def _():
        o_scr[...] = jnp.zeros_like(o_scr)

    @pl.when(step < n_tiles)
    def _():
        out_ref[...] = _ffn_rows(h_ref[...], p_ref[...], *ffn_refs)
        _sample_state_update(step, st_ref, st_out_ref, q_ref, k_ref, a_ref, v_ref, o_scr)

    @pl.when(step == n_tiles)
    def _():
        o = o_scr[...].reshape(xs_ref.shape[0], V_DIM)
        o_heads = [o[:, h * GLA_DV:(h + 1) * GLA_DV] for h in range(GLA_HEADS)]
        hs = _mix_out(xs_ref[...], o_heads, g_ref[...], ga_ref[...], gb_ref[...], ya_ref[...],
                      w_gn_ref[...], w_b_ref, w_o_ref, wn_mix_post_ref[...])
        outs_ref[...] = _ffn_rows(hs, ps_ref[...], *ffn_refs)


def _cast_w_in_kernel(wt_ref, gk_ref, *refs):
    n = (len(refs) - 3) // 2
    head_ref, tail_ref, gklr_ref = refs[n:n + 3]
    for src, dst in zip(refs[:n], refs[n + 3:]):
        dst[...] = src[...].astype(BF16)
    j = pl.program_id(0)
    blk = wt_ref[...].T.astype(BF16)

    @pl.when(j < N_HEAD // W_IN_BLOCK)
    def _():
        head_ref[...] = blk

    @pl.when(j >= N_HEAD // W_IN_BLOCK)
    def _():
        tail_ref[...] = blk

    @pl.when(j == 0)
    def _():
        rows = jnp.concatenate([gk_ref[...], jnp.zeros((LANES - GATE_RANK, D_MODEL), F32)], axis=0)
        gklr_ref[...] = rows.T.astype(BF16)


def _cast_w_in(w_in_t, others):
    n_head = N_HEAD // W_IN_BLOCK
    n_tail = N_TAIL // W_IN_BLOCK
    other_specs = [pl.BlockSpec((w.shape[0] // (n_head + n_tail), w.shape[1]), lambda j: (j, 0))
                   for w in others]

    def src_row(j):
        row = jnp.where(j < n_head, j * W_IN_BLOCK, GKLR_START + GATE_RANK + (j - n_head) * W_IN_BLOCK)
        return pl.multiple_of(row, GATE_RANK)

    return pl.pallas_call(
        _cast_w_in_kernel,
        grid=(n_head + n_tail,),
        in_specs=[pl.BlockSpec((pl.Element(W_IN_BLOCK), pl.Element(D_MODEL)), lambda j: (src_row(j), 0)),
                  pl.BlockSpec((pl.Element(GATE_RANK), pl.Element(D_MODEL)), lambda j: (GKLR_START, 0))]
                 + other_specs,
        out_specs=[pl.BlockSpec((D_MODEL, W_IN_BLOCK), lambda j: (0, jnp.minimum(j, n_head - 1))),
                   pl.BlockSpec((D_MODEL, W_IN_BLOCK), lambda j: (0, jnp.maximum(j - n_head, 0))),
                   pl.BlockSpec((D_MODEL, LANES), lambda j: (0, 0))] + other_specs,
        out_shape=[jax.ShapeDtypeStruct((D_MODEL, N_HEAD), BF16),
                   jax.ShapeDtypeStruct((D_MODEL, N_TAIL), BF16),
                   jax.ShapeDtypeStruct((D_MODEL, LANES), BF16)]
                  + [jax.ShapeDtypeStruct(w.shape, BF16) for w in others],
        compiler_params=pltpu.CompilerParams(
            dimension_semantics=("arbitrary",), vmem_limit_bytes=VMEM_LIMIT_BYTES),
        name="cast_weights",
    )(w_in_t, w_in_t, *others)


def _resident(shape):
    return pl.BlockSpec(shape, lambda *_: (0,) * len(shape), pipeline_mode=pl.Buffered(1))


def _level_map(t):
    idx = np.arange(t)
    xor = idx[:, None] ^ idx[None, :]
    lev = np.floor(np.log2(np.maximum(xor, 1))).astype(np.int32)
    lev = np.where(idx[:, None] > idx[None, :], lev, -2)
    lev = np.where(idx[:, None] == idx[None, :], -1, lev)
    return jnp.asarray(lev, dtype=jnp.int32)


def _mixer_weight_specs():
    return [
        _resident((1, D_MODEL)),
        _resident((D_MODEL, N_HEAD)),
        _resident((D_MODEL, N_TAIL)),
        _resident((D_MODEL, LANES)),
        _resident((LANES, QK_DIM)),
        _resident((1, QK_DIM)),
        _resident((3, CONV_DIM)),
        _resident((CONV_DIM, D_MODEL)),
        _resident((1, GLA_DV)),
        _resident((V_DIM, D_MODEL)),
        _resident((D_MODEL, D_MODEL)),
        _resident((1, D_MODEL)),
    ]


def _prompt_mixer(x, mixer_weights, later_weights):
    b, s, _ = x.shape
    t = SEQ_TILE
    c = GLA_CHUNK
    n_steps = b * (s // t)
    tri = jnp.asarray(np.tile(np.tril(np.ones((c, c), np.float32)), (1, 2)), dtype=BF16)

    def side_spec(shape):
        n_blocks = n_steps
        while shape[0] % (n_blocks * BF16_ROWS):
            n_blocks //= 2
        rep = n_steps // n_blocks
        return pl.BlockSpec((shape[0] // n_blocks, shape[1]),
                            lambda i, j: ((i * (s // t) + j) // rep, 0))

    side_specs = [side_spec(w.shape) for w in later_weights]
    return pl.pallas_call(
        _prompt_mixer_kernel,
        grid=(b, s // t),
        in_specs=[pl.BlockSpec((1, t, D_MODEL), lambda i, j: (i, j, 0)),
                  _resident((c // 2, c // 2)), _resident((c, 2 * c))] + _mixer_weight_specs() + side_specs,
        out_specs=[pl.BlockSpec((1, t, D_MODEL), lambda i, j: (i, j, 0)),
                   pl.BlockSpec((1, 1, 2, CONV_DIM), lambda i, j: (0, i, 0, 0)),
                   pl.BlockSpec((1, 1, GLA_HEADS, GLA_DK, GLA_DV), lambda i, j: (0, i, 0, 0, 0))]
                  + side_specs,
        out_shape=[jax.ShapeDtypeStruct((b, s, D_MODEL), F32),
                   jax.ShapeDtypeStruct((1, b, 2, CONV_DIM), F32),
                   jax.ShapeDtypeStruct((1, b, GLA_HEADS, GLA_DK, GLA_DV), F32)]
                  + [jax.ShapeDtypeStruct(w.shape, BF16) for w in later_weights],
        scratch_shapes=[pltpu.VMEM((t // c, 2, c, QK_DIM), F32)],
        compiler_params=pltpu.CompilerParams(
            dimension_semantics=("arbitrary", "arbitrary"), vmem_limit_bytes=VMEM_LIMIT_BYTES),
        name="prompt_mixer",
    )(x, _level_map(c // 2), tri, *mixer_weights, *later_weights)


def _sample_proj(x, conv_buf, p, proj_weights):
    n = x.shape[0]
    grouped = lambda w: (n // SUBLANES, SUBLANES, w)
    out_shapes = [(n, 2, CONV_DIM), grouped(QK_DIM), grouped(QK_DIM), grouped(QK_DIM), grouped(V_DIM),
                  (n, D_MODEL), (n, V_DIM), (n, D_MODEL), (n, D_MODEL), (n, D_MODEL), (n, PLE_DIM)]
    whole = lambda shape: pl.BlockSpec(shape, lambda i: (0,) * len(shape))
    return pl.pallas_call(
        _sample_proj_kernel,
        grid=(1,),
        in_specs=[_resident((n, 1, D_MODEL)), _resident((n, 2, CONV_DIM))] + _mixer_weight_specs()[:8]
                 + [_resident((n, 1, PLE_DIM))],
        out_specs=[whole(s) for s in out_shapes],
        out_shape=[jax.ShapeDtypeStruct(s, F32) for s in out_shapes],
        compiler_params=pltpu.CompilerParams(
            dimension_semantics=("arbitrary",), vmem_limit_bytes=VMEM_LIMIT_BYTES),
        name="sample_proj",
    )(x, conv_buf, *proj_weights, p)


def _ffn_and_sample_state(h, p, ffn_weights, state, sample_proj, x_s, p_s, out_weights):
    rows = h.shape[0]
    n = x_s.shape[0]
    n_tiles = rows // FFN_TILE
    assert n_tiles * SAMPLE_BLOCK == n, "one sample state block per prompt row tile"
    q, k, a, v, y_a, g, gate_a, gate_b = sample_proj
    tile = lambda i: jnp.minimum(i, n_tiles - 1)
    state_spec = pl.BlockSpec((SAMPLE_BLOCK, GLA_HEADS, GLA_DK, GLA_DV), lambda i: (tile(i), 0, 0, 0))
    return pl.pallas_call(
        _ffn_kernel,
        grid=(n_tiles + 1,),
        in_specs=[pl.BlockSpec((FFN_TILE, D_MODEL), lambda i: (tile(i), 0)),
                  pl.BlockSpec((FFN_TILE, PLE_DIM), lambda i: (tile(i), 0)),
                  _resident((1, D_MODEL)), _resident((D_MODEL, D_FF)), _resident((D_MODEL, D_FF)),
                  _resident((D_FF, D_MODEL)), _resident((1, D_MODEL)), _resident((PLE_DIM, D_MODEL)),
                  _resident((D_MODEL, D_MODEL)), _resident((1, D_MODEL)),
                  state_spec, _resident(q.shape), _resident(k.shape), _resident(a.shape),
                  _resident(v.shape), _resident((n, D_MODEL)), _resident((n, PLE_DIM)),
                  _resident((n, D_MODEL)), _resident((n, V_DIM)), _resident((n, D_MODEL)),
                  _resident((n, D_MODEL)),
                  _resident((1, GLA_DV)), _resident((V_DIM, D_MODEL)), _resident((D_MODEL, D_MODEL)),
                  _resident((1, D_MODEL))],
        out_specs=[pl.BlockSpec((FFN_TILE, D_MODEL), lambda i: (tile(i), 0)),
                   state_spec,
                   pl.BlockSpec((n, D_MODEL), lambda i: (0, 0))],
        out_shape=[jax.ShapeDtypeStruct((rows, D_MODEL), F32),
                   jax.ShapeDtypeStruct(state.shape, F32),
                   jax.ShapeDtypeStruct((n, D_MODEL), F32)],
        scratch_shapes=[pltpu.VMEM((n // SUBLANES, SUBLANES, V_DIM), F32)],
        compiler_params=pltpu.CompilerParams(
            dimension_semantics=("arbitrary",), vmem_limit_bytes=VMEM_LIMIT_BYTES),
        name="ffn_ple",
    )(h, p, *ffn_weights, state, q, k, a, v, x_s, p_s, y_a, g, gate_a, gate_b, *out_weights)


def kernel(x_prompt, x_sample, state_conv, state_gla, p_prompt, p_sample, w_norm_mix_pre, w_in, w_conv, w_a_out, w_gk, b_gk, w_gla_norm, w_b_out, w_o, w_norm_mix_post, w_norm_ffn_pre, w_ffn_gate, w_ffn_up, w_ffn_down, w_norm_ffn_post, w_ple_proj, w_ple_gate, w_norm_ple_post):
    depth = w_in.shape[0]
    batch, seq, _ = x_prompt.shape
    n_dec = x_sample.shape[0]
    assert x_sample.shape[1] == 1, "the sample group carries one new token per sequence"
    assert seq % SEQ_TILE == 0 and (batch * seq) % FFN_TILE == 0 and n_dec % SAMPLE_BLOCK == 0

    hp = x_prompt
    hs = x_sample
    conv_p, gla_p, conv_s, gla_s = [], [], [], []
    for i in range(depth):
        row = lambda w: w[i].reshape(1, -1)
        w_head, w_tail, w_gklr, w_a, w_b, w_o_b = _cast_w_in(
            jnp.swapaxes(w_in[i], 0, 1), (w_a_out[i], w_b_out[i], w_o[i]))
        w_gk_pad = jnp.pad(w_gk[i], ((0, LANES - GATE_RANK), (0, 0))).astype(BF16)
        mixer_weights = (row(w_norm_mix_pre), w_head, w_tail, w_gklr, w_gk_pad, row(b_gk), w_conv[i],
                         w_a, row(w_gla_norm), w_b, w_o_b, row(w_norm_mix_post))

        hp_mid, cbp, sp, w_gate, w_up, w_down, w_pp, w_pg = _prompt_mixer(
            hp, mixer_weights, (w_ffn_gate[i], w_ffn_up[i], w_ffn_down[i], w_ple_proj[i], w_ple_gate[i]))
        ffn_weights = (row(w_norm_ffn_pre), w_gate, w_up, w_down, row(w_norm_ffn_post), w_pp, w_pg,
                       row(w_norm_ple_post))
        cbs, *sample_proj, xs2d, ps2d = _sample_proj(hs, state_conv[i], p_sample[i], mixer_weights[:8])
        hp, ss, hs = _ffn_and_sample_state(
            hp_mid.reshape(batch * seq, D_MODEL), p_prompt[i].reshape(batch * seq, PLE_DIM), ffn_weights,
            state_gla[i], sample_proj, xs2d, ps2d, mixer_weights[8:])
        hp = hp.reshape(batch, seq, D_MODEL)
        hs = hs.reshape(n_dec, 1, D_MODEL)

        conv_p.append(cbp[0]); gla_p.append(sp[0])
        conv_s.append(cbs); gla_s.append(ss)
    return (hp, hs, jnp.stack(conv_p), jnp.stack(gla_p),
            jnp.stack(conv_s), jnp.stack(gla_s))
```

```python
import numpy as np
import jax
import jax.numpy as jnp
from jax import lax
from jax.experimental import pallas as pl
from jax.experimental.pallas import tpu as pltpu

D_MODEL = 1024
CONV_DIM = D_MODEL
GLA_HEADS = 4
GLA_DK = 128
GLA_DV = 256
QK_DIM = GLA_HEADS * GLA_DK
V_DIM = GLA_HEADS * GLA_DV
GATE_RANK = 16
GATE_NORMALIZER = 16.0
D_FF = 2816
PLE_DIM = 256
EPS = 1e-6
LOG2_E = 1.4426950408889634

LANES = 128
SUBLANES = 8
BF16_ROWS = 16
VMEM_LIMIT_BYTES = 56 * 1024 * 1024

OFF_B, OFF_C, OFF_X = 0, 1024, 2048
OFF_Q, OFF_K, OFF_V, OFF_G = 3072, 3584, 4096, 5120
OFF_GA, OFF_GB = 6144, 7168
N_HEAD = 6144
N_TAIL = 2048
GKLR_START = 6144

GLA_CHUNK = 256
N_LEVELS = 8
SEQ_TILE = 512
FFN_TILE = 512
FFN_CHUNK = 1408
SAMPLE_BLOCK = 4
W_IN_BLOCK = 1024

F32 = jnp.float32
BF16 = jnp.bfloat16


def _rms(x, w):
    return x * lax.rsqrt(jnp.mean(x * x, axis=-1, keepdims=True) + EPS) * w


def _sigmoid(x):
    return 1.0 / (1.0 + jnp.exp2(x * -LOG2_E))


def _log2_sigmoid(x, scale):
    return (jnp.minimum(x, 0.0) - jnp.log(1.0 + jnp.exp2(jnp.abs(x) * -LOG2_E))) * (scale * LOG2_E)


def _dot(a, b):
    return jnp.dot(a, b, preferred_element_type=F32)


def _dot_nt(a, b):
    return lax.dot_general(a, b, (((1,), (1,)), ((), ())), preferred_element_type=F32)


def _column_broadcast(row):
    return jnp.broadcast_to(row, (LANES, LANES)).T


def _projections(hn, w_head_ref, w_tail_ref, w_gklr_ref, w_gk_ref, b_gk_ref):
    def proj(off, width):
        if off < N_HEAD:
            return _dot(hn, w_head_ref[:, off:off + width])
        return _dot(hn, w_tail_ref[:, off - N_HEAD:off - N_HEAD + width])

    gk_lr = _dot(hn, w_gklr_ref[...])
    gk = _dot(gk_lr.astype(BF16), w_gk_ref[...]) + b_gk_ref[...]
    return proj, _log2_sigmoid(gk, 1.0 / GATE_NORMALIZER)


def _mix_out(x, o_heads, g, gate_a, gate_b, y_a, w_gn, w_b_ref, w_o_ref, wn_post):
    normed = []
    for h in range(GLA_HEADS):
        o = o_heads[h]
        gh = g[:, h * GLA_DV:(h + 1) * GLA_DV]
        o = o * lax.rsqrt(jnp.mean(o * o, axis=-1, keepdims=True) + EPS) * w_gn
        normed.append((o * (gh * _sigmoid(gh))).astype(BF16))
    y_b = _dot(jnp.concatenate(normed, axis=1), w_b_ref[...])
    merged = _sigmoid(gate_a) * y_a + _sigmoid(gate_b) * y_b
    mix = _dot(merged.astype(BF16), w_o_ref[...])
    return x + _rms(mix, wn_post)


def _prompt_mixer_kernel(x_ref, lev_ref, tri_ref, wn_pre_ref, w_head_ref, w_tail_ref, w_gklr_ref,
                         w_gk_ref, b_gk_ref, w_conv_ref, w_a_ref, w_gn_ref, w_b_ref, w_o_ref,
                         wn_post_ref, *rest):
    n_side = (len(rest) - 4) // 2
    side_src = rest[:n_side]
    h_ref, conv_ref, state_ref = rest[n_side:n_side + 3]
    side_dst = rest[n_side + 3:2 * n_side + 3]
    dec_ref = rest[-1]

    @pl.when(pl.program_id(1) == 0)
    def _():
        conv_ref[...] = jnp.zeros_like(conv_ref)
        state_ref[...] = jnp.zeros_like(state_ref)

    for src, dst in zip(side_src, side_dst):
        dst[...] = src[...].astype(BF16)

    t = SEQ_TILE
    x = x_ref[0]
    hn = _rms(x, wn_pre_ref[...]).astype(BF16)
    proj, logw2 = _projections(hn, w_head_ref, w_tail_ref, w_gklr_ref, w_gk_ref, b_gk_ref)

    u = proj(OFF_C, CONV_DIM) * proj(OFF_X, CONV_DIM)
    prev2 = conv_ref[0, 0, 0:1, :]
    prev1 = conv_ref[0, 0, 1:2, :]
    row = lax.broadcasted_iota(jnp.int32, (t, CONV_DIM), 0)
    u1 = jnp.where(row == 0, prev1, pltpu.roll(u, 1, 0))
    u2 = jnp.where(row == 0, prev2, jnp.where(row == 1, prev1, pltpu.roll(u, 2, 0)))
    wc = w_conv_ref[...]
    y_conv = wc[0:1] * u2 + wc[1:2] * u1 + wc[2:3] * u
    conv_ref[0, 0] = u[t - 2:t]
    y_a = _dot((proj(OFF_B, CONV_DIM) * y_conv).astype(BF16), w_a_ref[...])

    q = proj(OFF_Q, QK_DIM) * (GLA_DK ** -0.5)
    k = proj(OFF_K, QK_DIM)
    v = proj(OFF_V, V_DIM).astype(BF16)

    o_chunks = []
    for c in range(SEQ_TILE // GLA_CHUNK):
        rows = slice(c * GLA_CHUNK, (c + 1) * GLA_CHUNK)
        o_chunks.append(_gla_chunk(q[rows], k[rows], v[rows], logw2[rows], dec_ref.at[c], lev_ref,
                                   tri_ref, state_ref))
    o_heads = [jnp.concatenate([o[h] for o in o_chunks], axis=0) for h in range(GLA_HEADS)]

    h_ref[0] = _mix_out(x, o_heads, proj(OFF_G, V_DIM), proj(OFF_GA, D_MODEL), proj(OFF_GB, D_MODEL),
                        y_a, w_gn_ref[...], w_b_ref, w_o_ref, wn_post_ref[...])


def _gla_chunk(q, k, v, logw2, dec_ref, lev_ref, tri_ref, state_ref):
    t = GLA_CHUNK
    dec_ref[0] = logw2
    logw2 = dec_ref[0]
    hi = logw2.astype(BF16)
    lo = (logw2 - hi.astype(F32)).astype(BF16)
    cum = _dot(tri_ref[...], jnp.concatenate([hi, lo], axis=0))
    dec_ref[1] = cum
    cum_last = cum[t - 1:t]

    half = t // 2
    lev = lev_ref[...]
    rowq = lax.broadcasted_iota(jnp.int32, (t, QK_DIM), 0)

    def level_log2_factor(i):
        m = 1 << i
        pieces = []
        for blk in range(t // (2 * m)):
            r = blk * 2 * m + m - 1
            pieces.append(jnp.broadcast_to(dec_ref[1, r:r + 1, :], (2 * m, QK_DIM)))
        return -jnp.abs(cum - jnp.concatenate(pieces, axis=0))

    def head(a, h):
        return a[:, h * GLA_DK:(h + 1) * GLA_DK]

    def split_level(i):
        m = 1 << i
        qs, ks = [], []
        for blk in range(t // (2 * m)):
            first = slice(blk * 2 * m, blk * 2 * m + m)
            second = slice(blk * 2 * m + m, (blk + 1) * 2 * m)
            ref_row = dec_ref[1, blk * 2 * m + m - 1:blk * 2 * m + m, :]
            ks += [(k[first] * jnp.exp2(ref_row - cum[first])).astype(BF16), k_lv[0][second]]
            qs += [q_lv[0][first], (q[second] * jnp.exp2(cum[second] - ref_row)).astype(BF16)]
        return jnp.concatenate(qs, axis=0), jnp.concatenate(ks, axis=0)

    q_lv = [q.astype(BF16)]
    k_lv = [k.astype(BF16)]
    step_decay = jnp.exp2(logw2)
    r4 = rowq & 3
    q0 = q * step_decay
    q_lv += [q0.astype(BF16),
             (q0 * jnp.where(r4 == 3, pltpu.roll(step_decay, 1, 0), 1.0)).astype(BF16)]
    k_lv += [k_lv[0], (k * jnp.where(r4 == 0, pltpu.roll(step_decay, t - 1, 0), 1.0)).astype(BF16)]
    for i in range(2, N_LEVELS - 1):
        if (1 << i) % BF16_ROWS == 0:
            q_i, k_i = split_level(i)
        else:
            e = jnp.exp2(level_log2_factor(i))
            q_i, k_i = (q * e).astype(BF16), (k * e).astype(BF16)
        q_lv.append(q_i)
        k_lv.append(k_i)
    cum_mid = dec_ref[1, half - 1:half, :]
    q_top = (q[half:] * jnp.exp2(cum[half:] - cum_mid)).astype(BF16)
    k_top = (k[:half] * jnp.exp2(cum_mid - cum[:half])).astype(BF16)

    q_in = (q * jnp.exp2(cum)).astype(BF16)
    k_out = k * jnp.exp2(cum_last - cum)
    a_last = jnp.exp2(cum_last)

    no_q = jnp.zeros((t, GLA_DK), BF16)
    diags = []
    for h0 in range(0, GLA_HEADS, 2):
        pair = [[0.0, 0.0], [0.0, 0.0]]
        for i in range(N_LEVELS):
            q_pair = jnp.concatenate(
                [jnp.concatenate([head(q_lv[i], h0), no_q], axis=1),
                 jnp.concatenate([no_q, head(q_lv[i], h0 + 1)], axis=1)], axis=0)
            s_pair = _dot_nt(q_pair, k_lv[i][:, h0 * GLA_DK:(h0 + 2) * GLA_DK])
            for d in range(2):
                pair[d] = [jnp.where(lev == i - 1,
                                     s_pair[d * t + r0:d * t + r0 + half, r0:r0 + half], pair[d][j])
                           for j, r0 in enumerate((0, half))]
        diags += pair

    o_heads = []
    for h in range(GLA_HEADS):
        vh = v[:, h * GLA_DV:(h + 1) * GLA_DV]
        diag = [d.astype(BF16) for d in diags[h]]
        p_top = jnp.concatenate([diag[0], jnp.zeros((half, half), BF16)], axis=1)
        p_low = jnp.concatenate([_dot_nt(head(q_top, h), head(k_top, h)).astype(BF16), diag[1]], axis=1)
        s_old = state_ref[0, 0, h]
        lhs = jnp.concatenate([jnp.concatenate([p_top, p_low], axis=0), head(q_in, h)], axis=1)
        o_heads.append(_dot(lhs, jnp.concatenate([vh, s_old.astype(BF16)], axis=0)))
        a_col = _column_broadcast(head(a_last, h))
        a_col = jnp.concatenate([a_col, a_col], axis=1)
        state_ref[0, 0, h] = a_col * s_old + _dot(head(k_out, h).T.astype(BF16), vh)
    return o_heads


def _sample_proj_kernel(x_ref, cbuf_ref, wn_pre_ref, w_head_ref, w_tail_ref, w_gklr_ref, w_gk_ref,
                        b_gk_ref, w_conv_ref, w_a_ref,
                        ps_ref, conv_ref, q_ref, k_ref, a_ref, v_ref, ya_ref, g_ref, ga_ref, gb_ref,
                        xs2_ref, ps2_ref):
    x = x_ref[:, 0, :]
    xs2_ref[...] = x
    ps2_ref[...] = ps_ref[:, 0, :]
    hn = _rms(x, wn_pre_ref[...]).astype(BF16)
    proj, logw2 = _projections(hn, w_head_ref, w_tail_ref, w_gklr_ref, w_gk_ref, b_gk_ref)
    u = proj(OFF_C, CONV_DIM) * proj(OFF_X, CONV_DIM)
    buf0 = cbuf_ref[:, 0, :]
    buf1 = cbuf_ref[:, 1, :]
    wc = w_conv_ref[...]
    y_conv = wc[0:1] * buf0 + wc[1:2] * buf1 + wc[2:3] * u
    conv_ref[:, 0, :] = buf1
    conv_ref[:, 1, :] = u
    ya_ref[...] = _dot((proj(OFF_B, CONV_DIM) * y_conv).astype(BF16), w_a_ref[...])
    grouped = lambda a: a.reshape(a.shape[0] // SUBLANES, SUBLANES, a.shape[1])
    q_ref[...] = grouped(proj(OFF_Q, QK_DIM) * (GLA_DK ** -0.5))
    k_ref[...] = grouped(proj(OFF_K, QK_DIM))
    a_ref[...] = grouped(jnp.exp2(logw2))
    v_ref[...] = grouped(proj(OFF_V, V_DIM))
    g_ref[...] = proj(OFF_G, V_DIM)
    ga_ref[...] = proj(OFF_GA, D_MODEL)
    gb_ref[...] = proj(OFF_GB, D_MODEL)


def _sample_state_update(blk, st_ref, st_out_ref, q_ref, k_ref, a_ref, v_ref, o_scr):
    per_group = SUBLANES // SAMPLE_BLOCK
    group = blk // per_group
    first = (blk % per_group) * SAMPLE_BLOCK
    row_id = lax.broadcasted_iota(jnp.int32, (SUBLANES, V_DIM), 0)
    o_tile = o_scr[group]
    for n in range(SAMPLE_BLOCK):
        r = first + n
        q_row = q_ref[group, pl.ds(r, 1), :]
        k_row = k_ref[group, pl.ds(r, 1), :]
        a_row = a_ref[group, pl.ds(r, 1), :]
        v_row = v_ref[group, pl.ds(r, 1), :]
        o_parts = []
        for h in range(GLA_HEADS):
            sl = slice(h * GLA_DK, (h + 1) * GLA_DK)
            a_col = _column_broadcast(a_row[:, sl])
            k_col = _column_broadcast(k_row[:, sl])
            q_col = _column_broadcast(q_row[:, sl])
            for half in range(GLA_DV // LANES):
                c0 = half * LANES
                s_old = st_ref[n, h, :, c0:c0 + LANES]
                vv = v_row[:, h * GLA_DV + c0:h * GLA_DV + c0 + LANES]
                s_new = a_col * s_old + k_col * vv
                st_out_ref[n, h, :, c0:c0 + LANES] = s_new
                o_parts.append(jnp.sum(q_col * s_new, axis=0, keepdims=True))
        o_row = jnp.concatenate(o_parts, axis=1)
        o_tile = jnp.where(row_id == r, o_row, o_tile)
    o_scr[group] = o_tile


def _ffn_rows(h, p, wn_pre_ref, w_gate_ref, w_up_ref, w_down_ref, wn_post_ref, w_pp_ref, w_pg_ref,
              wn_ple_ref):
    f = _rms(h, wn_pre_ref[...]).astype(BF16)
    acc = None
    for c in range(D_FF // FFN_CHUNK):
        sl = slice(c * FFN_CHUNK, (c + 1) * FFN_CHUNK)
        gate = _dot(f, w_gate_ref[:, sl])
        up = _dot(f, w_up_ref[:, sl])
        act = (gate * _sigmoid(gate) * up).astype(BF16)
        part = _dot(act, w_down_ref[sl, :])
        acc = part if acc is None else acc + part
    h = h + _rms(acc, wn_post_ref[...])
    e = _dot(p.astype(BF16), w_pp_ref[...]) * _sigmoid(_dot(h.astype(BF16), w_pg_ref[...]))
    return h + _rms(e, wn_ple_ref[...])


def _ffn_kernel(h_ref, p_ref, wn_pre_ref, w_gate_ref, w_up_ref, w_down_ref, wn_post_ref,
                w_pp_ref, w_pg_ref, wn_ple_ref,
                st_ref, q_ref, k_ref, a_ref, v_ref, xs_ref, ps_ref, ya_ref, g_ref, ga_ref, gb_ref,
                w_gn_ref, w_b_ref, w_o_ref, wn_mix_post_ref,
                out_ref, st_out_ref, outs_ref, o_scr):
    step = pl.program_id(0)
    n_tiles = pl.num_programs(0) - 1
    ffn_refs = (wn_pre_ref, w_gate_ref, w_up_ref, w_down_ref, wn_post_ref, w_pp_ref, w_pg_ref, wn_ple_ref)

    @pl.when(step == 0)
    def _():
        o_scr[...] = jnp.zeros_like(o_scr)

    @pl.when(step < n_tiles)
    def _():
        out_ref[...] = _ffn_rows(h_ref[...], p_ref[...], *ffn_refs)
        _sample_state_update(step, st_ref, st_out_ref, q_ref, k_ref, a_ref, v_ref, o_scr)

    @pl.when(step == n_tiles)
    def _():
        o = o_scr[...].reshape(xs_ref.shape[0], V_DIM)
        o_heads = [o[:, h * GLA_DV:(h + 1) * GLA_DV] for h in range(GLA_HEADS)]
        hs = _mix_out(xs_ref[...], o_heads, g_ref[...], ga_ref[...], gb_ref[...], ya_ref[...],
                      w_gn_ref[...], w_b_ref, w_o_ref, wn_mix_post_ref[...])
        outs_ref[...] = _ffn_rows(hs, ps_ref[...], *ffn_refs)


def _cast_w_in_kernel(wt_ref, gk_ref, *refs):
    n = (len(refs) - 3) // 2
    head_ref, tail_ref, gklr_ref = refs[n:n + 3]
    for src, dst in zip(refs[:n], refs[n + 3:]):
        dst[...] = src[...].astype(BF16)
    j = pl.program_id(0)
    blk = wt_ref[...].T.astype(BF16)

    @pl.when(j < N_HEAD // W_IN_BLOCK)
    def _():
        head_ref[...] = blk

    @pl.when(j >= N_HEAD // W_IN_BLOCK)
    def _():
        tail_ref[...] = blk

    @pl.when(j == 0)
    def _():
        rows = jnp.concatenate([gk_ref[...], jnp.zeros((LANES - GATE_RANK, D_MODEL), F32)], axis=0)
        gklr_ref[...] = rows.T.astype(BF16)


def _cast_w_in(w_in_t, others):
    n_head = N_HEAD // W_IN_BLOCK
    n_tail = N_TAIL // W_IN_BLOCK
    other_specs = [pl.BlockSpec((w.shape[0] // (n_head + n_tail), w.shape[1]), lambda j: (j, 0))
                   for w in others]

    def src_row(j):
        row = jnp.where(j < n_head, j * W_IN_BLOCK, GKLR_START + GATE_RANK + (j - n_head) * W_IN_BLOCK)
        return pl.multiple_of(row, GATE_RANK)

    return pl.pallas_call(
        _cast_w_in_kernel,
        grid=(n_head + n_tail,),
        in_specs=[pl.BlockSpec((pl.Element(W_IN_BLOCK), pl.Element(D_MODEL)), lambda j: (src_row(j), 0)),
                  pl.BlockSpec((pl.Element(GATE_RANK), pl.Element(D_MODEL)), lambda j: (GKLR_START, 0))]
                 + other_specs,
        out_specs=[pl.BlockSpec((D_MODEL, W_IN_BLOCK), lambda j: (0, jnp.minimum(j, n_head - 1))),
                   pl.BlockSpec((D_MODEL, W_IN_BLOCK), lambda j: (0, jnp.maximum(j - n_head, 0))),
                   pl.BlockSpec((D_MODEL, LANES), lambda j: (0, 0))] + other_specs,
        out_shape=[jax.ShapeDtypeStruct((D_MODEL, N_HEAD), BF16),
                   jax.ShapeDtypeStruct((D_MODEL, N_TAIL), BF16),
                   jax.ShapeDtypeStruct((D_MODEL, LANES), BF16)]
                  + [jax.ShapeDtypeStruct(w.shape, BF16) for w in others],
        compiler_params=pltpu.CompilerParams(
            dimension_semantics=("arbitrary",), vmem_limit_bytes=VMEM_LIMIT_BYTES),
        name="cast_weights",
    )(w_in_t, w_in_t, *others)


def _resident(shape):
    return pl.BlockSpec(shape, lambda *_: (0,) * len(shape), pipeline_mode=pl.Buffered(1))


def _level_map(t):
    idx = np.arange(t)
    xor = idx[:, None] ^ idx[None, :]
    lev = np.floor(np.log2(np.maximum(xor, 1))).astype(np.int32)
    lev = np.where(idx[:, None] > idx[None, :], lev, -2)
    lev = np.where(idx[:, None] == idx[None, :], -1, lev)
    return jnp.asarray(lev, dtype=jnp.int32)


def _mixer_weight_specs():
    return [
        _resident((1, D_MODEL)),
        _resident((D_MODEL, N_HEAD)),
        _resident((D_MODEL, N_TAIL)),
        _resident((D_MODEL, LANES)),
        _resident((LANES, QK_DIM)),
        _resident((1, QK_DIM)),
        _resident((3, CONV_DIM)),
        _resident((CONV_DIM, D_MODEL)),
        _resident((1, GLA_DV)),
        _resident((V_DIM, D_MODEL)),
        _resident((D_MODEL, D_MODEL)),
        _resident((1, D_MODEL)),
    ]


def _prompt_mixer(x, mixer_weights, later_weights):
    b, s, _ = x.shape
    t = SEQ_TILE
    c = GLA_CHUNK
    n_steps = b * (s // t)
    tri = jnp.asarray(np.tile(np.tril(np.ones((c, c), np.float32)), (1, 2)), dtype=BF16)

    def side_spec(shape):
        n_blocks = n_steps
        while shape[0] % (n_blocks * BF16_ROWS):
            n_blocks //= 2
        rep = n_steps // n_blocks
        return pl.BlockSpec((shape[0] // n_blocks, shape[1]),
                            lambda i, j: ((i * (s // t) + j) // rep, 0))

    side_specs = [side_spec(w.shape) for w in later_weights]
    return pl.pallas_call(
        _prompt_mixer_kernel,
        grid=(b, s // t),
        in_specs=[pl.BlockSpec((1, t, D_MODEL), lambda i, j: (i, j, 0)),
                  _resident((c // 2, c // 2)), _resident((c, 2 * c))] + _mixer_weight_specs() + side_specs,
        out_specs=[pl.BlockSpec((1, t, D_MODEL), lambda i, j: (i, j, 0)),
                   pl.BlockSpec((1, 1, 2, CONV_DIM), lambda i, j: (0, i, 0, 0)),
                   pl.BlockSpec((1, 1, GLA_HEADS, GLA_DK, GLA_DV), lambda i, j: (0, i, 0, 0, 0))]
                  + side_specs,
        out_shape=[jax.ShapeDtypeStruct((b, s, D_MODEL), F32),
                   jax.ShapeDtypeStruct((1, b, 2, CONV_DIM), F32),
                   jax.ShapeDtypeStruct((1, b, GLA_HEADS, GLA_DK, GLA_DV), F32)]
                  + [jax.ShapeDtypeStruct(w.shape, BF16) for w in later_weights],
        scratch_shapes=[pltpu.VMEM((t // c, 2, c, QK_DIM), F32)],
        compiler_params=pltpu.CompilerParams(
            dimension_semantics=("arbitrary", "arbitrary"), vmem_limit_bytes=VMEM_LIMIT_BYTES),
        name="prompt_mixer",
    )(x, _level_map(c // 2), tri, *mixer_weights, *later_weights)


def _sample_proj(x, conv_buf, p, proj_weights):
    n = x.shape[0]
    grouped = lambda w: (n // SUBLANES, SUBLANES, w)
    out_shapes = [(n, 2, CONV_DIM), grouped(QK_DIM), grouped(QK_DIM), grouped(QK_DIM), grouped(V_DIM),
                  (n, D_MODEL), (n, V_DIM), (n, D_MODEL), (n, D_MODEL), (n, D_MODEL), (n, PLE_DIM)]
    whole = lambda shape: pl.BlockSpec(shape, lambda i: (0,) * len(shape))
    return pl.pallas_call(
        _sample_proj_kernel,
        grid=(1,),
        in_specs=[_resident((n, 1, D_MODEL)), _resident((n, 2, CONV_DIM))] + _mixer_weight_specs()[:8]
                 + [_resident((n, 1, PLE_DIM))],
        out_specs=[whole(s) for s in out_shapes],
        out_shape=[jax.ShapeDtypeStruct(s, F32) for s in out_shapes],
        compiler_params=pltpu.CompilerParams(
            dimension_semantics=("arbitrary",), vmem_limit_bytes=VMEM_LIMIT_BYTES),
        name="sample_proj",
    )(x, conv_buf, *proj_weights, p)


def _ffn_and_sample_state(h, p, ffn_weights, state, sample_proj, x_s, p_s, out_weights):
    rows = h.shape[0]
    n = x_s.shape[0]
    n_tiles = rows // FFN_TILE
    assert n_tiles * SAMPLE_BLOCK == n, "one sample state block per prompt row tile"
    q, k, a, v, y_a, g, gate_a, gate_b = sample_proj
    tile = lambda i: jnp.minimum(i, n_tiles - 1)
    state_spec = pl.BlockSpec((SAMPLE_BLOCK, GLA_HEADS, GLA_DK, GLA_DV), lambda i: (tile(i), 0, 0, 0))
    return pl.pallas_call(
        _ffn_kernel,
        grid=(n_tiles + 1,),
        in_specs=[pl.BlockSpec((FFN_TILE, D_MODEL), lambda i: (tile(i), 0)),
                  pl.BlockSpec((FFN_TILE, PLE_DIM), lambda i: (tile(i), 0)),
                  _resident((1, D_MODEL)), _resident((D_MODEL, D_FF)), _resident((D_MODEL, D_FF)),
                  _resident((D_FF, D_MODEL)), _resident((1, D_MODEL)), _resident((PLE_DIM, D_MODEL)),
                  _resident((D_MODEL, D_MODEL)), _resident((1, D_MODEL)),
                  state_spec, _resident(q.shape), _resident(k.shape), _resident(a.shape),
                  _resident(v.shape), _resident((n, D_MODEL)), _resident((n, PLE_DIM)),
                  _resident((n, D_MODEL)), _resident((n, V_DIM)), _resident((n, D_MODEL)),
                  _resident((n, D_MODEL)),
                  _resident((1, GLA_DV)), _resident((V_DIM, D_MODEL)), _resident((D_MODEL, D_MODEL)),
                  _resident((1, D_MODEL))],
        out_specs=[pl.BlockSpec((FFN_TILE, D_MODEL), lambda i: (tile(i), 0)),
                   state_spec,
                   pl.BlockSpec((n, D_MODEL), lambda i: (0, 0))],
        out_shape=[jax.ShapeDtypeStruct((rows, D_MODEL), F32),
                   jax.ShapeDtypeStruct(state.shape, F32),
                   jax.ShapeDtypeStruct((n, D_MODEL), F32)],
        scratch_shapes=[pltpu.VMEM((n // SUBLANES, SUBLANES, V_DIM), F32)],
        compiler_params=pltpu.CompilerParams(
            dimension_semantics=("arbitrary",), vmem_limit_bytes=VMEM_LIMIT_BYTES),
        name="ffn_ple",
    )(h, p, *ffn_weights, state, q, k, a, v, x_s, p_s, y_a, g, gate_a, gate_b, *out_weights)


def kernel(x_prompt, x_sample, state_conv, state_gla, p_prompt, p_sample, w_norm_mix_pre, w_in, w_conv, w_a_out, w_gk, b_gk, w_gla_norm, w_b_out, w_o, w_norm_mix_post, w_norm_ffn_pre, w_ffn_gate, w_ffn_up, w_ffn_down, w_norm_ffn_post, w_ple_proj, w_ple_gate, w_norm_ple_post):
    depth = w_in.shape[0]
    batch, seq, _ = x_prompt.shape
    n_dec = x_sample.shape[0]
    assert x_sample.shape[1] == 1, "the sample group carries one new token per sequence"
    assert seq % SEQ_TILE == 0 and (batch * seq) % FFN_TILE == 0 and n_dec % SAMPLE_BLOCK == 0

    hp = x_prompt
    hs = x_sample
    conv_p, gla_p, conv_s, gla_s = [], [], [], []
    for i in range(depth):
        row = lambda w: w[i].reshape(1, -1)
        w_head, w_tail, w_gklr, w_a, w_b, w_o_b = _cast_w_in(
            jnp.swapaxes(w_in[i], 0, 1), (w_a_out[i], w_b_out[i], w_o[i]))
        w_gk_pad = jnp.pad(w_gk[i], ((0, LANES - GATE_RANK), (0, 0))).astype(BF16)
        mixer_weights = (row(w_norm_mix_pre), w_head, w_tail, w_gklr, w_gk_pad, row(b_gk), w_conv[i],
                         w_a, row(w_gla_norm), w_b, w_o_b, row(w_norm_mix_post))

        hp_mid, cbp, sp, w_gate, w_up, w_down, w_pp, w_pg = _prompt_mixer(
            hp, mixer_weights, (w_ffn_gate[i], w_ffn_up[i], w_ffn_down[i], w_ple_proj[i], w_ple_gate[i]))
        ffn_weights = (row(w_norm_ffn_pre), w_gate, w_up, w_down, row(w_norm_ffn_post), w_pp, w_pg,
                       row(w_norm_ple_post))
        cbs, *sample_proj, xs2d, ps2d = _sample_proj(hs, state_conv[i], p_sample[i], mixer_weights[:8])
        hp, ss, hs = _ffn_and_sample_state(
            hp_mid.reshape(batch * seq, D_MODEL), p_prompt[i].reshape(batch * seq, PLE_DIM), ffn_weights,
            state_gla[i], sample_proj, xs2d, ps2d, mixer_weights[8:])
        hp = hp.reshape(batch, seq, D_MODEL)
        hs = hs.reshape(n_dec, 1, D_MODEL)

        conv_p.append(cbp[0]); gla_p.append(sp[0])
        conv_s.append(cbs); gla_s.append(ss)
    return (hp, hs, jnp.stack(conv_p), jnp.stack(gla_p),
            jnp.stack(conv_s), jnp.stack(gla_s))
```

```python
import numpy as np
import jax
import jax.numpy as jnp
from jax import lax
from jax.experimental import pallas as pl
from jax.experimental.pallas import tpu as pltpu

D_MODEL = 1024
CONV_DIM = D_MODEL
GLA_HEADS = 4
GLA_DK = 128
GLA_DV = 256
QK_DIM = GLA_HEADS * GLA_DK
V_DIM = GLA_HEADS * GLA_DV
GATE_RANK = 16
GATE_NORMALIZER = 16.0
D_FF = 2816
PLE_DIM = 256
EPS = 1e-6
LOG2_E = 1.4426950408889634

LANES = 128
SUBLANES = 8
BF16_ROWS = 16
VMEM_LIMIT_BYTES = 56 * 1024 * 1024

OFF_B, OFF_C, OFF_X = 0, 1024, 2048
OFF_Q, OFF_K, OFF_V, OFF_G = 3072, 3584, 4096, 5120
OFF_GA, OFF_GB = 6144, 7168
N_HEAD = 6144
N_TAIL = 2048
GKLR_START = 6144

GLA_CHUNK = 256
N_LEVELS = 8
SEQ_TILE = 512
FFN_TILE = 512
FFN_CHUNK = 1408
SAMPLE_BLOCK = 4
W_IN_BLOCK = 1024

F32 = jnp.float32
BF16 = jnp.bfloat16


def _rms(x, w):
    return x * lax.rsqrt(jnp.mean(x * x, axis=-1, keepdims=True) + EPS) * w


def _sigmoid(x):
    return 1.0 / (1.0 + jnp.exp2(x * -LOG2_E))


def _log2_sigmoid(x, scale):
    return (jnp.minimum(x, 0.0) - jnp.log(1.0 + jnp.exp2(jnp.abs(x) * -LOG2_E))) * (scale * LOG2_E)


def _dot(a, b):
    return jnp.dot(a, b, preferred_element_type=F32)


def _dot_nt(a, b):
    return lax.dot_general(a, b, (((1,), (1,)), ((), ())), preferred_element_type=F32)


def _column_broadcast(row):
    return jnp.broadcast_to(row, (LANES, LANES)).T


def _projections(hn, w_head_ref, w_tail_ref, w_gklr_ref, w_gk_ref, b_gk_ref):
    def proj(off, width):
        if off < N_HEAD:
            return _dot(hn, w_head_ref[:, off:off + width])
        return _dot(hn, w_tail_ref[:, off - N_HEAD:off - N_HEAD + width])

    gk_lr = _dot(hn, w_gklr_ref[...])
    gk = _dot(gk_lr.astype(BF16), w_gk_ref[...]) + b_gk_ref[...]
    return proj, _log2_sigmoid(gk, 1.0 / GATE_NORMALIZER)


def _mix_out(x, o_heads, g, gate_a, gate_b, y_a, w_gn, w_b_ref, w_o_ref, wn_post):
    normed = []
    for h in range(GLA_HEADS):
        o = o_heads[h]
        gh = g[:, h * GLA_DV:(h + 1) * GLA_DV]
        o = o * lax.rsqrt(jnp.mean(o * o, axis=-1, keepdims=True) + EPS) * w_gn
        normed.append((o * (gh * _sigmoid(gh))).astype(BF16))
    y_b = _dot(jnp.concatenate(normed, axis=1), w_b_ref[...])
    merged = _sigmoid(gate_a) * y_a + _sigmoid(gate_b) * y_b
    mix = _dot(merged.astype(BF16), w_o_ref[...])
    return x + _rms(mix, wn_post)


def _prompt_mixer_kernel(x_ref, lev_ref, tri_ref, wn_pre_ref, w_head_ref, w_tail_ref, w_gklr_ref,
                         w_gk_ref, b_gk_ref, w_conv_ref, w_a_ref, w_gn_ref, w_b_ref, w_o_ref,
                         wn_post_ref, *rest):
    n_side = (len(rest) - 4) // 2
    side_src = rest[:n_side]
    h_ref, conv_ref, state_ref = rest[n_side:n_side + 3]
    side_dst = rest[n_side + 3:2 * n_side + 3]
    dec_ref = rest[-1]

    @pl.when(pl.program_id(1) == 0)
    def _():
        conv_ref[...] = jnp.zeros_like(conv_ref)
        state_ref[...] = jnp.zeros_like(state_ref)

    for src, dst in zip(side_src, side_dst):
        dst[...] = src[...].astype(BF16)

    t = SEQ_TILE
    x = x_ref[0]
    hn = _rms(x, wn_pre_ref[...]).astype(BF16)
    proj, logw2 = _projections(hn, w_head_ref, w_tail_ref, w_gklr_ref, w_gk_ref, b_gk_ref)

    u = proj(OFF_C, CONV_DIM) * proj(OFF_X, CONV_DIM)
    prev2 = conv_ref[0, 0, 0:1, :]
    prev1 = conv_ref[0, 0, 1:2, :]
    row = lax.broadcasted_iota(jnp.int32, (t, CONV_DIM), 0)
    u1 = jnp.where(row == 0, prev1, pltpu.roll(u, 1, 0))
    u2 = jnp.where(row == 0, prev2, jnp.where(row == 1, prev1, pltpu.roll(u, 2, 0)))
    wc = w_conv_ref[...]
    y_conv = wc[0:1] * u2 + wc[1:2] * u1 + wc[2:3] * u
    conv_ref[0, 0] = u[t - 2:t]
    y_a = _dot((proj(OFF_B, CONV_DIM) * y_conv).astype(BF16), w_a_ref[...])

    q = proj(OFF_Q, QK_DIM) * (GLA_DK ** -0.5)
    k = proj(OFF_K, QK_DIM)
    v = proj(OFF_V, V_DIM).astype(BF16)

    o_chunks = []
    for c in range(SEQ_TILE // GLA_CHUNK):
        rows = slice(c * GLA_CHUNK, (c + 1) * GLA_CHUNK)
        o_chunks.append(_gla_chunk(q[rows], k[rows], v[rows], logw2[rows], dec_ref.at[c], lev_ref,
                                   tri_ref, state_ref))
    o_heads = [jnp.concatenate([o[h] for o in o_chunks], axis=0) for h in range(GLA_HEADS)]

    h_ref[0] = _mix_out(x, o_heads, proj(OFF_G, V_DIM), proj(OFF_GA, D_MODEL), proj(OFF_GB, D_MODEL),
                        y_a, w_gn_ref[...], w_b_ref, w_o_ref, wn_post_ref[...])


def _gla_chunk(q, k, v, logw2, dec_ref, lev_ref, tri_ref, state_ref):
    t = GLA_CHUNK
    dec_ref[0] = logw2
    logw2 = dec_ref[0]
    hi = logw2.astype(BF16)
    lo = (logw2 - hi.astype(F32)).astype(BF16)
    cum = _dot(tri_ref[...], jnp.concatenate([hi, lo], axis=0))
    dec_ref[1] = cum
    cum_last = cum[t - 1:t]

    half = t // 2
    lev = lev_ref[...]
    rowq = lax.broadcasted_iota(jnp.int32, (t, QK_DIM), 0)

    def level_log2_factor(i):
        m = 1 << i
        pieces = []
        for blk in range(t // (2 * m)):
            r = blk * 2 * m + m - 1
            pieces.append(jnp.broadcast_to(dec_ref[1, r:r + 1, :], (2 * m, QK_DIM)))
        return -jnp.abs(cum - jnp.concatenate(pieces, axis=0))

    def head(a, h):
        return a[:, h * GLA_DK:(h + 1) * GLA_DK]

    def split_level(i):
        m = 1 << i
        qs, ks = [], []
        for blk in range(t // (2 * m)):
            first = slice(blk * 2 * m, blk * 2 * m + m)
            second = slice(blk * 2 * m + m, (blk + 1) * 2 * m)
            ref_row = dec_ref[1, blk * 2 * m + m - 1:blk * 2 * m + m, :]
            ks += [(k[first] * jnp.exp2(ref_row - cum[first])).astype(BF16), k_lv[0][second]]
            qs += [q_lv[0][first], (q[second] * jnp.exp2(cum[second] - ref_row)).astype(BF16)]
        return jnp.concatenate(qs, axis=0), jnp.concatenate(ks, axis=0)

    q_lv = [q.astype(BF16)]
    k_lv = [k.astype(BF16)]
    step_decay = jnp.exp2(logw2)
    r4 = rowq & 3
    q0 = q * step_decay
    q_lv += [q0.astype(BF16),
             (q0 * jnp.where(r4 == 3, pltpu.roll(step_decay, 1, 0), 1.0)).astype(BF16)]
    k_lv += [k_lv[0], (k * jnp.where(r4 == 0, pltpu.roll(step_decay, t - 1, 0), 1.0)).astype(BF16)]
    for i in range(2, N_LEVELS - 1):
        if (1 << i) % BF16_ROWS == 0:
            q_i, k_i = split_level(i)
        else:
            e = jnp.exp2(level_log2_factor(i))
            q_i, k_i = (q * e).astype(BF16), (k * e).astype(BF16)
        q_lv.append(q_i)
        k_lv.append(k_i)
    cum_mid = dec_ref[1, half - 1:half, :]
    q_top = (q[half:] * jnp.exp2(cum[half:] - cum_mid)).astype(BF16)
    k_top = (k[:half] * jnp.exp2(cum_mid - cum[:half])).astype(BF16)

    q_in = (q * jnp.exp2(cum)).astype(BF16)
    k_out = k * jnp.exp2(cum_last - cum)
    a_last = jnp.exp2(cum_last)

    o_heads = []
    for h in range(GLA_HEADS):
        vh = v[:, h * GLA_DV:(h + 1) * GLA_DV]
        diag = [0.0, 0.0]
        for i in range(N_LEVELS):
            s_i = _dot_nt(head(q_lv[i], h), head(k_lv[i], h))
            diag = [jnp.where(lev == i - 1, s_i[r0:r0 + half, r0:r0 + half], diag[j])
                    for j, r0 in enumerate((0, half))]
        diag = [d.astype(BF16) for d in diag]
        p_top = jnp.concatenate([diag[0], jnp.zeros((half, half), BF16)], axis=1)
        p_low = jnp.concatenate([_dot_nt(head(q_top, h), head(k_top, h)).astype(BF16), diag[1]], axis=1)
        s_old = state_ref[0, 0, h]
        lhs = jnp.concatenate([jnp.concatenate([p_top, p_low], axis=0), head(q_in, h)], axis=1)
        o_heads.append(_dot(lhs, jnp.concatenate([vh, s_old.astype(BF16)], axis=0)))
        a_col = _column_broadcast(head(a_last, h))
        a_col = jnp.concatenate([a_col, a_col], axis=1)
        state_ref[0, 0, h] = a_col * s_old + _dot(head(k_out, h).T.astype(BF16), vh)
    return o_heads


def _sample_proj_kernel(x_ref, cbuf_ref, wn_pre_ref, w_head_ref, w_tail_ref, w_gklr_ref, w_gk_ref,
                        b_gk_ref, w_conv_ref, w_a_ref,
                        ps_ref, conv_ref, q_ref, k_ref, a_ref, v_ref, ya_ref, g_ref, ga_ref, gb_ref,
                        xs2_ref, ps2_ref):
    x = x_ref[:, 0, :]
    xs2_ref[...] = x
    ps2_ref[...] = ps_ref[:, 0, :]
    hn = _rms(x, wn_pre_ref[...]).astype(BF16)
    proj, logw2 = _projections(hn, w_head_ref, w_tail_ref, w_gklr_ref, w_gk_ref, b_gk_ref)
    u = proj(OFF_C, CONV_DIM) * proj(OFF_X, CONV_DIM)
    buf0 = cbuf_ref[:, 0, :]
    buf1 = cbuf_ref[:, 1, :]
    wc = w_conv_ref[...]
    y_conv = wc[0:1] * buf0 + wc[1:2] * buf1 + wc[2:3] * u
    conv_ref[:, 0, :] = buf1
    conv_ref[:, 1, :] = u
    ya_ref[...] = _dot((proj(OFF_B, CONV_DIM) * y_conv).astype(BF16), w_a_ref[...])
    grouped = lambda a: a.reshape(a.shape[0] // SUBLANES, SUBLANES, a.shape[1])
    q_ref[...] = grouped(proj(OFF_Q, QK_DIM) * (GLA_DK ** -0.5))
    k_ref[...] = grouped(proj(OFF_K, QK_DIM))
    a_ref[...] = grouped(jnp.exp2(logw2))
    v_ref[...] = grouped(proj(OFF_V, V_DIM))
    g_ref[...] = proj(OFF_G, V_DIM)
    ga_ref[...] = proj(OFF_GA, D_MODEL)
    gb_ref[...] = proj(OFF_GB, D_MODEL)


def _sample_state_update(blk, st_ref, st_out_ref, q_ref, k_ref, a_ref, v_ref, o_scr):
    per_group = SUBLANES // SAMPLE_BLOCK
    group = blk // per_group
    first = (blk % per_group) * SAMPLE_BLOCK
    row_id = lax.broadcasted_iota(jnp.int32, (SUBLANES, V_DIM), 0)
    o_tile = o_scr[group]
    for n in range(SAMPLE_BLOCK):
        r = first + n
        q_row = q_ref[group, pl.ds(r, 1), :]
        k_row = k_ref[group, pl.ds(r, 1), :]
        a_row = a_ref[group, pl.ds(r, 1), :]
        v_row = v_ref[group, pl.ds(r, 1), :]
        o_parts = []
        for h in range(GLA_HEADS):
            sl = slice(h * GLA_DK, (h + 1) * GLA_DK)
            a_col = _column_broadcast(a_row[:, sl])
            k_col = _column_broadcast(k_row[:, sl])
            q_col = _column_broadcast(q_row[:, sl])
            for half in range(GLA_DV // LANES):
                c0 = half * LANES
                s_old = st_ref[n, h, :, c0:c0 + LANES]
                vv = v_row[:, h * GLA_DV + c0:h * GLA_DV + c0 + LANES]
                s_new = a_col * s_old + k_col * vv
                st_out_ref[n, h, :, c0:c0 + LANES] = s_new
                o_parts.append(jnp.sum(q_col * s_new, axis=0, keepdims=True))
        o_row = jnp.concatenate(o_parts, axis=1)
        o_tile = jnp.where(row_id == r, o_row, o_tile)
    o_scr[group] = o_tile


def _ffn_rows(h, p, wn_pre_ref, w_gate_ref, w_up_ref, w_down_ref, wn_post_ref, w_pp_ref, w_pg_ref,
              wn_ple_ref):
    f = _rms(h, wn_pre_ref[...]).astype(BF16)
    acts = []
    for c in range(D_FF // FFN_CHUNK):
        sl = slice(c * FFN_CHUNK, (c + 1) * FFN_CHUNK)
        gate = _dot(f, w_gate_ref[:, sl])
        up = _dot(f, w_up_ref[:, sl])
        acts.append((gate * _sigmoid(gate) * up).astype(BF16))
    h = h + _rms(_dot(jnp.concatenate(acts, axis=1), w_down_ref[...]), wn_post_ref[...])
    e = _dot(p.astype(BF16), w_pp_ref[...]) * _sigmoid(_dot(h.astype(BF16), w_pg_ref[...]))
    return h + _rms(e, wn_ple_ref[...])


def _ffn_kernel(h_ref, p_ref, wn_pre_ref, w_gate_ref, w_up_ref, w_down_ref, wn_post_ref,
                w_pp_ref, w_pg_ref, wn_ple_ref,
                st_ref, q_ref, k_ref, a_ref, v_ref, xs_ref, ps_ref, ya_ref, g_ref, ga_ref, gb_ref,
                w_gn_ref, w_b_ref, w_o_ref, wn_mix_post_ref,
                out_ref, st_out_ref, outs_ref, o_scr):
    step = pl.program_id(0)
    n_tiles = pl.num_programs(0) - 1
    ffn_refs = (wn_pre_ref, w_gate_ref, w_up_ref, w_down_ref, wn_post_ref, w_pp_ref, w_pg_ref, wn_ple_ref)

    @pl.when(step == 0)
    def _():
        o_scr[...] = jnp.zeros_like(o_scr)

    @pl.when(step < n_tiles)
    def _():
        out_ref[...] = _ffn_rows(h_ref[...], p_ref[...], *ffn_refs)
        _sample_state_update(step, st_ref, st_out_ref, q_ref, k_ref, a_ref, v_ref, o_scr)

    @pl.when(step == n_tiles)
    def _():
        o = o_scr[...].reshape(xs_ref.shape[0], V_DIM)
        o_heads = [o[:, h * GLA_DV:(h + 1) * GLA_DV] for h in range(GLA_HEADS)]
        hs = _mix_out(xs_ref[...], o_heads, g_ref[...], ga_ref[...], gb_ref[...], ya_ref[...],
                      w_gn_ref[...], w_b_ref, w_o_ref, wn_mix_post_ref[...])
        outs_ref[...] = _ffn_rows(hs, ps_ref[...], *ffn_refs)


def _cast_w_in_kernel(wt_ref, gk_ref, *refs):
    n = (len(refs) - 3) // 2
    head_ref, tail_ref, gklr_ref = refs[n:n + 3]
    for src, dst in zip(refs[:n], refs[n + 3:]):
        dst[...] = src[...].astype(BF16)
    j = pl.program_id(0)
    blk = wt_ref[...].T.astype(BF16)

    @pl.when(j < N_HEAD // W_IN_BLOCK)
    def _():
        head_ref[...] = blk

    @pl.when(j >= N_HEAD // W_IN_BLOCK)
    def _():
        tail_ref[...] = blk

    @pl.when(j == 0)
    def _():
        rows = jnp.concatenate([gk_ref[...], jnp.zeros((LANES - GATE_RANK, D_MODEL), F32)], axis=0)
        gklr_ref[...] = rows.T.astype(BF16)


def _cast_w_in(w_in_t, others):
    n_head = N_HEAD // W_IN_BLOCK
    n_tail = N_TAIL // W_IN_BLOCK
    other_specs = [pl.BlockSpec((w.shape[0] // (n_head + n_tail), w.shape[1]), lambda j: (j, 0))
                   for w in others]

    def src_row(j):
        row = jnp.where(j < n_head, j * W_IN_BLOCK, GKLR_START + GATE_RANK + (j - n_head) * W_IN_BLOCK)
        return pl.multiple_of(row, GATE_RANK)

    return pl.pallas_call(
        _cast_w_in_kernel,
        grid=(n_head + n_tail,),
        in_specs=[pl.BlockSpec((pl.Element(W_IN_BLOCK), pl.Element(D_MODEL)), lambda j: (src_row(j), 0)),
                  pl.BlockSpec((pl.Element(GATE_RANK), pl.Element(D_MODEL)), lambda j: (GKLR_START, 0))]
                 + other_specs,
        out_specs=[pl.BlockSpec((D_MODEL, W_IN_BLOCK), lambda j: (0, jnp.minimum(j, n_head - 1))),
                   pl.BlockSpec((D_MODEL, W_IN_BLOCK), lambda j: (0, jnp.maximum(j - n_head, 0))),
                   pl.BlockSpec((D_MODEL, LANES), lambda j: (0, 0))] + other_specs,
        out_shape=[jax.ShapeDtypeStruct((D_MODEL, N_HEAD), BF16),
                   jax.ShapeDtypeStruct((D_MODEL, N_TAIL), BF16),
                   jax.ShapeDtypeStruct((D_MODEL, LANES), BF16)]
                  + [jax.ShapeDtypeStruct(w.shape, BF16) for w in others],
        compiler_params=pltpu.CompilerParams(
            dimension_semantics=("arbitrary",), vmem_limit_bytes=VMEM_LIMIT_BYTES),
        name="cast_weights",
    )(w_in_t, w_in_t, *others)


def _resident(shape):
    return pl.BlockSpec(shape, lambda *_: (0,) * len(shape), pipeline_mode=pl.Buffered(1))


def _level_map(t):
    idx = np.arange(t)
    xor = idx[:, None] ^ idx[None, :]
    lev = np.floor(np.log2(np.maximum(xor, 1))).astype(np.int32)
    lev = np.where(idx[:, None] > idx[None, :], lev, -2)
    lev = np.where(idx[:, None] == idx[None, :], -1, lev)
    return jnp.asarray(lev, dtype=jnp.int32)


def _mixer_weight_specs():
    return [
        _resident((1, D_MODEL)),
        _resident((D_MODEL, N_HEAD)),
        _resident((D_MODEL, N_TAIL)),
        _resident((D_MODEL, LANES)),
        _resident((LANES, QK_DIM)),
        _resident((1, QK_DIM)),
        _resident((3, CONV_DIM)),
        _resident((CONV_DIM, D_MODEL)),
        _resident((1, GLA_DV)),
        _resident((V_DIM, D_MODEL)),
        _resident((D_MODEL, D_MODEL)),
        _resident((1, D_MODEL)),
    ]


def _prompt_mixer(x, mixer_weights, later_weights):
    b, s, _ = x.shape
    t = SEQ_TILE
    c = GLA_CHUNK
    n_steps = b * (s // t)
    tri = jnp.asarray(np.tile(np.tril(np.ones((c, c), np.float32)), (1, 2)), dtype=BF16)

    def side_spec(shape):
        n_blocks = n_steps
        while shape[0] % (n_blocks * BF16_ROWS):
            n_blocks //= 2
        rep = n_steps // n_blocks
        return pl.BlockSpec((shape[0] // n_blocks, shape[1]),
                            lambda i, j: ((i * (s // t) + j) // rep, 0))

    side_specs = [side_spec(w.shape) for w in later_weights]
    return pl.pallas_call(
        _prompt_mixer_kernel,
        grid=(b, s // t),
        in_specs=[pl.BlockSpec((1, t, D_MODEL), lambda i, j: (i, j, 0)),
                  _resident((c // 2, c // 2)), _resident((c, 2 * c))] + _mixer_weight_specs() + side_specs,
        out_specs=[pl.BlockSpec((1, t, D_MODEL), lambda i, j: (i, j, 0)),
                   pl.BlockSpec((1, 1, 2, CONV_DIM), lambda i, j: (0, i, 0, 0)),
                   pl.BlockSpec((1, 1, GLA_HEADS, GLA_DK, GLA_DV), lambda i, j: (0, i, 0, 0, 0))]
                  + side_specs,
        out_shape=[jax.ShapeDtypeStruct((b, s, D_MODEL), F32),
                   jax.ShapeDtypeStruct((1, b, 2, CONV_DIM), F32),
                   jax.ShapeDtypeStruct((1, b, GLA_HEADS, GLA_DK, GLA_DV), F32)]
                  + [jax.ShapeDtypeStruct(w.shape, BF16) for w in later_weights],
        scratch_shapes=[pltpu.VMEM((t // c, 2, c, QK_DIM), F32)],
        compiler_params=pltpu.CompilerParams(
            dimension_semantics=("arbitrary", "arbitrary"), vmem_limit_bytes=VMEM_LIMIT_BYTES),
        name="prompt_mixer",
    )(x, _level_map(c // 2), tri, *mixer_weights, *later_weights)


def _sample_proj(x, conv_buf, p, proj_weights):
    n = x.shape[0]
    grouped = lambda w: (n // SUBLANES, SUBLANES, w)
    out_shapes = [(n, 2, CONV_DIM), grouped(QK_DIM), grouped(QK_DIM), grouped(QK_DIM), grouped(V_DIM),
                  (n, D_MODEL), (n, V_DIM), (n, D_MODEL), (n, D_MODEL), (n, D_MODEL), (n, PLE_DIM)]
    whole = lambda shape: pl.BlockSpec(shape, lambda i: (0,) * len(shape))
    return pl.pallas_call(
        _sample_proj_kernel,
        grid=(1,),
        in_specs=[_resident((n, 1, D_MODEL)), _resident((n, 2, CONV_DIM))] + _mixer_weight_specs()[:8]
                 + [_resident((n, 1, PLE_DIM))],
        out_specs=[whole(s) for s in out_shapes],
        out_shape=[jax.ShapeDtypeStruct(s, F32) for s in out_shapes],
        compiler_params=pltpu.CompilerParams(
            dimension_semantics=("arbitrary",), vmem_limit_bytes=VMEM_LIMIT_BYTES),
        name="sample_proj",
    )(x, conv_buf, *proj_weights, p)


def _ffn_and_sample_state(h, p, ffn_weights, state, sample_proj, x_s, p_s, out_weights):
    rows = h.shape[0]
    n = x_s.shape[0]
    n_tiles = rows // FFN_TILE
    assert n_tiles * SAMPLE_BLOCK == n, "one sample state block per prompt row tile"
    q, k, a, v, y_a, g, gate_a, gate_b = sample_proj
    tile = lambda i: jnp.minimum(i, n_tiles - 1)
    state_spec = pl.BlockSpec((SAMPLE_BLOCK, GLA_HEADS, GLA_DK, GLA_DV), lambda i: (tile(i), 0, 0, 0))
    return pl.pallas_call(
        _ffn_kernel,
        grid=(n_tiles + 1,),
        in_specs=[pl.BlockSpec((FFN_TILE, D_MODEL), lambda i: (tile(i), 0)),
                  pl.BlockSpec((FFN_TILE, PLE_DIM), lambda i: (tile(i), 0)),
                  _resident((1, D_MODEL)), _resident((D_MODEL, D_FF)), _resident((D_MODEL, D_FF)),
                  _resident((D_FF, D_MODEL)), _resident((1, D_MODEL)), _resident((PLE_DIM, D_MODEL)),
                  _resident((D_MODEL, D_MODEL)), _resident((1, D_MODEL)),
                  state_spec, _resident(q.shape), _resident(k.shape), _resident(a.shape),
                  _resident(v.shape), _resident((n, D_MODEL)), _resident((n, PLE_DIM)),
                  _resident((n, D_MODEL)), _resident((n, V_DIM)), _resident((n, D_MODEL)),
                  _resident((n, D_MODEL)),
                  _resident((1, GLA_DV)), _resident((V_DIM, D_MODEL)), _resident((D_MODEL, D_MODEL)),
                  _resident((1, D_MODEL))],
        out_specs=[pl.BlockSpec((FFN_TILE, D_MODEL), lambda i: (tile(i), 0)),
                   state_spec,
                   pl.BlockSpec((n, D_MODEL), lambda i: (0, 0))],
        out_shape=[jax.ShapeDtypeStruct((rows, D_MODEL), F32),
                   jax.ShapeDtypeStruct(state.shape, F32),
                   jax.ShapeDtypeStruct((n, D_MODEL), F32)],
        scratch_shapes=[pltpu.VMEM((n // SUBLANES, SUBLANES, V_DIM), F32)],
        compiler_params=pltpu.CompilerParams(
            dimension_semantics=("arbitrary",), vmem_limit_bytes=VMEM_LIMIT_BYTES),
        name="ffn_ple",
    )(h, p, *ffn_weights, state, q, k, a, v, x_s, p_s, y_a, g, gate_a, gate_b, *out_weights)


def kernel(x_prompt, x_sample, state_conv, state_gla, p_prompt, p_sample, w_norm_mix_pre, w_in, w_conv, w_a_out, w_gk, b_gk, w_gla_norm, w_b_out, w_o, w_norm_mix_post, w_norm_ffn_pre, w_ffn_gate, w_ffn_up, w_ffn_down, w_norm_ffn_post, w_ple_proj, w_ple_gate, w_norm_ple_post):
    depth = w_in.shape[0]
    batch, seq, _ = x_prompt.shape
    n_dec = x_sample.shape[0]
    assert x_sample.shape[1] == 1, "the sample group carries one new token per sequence"
    assert seq % SEQ_TILE == 0 and (batch * seq) % FFN_TILE == 0 and n_dec % SAMPLE_BLOCK == 0

    hp = x_prompt
    hs = x_sample
    conv_p, gla_p, conv_s, gla_s = [], [], [], []
    for i in range(depth):
        row = lambda w: w[i].reshape(1, -1)
        w_head, w_tail, w_gklr, w_a, w_b, w_o_b = _cast_w_in(
            jnp.swapaxes(w_in[i], 0, 1), (w_a_out[i], w_b_out[i], w_o[i]))
        w_gk_pad = jnp.pad(w_gk[i], ((0, LANES - GATE_RANK), (0, 0))).astype(BF16)
        mixer_weights = (row(w_norm_mix_pre), w_head, w_tail, w_gklr, w_gk_pad, row(b_gk), w_conv[i],
                         w_a, row(w_gla_norm), w_b, w_o_b, row(w_norm_mix_post))

        hp_mid, cbp, sp, w_gate, w_up, w_down, w_pp, w_pg = _prompt_mixer(
            hp, mixer_weights, (w_ffn_gate[i], w_ffn_up[i], w_ffn_down[i], w_ple_proj[i], w_ple_gate[i]))
        ffn_weights = (row(w_norm_ffn_pre), w_gate, w_up, w_down, row(w_norm_ffn_post), w_pp, w_pg,
                       row(w_norm_ple_post))
        cbs, *sample_proj, xs2d, ps2d = _sample_proj(hs, state_conv[i], p_sample[i], mixer_weights[:8])
        hp, ss, hs = _ffn_and_sample_state(
            hp_mid.reshape(batch * seq, D_MODEL), p_prompt[i].reshape(batch * seq, PLE_DIM), ffn_weights,
            state_gla[i], sample_proj, xs2d, ps2d, mixer_weights[8:])
        hp = hp.reshape(batch, seq, D_MODEL)
        hs = hs.reshape(n_dec, 1, D_MODEL)

        conv_p.append(cbp[0]); gla_p.append(sp[0])
        conv_s.append(cbs); gla_s.append(ss)
    return (hp, hs, jnp.stack(conv_p), jnp.stack(gla_p),
            jnp.stack(conv_s), jnp.stack(gla_s))
```

```python
import numpy as np
import jax
import jax.numpy as jnp
from jax import lax
from jax.experimental import pallas as pl
from jax.experimental.pallas import tpu as pltpu

D_MODEL = 1024
CONV_DIM = D_MODEL
GLA_HEADS = 4
GLA_DK = 128
GLA_DV = 256
QK_DIM = GLA_HEADS * GLA_DK
V_DIM = GLA_HEADS * GLA_DV
GATE_RANK = 16
GATE_NORMALIZER = 16.0
D_FF = 2816
PLE_DIM = 256
EPS = 1e-6
LOG2_E = 1.4426950408889634

LANES = 128
SUBLANES = 8
BF16_ROWS = 16
VMEM_LIMIT_BYTES = 56 * 1024 * 1024

OFF_B, OFF_C, OFF_X = 0, 1024, 2048
OFF_Q, OFF_K, OFF_V, OFF_G = 3072, 3584, 4096, 5120
OFF_GA, OFF_GB = 6144, 7168
N_HEAD = 6144
N_TAIL = 2048
GKLR_START = 6144

GLA_CHUNK = 256
N_LEVELS = 8
SEQ_TILE = 512
FFN_TILE = 512
FFN_CHUNK = 1408
SAMPLE_BLOCK = 4
W_IN_BLOCK = 1024

F32 = jnp.float32
BF16 = jnp.bfloat16


def _rms(x, w):
    return x * lax.rsqrt(jnp.mean(x * x, axis=-1, keepdims=True) + EPS) * w


def _sigmoid(x):
    return 1.0 / (1.0 + jnp.exp2(x * -LOG2_E))


def _log2_sigmoid(x, scale):
    return (jnp.minimum(x, 0.0) - jnp.log(1.0 + jnp.exp2(jnp.abs(x) * -LOG2_E))) * (scale * LOG2_E)


def _dot(a, b):
    return jnp.dot(a, b, preferred_element_type=F32)


def _dot_nt(a, b):
    return lax.dot_general(a, b, (((1,), (1,)), ((), ())), preferred_element_type=F32)


def _column_broadcast(row):
    return jnp.broadcast_to(row, (LANES, LANES)).T


def _projections(hn, w_head_ref, w_tail_ref, w_gklr_ref, w_gk_ref, b_gk_ref):
    def proj(off, width):
        if off < N_HEAD:
            return _dot(hn, w_head_ref[:, off:off + width])
        return _dot(hn, w_tail_ref[:, off - N_HEAD:off - N_HEAD + width])

    gk_lr = _dot(hn, w_gklr_ref[...])
    gk = _dot(gk_lr.astype(BF16), w_gk_ref[...]) + b_gk_ref[...]
    return proj, _log2_sigmoid(gk, 1.0 / GATE_NORMALIZER)


def _mix_out(x, o_heads, g, gate_a, gate_b, y_a, w_gn, w_b_ref, w_o_ref, wn_post):
    normed = []
    for h in range(GLA_HEADS):
        o = o_heads[h]
        gh = g[:, h * GLA_DV:(h + 1) * GLA_DV]
        o = o * lax.rsqrt(jnp.mean(o * o, axis=-1, keepdims=True) + EPS) * w_gn
        normed.append((o * (gh * _sigmoid(gh))).astype(BF16))
    y_b = _dot(jnp.concatenate(normed, axis=1), w_b_ref[...])
    merged = _sigmoid(gate_a) * y_a + _sigmoid(gate_b) * y_b
    mix = _dot(merged.astype(BF16), w_o_ref[...])
    return x + _rms(mix, wn_post)


def _prompt_mixer_kernel(x_ref, lev_ref, tri_ref, wn_pre_ref, w_head_ref, w_tail_ref, w_gklr_ref,
                         w_gk_ref, b_gk_ref, w_conv_ref, w_a_ref, w_gn_ref, w_b_ref, w_o_ref,
                         wn_post_ref, *rest):
    n_side = (len(rest) - 4) // 2
    side_src = rest[:n_side]
    h_ref, conv_ref, state_ref = rest[n_side:n_side + 3]
    side_dst = rest[n_side + 3:2 * n_side + 3]
    dec_ref = rest[-1]

    @pl.when(pl.program_id(1) == 0)
    def _():
        conv_ref[...] = jnp.zeros_like(conv_ref)
        state_ref[...] = jnp.zeros_like(state_ref)

    for src, dst in zip(side_src, side_dst):
        dst[...] = src[...].astype(BF16)

    t = SEQ_TILE
    x = x_ref[0]
    hn = _rms(x, wn_pre_ref[...]).astype(BF16)
    proj, logw2 = _projections(hn, w_head_ref, w_tail_ref, w_gklr_ref, w_gk_ref, b_gk_ref)

    u = proj(OFF_C, CONV_DIM) * proj(OFF_X, CONV_DIM)
    prev2 = conv_ref[0, 0, 0:1, :]
    prev1 = conv_ref[0, 0, 1:2, :]
    row = lax.broadcasted_iota(jnp.int32, (t, CONV_DIM), 0)
    u1 = jnp.where(row == 0, prev1, pltpu.roll(u, 1, 0))
    u2 = jnp.where(row == 0, prev2, jnp.where(row == 1, prev1, pltpu.roll(u, 2, 0)))
    wc = w_conv_ref[...]
    y_conv = wc[0:1] * u2 + wc[1:2] * u1 + wc[2:3] * u
    conv_ref[0, 0] = u[t - 2:t]
    y_a = _dot((proj(OFF_B, CONV_DIM) * y_conv).astype(BF16), w_a_ref[...])

    q = proj(OFF_Q, QK_DIM) * (GLA_DK ** -0.5)
    k = proj(OFF_K, QK_DIM)
    v = proj(OFF_V, V_DIM).astype(BF16)

    o_chunks = []
    for c in range(SEQ_TILE // GLA_CHUNK):
        rows = slice(c * GLA_CHUNK, (c + 1) * GLA_CHUNK)
        o_chunks.append(_gla_chunk(q[rows], k[rows], v[rows], logw2[rows], dec_ref.at[c], lev_ref,
                                   tri_ref, state_ref))
    o_heads = [jnp.concatenate([o[h] for o in o_chunks], axis=0) for h in range(GLA_HEADS)]

    h_ref[0] = _mix_out(x, o_heads, proj(OFF_G, V_DIM), proj(OFF_GA, D_MODEL), proj(OFF_GB, D_MODEL),
                        y_a, w_gn_ref[...], w_b_ref, w_o_ref, wn_post_ref[...])


def _gla_chunk(q, k, v, logw2, dec_ref, lev_ref, tri_ref, state_ref):
    t = GLA_CHUNK
    dec_ref[0] = logw2
    logw2 = dec_ref[0]
    hi = logw2.astype(BF16)
    lo = (logw2 - hi.astype(F32)).astype(BF16)
    cum = _dot(tri_ref[...], jnp.concatenate([hi, lo], axis=0))
    dec_ref[1] = cum
    cum_last = cum[t - 1:t]

    half = t // 2
    lev = lev_ref[...]
    rowq = lax.broadcasted_iota(jnp.int32, (t, QK_DIM), 0)

    def level_log2_factor(i):
        m = 1 << i
        pieces = []
        for blk in range(t // (2 * m)):
            r = blk * 2 * m + m - 1
            pieces.append(jnp.broadcast_to(dec_ref[1, r:r + 1, :], (2 * m, QK_DIM)))
        return -jnp.abs(cum - jnp.concatenate(pieces, axis=0))

    def head(a, h):
        return a[:, h * GLA_DK:(h + 1) * GLA_DK]

    def split_level(i):
        m = 1 << i
        qs, ks = [], []
        for blk in range(t // (2 * m)):
            first = slice(blk * 2 * m, blk * 2 * m + m)
            second = slice(blk * 2 * m + m, (blk + 1) * 2 * m)
            ref_row = dec_ref[1, blk * 2 * m + m - 1:blk * 2 * m + m, :]
            ks += [(k[first] * jnp.exp2(ref_row - cum[first])).astype(BF16), k_lv[0][second]]
            qs += [q_lv[0][first], (q[second] * jnp.exp2(cum[second] - ref_row)).astype(BF16)]
        return jnp.concatenate(qs, axis=0), jnp.concatenate(ks, axis=0)

    q_lv = [q.astype(BF16)]
    k_lv = [k.astype(BF16)]
    step_decay = jnp.exp2(logw2)
    r4 = rowq & 3
    q0 = q * step_decay
    q_lv += [q0.astype(BF16),
             (q0 * jnp.where(r4 == 3, pltpu.roll(step_decay, 1, 0), 1.0)).astype(BF16)]
    k_lv += [k_lv[0], (k * jnp.where(r4 == 0, pltpu.roll(step_decay, t - 1, 0), 1.0)).astype(BF16)]
    for i in range(2, N_LEVELS - 1):
        if (1 << i) % BF16_ROWS == 0:
            q_i, k_i = split_level(i)
        else:
            e = jnp.exp2(level_log2_factor(i))
            q_i, k_i = (q * e).astype(BF16), (k * e).astype(BF16)
        q_lv.append(q_i)
        k_lv.append(k_i)
    cum_mid = dec_ref[1, half - 1:half, :]
    q_top = (q[half:] * jnp.exp2(cum[half:] - cum_mid)).astype(BF16)
    k_top = (k[:half] * jnp.exp2(cum_mid - cum[:half])).astype(BF16)

    q_in = (q * jnp.exp2(cum)).astype(BF16)
    k_out = k * jnp.exp2(cum_last - cum)
    a_last = jnp.exp2(cum_last)

    o_heads = []
    for h in range(GLA_HEADS):
        vh = v[:, h * GLA_DV:(h + 1) * GLA_DV]
        diag = [0.0, 0.0]
        for i in range(N_LEVELS):
            s_i = _dot_nt(head(q_lv[i], h), head(k_lv[i], h))
            diag = [jnp.where(lev == i - 1, s_i[r0:r0 + half, r0:r0 + half], diag[j])
                    for j, r0 in enumerate((0, half))]
        diag = [d.astype(BF16) for d in diag]
        p_top = jnp.concatenate([diag[0], jnp.zeros((half, half), BF16)], axis=1)
        p_low = jnp.concatenate([_dot_nt(head(q_top, h), head(k_top, h)).astype(BF16), diag[1]], axis=1)
        s_old = state_ref[0, 0, h]
        lhs = jnp.concatenate([jnp.concatenate([p_top, p_low], axis=0), head(q_in, h)], axis=1)
        o_heads.append(_dot(lhs, jnp.concatenate([vh, s_old.astype(BF16)], axis=0)))
        a_col = _column_broadcast(head(a_last, h))
        a_col = jnp.concatenate([a_col, a_col], axis=1)
        state_ref[0, 0, h] = a_col * s_old + _dot(head(k_out, h).T.astype(BF16), vh)
    return o_heads


def _sample_proj_kernel(x_ref, cbuf_ref, wn_pre_ref, w_head_ref, w_tail_ref, w_gklr_ref, w_gk_ref,
                        b_gk_ref, w_conv_ref, w_a_ref,
                        ps_ref, conv_ref, q_ref, k_ref, a_ref, v_ref, ya_ref, g_ref, ga_ref, gb_ref,
                        xs2_ref, ps2_ref):
    x = x_ref[:, 0, :]
    xs2_ref[...] = x
    ps2_ref[...] = ps_ref[:, 0, :]
    hn = _rms(x, wn_pre_ref[...]).astype(BF16)
    proj, logw2 = _projections(hn, w_head_ref, w_tail_ref, w_gklr_ref, w_gk_ref, b_gk_ref)
    u = proj(OFF_C, CONV_DIM) * proj(OFF_X, CONV_DIM)
    buf0 = cbuf_ref[:, 0, :]
    buf1 = cbuf_ref[:, 1, :]
    wc = w_conv_ref[...]
    y_conv = wc[0:1] * buf0 + wc[1:2] * buf1 + wc[2:3] * u
    conv_ref[:, 0, :] = buf1
    conv_ref[:, 1, :] = u
    ya_ref[...] = _dot((proj(OFF_B, CONV_DIM) * y_conv).astype(BF16), w_a_ref[...])
    grouped = lambda a: a.reshape(a.shape[0] // SUBLANES, SUBLANES, a.shape[1])
    q_ref[...] = grouped(proj(OFF_Q, QK_DIM) * (GLA_DK ** -0.5))
    k_ref[...] = grouped(proj(OFF_K, QK_DIM))
    a_ref[...] = grouped(jnp.exp2(logw2))
    v_ref[...] = grouped(proj(OFF_V, V_DIM))
    g_ref[...] = proj(OFF_G, V_DIM)
    ga_ref[...] = proj(OFF_GA, D_MODEL)
    gb_ref[...] = proj(OFF_GB, D_MODEL)


def _sample_state_update(blk, st_ref, st_out_ref, q_ref, k_ref, a_ref, v_ref, o_scr):
    per_group = SUBLANES // SAMPLE_BLOCK
    group = blk // per_group
    first = (blk % per_group) * SAMPLE_BLOCK
    row_id = lax.broadcasted_iota(jnp.int32, (SUBLANES, V_DIM), 0)
    o_tile = o_scr[group]
    for n in range(SAMPLE_BLOCK):
        r = first + n
        q_row = q_ref[group, pl.ds(r, 1), :]
        k_row = k_ref[group, pl.ds(r, 1), :]
        a_row = a_ref[group, pl.ds(r, 1), :]
        v_row = v_ref[group, pl.ds(r, 1), :]
        o_parts = []
        for h in range(GLA_HEADS):
            sl = slice(h * GLA_DK, (h + 1) * GLA_DK)
            a_col = _column_broadcast(a_row[:, sl])
            k_col = _column_broadcast(k_row[:, sl])
            q_col = _column_broadcast(q_row[:, sl])
            for half in range(GLA_DV // LANES):
                c0 = half * LANES
                s_old = st_ref[n, h, :, c0:c0 + LANES]
                vv = v_row[:, h * GLA_DV + c0:h * GLA_DV + c0 + LANES]
                s_new = a_col * s_old + k_col * vv
                st_out_ref[n, h, :, c0:c0 + LANES] = s_new
                o_parts.append(jnp.sum(q_col * s_new, axis=0, keepdims=True))
        o_row = jnp.concatenate(o_parts, axis=1)
        o_tile = jnp.where(row_id == r, o_row, o_tile)
    o_scr[group] = o_tile


def _ffn_rows(h, p, wn_pre_ref, w_gate_ref, w_up_ref, w_down_ref, wn_post_ref, w_pp_ref, w_pg_ref,
              wn_ple_ref):
    f = _rms(h, wn_pre_ref[...]).astype(BF16)
    gate = _dot(f, w_gate_ref[...])
    up = _dot(f, w_up_ref[...])
    act = (gate * _sigmoid(gate) * up).astype(BF16)
    h = h + _rms(_dot(act, w_down_ref[...]), wn_post_ref[...])
    e = _dot(p.astype(BF16), w_pp_ref[...]) * _sigmoid(_dot(h.astype(BF16), w_pg_ref[...]))
    return h + _rms(e, wn_ple_ref[...])


def _ffn_kernel(h_ref, p_ref, wn_pre_ref, w_gate_ref, w_up_ref, w_down_ref, wn_post_ref,
                w_pp_ref, w_pg_ref, wn_ple_ref,
                st_ref, q_ref, k_ref, a_ref, v_ref, xs_ref, ps_ref, ya_ref, g_ref, ga_ref, gb_ref,
                w_gn_ref, w_b_ref, w_o_ref, wn_mix_post_ref,
                out_ref, st_out_ref, outs_ref, o_scr):
    step = pl.program_id(0)
    n_tiles = pl.num_programs(0) - 1
    ffn_refs = (wn_pre_ref, w_gate_ref, w_up_ref, w_down_ref, wn_post_ref, w_pp_ref, w_pg_ref, wn_ple_ref)

    @pl.when(step == 0)
    def _():
        o_scr[...] = jnp.zeros_like(o_scr)

    @pl.when(step < n_tiles)
    def _():
        out_ref[...] = _ffn_rows(h_ref[...], p_ref[...], *ffn_refs)
        _sample_state_update(step, st_ref, st_out_ref, q_ref, k_ref, a_ref, v_ref, o_scr)

    @pl.when(step == n_tiles)
    def _():
        o = o_scr[...].reshape(xs_ref.shape[0], V_DIM)
        o_heads = [o[:, h * GLA_DV:(h + 1) * GLA_DV] for h in range(GLA_HEADS)]
        hs = _mix_out(xs_ref[...], o_heads, g_ref[...], ga_ref[...], gb_ref[...], ya_ref[...],
                      w_gn_ref[...], w_b_ref, w_o_ref, wn_mix_post_ref[...])
        outs_ref[...] = _ffn_rows(hs, ps_ref[...], *ffn_refs)


def _cast_w_in_kernel(wt_ref, gk_ref, *refs):
    n = (len(refs) - 3) // 2
    head_ref, tail_ref, gklr_ref = refs[n:n + 3]
    for src, dst in zip(refs[:n], refs[n + 3:]):
        dst[...] = src[...].astype(BF16)
    j = pl.program_id(0)
    blk = wt_ref[...].T.astype(BF16)

    @pl.when(j < N_HEAD // W_IN_BLOCK)
    def _():
        head_ref[...] = blk

    @pl.when(j >= N_HEAD // W_IN_BLOCK)
    def _():
        tail_ref[...] = blk

    @pl.when(j == 0)
    def _():
        rows = jnp.concatenate([gk_ref[...], jnp.zeros((LANES - GATE_RANK, D_MODEL), F32)], axis=0)
        gklr_ref[...] = rows.T.astype(BF16)


def _cast_w_in(w_in_t, others):
    n_head = N_HEAD // W_IN_BLOCK
    n_tail = N_TAIL // W_IN_BLOCK
    other_specs = [pl.BlockSpec((w.shape[0] // (n_head + n_tail), w.shape[1]), lambda j: (j, 0))
                   for w in others]

    def src_row(j):
        row = jnp.where(j < n_head, j * W_IN_BLOCK, GKLR_START + GATE_RANK + (j - n_head) * W_IN_BLOCK)
        return pl.multiple_of(row, GATE_RANK)

    return pl.pallas_call(
        _cast_w_in_kernel,
        grid=(n_head + n_tail,),
        in_specs=[pl.BlockSpec((pl.Element(W_IN_BLOCK), pl.Element(D_MODEL)), lambda j: (src_row(j), 0)),
                  pl.BlockSpec((pl.Element(GATE_RANK), pl.Element(D_MODEL)), lambda j: (GKLR_START, 0))]
                 + other_specs,
        out_specs=[pl.BlockSpec((D_MODEL, W_IN_BLOCK), lambda j: (0, jnp.minimum(j, n_head - 1))),
                   pl.BlockSpec((D_MODEL, W_IN_BLOCK), lambda j: (0, jnp.maximum(j - n_head, 0))),
                   pl.BlockSpec((D_MODEL, LANES), lambda j: (0, 0))] + other_specs,
        out_shape=[jax.ShapeDtypeStruct((D_MODEL, N_HEAD), BF16),
                   jax.ShapeDtypeStruct((D_MODEL, N_TAIL), BF16),
                   jax.ShapeDtypeStruct((D_MODEL, LANES), BF16)]
                  + [jax.ShapeDtypeStruct(w.shape, BF16) for w in others],
        compiler_params=pltpu.CompilerParams(
            dimension_semantics=("arbitrary",), vmem_limit_bytes=VMEM_LIMIT_BYTES),
        name="cast_weights",
    )(w_in_t, w_in_t, *others)


def _resident(shape):
    return pl.BlockSpec(shape, lambda *_: (0,) * len(shape), pipeline_mode=pl.Buffered(1))


def _level_map(t):
    idx = np.arange(t)
    xor = idx[:, None] ^ idx[None, :]
    lev = np.floor(np.log2(np.maximum(xor, 1))).astype(np.int32)
    lev = np.where(idx[:, None] > idx[None, :], lev, -2)
    lev = np.where(idx[:, None] == idx[None, :], -1, lev)
    return jnp.asarray(lev, dtype=jnp.int32)


def _mixer_weight_specs():
    return [
        _resident((1, D_MODEL)),
        _resident((D_MODEL, N_HEAD)),
        _resident((D_MODEL, N_TAIL)),
        _resident((D_MODEL, LANES)),
        _resident((LANES, QK_DIM)),
        _resident((1, QK_DIM)),
        _resident((3, CONV_DIM)),
        _resident((CONV_DIM, D_MODEL)),
        _resident((1, GLA_DV)),
        _resident((V_DIM, D_MODEL)),
        _resident((D_MODEL, D_MODEL)),
        _resident((1, D_MODEL)),
    ]


def _prompt_mixer(x, mixer_weights, later_weights):
    b, s, _ = x.shape
    t = SEQ_TILE
    c = GLA_CHUNK
    n_steps = b * (s // t)
    tri = jnp.asarray(np.tile(np.tril(np.ones((c, c), np.float32)), (1, 2)), dtype=BF16)

    def side_spec(shape):
        n_blocks = n_steps
        while shape[0] % (n_blocks * BF16_ROWS):
            n_blocks //= 2
        rep = n_steps // n_blocks
        return pl.BlockSpec((shape[0] // n_blocks, shape[1]),
                            lambda i, j: ((i * (s // t) + j) // rep, 0))

    side_specs = [side_spec(w.shape) for w in later_weights]
    return pl.pallas_call(
        _prompt_mixer_kernel,
        grid=(b, s // t),
        in_specs=[pl.BlockSpec((1, t, D_MODEL), lambda i, j: (i, j, 0)),
                  _resident((c // 2, c // 2)), _resident((c, 2 * c))] + _mixer_weight_specs() + side_specs,
        out_specs=[pl.BlockSpec((1, t, D_MODEL), lambda i, j: (i, j, 0)),
                   pl.BlockSpec((1, 1, 2, CONV_DIM), lambda i, j: (0, i, 0, 0)),
                   pl.BlockSpec((1, 1, GLA_HEADS, GLA_DK, GLA_DV), lambda i, j: (0, i, 0, 0, 0))]
                  + side_specs,
        out_shape=[jax.ShapeDtypeStruct((b, s, D_MODEL), F32),
                   jax.ShapeDtypeStruct((1, b, 2, CONV_DIM), F32),
                   jax.ShapeDtypeStruct((1, b, GLA_HEADS, GLA_DK, GLA_DV), F32)]
                  + [jax.ShapeDtypeStruct(w.shape, BF16) for w in later_weights],
        scratch_shapes=[pltpu.VMEM((t // c, 2, c, QK_DIM), F32)],
        compiler_params=pltpu.CompilerParams(
            dimension_semantics=("arbitrary", "arbitrary"), vmem_limit_bytes=VMEM_LIMIT_BYTES),
        name="prompt_mixer",
    )(x, _level_map(c // 2), tri, *mixer_weights, *later_weights)


def _sample_proj(x, conv_buf, p, proj_weights):
    n = x.shape[0]
    grouped = lambda w: (n // SUBLANES, SUBLANES, w)
    out_shapes = [(n, 2, CONV_DIM), grouped(QK_DIM), grouped(QK_DIM), grouped(QK_DIM), grouped(V_DIM),
                  (n, D_MODEL), (n, V_DIM), (n, D_MODEL), (n, D_MODEL), (n, D_MODEL), (n, PLE_DIM)]
    whole = lambda shape: pl.BlockSpec(shape, lambda i: (0,) * len(shape))
    return pl.pallas_call(
        _sample_proj_kernel,
        grid=(1,),
        in_specs=[_resident((n, 1, D_MODEL)), _resident((n, 2, CONV_DIM))] + _mixer_weight_specs()[:8]
                 + [_resident((n, 1, PLE_DIM))],
        out_specs=[whole(s) for s in out_shapes],
        out_shape=[jax.ShapeDtypeStruct(s, F32) for s in out_shapes],
        compiler_params=pltpu.CompilerParams(
            dimension_semantics=("arbitrary",), vmem_limit_bytes=VMEM_LIMIT_BYTES),
        name="sample_proj",
    )(x, conv_buf, *proj_weights, p)


def _ffn_and_sample_state(h, p, ffn_weights, state, sample_proj, x_s, p_s, out_weights):
    rows = h.shape[0]
    n = x_s.shape[0]
    n_tiles = rows // FFN_TILE
    assert n_tiles * SAMPLE_BLOCK == n, "one sample state block per prompt row tile"
    q, k, a, v, y_a, g, gate_a, gate_b = sample_proj
    tile = lambda i: jnp.minimum(i, n_tiles - 1)
    state_spec = pl.BlockSpec((SAMPLE_BLOCK, GLA_HEADS, GLA_DK, GLA_DV), lambda i: (tile(i), 0, 0, 0))
    return pl.pallas_call(
        _ffn_kernel,
        grid=(n_tiles + 1,),
        in_specs=[pl.BlockSpec((FFN_TILE, D_MODEL), lambda i: (tile(i), 0)),
                  pl.BlockSpec((FFN_TILE, PLE_DIM), lambda i: (tile(i), 0)),
                  _resident((1, D_MODEL)), _resident((D_MODEL, D_FF)), _resident((D_MODEL, D_FF)),
                  _resident((D_FF, D_MODEL)), _resident((1, D_MODEL)), _resident((PLE_DIM, D_MODEL)),
                  _resident((D_MODEL, D_MODEL)), _resident((1, D_MODEL)),
                  state_spec, _resident(q.shape), _resident(k.shape), _resident(a.shape),
                  _resident(v.shape), _resident((n, D_MODEL)), _resident((n, PLE_DIM)),
                  _resident((n, D_MODEL)), _resident((n, V_DIM)), _resident((n, D_MODEL)),
                  _resident((n, D_MODEL)),
                  _resident((1, GLA_DV)), _resident((V_DIM, D_MODEL)), _resident((D_MODEL, D_MODEL)),
                  _resident((1, D_MODEL))],
        out_specs=[pl.BlockSpec((FFN_TILE, D_MODEL), lambda i: (tile(i), 0)),
                   state_spec,
                   pl.BlockSpec((n, D_MODEL), lambda i: (0, 0))],
        out_shape=[jax.ShapeDtypeStruct((rows, D_MODEL), F32),
                   jax.ShapeDtypeStruct(state.shape, F32),
                   jax.ShapeDtypeStruct((n, D_MODEL), F32)],
        scratch_shapes=[pltpu.VMEM((n // SUBLANES, SUBLANES, V_DIM), F32)],
        compiler_params=pltpu.CompilerParams(
            dimension_semantics=("arbitrary",), vmem_limit_bytes=VMEM_LIMIT_BYTES),
        name="ffn_ple",
    )(h, p, *ffn_weights, state, q, k, a, v, x_s, p_s, y_a, g, gate_a, gate_b, *out_weights)


def kernel(x_prompt, x_sample, state_conv, state_gla, p_prompt, p_sample, w_norm_mix_pre, w_in, w_conv, w_a_out, w_gk, b_gk, w_gla_norm, w_b_out, w_o, w_norm_mix_post, w_norm_ffn_pre, w_ffn_gate, w_ffn_up, w_ffn_down, w_norm_ffn_post, w_ple_proj, w_ple_gate, w_norm_ple_post):
    depth = w_in.shape[0]
    batch, seq, _ = x_prompt.shape
    n_dec = x_sample.shape[0]
    assert x_sample.shape[1] == 1, "the sample group carries one new token per sequence"
    assert seq % SEQ_TILE == 0 and (batch * seq) % FFN_TILE == 0 and n_dec % SAMPLE_BLOCK == 0

    hp = x_prompt
    hs = x_sample
    conv_p, gla_p, conv_s, gla_s = [], [], [], []
    for i in range(depth):
        row = lambda w: w[i].reshape(1, -1)
        w_head, w_tail, w_gklr, w_a, w_b, w_o_b = _cast_w_in(
            jnp.swapaxes(w_in[i], 0, 1), (w_a_out[i], w_b_out[i], w_o[i]))
        w_gk_pad = jnp.pad(w_gk[i], ((0, LANES - GATE_RANK), (0, 0))).astype(BF16)
        mixer_weights = (row(w_norm_mix_pre), w_head, w_tail, w_gklr, w_gk_pad, row(b_gk), w_conv[i],
                         w_a, row(w_gla_norm), w_b, w_o_b, row(w_norm_mix_post))

        hp_mid, cbp, sp, w_gate, w_up, w_down, w_pp, w_pg = _prompt_mixer(
            hp, mixer_weights, (w_ffn_gate[i], w_ffn_up[i], w_ffn_down[i], w_ple_proj[i], w_ple_gate[i]))
        ffn_weights = (row(w_norm_ffn_pre), w_gate, w_up, w_down, row(w_norm_ffn_post), w_pp, w_pg,
                       row(w_norm_ple_post))
        cbs, *sample_proj, xs2d, ps2d = _sample_proj(hs, state_conv[i], p_sample[i], mixer_weights[:8])
        hp, ss, hs = _ffn_and_sample_state(
            hp_mid.reshape(batch * seq, D_MODEL), p_prompt[i].reshape(batch * seq, PLE_DIM), ffn_weights,
            state_gla[i], sample_proj, xs2d, ps2d, mixer_weights[8:])
        hp = hp.reshape(batch, seq, D_MODEL)
        hs = hs.reshape(n_dec, 1, D_MODEL)

        conv_p.append(cbp[0]); gla_p.append(sp[0])
        conv_s.append(cbs); gla_s.append(ss)
    return (hp, hs, jnp.stack(conv_p), jnp.stack(gla_p),
            jnp.stack(conv_s), jnp.stack(gla_s))
```
